```python
import jax, jax.numpy as jnp
from jax import lax
import numpy as np

D_MODEL = 1024
BATCH = 8
SEQ = 4096
DEPTH = 2

GRID_W = 64
CTX_LEN = 256
N_MIXERS = 2
N_HEADS = 8
N_KV_HEADS = 2
HEAD_DIM = 128
KV_GROUP = N_HEADS // N_KV_HEADS
ROPE_THETA = 10000.0
Q_BLOCK = 128
CONV_WIDTH = 3
N_EXPERTS = 256
TOP_K = 8
N_GROUPS = 8
TOPK_GROUPS = 4
EXPERT_FF = 256
SHARED_FF = 256
ROUTED_SCALE = 2.5
MOE_BLOCK = 128
LN_EPS = 1e-5
QK_EPS = 1e-6
DN_ALPHA = (2 * DEPTH) ** 0.25
DN_BETA = (8 * DEPTH) ** -0.25

kernel_name = "hybrid_gqa_shortconv_moe_dit"


def layer_norm(x, g, b):
    xf = x.astype(jnp.float32)
    mu = xf.mean(-1, keepdims=True)
    var = jnp.square(xf - mu).mean(-1, keepdims=True)
    return ((xf - mu) * lax.rsqrt(var + LN_EPS)).astype(x.dtype) * g + b


def rms_norm(x, g):
    xf = x.astype(jnp.float32)
    return (xf * lax.rsqrt(jnp.mean(xf * xf, -1, keepdims=True) + QK_EPS)).astype(x.dtype) * g


def axial_rope_angles(n_tok):
    rows = n_tok // GRID_W
    row = jnp.repeat(jnp.arange(rows, dtype=jnp.float32), GRID_W)
    col = jnp.tile(jnp.arange(GRID_W, dtype=jnp.float32), rows)
    axis_dim = HEAD_DIM // 2
    freqs = ROPE_THETA ** (-jnp.arange(0, axis_dim, 2, dtype=jnp.float32) / axis_dim)
    return jnp.concatenate([row[:, None] * freqs, col[:, None] * freqs], axis=-1)


def apply_rope(x, ang):
    xf = x.astype(jnp.float32).reshape(*x.shape[:-1], HEAD_DIM // 2, 2)
    cos = jnp.cos(ang)[None, :, None, :]
    sin = jnp.sin(ang)[None, :, None, :]
    x0, x1 = xf[..., 0], xf[..., 1]
    out = jnp.stack([x0 * cos - x1 * sin, x0 * sin + x1 * cos], axis=-1)
    return out.reshape(x.shape).astype(x.dtype)


def gqa(q, k, v):
    b, sq = q.shape[:2]
    qg = q.reshape(b, sq, N_KV_HEADS, KV_GROUP, HEAD_DIM)
    s = jnp.einsum('bqkgd,bskd->bkgqs', qg, k) * (HEAD_DIM ** -0.5)
    p = jax.nn.softmax(s.astype(jnp.float32), axis=-1).astype(v.dtype)
    o = jnp.einsum('bkgqs,bskd->bqkgd', p, v)
    return o.reshape(b, sq, N_HEADS * HEAD_DIM)


def attention_mixer(hx, hc, w_qkv, q_g, k_g, w_o, ctx_out):
    hq = N_HEADS * HEAD_DIM
    kd = N_KV_HEADS * HEAD_DIM

    def kv_heads(kv):
        k = kv[..., :kd].reshape(*kv.shape[:2], N_KV_HEADS, HEAD_DIM)
        v = kv[..., kd:].reshape(*kv.shape[:2], N_KV_HEADS, HEAD_DIM)
        return rms_norm(k, k_g), v

    qkv = hx @ w_qkv
    qx = rms_norm(qkv[..., :hq].reshape(*hx.shape[:2], N_HEADS, HEAD_DIM), q_g)
    kx, vx = kv_heads(qkv[..., hq:])
    ang = axial_rope_angles(hx.shape[1])
    qx, kx = apply_rope(qx, ang), apply_rope(kx, ang)
    if ctx_out:
        qkv_c = hc @ w_qkv
        qc = rms_norm(qkv_c[..., :hq].reshape(*hc.shape[:2], N_HEADS, HEAD_DIM), q_g)
        kc, vc = kv_heads(qkv_c[..., hq:])
    else:
        kc, vc = kv_heads(hc @ w_qkv[:, hq:])
    k_all = jnp.concatenate([kc, kx], axis=1)
    v_all = jnp.concatenate([vc, vx], axis=1)
    b, n = hx.shape[:2]
    nb = n // Q_BLOCK
    q_blocks = qx.reshape(b, nb, Q_BLOCK, N_HEADS, HEAD_DIM).transpose(1, 0, 2, 3, 4)
    o = lax.map(lambda qb: gqa(qb, k_all, v_all), q_blocks)
    yx = o.transpose(1, 0, 2, 3).reshape(b, n, hq) @ w_o
    yc = gqa(qc, kc, vc) @ w_o if ctx_out else None
    return yx, yc


def short_conv_mixer(hx, hc, w_in, taps, w_out, ctx_out):
    pad = CONV_WIDTH // 2

    def mix(h):
        n = h.shape[1]
        bg, cg, v = jnp.split(h @ w_in, 3, axis=-1)
        u = jnp.pad(cg * v, ((0, 0), (pad, pad), (0, 0)))
        conv = sum(u[:, j:j + n] * taps[j] for j in range(CONV_WIDTH))
        return (bg * conv) @ w_out

    return mix(hx), (mix(hc) if ctx_out else None)


def moe_ffn(h, router_w, router_bias, w_gate_up, w_down, sw_gate_up, sw_down):
    t, d = h.shape
    scores = jax.nn.sigmoid((h @ router_w).astype(jnp.float32))
    biased = scores + router_bias.astype(jnp.float32)
    per_group = N_EXPERTS // N_GROUPS
    grp_score = lax.top_k(biased.reshape(t, N_GROUPS, per_group), 2)[0].sum(-1)
    _, top_grp = lax.top_k(grp_score, TOPK_GROUPS)
    grp_mask = jnp.any(top_grp[:, :, None] == jnp.arange(N_GROUPS)[None, None, :], axis=1)
    masked = jnp.where(jnp.repeat(grp_mask, per_group, axis=1), biased, -jnp.inf)
    _, eidx = lax.top_k(masked, TOP_K)
    gates = jnp.take_along_axis(scores, eidx, axis=1)
    gates = (gates / gates.sum(-1, keepdims=True) * ROUTED_SCALE).astype(h.dtype)

    tk = t * TOP_K
    eid = eidx.reshape(tk)
    tok = jnp.repeat(jnp.arange(t, dtype=jnp.int32), TOP_K)
    gw = gates.reshape(tk)
    order = jnp.argsort(eid)
    e_s, tok_s, gw_s = eid[order], tok[order], gw[order]
    counts = jnp.bincount(eid, length=N_EXPERTS)
    starts = jnp.cumsum(counts) - counts
    pcounts = (counts + MOE_BLOCK - 1) // MOE_BLOCK * MOE_BLOCK
    pends = jnp.cumsum(pcounts)
    pstarts = pends - pcounts
    dest = pstarts[e_s] + jnp.arange(tk, dtype=jnp.int32) - starts[e_s]
    n_blk = (tk + N_EXPERTS * (MOE_BLOCK - 1) + MOE_BLOCK - 1) // MOE_BLOCK
    p = n_blk * MOE_BLOCK
    buf_tok = jnp.full((p,), t, jnp.int32).at[dest].set(tok_s)
    buf_w = jnp.zeros((p,), h.dtype).at[dest].set(gw_s)
    blk_e = jnp.minimum(jnp.searchsorted(pends, jnp.arange(n_blk, dtype=jnp.int32) * MOE_BLOCK,
                                         side='right'), N_EXPERTS - 1)
    h_pad = jnp.concatenate([h, jnp.zeros((1, d), h.dtype)], axis=0)

    def expert_block(acc, xs):
        tb, wb, e = xs
        g, u = jnp.split(h_pad[tb] @ w_gate_up[e], 2, axis=-1)
        yb = (jax.nn.silu(g) * u) @ w_down[e]
        return acc.at[tb].add(yb * wb[:, None]), None

    routed, _ = lax.scan(expert_block, jnp.zeros_like(h_pad),
                         (buf_tok.reshape(n_blk, MOE_BLOCK), buf_w.reshape(n_blk, MOE_BLOCK), blk_e))
    sg, su = jnp.split(h @ sw_gate_up, 2, axis=-1)
    return (jax.nn.silu(sg) * su) @ sw_down + routed[:t]


def setup_inputs(seed: int = 0) -> dict:
    key = jax.random.key(seed)
    ks = jax.random.split(key, 21)

    def nrm(k, shape, scale):
        return jax.random.normal(k, shape, jnp.float32) * scale

    n_attn = (DEPTH + N_MIXERS - 1) // N_MIXERS
    n_conv = DEPTH // N_MIXERS
    qkv_dim = (N_HEADS + 2 * N_KV_HEADS) * HEAD_DIM
    hq = N_HEADS * HEAD_DIM
    d = D_MODEL
    return {
        "x": nrm(ks[0], (BATCH, SEQ, d), 1.0),
        "c": nrm(ks[1], (BATCH, d), 1.0),
        "ctx": nrm(ks[2], (BATCH, CTX_LEN, d), 1.0),
        "c_ctx": nrm(ks[3], (d,), 1.0),
        "ada_w": nrm(ks[4], (DEPTH, d, 6 * d), 0.5 * d ** -0.5),
        "ada_b": nrm(ks[5], (DEPTH, 6 * d), 0.02),
        "ln_g": 1.0 + nrm(ks[6], (DEPTH, 2, d), 0.02),
        "ln_b": nrm(ks[7], (DEPTH, 2, d), 0.02),
        "attn_w_qkv": nrm(ks[8], (n_attn, d, qkv_dim), d ** -0.5),
        "attn_q_norm": 1.0 + nrm(ks[9], (n_attn, HEAD_DIM), 0.02),
        "attn_k_norm": 1.0 + nrm(ks[10], (n_attn, HEAD_DIM), 0.02),
        "attn_w_o": nrm(ks[11], (n_attn, hq, d), DN_BETA * hq ** -0.5),
        "conv_w_in": nrm(ks[12], (n_conv, d, 3 * d), d ** -0.5),
        "conv_taps": nrm(ks[13], (n_conv, CONV_WIDTH, d), CONV_WIDTH ** -0.5),
        "conv_w_out": nrm(ks[14], (n_conv, d, d), DN_BETA * d ** -0.5),
        "router_w": nrm(ks[15], (DEPTH, d, N_EXPERTS), d ** -0.5),
        "router_bias": nrm(ks[16], (DEPTH, N_EXPERTS), 0.01),
        "exp_w_gate_up": nrm(ks[17], (DEPTH, N_EXPERTS, d, 2 * EXPERT_FF), d ** -0.5),
        "exp_w_down": nrm(ks[18], (DEPTH, N_EXPERTS, EXPERT_FF, d), DN_BETA * EXPERT_FF ** -0.5),
        "shared_w_gate_up": nrm(ks[19], (DEPTH, d, 2 * SHARED_FF), d ** -0.5),
        "shared_w_down": nrm(ks[20], (DEPTH, SHARED_FF, d), DN_BETA * SHARED_FF ** -0.5),
    }


def reference(x, c, ctx, c_ctx, ada_w, ada_b, ln_g, ln_b, attn_w_qkv, attn_q_norm, attn_k_norm,
              attn_w_o, conv_w_in, conv_taps, conv_w_out, router_w, router_bias, exp_w_gate_up,
              exp_w_down, shared_w_gate_up, shared_w_down):
    b, n, d = x.shape
    cl = ctx.shape[1]
    silu_c = jax.nn.silu(c)
    silu_cc = jax.nn.silu(c_ctx)
    for i in range(DEPTH):
        last = i == DEPTH - 1
        is_attn = i % N_MIXERS == 0
        j = i // N_MIXERS
        mx = (silu_c @ ada_w[i] + ada_b[i]).reshape(b, 6, 1, d)
        mc = (silu_cc @ ada_w[i] + ada_b[i]).reshape(6, 1, d)

        hx = x * (1.0 + mx[:, 1]) + mx[:, 0]
        hc = ctx * (1.0 + mc[1]) + mc[0] if (is_attn or not last) else None
        if is_attn:
            yx, yc = attention_mixer(hx, hc, attn_w_qkv[j], attn_q_norm[j], attn_k_norm[j],
                                     attn_w_o[j], not last)
        else:
            yx, yc = short_conv_mixer(hx, hc, conv_w_in[j], conv_taps[j], conv_w_out[j], not last)
        x = layer_norm(DN_ALPHA * x + mx[:, 2] * yx, ln_g[i, 0], ln_b[i, 0])
        if not last:
            ctx = layer_norm(DN_ALPHA * ctx + mc[2] * yc, ln_g[i, 0], ln_b[i, 0])

        hx = (x * (1.0 + mx[:, 4]) + mx[:, 3]).reshape(b * n, d)
        moe_args = (router_w[i], router_bias[i], exp_w_gate_up[i], exp_w_down[i],
                    shared_w_gate_up[i], shared_w_down[i])
        if not last:
            hc = (ctx * (1.0 + mc[4]) + mc[3]).reshape(b * cl, d)
            out = moe_ffn(jnp.concatenate([hc, hx], axis=0), *moe_args)
            ctx = layer_norm(DN_ALPHA * ctx + mc[5] * out[:b * cl].reshape(b, cl, d),
                             ln_g[i, 1], ln_b[i, 1])
            yx = out[b * cl:].reshape(b, n, d)
        else:
            yx = moe_ffn(hx, *moe_args).reshape(b, n, d)
        x = layer_norm(DN_ALPHA * x + mx[:, 5] * yx, ln_g[i, 1], ln_b[i, 1])
    return x
```

```python
import functools

import jax
import jax.numpy as jnp
from jax import lax
from jax.experimental import pallas as pl
from jax.experimental.pallas import tpu as pltpu

F32 = jnp.float32
BF16 = jnp.bfloat16
I32 = jnp.int32

N_HEADS = 8
N_KV_HEADS = 2
HEAD_DIM = 128
KV_GROUP = N_HEADS // N_KV_HEADS
GRID_W = 64
ROPE_THETA = 10000.0
N_EXPERTS = 256
TOP_K = 8
N_GROUPS = 8
TOPK_GROUPS = 4
PER_GROUP = N_EXPERTS // N_GROUPS
ROUTED_SCALE = 2.5
LN_EPS = 1e-5
QK_EPS = 1e-6
N_MOD = 6
MOD_ROWS = 16

LANES = 128
SUBLANES = 8
VMEM_LIMIT = 56 * 1024 * 1024

ROW_TILE = 512
ATTN_TQ = 256
ATTN_TK = 512
MOE_BM = 128
DISPATCH_TOK = 256
CONV_LANES = 128

HIGHEST = lax.Precision.HIGHEST


def _cparams(sem):
    return pltpu.CompilerParams(dimension_semantics=sem, vmem_limit_bytes=VMEM_LIMIT)


def _silu(v):
    return v * jax.nn.sigmoid(v)


def _mod_kernel(c_ref, w_ref, b_ref, o_ref):
    s = _silu(c_ref[...])
    o_ref[...] = jnp.dot(s, w_ref[...], precision=HIGHEST, preferred_element_type=F32) + b_ref[...]


def _modulation(cond, ada_w, ada_b):
    depth, d, nd = ada_w.shape
    tn = 1536
    out = pl.pallas_call(
        _mod_kernel,
        grid=(depth, nd // tn),
        in_specs=[
            pl.BlockSpec((MOD_ROWS, d), lambda l, j: (0, 0)),
            pl.BlockSpec((None, d, tn), lambda l, j: (l, 0, j)),
            pl.BlockSpec((None, 1, tn), lambda l, j: (l, 0, j)),
        ],
        out_specs=pl.BlockSpec((None, MOD_ROWS, tn), lambda l, j: (l, 0, j)),
        out_shape=jax.ShapeDtypeStruct((depth, MOD_ROWS, nd), F32),
        compiler_params=_cparams(("arbitrary", "arbitrary")),
        name="adaln_modulation",
    )(cond, ada_w, ada_b.reshape(depth, 1, nd))
    return out.reshape(depth, MOD_ROWS * N_MOD, 1, d)


def _mod_spec(comp, tm, n_lat, seq, ctx_row, d):
    def index(i, *_):
        row0 = i * tm
        r = jnp.where(row0 < n_lat, row0 // seq, ctx_row)
        return (r * N_MOD + comp, 0, 0)

    return pl.BlockSpec((None, 1, d), index)


def _qkv_kernel(x_ref, shift_ref, scale_ref, w_ref, qg_ref, kg_ref, cos_ref, sin_ref,
                q_ref, k_ref, v_ref):
    h = (x_ref[...] * (1.0 + scale_ref[...]) + shift_ref[...]).astype(BF16)
    qkv = jnp.dot(h, w_ref[...], preferred_element_type=F32)
    cos = cos_ref[...]
    sin = sin_ref[...]
    hq = N_HEADS * HEAD_DIM
    kd = N_KV_HEADS * HEAD_DIM

    def norm_rope(t, g, post):
        t = t * lax.rsqrt(jnp.mean(t * t, axis=-1, keepdims=True) + QK_EPS) * g
        t = t * cos + pltpu.roll(t, HEAD_DIM // 2, axis=1) * sin
        return (t * post).astype(BF16)

    for hd in range(N_HEADS):
        sl = slice(hd * HEAD_DIM, (hd + 1) * HEAD_DIM)
        q_ref[:, sl] = norm_rope(qkv[:, sl], qg_ref[...], HEAD_DIM ** -0.5)
    for hd in range(N_KV_HEADS):
        sl = slice(hd * HEAD_DIM, (hd + 1) * HEAD_DIM)
        k_ref[:, sl] = norm_rope(qkv[:, hq + hd * HEAD_DIM: hq + (hd + 1) * HEAD_DIM], kg_ref[...], 1.0)
    v_ref[...] = qkv[:, hq + kd:].astype(BF16)


def _rope_tables(seq, tm):
    rows = seq // GRID_W
    row = jnp.repeat(jnp.arange(rows, dtype=F32), GRID_W)
    col = jnp.tile(jnp.arange(GRID_W, dtype=F32), rows)
    axis_dim = HEAD_DIM // 2
    freqs = ROPE_THETA ** (-jnp.arange(0, axis_dim, 2, dtype=F32) / axis_dim)
    ang = jnp.concatenate([row[:, None] * freqs, col[:, None] * freqs], axis=-1)
    cos = jnp.concatenate([jnp.cos(ang), jnp.cos(ang)], axis=-1)
    sin = jnp.concatenate([-jnp.sin(ang), jnp.sin(ang)], axis=-1)
    cos = jnp.concatenate([cos, jnp.ones((tm, HEAD_DIM), F32)], axis=0)
    sin = jnp.concatenate([sin, jnp.zeros((tm, HEAD_DIM), F32)], axis=0)
    return cos.reshape(seq // tm + 1, tm, HEAD_DIM), sin.reshape(seq // tm + 1, tm, HEAD_DIM)


def _qkv_project(xall, mod, w_qkv, q_g, k_g, n_lat, seq, ctx_row):
    t, d = xall.shape
    tm = ROW_TILE
    hq = N_HEADS * HEAD_DIM
    kd = N_KV_HEADS * HEAD_DIM
    perm = jnp.concatenate([jnp.arange(0, HEAD_DIM, 2), jnp.arange(1, HEAD_DIM, 2)])
    cols = jnp.concatenate([hd * HEAD_DIM + perm for hd in range(N_HEADS + N_KV_HEADS)]
                           + [jnp.arange(hq + kd, hq + 2 * kd)])
    w = w_qkv[:, cols].astype(BF16)
    cos, sin = _rope_tables(seq, tm)
    n_pos = seq // tm

    def pos_index(i):
        row0 = i * tm
        return (jnp.where(row0 < n_lat, (row0 % seq) // tm, n_pos), 0, 0)

    const = lambda i: (0, 0)
    return pl.pallas_call(
        _qkv_kernel,
        grid=(t // tm,),
        in_specs=[
            pl.BlockSpec((tm, d), lambda i: (i, 0)),
            _mod_spec(0, tm, n_lat, seq, ctx_row, d),
            _mod_spec(1, tm, n_lat, seq, ctx_row, d),
            pl.BlockSpec(w.shape, const),
            pl.BlockSpec((1, HEAD_DIM), const),
            pl.BlockSpec((1, HEAD_DIM), const),
            pl.BlockSpec((None, tm, HEAD_DIM), pos_index),
            pl.BlockSpec((None, tm, HEAD_DIM), pos_index),
        ],
        out_specs=[
            pl.BlockSpec((tm, hq), lambda i: (i, 0)),
            pl.BlockSpec((tm, kd), lambda i: (i, 0)),
            pl.BlockSpec((tm, kd), lambda i: (i, 0)),
        ],
        out_shape=[
            jax.ShapeDtypeStruct((t, hq), BF16),
            jax.ShapeDtypeStruct((t, kd), BF16),
            jax.ShapeDtypeStruct((t, kd), BF16),
        ],
        compiler_params=_cparams(("parallel",)),
        name="qkv_norm_rope",
    )(xall, mod, mod, w, q_g[perm].reshape(1, HEAD_DIM), k_g[perm].reshape(1, HEAD_DIM), cos, sin)


def _attn_kernel(*refs, n_lat_chunks, tk):
    if n_lat_chunks:
        q_ref, kc_ref, vc_ref, kl_ref, vl_ref, o_ref = refs
    else:
        q_ref, kc_ref, vc_ref, o_ref = refs
    tq = q_ref.shape[0]
    q = jnp.concatenate([q_ref[:, h * HEAD_DIM:(h + 1) * HEAD_DIM] for h in range(KV_GROUP)], axis=0)
    rows = KV_GROUP * tq

    def chunk(k, v, m, l, acc):
        s = lax.dot_general(q, k, (((1,), (1,)), ((), ())), preferred_element_type=F32)
        m_new = jnp.maximum(m, jnp.max(s, axis=-1, keepdims=True))
        p = jnp.exp(s - m_new)
        a = jnp.exp(m - m_new)
        l = a * l + jnp.sum(p, axis=-1, keepdims=True)
        acc = a * acc + jnp.dot(p.astype(BF16), v, preferred_element_type=F32)
        return m_new, l, acc

    m = jnp.full((rows, 1), -jnp.inf, F32)
    l = jnp.zeros((rows, 1), F32)
    acc = jnp.zeros((rows, HEAD_DIM), F32)
    m, l, acc = chunk(kc_ref[...], vc_ref[...], m, l, acc)
    for c in range(n_lat_chunks):
        m, l, acc = chunk(kl_ref[c * tk:(c + 1) * tk, :], vl_ref[c * tk:(c + 1) * tk, :], m, l, acc)
    o = (acc / l).astype(BF16)
    for h in range(KV_GROUP):
        o_ref[:, h * HEAD_DIM:(h + 1) * HEAD_DIM] = o[h * tq:(h + 1) * tq]


def _attention(q, k, v, b, seq, cl, n_lat):
    t = q.shape[0]
    gw = KV_GROUP * HEAD_DIM
    tq = min(ATTN_TQ, seq)
    tk = min(ATTN_TK, seq)
    nq = seq // tq
    ctx_blk0 = n_lat // cl
    hq = N_HEADS * HEAD_DIM

    ctx_kv = pl.BlockSpec((cl, HEAD_DIM), lambda bi, g, qi: (ctx_blk0 + bi, g))
    lat_kv = pl.BlockSpec((seq, HEAD_DIM), lambda bi, g, qi: (bi, g))
    lat_q = pl.BlockSpec((tq, gw), lambda bi, g, qi: (bi * nq + qi, g))
    o_lat = pl.pallas_call(
        functools.partial(_attn_kernel, n_lat_chunks=seq // tk, tk=tk),
        grid=(b, N_KV_HEADS, nq),
        in_specs=[lat_q, ctx_kv, ctx_kv, lat_kv, lat_kv],
        out_specs=lat_q,
        out_shape=jax.ShapeDtypeStruct((n_lat, hq), BF16),
        compiler_params=_cparams(("parallel", "parallel", "arbitrary")),
        name="attention_latent",
    )(q, k, v, k, v)

    ctx_q = pl.BlockSpec((cl, gw), lambda bi, g: (ctx_blk0 + bi, g))
    ctx_kv2 = pl.BlockSpec((cl, HEAD_DIM), lambda bi, g: (ctx_blk0 + bi, g))
    o_ctx = pl.pallas_call(
        functools.partial(_attn_kernel, n_lat_chunks=0, tk=tk),
        grid=(b, N_KV_HEADS),
        in_specs=[ctx_q, ctx_kv2, ctx_kv2],
        out_specs=pl.BlockSpec((cl, gw), lambda bi, g: (bi, g)),
        out_shape=jax.ShapeDtypeStruct((t - n_lat, hq), BF16),
        compiler_params=_cparams(("parallel", "parallel")),
        name="attention_context",
    )(q, k, v)
    return jnp.concatenate([o_lat, o_ctx], axis=0)


def _in_proj_kernel(x_ref, shift_ref, scale_ref, w_ref, o_ref):
    h = (x_ref[...] * (1.0 + scale_ref[...]) + shift_ref[...]).astype(BF16)
    o_ref[...] = jnp.dot(h, w_ref[...], preferred_element_type=F32)


def _in_project(xall, mod, w_in, n_rows, n_lat, seq, ctx_row):
    d = xall.shape[1]
    n_out = w_in.shape[1]
    tm = ROW_TILE
    return pl.pallas_call(
        _in_proj_kernel,
        grid=(n_rows // tm,),
        in_specs=[
            pl.BlockSpec((tm, d), lambda i: (i, 0)),
            _mod_spec(0, tm, n_lat, seq, ctx_row, d),
            _mod_spec(1, tm, n_lat, seq, ctx_row, d),
            pl.BlockSpec(w_in.shape, lambda i: (0, 0)),
        ],
        out_specs=pl.BlockSpec((tm, n_out), lambda i: (i, 0)),
        out_shape=jax.ShapeDtypeStruct((n_rows, n_out), F32),
        compiler_params=_cparams(("parallel",)),
        name="conv_in_proj",
    )(xall, mod, mod, w_in.astype(BF16))


def _conv_kernel(bg_ref, cg_ref, v_ref, taps_ref, o_ref):
    u = cg_ref[...] * v_ref[...]
    n = u.shape[0]
    pos = lax.broadcasted_iota(I32, u.shape, 0)
    prev = jnp.where(pos == 0, 0.0, pltpu.roll(u, 1, axis=0))
    nxt = jnp.where(pos == n - 1, 0.0, pltpu.roll(u, n - 1, axis=0))
    conv = prev * taps_ref[0:1, :] + u * taps_ref[1:2, :] + nxt * taps_ref[2:3, :]
    o_ref[...] = (bg_ref[...] * conv).astype(BF16)


def _short_conv(zin, taps, n_seqs, seq, d):
    tc = CONV_LANES
    nj = d // tc
    return pl.pallas_call(
        _conv_kernel,
        grid=(n_seqs, nj),
        in_specs=[
            pl.BlockSpec((seq, tc), lambda s, j: (s, j)),
            pl.BlockSpec((seq, tc), lambda s, j: (s, nj + j)),
            pl.BlockSpec((seq, tc), lambda s, j: (s, 2 * nj + j)),
            pl.BlockSpec((taps.shape[0], tc), lambda s, j: (0, j)),
        ],
        out_specs=pl.BlockSpec((seq, tc), lambda s, j: (s, j)),
        out_shape=jax.ShapeDtypeStruct((n_seqs * seq, d), BF16),
        compiler_params=_cparams(("parallel", "parallel")),
        name="short_conv",
    )(zin, zin, zin, taps)


def _layer_norm(z, g, b):
    mu = jnp.mean(z, axis=-1, keepdims=True)
    zc = z - mu
    var = jnp.mean(zc * zc, axis=-1, keepdims=True)
    return zc * lax.rsqrt(var + LN_EPS) * g + b


def _post_kernel(a_ref, x_ref, gate_ref, lng_ref, lnb_ref, shift_ref, scale_ref, w_ref,
                 rwt_ref, rb_ref, tri_ref,
                 x1_ref, h2_ref, eidx_ref, gates_ref, rank_ref, cnt_ref, *, dn_alpha):
    tm = a_ref.shape[0]
    d = x_ref.shape[1]
    y = jnp.dot(a_ref[...], w_ref[...], preferred_element_type=F32)
    x1 = _layer_norm(dn_alpha * x_ref[...] + gate_ref[...] * y, lng_ref[...], lnb_ref[...])
    x1_ref[...] = x1
    h2 = x1 * (1.0 + scale_ref[...]) + shift_ref[...]
    for j in range(d // LANES):
        h2_ref[:, j, :] = h2[:, j * LANES:(j + 1) * LANES]

    logits = lax.dot_general(rwt_ref[...], h2, (((1,), (1,)), ((), ())),
                             precision=HIGHEST, preferred_element_type=F32)
    scores = jax.nn.sigmoid(logits)
    biased = scores + rb_ref[...]
    neg = -jnp.inf
    big = jnp.int32(1 << 30)
    row = lax.broadcasted_iota(I32, (N_EXPERTS, tm), 0)

    def argmax_rows(vals, idx):
        mx = jnp.max(vals, axis=0, keepdims=True)
        return mx, jnp.min(jnp.where(vals == mx, idx, big), axis=0, keepdims=True)

    gs = []
    grp_row = lax.broadcasted_iota(I32, (PER_GROUP, tm), 0)
    for g in range(N_GROUPS):
        bg = biased[g * PER_GROUP:(g + 1) * PER_GROUP]
        ig = grp_row + g * PER_GROUP
        m1, i1 = argmax_rows(bg, ig)
        m2 = jnp.max(jnp.where(ig == i1, neg, bg), axis=0, keepdims=True)
        gs.append(m1 + m2)
    gsc = jnp.concatenate(gs, axis=0)
    grow = lax.broadcasted_iota(I32, (N_GROUPS, tm), 0)
    gsel = jnp.zeros((N_GROUPS, tm), F32)
    for _ in range(TOPK_GROUPS):
        _, gi = argmax_rows(gsc, grow)
        hit = grow == gi
        gsel = jnp.where(hit, 1.0, gsel)
        gsc = jnp.where(hit, neg, gsc)
    cur = jnp.concatenate(
        [jnp.where(gsel[g:g + 1] > 0.0, biased[g * PER_GROUP:(g + 1) * PER_GROUP], neg)
         for g in range(N_GROUPS)], axis=0)

    onehot = jnp.zeros((N_EXPERTS, tm), F32)
    idxs, gvals = [], []
    for _ in range(TOP_K):
        _, ei = argmax_rows(cur, row)
        hit = row == ei
        gvals.append(jnp.sum(jnp.where(hit, scores, 0.0), axis=0, keepdims=True))
        idxs.append(ei)
        onehot = jnp.where(hit, 1.0, onehot)
        cur = jnp.where(hit, neg, cur)
    gv = jnp.concatenate(gvals, axis=0)
    gates_ref[...] = gv / jnp.sum(gv, axis=0, keepdims=True) * ROUTED_SCALE
    eidx_ref[...] = jnp.concatenate(idxs, axis=0)

    @pl.when(pl.program_id(0) == 0)
    def _():
        cnt_ref[...] = jnp.zeros_like(cnt_ref)

    prefix = jnp.dot(onehot.astype(BF16), tri_ref[...], preferred_element_type=F32)
    pos = prefix + cnt_ref[...]
    rank_ref[...] = jnp.concatenate(
        [jnp.sum(jnp.where(row == ei, pos, 0.0), axis=0, keepdims=True) for ei in idxs],
        axis=0).astype(I32)
    cnt_ref[...] = cnt_ref[...] + jnp.sum(onehot, axis=1, keepdims=True)


def _post_mixer(a, xall, mod, ln_g, ln_b, w, router_w, router_bias, n_rows, n_lat, seq, ctx_row,
                dn_alpha):
    d = xall.shape[1]
    tm = ROW_TILE
    nl = d // LANES
    tri = (lax.broadcasted_iota(I32, (tm, tm), 0) < lax.broadcasted_iota(I32, (tm, tm), 1)).astype(BF16)
    const = lambda i: (0, 0)
    row_blk = pl.BlockSpec((tm, d), lambda i: (i, 0))
    k_blk = pl.BlockSpec((TOP_K, tm), lambda i: (0, i))
    return pl.pallas_call(
        functools.partial(_post_kernel, dn_alpha=dn_alpha),
        grid=(n_rows // tm,),
        in_specs=[
            row_blk, row_blk,
            _mod_spec(2, tm, n_lat, seq, ctx_row, d),
            pl.BlockSpec((1, d), const), pl.BlockSpec((1, d), const),
            _mod_spec(3, tm, n_lat, seq, ctx_row, d),
            _mod_spec(4, tm, n_lat, seq, ctx_row, d),
            pl.BlockSpec(w.shape, const),
            pl.BlockSpec((N_EXPERTS, d), const),
            pl.BlockSpec((N_EXPERTS, 1), const),
            pl.BlockSpec((tm, tm), const),
        ],
        out_specs=[
            row_blk,
            pl.BlockSpec((tm, nl, LANES), lambda i: (i, 0, 0)),
            k_blk, k_blk, k_blk,
            pl.BlockSpec((N_EXPERTS, 1), const),
        ],
        out_shape=[
            jax.ShapeDtypeStruct((n_rows, d), F32),
            jax.ShapeDtypeStruct((n_rows, nl, LANES), F32),
            jax.ShapeDtypeStruct((TOP_K, n_rows), I32),
            jax.ShapeDtypeStruct((TOP_K, n_rows), F32),
            jax.ShapeDtypeStruct((TOP_K, n_rows), I32),
            jax.ShapeDtypeStruct((N_EXPERTS, 1), F32),
        ],
        compiler_params=_cparams(("arbitrary",)),
        name="post_mixer_router",
    )(a, xall, mod, ln_g.reshape(1, d), ln_b.reshape(1, d), mod, mod, w.astype(BF16),
      router_w.T, router_bias.reshape(N_EXPERTS, 1), tri)


def _dispatch_kernel(dest_ref, h_hbm, init_hbm, xs_hbm, sem):
    del init_hbm
    td = dest_ref.shape[1]
    base = pl.program_id(0) * td

    def issue(t, carry):
        for k in range(TOP_K):
            pltpu.make_async_copy(h_hbm.at[base + t], xs_hbm.at[dest_ref[k, t]], sem).start()
        return carry

    lax.fori_loop(0, td, issue, 0)
    pltpu.make_async_copy(xs_hbm.at[pl.ds(0, TOP_K * td)], xs_hbm.at[pl.ds(0, TOP_K * td)], sem).wait()


def _dispatch(dest, h_slab, n_slots):
    n_tok, nl, lanes = h_slab.shape
    td = DISPATCH_TOK
    init = jnp.zeros((n_slots, nl, lanes), F32)
    return pl.pallas_call(
        _dispatch_kernel,
        grid=(n_tok // td,),
        in_specs=[
            pl.BlockSpec((TOP_K, td), lambda i: (0, i), memory_space=pltpu.SMEM),
            pl.BlockSpec(memory_space=pl.ANY),
            pl.BlockSpec(memory_space=pl.ANY),
        ],
        out_specs=pl.BlockSpec(memory_space=pl.ANY),
        out_shape=jax.ShapeDtypeStruct((n_slots, nl, lanes), F32),
        scratch_shapes=[pltpu.SemaphoreType.DMA(())],
        input_output_aliases={2: 0},
        compiler_params=_cparams(("arbitrary",)),
        name="moe_dispatch",
    )(dest, h_slab, init)


def _gmm_kernel(blk_e_ref, n_used_ref, xs_ref, wgu_ref, wd_ref, ys_ref):
    del blk_e_ref
    nl = xs_ref.shape[1]
    ff = wd_ref.shape[0]

    @pl.when(pl.program_id(0) < n_used_ref[0])
    def _():
        x = jnp.concatenate([xs_ref[:, j, :] for j in range(nl)], axis=-1).astype(BF16)
        gu = jnp.dot(x, wgu_ref[...], preferred_element_type=F32)
        act = (_silu(gu[:, :ff]) * gu[:, ff:]).astype(BF16)
        y = jnp.dot(act, wd_ref[...], preferred_element_type=F32)
        for j in range(nl):
            ys_ref[:, j, :] = y[:, j * LANES:(j + 1) * LANES]

    @pl.when(pl.program_id(0) >= n_used_ref[0])
    def _():
        ys_ref[...] = jnp.zeros_like(ys_ref)


def _grouped_ffn(xs, blk_e, n_used, wgu, wd):
    n_slots, nl, lanes = xs.shape
    bm = MOE_BM
    _, d, ff2 = wgu.shape
    ff = wd.shape[1]
    grid_spec = pltpu.PrefetchScalarGridSpec(
        num_scalar_prefetch=2,
        grid=(n_slots // bm,),
        in_specs=[
            pl.BlockSpec((bm, nl, lanes), lambda i, be, nu: (i, 0, 0)),
            pl.BlockSpec((None, d, ff2), lambda i, be, nu: (be[i], 0, 0)),
            pl.BlockSpec((None, ff, d), lambda i, be, nu: (be[i], 0, 0)),
        ],
        out_specs=pl.BlockSpec((bm, nl, lanes), lambda i, be, nu: (i, 0, 0)),
    )
    return pl.pallas_call(
        _gmm_kernel,
        grid_spec=grid_spec,
        out_shape=jax.ShapeDtypeStruct((n_slots, nl, lanes), F32),
        compiler_params=_cparams(("arbitrary",)),
        name="moe_grouped_ffn",
    )(blk_e, n_used, xs, wgu, wd)


def _combine_kernel(dest_ref, gates_ref, ys_hbm, x1_ref, h2_ref, gate_ref, lng_ref, lnb_ref,
                    swgu_ref, swd_ref, o_ref, buf, sem, *, dn_alpha):
    tc = x1_ref.shape[0]
    nl = h2_ref.shape[1]
    ff = swd_ref.shape[0]

    def issue(t, carry):
        for k in range(TOP_K):
            pltpu.make_async_copy(ys_hbm.at[dest_ref[k, t]], buf.at[k, t], sem).start()
        return carry

    lax.fori_loop(0, tc, issue, 0)

    h = jnp.concatenate([h2_ref[:, j, :] for j in range(nl)], axis=-1).astype(BF16)
    gu = jnp.dot(h, swgu_ref[...], preferred_element_type=F32)
    act = (_silu(gu[:, :ff]) * gu[:, ff:]).astype(BF16)
    y = jnp.dot(act, swd_ref[...], preferred_element_type=F32)

    for k in range(TOP_K):
        pltpu.make_async_copy(ys_hbm.at[pl.ds(0, tc)], buf.at[k], sem).wait()
    for k in range(TOP_K):
        yk = jnp.concatenate([buf[k, :, j, :] for j in range(nl)], axis=-1)
        y = y + gates_ref[:, k:k + 1] * yk
    o_ref[...] = _layer_norm(dn_alpha * x1_ref[...] + gate_ref[...] * y, lng_ref[...], lnb_ref[...])


def _combine(dest, gates_t, ys, x1, h2_slab, mod, ln_g, ln_b, swgu, swd, n_rows, n_lat, seq, ctx_row,
             dn_alpha):
    d = x1.shape[1]
    nl = d // LANES
    tc = DISPATCH_TOK
    const = lambda i: (0, 0)
    return pl.pallas_call(
        functools.partial(_combine_kernel, dn_alpha=dn_alpha),
        grid=(n_rows // tc,),
        in_specs=[
            pl.BlockSpec((TOP_K, tc), lambda i: (0, i), memory_space=pltpu.SMEM),
            pl.BlockSpec((tc, TOP_K), lambda i: (i, 0)),
            pl.BlockSpec(memory_space=pl.ANY),
            pl.BlockSpec((tc, d), lambda i: (i, 0)),
            pl.BlockSpec((tc, nl, LANES), lambda i: (i, 0, 0)),
            _mod_spec(5, tc, n_lat, seq, ctx_row, d),
            pl.BlockSpec((1, d), const), pl.BlockSpec((1, d), const),
            pl.BlockSpec(swgu.shape, const),
            pl.BlockSpec(swd.shape, const),
        ],
        out_specs=pl.BlockSpec((tc, d), lambda i: (i, 0)),
        out_shape=jax.ShapeDtypeStruct((n_rows, d), F32),
        scratch_shapes=[pltpu.VMEM((TOP_K, tc, nl, LANES), F32), pltpu.SemaphoreType.DMA(())],
        compiler_params=_cparams(("arbitrary",)),
        name="moe_combine",
    )(dest, gates_t, ys, x1, h2_slab, mod, ln_g.reshape(1, d), ln_b.reshape(1, d),
      swgu.astype(BF16), swd.astype(BF16))


def _moe(x1, h2_slab, eidx, gates, rank, counts, mod, ln_g, ln_b, wgu, wd, swgu, swd,
         n_rows, n_lat, seq, ctx_row, dn_alpha):
    bm = MOE_BM
    counts = counts.reshape(N_EXPERTS).astype(I32)
    pcounts = (counts + bm - 1) // bm * bm
    pends = jnp.cumsum(pcounts)
    pstarts = pends - pcounts
    dest = pstarts[eidx] + rank
    n_blk = (n_rows * TOP_K + N_EXPERTS * (bm - 1) + bm - 1) // bm
    blk_start = jnp.arange(n_blk, dtype=I32) * bm
    blk_e = jnp.minimum(jnp.sum((pends[None, :] <= blk_start[:, None]).astype(I32), axis=1),
                        N_EXPERTS - 1)
    n_used = (pends[-1] // bm).astype(I32).reshape(1)
    xs = _dispatch(dest, h2_slab, n_blk * bm)
    ys = _grouped_ffn(xs, blk_e, n_used, wgu.astype(BF16), wd.astype(BF16))
    return _combine(dest, gates.T, ys, x1, h2_slab, mod, ln_g, ln_b, swgu, swd,
                    n_rows, n_lat, seq, ctx_row, dn_alpha)


def kernel(x, c, ctx, c_ctx, ada_w, ada_b, ln_g, ln_b, attn_w_qkv, attn_q_norm, attn_k_norm, attn_w_o, conv_w_in, conv_taps, conv_w_out, router_w, router_bias, exp_w_gate_up, exp_w_down, shared_w_gate_up, shared_w_down):
    b, seq, d = x.shape
    cl = ctx.shape[1]
    depth = ada_w.shape[0]
    n_lat = b * seq
    n_ctx = b * cl
    dn_alpha = (2 * depth) ** 0.25
    assert depth == 2 and b < MOD_ROWS
    assert seq % ROW_TILE == 0 and n_ctx % ROW_TILE == 0 and seq % GRID_W == 0

    cond = jnp.zeros((MOD_ROWS, d), F32).at[:b].set(c).at[b].set(c_ctx)
    mod = _modulation(cond, ada_w, ada_b)
    xall = jnp.concatenate([x.reshape(n_lat, d), ctx.reshape(n_ctx, d)], axis=0)
    n_all = n_lat + n_ctx

    q, k, v = _qkv_project(xall, mod[0], attn_w_qkv[0], attn_q_norm[0], attn_k_norm[0], n_lat, seq, b)
    o = _attention(q, k, v, b, seq, cl, n_lat)
    x1, h2, eidx, gates, rank, counts = _post_mixer(
        o, xall, mod[0], ln_g[0, 0], ln_b[0, 0], attn_w_o[0], router_w[0], router_bias[0],
        n_all, n_lat, seq, b, dn_alpha)
    xall = _moe(x1, h2, eidx, gates, rank, counts, mod[0], ln_g[0, 1], ln_b[0, 1],
                exp_w_gate_up[0], exp_w_down[0], shared_w_gate_up[0], shared_w_down[0],
                n_all, n_lat, seq, b, dn_alpha)

    zin = _in_project(xall, mod[1], conv_w_in[0], n_lat, n_lat, seq, b)
    a = _short_conv(zin, conv_taps[0], b, seq, d)
    x1, h2, eidx, gates, rank, counts = _post_mixer(
        a, xall, mod[1], ln_g[1, 0], ln_b[1, 0], conv_w_out[0], router_w[1], router_bias[1],
        n_lat, n_lat, seq, b, dn_alpha)
    out = _moe(x1, h2, eidx, gates, rank, counts, mod[1], ln_g[1, 1], ln_b[1, 1],
               exp_w_gate_up[1], exp_w_down[1], shared_w_gate_up[1], shared_w_down[1],
               n_lat, n_lat, seq, b, dn_alpha)
    return out.reshape(b, seq, d)
```

```python
import functools

import jax
import jax.numpy as jnp
from jax import lax
from jax.experimental import pallas as pl
from jax.experimental.pallas import tpu as pltpu

F32 = jnp.float32
BF16 = jnp.bfloat16
I32 = jnp.int32

N_HEADS = 8
N_KV_HEADS = 2
HEAD_DIM = 128
KV_GROUP = N_HEADS // N_KV_HEADS
GRID_W = 64
ROPE_THETA = 10000.0
N_EXPERTS = 256
TOP_K = 8
N_GROUPS = 8
TOPK_GROUPS = 4
PER_GROUP = N_EXPERTS // N_GROUPS
ROUTED_SCALE = 2.5
LN_EPS = 1e-5
QK_EPS = 1e-6
N_MOD = 6
MOD_ROWS = 16

LANES = 128
SUBLANES = 8
VMEM_LIMIT = 56 * 1024 * 1024

ROW_TILE = 512
ATTN_TQ = 256
ATTN_TK = 512
MOE_BM = 128
DISPATCH_TOK = 256
CONV_LANES = 128

HIGHEST = lax.Precision.HIGHEST


def _cparams(sem):
    return pltpu.CompilerParams(dimension_semantics=sem, vmem_limit_bytes=VMEM_LIMIT)


def _silu(v):
    return v * jax.nn.sigmoid(v)


def _mod_kernel(c_ref, w_ref, b_ref, o_ref):
    s = _silu(c_ref[...])
    o_ref[...] = jnp.dot(s, w_ref[...], precision=HIGHEST, preferred_element_type=F32) + b_ref[...]


def _modulation(cond, ada_w, ada_b):
    depth, d, nd = ada_w.shape
    tn = 1536
    out = pl.pallas_call(
        _mod_kernel,
        grid=(depth, nd // tn),
        in_specs=[
            pl.BlockSpec((MOD_ROWS, d), lambda l, j: (0, 0)),
            pl.BlockSpec((None, d, tn), lambda l, j: (l, 0, j)),
            pl.BlockSpec((None, 1, tn), lambda l, j: (l, 0, j)),
        ],
        out_specs=pl.BlockSpec((None, MOD_ROWS, tn), lambda l, j: (l, 0, j)),
        out_shape=jax.ShapeDtypeStruct((depth, MOD_ROWS, nd), F32),
        compiler_params=_cparams(("arbitrary", "arbitrary")),
        name="adaln_modulation",
    )(cond, ada_w, ada_b.reshape(depth, 1, nd))
    return out.reshape(depth, MOD_ROWS * N_MOD, 1, d)


def _mod_spec(comp, tm, n_lat, seq, ctx_row, d):
    def index(i, *_):
        row0 = i * tm
        r = jnp.where(row0 < n_lat, row0 // seq, ctx_row)
        return (r * N_MOD + comp, 0, 0)

    return pl.BlockSpec((None, 1, d), index)


def _qkv_kernel(x_ref, shift_ref, scale_ref, w_ref, qg_ref, kg_ref, cos_ref, sin_ref,
                q_ref, k_ref, v_ref):
    h = (x_ref[...] * (1.0 + scale_ref[...]) + shift_ref[...]).astype(BF16)
    qkv = jnp.dot(h, w_ref[...], preferred_element_type=F32)
    cos = cos_ref[...]
    sin = sin_ref[...]
    hq = N_HEADS * HEAD_DIM
    kd = N_KV_HEADS * HEAD_DIM

    def norm_rope(t, g, post):
        t = t * lax.rsqrt(jnp.mean(t * t, axis=-1, keepdims=True) + QK_EPS) * g
        t = t * cos + pltpu.roll(t, HEAD_DIM // 2, axis=1) * sin
        return (t * post).astype(BF16)

    for hd in range(N_HEADS):
        sl = slice(hd * HEAD_DIM, (hd + 1) * HEAD_DIM)
        q_ref[:, sl] = norm_rope(qkv[:, sl], qg_ref[...], HEAD_DIM ** -0.5)
    for hd in range(N_KV_HEADS):
        sl = slice(hd * HEAD_DIM, (hd + 1) * HEAD_DIM)
        k_ref[:, sl] = norm_rope(qkv[:, hq + hd * HEAD_DIM: hq + (hd + 1) * HEAD_DIM], kg_ref[...], 1.0)
    v_ref[...] = qkv[:, hq + kd:].astype(BF16)


def _rope_tables(seq, tm):
    rows = seq // GRID_W
    row = jnp.repeat(jnp.arange(rows, dtype=F32), GRID_W)
    col = jnp.tile(jnp.arange(GRID_W, dtype=F32), rows)
    axis_dim = HEAD_DIM // 2
    freqs = ROPE_THETA ** (-jnp.arange(0, axis_dim, 2, dtype=F32) / axis_dim)
    ang = jnp.concatenate([row[:, None] * freqs, col[:, None] * freqs], axis=-1)
    cos = jnp.concatenate([jnp.cos(ang), jnp.cos(ang)], axis=-1)
    sin = jnp.concatenate([-jnp.sin(ang), jnp.sin(ang)], axis=-1)
    cos = jnp.concatenate([cos, jnp.ones((tm, HEAD_DIM), F32)], axis=0)
    sin = jnp.concatenate([sin, jnp.zeros((tm, HEAD_DIM), F32)], axis=0)
    return cos.reshape(seq // tm + 1, tm, HEAD_DIM), sin.reshape(seq // tm + 1, tm, HEAD_DIM)


def _qkv_project(xall, mod, w_qkv, q_g, k_g, n_lat, seq, ctx_row):
    t, d = xall.shape
    tm = ROW_TILE
    hq = N_HEADS * HEAD_DIM
    kd = N_KV_HEADS * HEAD_DIM
    perm = jnp.concatenate([jnp.arange(0, HEAD_DIM, 2), jnp.arange(1, HEAD_DIM, 2)])
    cols = jnp.concatenate([hd * HEAD_DIM + perm for hd in range(N_HEADS + N_KV_HEADS)]
                           + [jnp.arange(hq + kd, hq + 2 * kd)])
    w = w_qkv[:, cols].astype(BF16)
    cos, sin = _rope_tables(seq, tm)
    n_pos = seq // tm

    def pos_index(i):
        row0 = i * tm
        return (jnp.where(row0 < n_lat, (row0 % seq) // tm, n_pos), 0, 0)

    const = lambda i: (0, 0)
    return pl.pallas_call(
        _qkv_kernel,
        grid=(t // tm,),
        in_specs=[
            pl.BlockSpec((tm, d), lambda i: (i, 0)),
            _mod_spec(0, tm, n_lat, seq, ctx_row, d),
            _mod_spec(1, tm, n_lat, seq, ctx_row, d),
            pl.BlockSpec(w.shape, const),
            pl.BlockSpec((1, HEAD_DIM), const),
            pl.BlockSpec((1, HEAD_DIM), const),
            pl.BlockSpec((None, tm, HEAD_DIM), pos_index),
            pl.BlockSpec((None, tm, HEAD_DIM), pos_index),
        ],
        out_specs=[
            pl.BlockSpec((tm, hq), lambda i: (i, 0)),
            pl.BlockSpec((tm, kd), lambda i: (i, 0)),
            pl.BlockSpec((tm, kd), lambda i: (i, 0)),
        ],
        out_shape=[
            jax.ShapeDtypeStruct((t, hq), BF16),
            jax.ShapeDtypeStruct((t, kd), BF16),
            jax.ShapeDtypeStruct((t, kd), BF16),
        ],
        compiler_params=_cparams(("parallel",)),
        name="qkv_norm_rope",
    )(xall, mod, mod, w, q_g[perm].reshape(1, HEAD_DIM), k_g[perm].reshape(1, HEAD_DIM), cos, sin)


def _attn_kernel(*refs, n_lat_chunks, tk):
    if n_lat_chunks:
        q_ref, kc_ref, vc_ref, kl_ref, vl_ref, o_ref = refs
    else:
        q_ref, kc_ref, vc_ref, o_ref = refs
    tq = q_ref.shape[0]
    q = jnp.concatenate([q_ref[:, h * HEAD_DIM:(h + 1) * HEAD_DIM] for h in range(KV_GROUP)], axis=0)
    rows = KV_GROUP * tq

    def chunk(k, v, m, l, acc):
        s = lax.dot_general(q, k, (((1,), (1,)), ((), ())), preferred_element_type=F32)
        m_new = jnp.maximum(m, jnp.max(s, axis=-1, keepdims=True))
        p = jnp.exp(s - m_new)
        a = jnp.exp(m - m_new)
        l = a * l + jnp.sum(p, axis=-1, keepdims=True)
        acc = a * acc + jnp.dot(p.astype(BF16), v, preferred_element_type=F32)
        return m_new, l, acc

    m = jnp.full((rows, 1), -jnp.inf, F32)
    l = jnp.zeros((rows, 1), F32)
    acc = jnp.zeros((rows, HEAD_DIM), F32)
    m, l, acc = chunk(kc_ref[...], vc_ref[...], m, l, acc)
    for c in range(n_lat_chunks):
        m, l, acc = chunk(kl_ref[c * tk:(c + 1) * tk, :], vl_ref[c * tk:(c + 1) * tk, :], m, l, acc)
    o = (acc / l).astype(BF16)
    for h in range(KV_GROUP):
        o_ref[:, h * HEAD_DIM:(h + 1) * HEAD_DIM] = o[h * tq:(h + 1) * tq]


def _attention(q, k, v, b, seq, cl, n_lat):
    t = q.shape[0]
    gw = KV_GROUP * HEAD_DIM
    tq = min(ATTN_TQ, seq)
    tk = min(ATTN_TK, seq)
    nq = seq // tq
    ctx_blk0 = n_lat // cl
    hq = N_HEADS * HEAD_DIM

    ctx_kv = pl.BlockSpec((cl, HEAD_DIM), lambda bi, g, qi: (ctx_blk0 + bi, g))
    lat_kv = pl.BlockSpec((seq, HEAD_DIM), lambda bi, g, qi: (bi, g))
    lat_q = pl.BlockSpec((tq, gw), lambda bi, g, qi: (bi * nq + qi, g))
    o_lat = pl.pallas_call(
        functools.partial(_attn_kernel, n_lat_chunks=seq // tk, tk=tk),
        grid=(b, N_KV_HEADS, nq),
        in_specs=[lat_q, ctx_kv, ctx_kv, lat_kv, lat_kv],
        out_specs=lat_q,
        out_shape=jax.ShapeDtypeStruct((n_lat, hq), BF16),
        compiler_params=_cparams(("parallel", "parallel", "arbitrary")),
        name="attention_latent",
    )(q, k, v, k, v)

    ctx_q = pl.BlockSpec((cl, gw), lambda bi, g: (ctx_blk0 + bi, g))
    ctx_kv2 = pl.BlockSpec((cl, HEAD_DIM), lambda bi, g: (ctx_blk0 + bi, g))
    o_ctx = pl.pallas_call(
        functools.partial(_attn_kernel, n_lat_chunks=0, tk=tk),
        grid=(b, N_KV_HEADS),
        in_specs=[ctx_q, ctx_kv2, ctx_kv2],
        out_specs=pl.BlockSpec((cl, gw), lambda bi, g: (bi, g)),
        out_shape=jax.ShapeDtypeStruct((t - n_lat, hq), BF16),
        compiler_params=_cparams(("parallel", "parallel")),
        name="attention_context",
    )(q, k, v)
    return jnp.concatenate([o_lat, o_ctx], axis=0)


def _in_proj_kernel(x_ref, shift_ref, scale_ref, w_ref, o_ref):
    h = (x_ref[...] * (1.0 + scale_ref[...]) + shift_ref[...]).astype(BF16)
    o_ref[...] = jnp.dot(h, w_ref[...], preferred_element_type=F32)


def _in_project(xall, mod, w_in, n_rows, n_lat, seq, ctx_row):
    d = xall.shape[1]
    n_out = w_in.shape[1]
    tm = ROW_TILE
    return pl.pallas_call(
        _in_proj_kernel,
        grid=(n_rows // tm,),
        in_specs=[
            pl.BlockSpec((tm, d), lambda i: (i, 0)),
            _mod_spec(0, tm, n_lat, seq, ctx_row, d),
            _mod_spec(1, tm, n_lat, seq, ctx_row, d),
            pl.BlockSpec(w_in.shape, lambda i: (0, 0)),
        ],
        out_specs=pl.BlockSpec((tm, n_out), lambda i: (i, 0)),
        out_shape=jax.ShapeDtypeStruct((n_rows, n_out), F32),
        compiler_params=_cparams(("parallel",)),
        name="conv_in_proj",
    )(xall, mod, mod, w_in.astype(BF16))


def _conv_kernel(bg_ref, cg_ref, v_ref, taps_ref, o_ref):
    u = cg_ref[...] * v_ref[...]
    n = u.shape[0]
    pos = lax.broadcasted_iota(I32, u.shape, 0)
    prev = jnp.where(pos == 0, 0.0, pltpu.roll(u, 1, axis=0))
    nxt = jnp.where(pos == n - 1, 0.0, pltpu.roll(u, n - 1, axis=0))
    conv = prev * taps_ref[0:1, :] + u * taps_ref[1:2, :] + nxt * taps_ref[2:3, :]
    o_ref[...] = (bg_ref[...] * conv).astype(BF16)


def _short_conv(zin, taps, n_seqs, seq, d):
    tc = CONV_LANES
    nj = d // tc
    return pl.pallas_call(
        _conv_kernel,
        grid=(n_seqs, nj),
        in_specs=[
            pl.BlockSpec((seq, tc), lambda s, j: (s, j)),
            pl.BlockSpec((seq, tc), lambda s, j: (s, nj + j)),
            pl.BlockSpec((seq, tc), lambda s, j: (s, 2 * nj + j)),
            pl.BlockSpec((taps.shape[0], tc), lambda s, j: (0, j)),
        ],
        out_specs=pl.BlockSpec((seq, tc), lambda s, j: (s, j)),
        out_shape=jax.ShapeDtypeStruct((n_seqs * seq, d), BF16),
        compiler_params=_cparams(("parallel", "parallel")),
        name="short_conv",
    )(zin, zin, zin, taps)


def _layer_norm(z, g, b):
    mu = jnp.mean(z, axis=-1, keepdims=True)
    zc = z - mu
    var = jnp.mean(zc * zc, axis=-1, keepdims=True)
    return zc * lax.rsqrt(var + LN_EPS) * g + b


def _post_kernel(a_ref, x_ref, gate_ref, lng_ref, lnb_ref, shift_ref, scale_ref, w_ref,
                 rwt_ref, rb_ref, tri_ref,
                 x1_ref, h2_ref, eidx_ref, gates_ref, rank_ref, cnt_ref, *, dn_alpha):
    tm = a_ref.shape[0]
    d = x_ref.shape[1]
    y = jnp.dot(a_ref[...], w_ref[...], preferred_element_type=F32)
    x1 = _layer_norm(dn_alpha * x_ref[...] + gate_ref[...] * y, lng_ref[...], lnb_ref[...])
    x1_ref[...] = x1
    h2 = x1 * (1.0 + scale_ref[...]) + shift_ref[...]
    h2_ref[...] = h2

    logits = lax.dot_general(rwt_ref[...], h2, (((1,), (1,)), ((), ())),
                             precision=HIGHEST, preferred_element_type=F32)
    scores = jax.nn.sigmoid(logits)
    biased = scores + rb_ref[...]
    neg = -jnp.inf
    big = jnp.int32(1 << 30)
    row = lax.broadcasted_iota(I32, (N_EXPERTS, tm), 0)

    def argmax_rows(vals, idx):
        mx = jnp.max(vals, axis=0, keepdims=True)
        return mx, jnp.min(jnp.where(vals == mx, idx, big), axis=0, keepdims=True)

    gs = []
    grp_row = lax.broadcasted_iota(I32, (PER_GROUP, tm), 0)
    for g in range(N_GROUPS):
        bg = biased[g * PER_GROUP:(g + 1) * PER_GROUP]
        ig = grp_row + g * PER_GROUP
        m1, i1 = argmax_rows(bg, ig)
        m2 = jnp.max(jnp.where(ig == i1, neg, bg), axis=0, keepdims=True)
        gs.append(m1 + m2)
    gsc = jnp.concatenate(gs, axis=0)
    grow = lax.broadcasted_iota(I32, (N_GROUPS, tm), 0)
    gsel = jnp.zeros((N_GROUPS, tm), F32)
    for _ in range(TOPK_GROUPS):
        _, gi = argmax_rows(gsc, grow)
        hit = grow == gi
        gsel = jnp.where(hit, 1.0, gsel)
        gsc = jnp.where(hit, neg, gsc)
    cur = jnp.concatenate(
        [jnp.where(gsel[g:g + 1] > 0.0, biased[g * PER_GROUP:(g + 1) * PER_GROUP], neg)
         for g in range(N_GROUPS)], axis=0)

    onehot = jnp.zeros((N_EXPERTS, tm), F32)
    idxs, gvals = [], []
    for _ in range(TOP_K):
        _, ei = argmax_rows(cur, row)
        hit = row == ei
        gvals.append(jnp.sum(jnp.where(hit, scores, 0.0), axis=0, keepdims=True))
        idxs.append(ei)
        onehot = jnp.where(hit, 1.0, onehot)
        cur = jnp.where(hit, neg, cur)
    gv = jnp.concatenate(gvals, axis=0)
    gates_ref[...] = gv / jnp.sum(gv, axis=0, keepdims=True) * ROUTED_SCALE
    eidx_ref[...] = jnp.concatenate(idxs, axis=0)

    @pl.when(pl.program_id(0) == 0)
    def _():
        cnt_ref[...] = jnp.zeros_like(cnt_ref)

    prefix = jnp.dot(onehot.astype(BF16), tri_ref[...], preferred_element_type=F32)
    pos = prefix + cnt_ref[...]
    rank_ref[...] = jnp.concatenate(
        [jnp.sum(jnp.where(row == ei, pos, 0.0), axis=0, keepdims=True) for ei in idxs],
        axis=0).astype(I32)
    cnt_ref[...] = cnt_ref[...] + jnp.sum(onehot, axis=1, keepdims=True)


def _post_mixer(a, xall, mod, ln_g, ln_b, w, router_w, router_bias, n_rows, n_lat, seq, ctx_row,
                dn_alpha):
    d = xall.shape[1]
    tm = ROW_TILE
    tri =(lax.broadcasted_iota(I32, (tm, tm), 0) < lax.broadcasted_iota(I32, (tm, tm), 1)).astype(BF16)
    const = lambda i: (0, 0)
    row_blk = pl.BlockSpec((tm, d), lambda i: (i, 0))
    k_blk = pl.BlockSpec((TOP_K, tm), lambda i: (0, i))
    return pl.pallas_call(
        functools.partial(_post_kernel, dn_alpha=dn_alpha),
        grid=(n_rows // tm,),
        in_specs=[
            row_blk, row_blk,
            _mod_spec(2, tm, n_lat, seq, ctx_row, d),
            pl.BlockSpec((1, d), const), pl.BlockSpec((1, d), const),
            _mod_spec(3, tm, n_lat, seq, ctx_row, d),
            _mod_spec(4, tm, n_lat, seq, ctx_row, d),
            pl.BlockSpec(w.shape, const),
            pl.BlockSpec((N_EXPERTS, d), const),
            pl.BlockSpec((N_EXPERTS, 1), const),
            pl.BlockSpec((tm, tm), const),
        ],
        out_specs=[
            row_blk, row_blk,
            k_blk, k_blk, k_blk,
            pl.BlockSpec((N_EXPERTS, 1), const),
        ],
        out_shape=[
            jax.ShapeDtypeStruct((n_rows, d), F32),
            jax.ShapeDtypeStruct((n_rows, d), F32),
            jax.ShapeDtypeStruct((TOP_K, n_rows), I32),
            jax.ShapeDtypeStruct((TOP_K, n_rows), F32),
            jax.ShapeDtypeStruct((TOP_K, n_rows), I32),
            jax.ShapeDtypeStruct((N_EXPERTS, 1), F32),
        ],
        compiler_params=_cparams(("arbitrary",)),
        name="post_mixer_router",
    )(a, xall, mod, ln_g.reshape(1, d), ln_b.reshape(1, d), mod, mod, w.astype(BF16),
      router_w.T, router_bias.reshape(N_EXPERTS, 1), tri)


def _slots_kernel(pstart_ref, eidx_ref, rank_ref, dest_ref):
    e = eidx_ref[...]

    def pick(i, acc):
        return jnp.where(e == i, pstart_ref[i], acc)

    dest_ref[...] = lax.fori_loop(0, N_EXPERTS, pick, jnp.zeros_like(e)) + rank_ref[...]


def _slots(pstarts, eidx, rank):
    n_tok = eidx.shape[1]
    tn = 2048 if n_tok % 2048 == 0 else ROW_TILE
    blk = pl.BlockSpec((TOP_K, tn), lambda i, ps: (0, i))
    return pl.pallas_call(
        _slots_kernel,
        grid_spec=pltpu.PrefetchScalarGridSpec(
            num_scalar_prefetch=1, grid=(n_tok // tn,), in_specs=[blk, blk], out_specs=blk),
        out_shape=jax.ShapeDtypeStruct((TOP_K, n_tok), I32),
        compiler_params=_cparams(("arbitrary",)),
        name="moe_slots",
    )(pstarts, eidx, rank)


def _dispatch_kernel(pends_ref, pcounts_ref, dest_ref, h_ref, xs_hbm, zbuf, sem):
    td = dest_ref.shape[1]
    bm = zbuf.shape[0]

    @pl.when(pl.program_id(0) == 0)
    def _():
        zbuf[...] = jnp.zeros_like(zbuf)

        def tail(e):
            return xs_hbm.at[pl.ds(pl.multiple_of(pends_ref[e] - bm, bm), bm)]

        def zstart(e, carry):
            @pl.when(pcounts_ref[e] > 0)
            def _():
                pltpu.make_async_copy(zbuf, tail(e), sem).start()
            return carry

        def zwait(e, carry):
            @pl.when(pcounts_ref[e] > 0)
            def _():
                pltpu.make_async_copy(zbuf, tail(e), sem).wait()
            return carry

        lax.fori_loop(0, N_EXPERTS, zstart, 0)
        lax.fori_loop(0, N_EXPERTS, zwait, 0)

        def unused(j):
            return xs_hbm.at[pl.ds(pl.multiple_of(j * bm, bm), bm)]

        def ustart(j, carry):
            pltpu.make_async_copy(zbuf, unused(j), sem).start()
            return carry

        def uwait(j, carry):
            pltpu.make_async_copy(zbuf, unused(j), sem).wait()
            return carry

        first_unused = pends_ref[N_EXPERTS - 1] // bm
        lax.fori_loop(first_unused, xs_hbm.shape[0] // bm, ustart, 0)
        lax.fori_loop(first_unused, xs_hbm.shape[0] // bm, uwait, 0)

    def issue(t, carry):
        for k in range(TOP_K):
            pltpu.make_async_copy(h_ref.at[pl.ds(t, 1)], xs_hbm.at[pl.ds(dest_ref[k, t], 1)], sem).start()
        return carry

    lax.fori_loop(0, td, issue, 0)
    pltpu.make_async_copy(xs_hbm.at[pl.ds(0, TOP_K * td)], xs_hbm.at[pl.ds(0, TOP_K * td)], sem).wait()


def _dispatch(dest, h2, pends, pcounts, n_slots):
    n_tok, d = h2.shape
    td = DISPATCH_TOK
    grid_spec = pltpu.PrefetchScalarGridSpec(
        num_scalar_prefetch=2,
        grid=(n_tok // td,),
        in_specs=[
            pl.BlockSpec((TOP_K, td), lambda i, pe, pc: (0, i), memory_space=pltpu.SMEM),
            pl.BlockSpec((td, d), lambda i, pe, pc: (i, 0)),
        ],
        out_specs=pl.BlockSpec(memory_space=pl.ANY),
        scratch_shapes=[pltpu.VMEM((MOE_BM, d), F32), pltpu.SemaphoreType.DMA(())],
    )
    return pl.pallas_call(
        _dispatch_kernel,
        grid_spec=grid_spec,
        out_shape=jax.ShapeDtypeStruct((n_slots, d), F32),
        compiler_params=_cparams(("arbitrary",)),
        name="moe_dispatch",
    )(pends, pcounts, dest, h2)


def _gmm_kernel(blk_e_ref, n_used_ref, xs_ref, wgu_ref, wd_ref, ys_ref, wgu_bf, wd_bf):
    i = pl.program_id(0)
    ff = wd_ref.shape[0]

    @pl.when((i == 0) | (blk_e_ref[i] != blk_e_ref[jnp.maximum(i - 1, 0)]))
    def _():
        wgu_bf[...] = wgu_ref[...].astype(BF16)
        wd_bf[...] = wd_ref[...].astype(BF16)

    @pl.when(i < n_used_ref[0])
    def _():
        gu = jnp.dot(xs_ref[...].astype(BF16), wgu_bf[...], preferred_element_type=F32)
        act = (_silu(gu[:, :ff]) * gu[:, ff:]).astype(BF16)
        ys_ref[...] = jnp.dot(act, wd_bf[...], preferred_element_type=F32)

    @pl.when(i >= n_used_ref[0])
    def _():
        ys_ref[...] = jnp.zeros_like(ys_ref)


def _grouped_ffn(xs, blk_e, n_used, wgu, wd, layer):
    n_slots, d = xs.shape
    bm = MOE_BM
    ff2 = wgu.shape[3]
    ff = wd.shape[2]
    grid_spec = pltpu.PrefetchScalarGridSpec(
        num_scalar_prefetch=2,
        grid=(n_slots // bm,),
        in_specs=[
            pl.BlockSpec((bm, d), lambda i, be, nu: (jnp.minimum(i, nu[0] - 1), 0)),
            pl.BlockSpec((None, None, d, ff2), lambda i, be, nu: (layer, be[i], 0, 0)),
            pl.BlockSpec((None, None, ff, d), lambda i, be, nu: (layer, be[i], 0, 0)),
        ],
        out_specs=pl.BlockSpec((bm, d), lambda i, be, nu: (i, 0)),
        scratch_shapes=[pltpu.VMEM((d, ff2), BF16), pltpu.VMEM((ff, d), BF16)],
    )
    return pl.pallas_call(
        _gmm_kernel,
        grid_spec=grid_spec,
        out_shape=jax.ShapeDtypeStruct((n_slots, d), F32),
        compiler_params=_cparams(("arbitrary",)),
        name="moe_grouped_ffn",
    )(blk_e, n_used, xs, wgu, wd)


def _combine_kernel(dest_ref, gates_ref, ys_hbm, x1_ref, h2_ref, gate_ref, lng_ref, lnb_ref,
                    swgu_ref, swd_ref, o_ref, buf, sem, *, dn_alpha):
    tc = x1_ref.shape[0]
    ff = swd_ref.shape[0]

    def issue(t, carry):
        for k in range(TOP_K):
            pltpu.make_async_copy(ys_hbm.at[pl.ds(dest_ref[k, t], 1)], buf.at[k, pl.ds(t, 1)], sem).start()
        return carry

    lax.fori_loop(0, tc, issue, 0)

    gu = jnp.dot(h2_ref[...].astype(BF16), swgu_ref[...], preferred_element_type=F32)
    act = (_silu(gu[:, :ff]) * gu[:, ff:]).astype(BF16)
    y = jnp.dot(act, swd_ref[...], preferred_element_type=F32)

    for k in range(TOP_K):
        pltpu.make_async_copy(ys_hbm.at[pl.ds(0, tc)], buf.at[k], sem).wait()
    for k in range(TOP_K):
        y = y + gates_ref[:, k:k + 1] * buf[k]
    o_ref[...] = _layer_norm(dn_alpha * x1_ref[...] + gate_ref[...] * y, lng_ref[...], lnb_ref[...])


def _combine(dest, gates_t, ys, x1, h2, mod, ln_g, ln_b, swgu, swd, n_rows, n_lat, seq, ctx_row,
             dn_alpha):
    d = x1.shape[1]
    tc = DISPATCH_TOK
    const = lambda i: (0, 0)
    return pl.pallas_call(
        functools.partial(_combine_kernel, dn_alpha=dn_alpha),
        grid=(n_rows // tc,),
        in_specs=[
            pl.BlockSpec((TOP_K, tc), lambda i: (0, i), memory_space=pltpu.SMEM),
            pl.BlockSpec((tc, TOP_K), lambda i: (i, 0)),
            pl.BlockSpec(memory_space=pl.ANY),
            pl.BlockSpec((tc, d), lambda i: (i, 0)),
            pl.BlockSpec((tc, d), lambda i: (i, 0)),
            _mod_spec(5, tc, n_lat, seq, ctx_row, d),
            pl.BlockSpec((1, d), const), pl.BlockSpec((1, d), const),
            pl.BlockSpec(swgu.shape, const),
            pl.BlockSpec(swd.shape, const),
        ],
        out_specs=pl.BlockSpec((tc, d), lambda i: (i, 0)),
        out_shape=jax.ShapeDtypeStruct((n_rows, d), F32),
        scratch_shapes=[pltpu.VMEM((TOP_K, tc, d), F32), pltpu.SemaphoreType.DMA(())],
        compiler_params=_cparams(("arbitrary",)),
        name="moe_combine",
    )(dest, gates_t, ys, x1, h2, mod, ln_g.reshape(1, d), ln_b.reshape(1, d),
      swgu.astype(BF16), swd.astype(BF16))


def _moe(x1, h2, eidx, gates, rank, counts, mod, ln_g, ln_b, wgu, wd, layer, swgu, swd,
         n_rows, n_lat, seq, ctx_row, dn_alpha):
    bm = MOE_BM
    counts = counts.reshape(N_EXPERTS).astype(I32)
    pcounts = (counts + bm - 1) // bm * bm
    pends = jnp.cumsum(pcounts)
    pstarts = pends - pcounts
    dest = _slots(pstarts, eidx, rank)
    n_blk = (n_rows * TOP_K + N_EXPERTS * (bm - 1) + bm - 1) // bm
    blk_start = jnp.arange(n_blk, dtype=I32) * bm
    blk_e = jnp.minimum(jnp.sum((pends[None, :] <= blk_start[:, None]).astype(I32), axis=1),
                        N_EXPERTS - 1)
    n_used = (pends[-1] // bm).astype(I32).reshape(1)
    xs = _dispatch(dest, h2, pends, pcounts, n_blk * bm)
    ys = _grouped_ffn(xs, blk_e, n_used, wgu, wd, layer)
    return _combine(dest, gates.T, ys, x1, h2, mod, ln_g, ln_b, swgu, swd,
                    n_rows, n_lat, seq, ctx_row, dn_alpha)


def kernel(x, c, ctx, c_ctx, ada_w, ada_b, ln_g, ln_b, attn_w_qkv, attn_q_norm, attn_k_norm, attn_w_o, conv_w_in, conv_taps, conv_w_out, router_w, router_bias, exp_w_gate_up, exp_w_down, shared_w_gate_up, shared_w_down):
    b, seq, d = x.shape
    cl = ctx.shape[1]
    depth = ada_w.shape[0]
    n_lat = b * seq
    n_ctx = b * cl
    dn_alpha = (2 * depth) ** 0.25
    assert depth == 2 and b < MOD_ROWS
    assert seq % ROW_TILE == 0 and n_ctx % ROW_TILE == 0 and seq % GRID_W == 0

    cond = jnp.zeros((MOD_ROWS, d), F32).at[:b].set(c).at[b].set(c_ctx)
    mod = _modulation(cond, ada_w, ada_b)
    xall = jnp.concatenate([x.reshape(n_lat, d), ctx.reshape(n_ctx, d)], axis=0)
    n_all = n_lat + n_ctx

    q, k, v = _qkv_project(xall, mod[0], attn_w_qkv[0], attn_q_norm[0], attn_k_norm[0], n_lat, seq, b)
    o = _attention(q, k, v, b, seq, cl, n_lat)
    x1, h2, eidx, gates, rank, counts = _post_mixer(
        o, xall, mod[0], ln_g[0, 0], ln_b[0, 0], attn_w_o[0], router_w[0], router_bias[0],
        n_all, n_lat, seq, b, dn_alpha)
    xall = _moe(x1, h2, eidx, gates, rank, counts, mod[0], ln_g[0, 1], ln_b[0, 1],
                exp_w_gate_up, exp_w_down, 0, shared_w_gate_up[0], shared_w_down[0],
                n_all, n_lat, seq, b, dn_alpha)

    zin = _in_project(xall, mod[1], conv_w_in[0], n_lat, n_lat, seq, b)
    a = _short_conv(zin, conv_taps[0], b, seq, d)
    x1, h2, eidx, gates, rank, counts = _post_mixer(
        a, xall, mod[1], ln_g[1, 0], ln_b[1, 0], conv_w_out[0], router_w[1], router_bias[1],
        n_lat, n_lat, seq, b, dn_alpha)
    out = _moe(x1, h2, eidx, gates, rank, counts, mod[1], ln_g[1, 1], ln_b[1, 1],
               exp_w_gate_up, exp_w_down, 1, shared_w_gate_up[1], shared_w_down[1],
               n_lat, n_lat, seq, b, dn_alpha)
    return out.reshape(b, seq, d)
```

```python
import functools

import jax
import jax.numpy as jnp
from jax import lax
from jax.experimental import pallas as pl
from jax.experimental.pallas import tpu as pltpu

F32 = jnp.float32
BF16 = jnp.bfloat16
I32 = jnp.int32

N_HEADS = 8
N_KV_HEADS = 2
HEAD_DIM = 128
KV_GROUP = N_HEADS // N_KV_HEADS
GRID_W = 64
ROPE_THETA = 10000.0
N_EXPERTS = 256
TOP_K = 8
N_GROUPS = 8
TOPK_GROUPS = 4
PER_GROUP = N_EXPERTS // N_GROUPS
ROUTED_SCALE = 2.5
LN_EPS = 1e-5
QK_EPS = 1e-6
N_MOD = 6
MOD_ROWS = 16

LANES = 128
SUBLANES = 8
VMEM_LIMIT = 56 * 1024 * 1024

ROW_TILE = 512
ATTN_TQ = 256
ATTN_TK = 512
MOE_BM = 256
DISPATCH_TOK = 256
CONV_LANES = 128

HIGHEST = lax.Precision.HIGHEST


def _cparams(sem):
    return pltpu.CompilerParams(dimension_semantics=sem, vmem_limit_bytes=VMEM_LIMIT)


def _silu(v):
    return v * jax.nn.sigmoid(v)


def _mod_kernel(c_ref, w_ref, b_ref, o_ref):
    s = _silu(c_ref[...])
    o_ref[...] = jnp.dot(s, w_ref[...], precision=HIGHEST, preferred_element_type=F32) + b_ref[...]


def _modulation(cond, ada_w, ada_b):
    depth, d, nd = ada_w.shape
    tn = 1536
    out = pl.pallas_call(
        _mod_kernel,
        grid=(depth, nd // tn),
        in_specs=[
            pl.BlockSpec((MOD_ROWS, d), lambda l, j: (0, 0)),
            pl.BlockSpec((None, d, tn), lambda l, j: (l, 0, j)),
            pl.BlockSpec((None, 1, tn), lambda l, j: (l, 0, j)),
        ],
        out_specs=pl.BlockSpec((None, MOD_ROWS, tn), lambda l, j: (l, 0, j)),
        out_shape=jax.ShapeDtypeStruct((depth, MOD_ROWS, nd), F32),
        compiler_params=_cparams(("arbitrary", "arbitrary")),
        name="adaln_modulation",
    )(cond, ada_w, ada_b.reshape(depth, 1, nd))
    return out.reshape(depth, MOD_ROWS * N_MOD, 1, d)


def _mod_spec(comp, tm, n_lat, seq, ctx_row, d):
    def index(i, *_):
        row0 = i * tm
        r = jnp.where(row0 < n_lat, row0 // seq, ctx_row)
        return (r * N_MOD + comp, 0, 0)

    return pl.BlockSpec((None, 1, d), index)


def _qkv_kernel(x_ref, shift_ref, scale_ref, w_ref, qg_ref, kg_ref, cos_ref, sin_ref,
                q_ref, k_ref, v_ref):
    h = (x_ref[...] * (1.0 + scale_ref[...]) + shift_ref[...]).astype(BF16)
    qkv = jnp.dot(h, w_ref[...], preferred_element_type=F32)
    cos = cos_ref[...]
    sin = sin_ref[...]
    hq = N_HEADS * HEAD_DIM
    kd = N_KV_HEADS * HEAD_DIM

    def norm_rope(t, g, post):
        t = t * lax.rsqrt(jnp.mean(t * t, axis=-1, keepdims=True) + QK_EPS) * g
        t = t * cos + pltpu.roll(t, HEAD_DIM // 2, axis=1) * sin
        return (t * post).astype(BF16)

    for hd in range(N_HEADS):
        sl = slice(hd * HEAD_DIM, (hd + 1) * HEAD_DIM)
        q_ref[:, sl] = norm_rope(qkv[:, sl], qg_ref[...], HEAD_DIM ** -0.5)
    for hd in range(N_KV_HEADS):
        sl = slice(hd * HEAD_DIM, (hd + 1) * HEAD_DIM)
        k_ref[:, sl] = norm_rope(qkv[:, hq + hd * HEAD_DIM: hq + (hd + 1) * HEAD_DIM], kg_ref[...], 1.0)
    v_ref[...] = qkv[:, hq + kd:].astype(BF16)


def _rope_tables(seq, tm):
    rows = seq // GRID_W
    row = jnp.repeat(jnp.arange(rows, dtype=F32), GRID_W)
    col = jnp.tile(jnp.arange(GRID_W, dtype=F32), rows)
    axis_dim = HEAD_DIM // 2
    freqs = ROPE_THETA ** (-jnp.arange(0, axis_dim, 2, dtype=F32) / axis_dim)
    ang = jnp.concatenate([row[:, None] * freqs, col[:, None] * freqs], axis=-1)
    cos = jnp.concatenate([jnp.cos(ang), jnp.cos(ang)], axis=-1)
    sin = jnp.concatenate([-jnp.sin(ang), jnp.sin(ang)], axis=-1)
    cos = jnp.concatenate([cos, jnp.ones((tm, HEAD_DIM), F32)], axis=0)
    sin = jnp.concatenate([sin, jnp.zeros((tm, HEAD_DIM), F32)], axis=0)
    return cos.reshape(seq // tm + 1, tm, HEAD_DIM), sin.reshape(seq // tm + 1, tm, HEAD_DIM)


def _qkv_project(xall, mod, w_qkv, q_g, k_g, n_lat, seq, ctx_row):
    t, d = xall.shape
    tm = ROW_TILE
    hq = N_HEADS * HEAD_DIM
    kd = N_KV_HEADS * HEAD_DIM
    perm = jnp.concatenate([jnp.arange(0, HEAD_DIM, 2), jnp.arange(1, HEAD_DIM, 2)])
    cols = jnp.concatenate([hd * HEAD_DIM + perm for hd in range(N_HEADS + N_KV_HEADS)]
                           + [jnp.arange(hq + kd, hq + 2 * kd)])
    w = w_qkv[:, cols].astype(BF16)
    cos, sin = _rope_tables(seq, tm)
    n_pos = seq // tm

    def pos_index(i):
        row0 = i * tm
        return (jnp.where(row0 < n_lat, (row0 % seq) // tm, n_pos), 0, 0)

    const = lambda i: (0, 0)
    return pl.pallas_call(
        _qkv_kernel,
        grid=(t // tm,),
        in_specs=[
            pl.BlockSpec((tm, d), lambda i: (i, 0)),
            _mod_spec(0, tm, n_lat, seq, ctx_row, d),
            _mod_spec(1, tm, n_lat, seq, ctx_row, d),
            pl.BlockSpec(w.shape, const),
            pl.BlockSpec((1, HEAD_DIM), const),
            pl.BlockSpec((1, HEAD_DIM), const),
            pl.BlockSpec((None, tm, HEAD_DIM), pos_index),
            pl.BlockSpec((None, tm, HEAD_DIM), pos_index),
        ],
        out_specs=[
            pl.BlockSpec((tm, hq), lambda i: (i, 0)),
            pl.BlockSpec((tm, kd), lambda i: (i, 0)),
            pl.BlockSpec((tm, kd), lambda i: (i, 0)),
        ],
        out_shape=[
            jax.ShapeDtypeStruct((t, hq), BF16),
            jax.ShapeDtypeStruct((t, kd), BF16),
            jax.ShapeDtypeStruct((t, kd), BF16),
        ],
        compiler_params=_cparams(("parallel",)),
        name="qkv_norm_rope",
    )(xall, mod, mod, w, q_g[perm].reshape(1, HEAD_DIM), k_g[perm].reshape(1, HEAD_DIM), cos, sin)


def _attn_kernel(*refs, n_lat_chunks, tk):
    if n_lat_chunks:
        q_ref, kc_ref, vc_ref, kl_ref, vl_ref, o_ref = refs
    else:
        q_ref, kc_ref, vc_ref, o_ref = refs
    tq = q_ref.shape[0]
    q = jnp.concatenate([q_ref[:, h * HEAD_DIM:(h + 1) * HEAD_DIM] for h in range(KV_GROUP)], axis=0)
    rows = KV_GROUP * tq

    def chunk(k, v, m, l, acc):
        s = lax.dot_general(q, k, (((1,), (1,)), ((), ())), preferred_element_type=F32)
        m_new = jnp.maximum(m, jnp.max(s, axis=-1, keepdims=True))
        p = jnp.exp(s - m_new)
        a = jnp.exp(m - m_new)
        l = a * l + jnp.sum(p, axis=-1, keepdims=True)
        acc = a * acc + jnp.dot(p.astype(BF16), v, preferred_element_type=F32)
        return m_new, l, acc

    m = jnp.full((rows, 1), -jnp.inf, F32)
    l = jnp.zeros((rows, 1), F32)
    acc = jnp.zeros((rows, HEAD_DIM), F32)
    m, l, acc = chunk(kc_ref[...], vc_ref[...], m, l, acc)
    for c in range(n_lat_chunks):
        m, l, acc = chunk(kl_ref[c * tk:(c + 1) * tk, :], vl_ref[c * tk:(c + 1) * tk, :], m, l, acc)
    o = (acc / l).astype(BF16)
    for h in range(KV_GROUP):
        o_ref[:, h * HEAD_DIM:(h + 1) * HEAD_DIM] = o[h * tq:(h + 1) * tq]


def _attention(q, k, v, b, seq, cl, n_lat):
    t = q.shape[0]
    gw = KV_GROUP * HEAD_DIM
    tq = min(ATTN_TQ, seq)
    tk = min(ATTN_TK, seq)
    nq = seq // tq
    ctx_blk0 = n_lat // cl
    hq = N_HEADS * HEAD_DIM

    ctx_kv = pl.BlockSpec((cl, HEAD_DIM), lambda bi, g, qi: (ctx_blk0 + bi, g))
    lat_kv = pl.BlockSpec((seq, HEAD_DIM), lambda bi, g, qi: (bi, g))
    lat_q = pl.BlockSpec((tq, gw), lambda bi, g, qi: (bi * nq + qi, g))
    o_lat = pl.pallas_call(
        functools.partial(_attn_kernel, n_lat_chunks=seq // tk, tk=tk),
        grid=(b, N_KV_HEADS, nq),
        in_specs=[lat_q, ctx_kv, ctx_kv, lat_kv, lat_kv],
        out_specs=lat_q,
        out_shape=jax.ShapeDtypeStruct((n_lat, hq), BF16),
        compiler_params=_cparams(("parallel", "parallel", "arbitrary")),
        name="attention_latent",
    )(q, k, v, k, v)

    ctx_q = pl.BlockSpec((cl, gw), lambda bi, g: (ctx_blk0 + bi, g))
    ctx_kv2 = pl.BlockSpec((cl, HEAD_DIM), lambda bi, g: (ctx_blk0 + bi, g))
    o_ctx = pl.pallas_call(
        functools.partial(_attn_kernel, n_lat_chunks=0, tk=tk),
        grid=(b, N_KV_HEADS),
        in_specs=[ctx_q, ctx_kv2, ctx_kv2],
        out_specs=pl.BlockSpec((cl, gw), lambda bi, g: (bi, g)),
        out_shape=jax.ShapeDtypeStruct((t - n_lat, hq), BF16),
        compiler_params=_cparams(("parallel", "parallel")),
        name="attention_context",
    )(q, k, v)
    return jnp.concatenate([o_lat, o_ctx], axis=0)


def _in_proj_kernel(x_ref, shift_ref, scale_ref, w_ref, o_ref):
    h = (x_ref[...] * (1.0 + scale_ref[...]) + shift_ref[...]).astype(BF16)
    o_ref[...] = jnp.dot(h, w_ref[...], preferred_element_type=F32)


def _in_project(xall, mod, w_in, n_rows, n_lat, seq, ctx_row):
    d = xall.shape[1]
    n_out = w_in.shape[1]
    tm = ROW_TILE
    return pl.pallas_call(
        _in_proj_kernel,
        grid=(n_rows // tm,),
        in_specs=[
            pl.BlockSpec((tm, d), lambda i: (i, 0)),
            _mod_spec(0, tm, n_lat, seq, ctx_row, d),
            _mod_spec(1, tm, n_lat, seq, ctx_row, d),
            pl.BlockSpec(w_in.shape, lambda i: (0, 0)),
        ],
        out_specs=pl.BlockSpec((tm, n_out), lambda i: (i, 0)),
        out_shape=jax.ShapeDtypeStruct((n_rows, n_out), F32),
        compiler_params=_cparams(("parallel",)),
        name="conv_in_proj",
    )(xall, mod, mod, w_in.astype(BF16))


def _conv_kernel(bg_ref, cg_ref, v_ref, taps_ref, o_ref):
    u = cg_ref[...] * v_ref[...]
    n = u.shape[0]
    pos = lax.broadcasted_iota(I32, u.shape, 0)
    prev = jnp.where(pos == 0, 0.0, pltpu.roll(u, 1, axis=0))
    nxt = jnp.where(pos == n - 1, 0.0, pltpu.roll(u, n - 1, axis=0))
    conv = prev * taps_ref[0:1, :] + u * taps_ref[1:2, :] + nxt * taps_ref[2:3, :]
    o_ref[...] = (bg_ref[...] * conv).astype(BF16)


def _short_conv(zin, taps, n_seqs, seq, d):
    tc = CONV_LANES
    nj = d // tc
    return pl.pallas_call(
        _conv_kernel,
        grid=(n_seqs, nj),
        in_specs=[
            pl.BlockSpec((seq, tc), lambda s, j: (s, j)),
            pl.BlockSpec((seq, tc), lambda s, j: (s, nj + j)),
            pl.BlockSpec((seq, tc), lambda s, j: (s, 2 * nj + j)),
            pl.BlockSpec((taps.shape[0], tc), lambda s, j: (0, j)),
        ],
        out_specs=pl.BlockSpec((seq, tc), lambda s, j: (s, j)),
        out_shape=jax.ShapeDtypeStruct((n_seqs * seq, d), BF16),
        compiler_params=_cparams(("parallel", "parallel")),
        name="short_conv",
    )(zin, zin, zin, taps)


def _layer_norm(z, g, b):
    mu = jnp.mean(z, axis=-1, keepdims=True)
    zc = z - mu
    var = jnp.mean(zc * zc, axis=-1, keepdims=True)
    return zc * lax.rsqrt(var + LN_EPS) * g + b


def _post_kernel(a_ref, x_ref, gate_ref, lng_ref, lnb_ref, shift_ref, scale_ref, w_ref,
                 rwt_ref, rb_ref, tri_ref,
                 x1_ref, h2_ref, eidx_ref, gates_ref, rank_ref, cnt_ref, *, dn_alpha):
    tm = a_ref.shape[0]
    d = x_ref.shape[1]
    y = jnp.dot(a_ref[...], w_ref[...], preferred_element_type=F32)
    x1 = _layer_norm(dn_alpha * x_ref[...] + gate_ref[...] * y, lng_ref[...], lnb_ref[...])
    x1_ref[...] = x1
    h2 = x1 * (1.0 + scale_ref[...]) + shift_ref[...]
    h2_ref[...] = h2

    logits = lax.dot_general(rwt_ref[...], h2, (((1,), (1,)), ((), ())),
                             precision=HIGHEST, preferred_element_type=F32)
    scores = jax.nn.sigmoid(logits)
    biased = scores + rb_ref[...]
    neg = -jnp.inf
    big = jnp.int32(1 << 30)
    row = lax.broadcasted_iota(I32, (N_EXPERTS, tm), 0)

    def argmax_rows(vals, idx):
        mx = jnp.max(vals, axis=0, keepdims=True)
        return mx, jnp.min(jnp.where(vals == mx, idx, big), axis=0, keepdims=True)

    gs = []
    grp_row = lax.broadcasted_iota(I32, (PER_GROUP, tm), 0)
    for g in range(N_GROUPS):
        bg = biased[g * PER_GROUP:(g + 1) * PER_GROUP]
        ig = grp_row + g * PER_GROUP
        m1, i1 = argmax_rows(bg, ig)
        m2 = jnp.max(jnp.where(ig == i1, neg, bg), axis=0, keepdims=True)
        gs.append(m1 + m2)
    gsc = jnp.concatenate(gs, axis=0)
    grow = lax.broadcasted_iota(I32, (N_GROUPS, tm), 0)
    gsel = jnp.zeros((N_GROUPS, tm), F32)
    for _ in range(TOPK_GROUPS):
        _, gi = argmax_rows(gsc, grow)
        hit = grow == gi
        gsel = jnp.where(hit, 1.0, gsel)
        gsc = jnp.where(hit, neg, gsc)
    cur = jnp.concatenate(
        [jnp.where(gsel[g:g + 1] > 0.0, biased[g * PER_GROUP:(g + 1) * PER_GROUP], neg)
         for g in range(N_GROUPS)], axis=0)

    onehot = jnp.zeros((N_EXPERTS, tm), F32)
    idxs, gvals = [], []
    for _ in range(TOP_K):
        _, ei = argmax_rows(cur, row)
        hit = row == ei
        gvals.append(jnp.sum(jnp.where(hit, scores, 0.0), axis=0, keepdims=True))
        idxs.append(ei)
        onehot = jnp.where(hit, 1.0, onehot)
        cur = jnp.where(hit, neg, cur)
    gv = jnp.concatenate(gvals, axis=0)
    gates_ref[...] = gv / jnp.sum(gv, axis=0, keepdims=True) * ROUTED_SCALE
    eidx_ref[...] = jnp.concatenate(idxs, axis=0)

    @pl.when(pl.program_id(0) == 0)
    def _():
        cnt_ref[...] = jnp.zeros_like(cnt_ref)

    prefix = jnp.dot(onehot.astype(BF16), tri_ref[...], preferred_element_type=F32)
    pos = prefix + cnt_ref[...]
    rank_ref[...] = jnp.concatenate(
        [jnp.sum(jnp.where(row == ei, pos, 0.0), axis=0, keepdims=True) for ei in idxs],
        axis=0).astype(I32)
    cnt_ref[...] = cnt_ref[...] + jnp.sum(onehot, axis=1, keepdims=True)


def _post_mixer(a, xall, mod, ln_g, ln_b, w, router_w, router_bias, n_rows, n_lat, seq, ctx_row,
                dn_alpha):
    d = xall.shape[1]
    tm = ROW_TILE
    tri =(lax.broadcasted_iota(I32, (tm, tm), 0) < lax.broadcasted_iota(I32, (tm, tm), 1)).astype(BF16)
    const = lambda i: (0, 0)
    row_blk = pl.BlockSpec((tm, d), lambda i: (i, 0))
    k_blk = pl.BlockSpec((TOP_K, tm), lambda i: (0, i))
    return pl.pallas_call(
        functools.partial(_post_kernel, dn_alpha=dn_alpha),
        grid=(n_rows // tm,),
        in_specs=[
            row_blk, row_blk,
            _mod_spec(2, tm, n_lat, seq, ctx_row, d),
            pl.BlockSpec((1, d), const), pl.BlockSpec((1, d), const),
            _mod_spec(3, tm, n_lat, seq, ctx_row, d),
            _mod_spec(4, tm, n_lat, seq, ctx_row, d),
            pl.BlockSpec(w.shape, const),
            pl.BlockSpec((N_EXPERTS, d), const),
            pl.BlockSpec((N_EXPERTS, 1), const),
            pl.BlockSpec((tm, tm), const),
        ],
        out_specs=[
            row_blk, row_blk,
            k_blk, k_blk, k_blk,
            pl.BlockSpec((N_EXPERTS, 1), const),
        ],
        out_shape=[
            jax.ShapeDtypeStruct((n_rows, d), F32),
            jax.ShapeDtypeStruct((n_rows, d), F32),
            jax.ShapeDtypeStruct((TOP_K, n_rows), I32),
            jax.ShapeDtypeStruct((TOP_K, n_rows), F32),
            jax.ShapeDtypeStruct((TOP_K, n_rows), I32),
            jax.ShapeDtypeStruct((N_EXPERTS, 1), F32),
        ],
        compiler_params=_cparams(("arbitrary",)),
        name="post_mixer_router",
    )(a, xall, mod, ln_g.reshape(1, d), ln_b.reshape(1, d), mod, mod, w.astype(BF16),
      router_w.T, router_bias.reshape(N_EXPERTS, 1), tri)


def _slots_kernel(pstart_ref, eidx_ref, rank_ref, dest_ref):
    e = eidx_ref[...]

    def pick(i, acc):
        return jnp.where(e == i, pstart_ref[i], acc)

    dest_ref[...] = lax.fori_loop(0, N_EXPERTS, pick, jnp.zeros_like(e)) + rank_ref[...]


def _slots(pstarts, eidx, rank):
    n_tok = eidx.shape[1]
    tn = 2048 if n_tok % 2048 == 0 else ROW_TILE
    blk = pl.BlockSpec((TOP_K, tn), lambda i, ps: (0, i))
    return pl.pallas_call(
        _slots_kernel,
        grid_spec=pltpu.PrefetchScalarGridSpec(
            num_scalar_prefetch=1, grid=(n_tok // tn,), in_specs=[blk, blk], out_specs=blk),
        out_shape=jax.ShapeDtypeStruct((TOP_K, n_tok), I32),
        compiler_params=_cparams(("arbitrary",)),
        name="moe_slots",
    )(pstarts, eidx, rank)


def _dispatch_kernel(pends_ref, pcounts_ref, dest_ref, h_ref, xs_hbm, zbuf, sem):
    td = dest_ref.shape[1]
    bm = zbuf.shape[0]

    @pl.when(pl.program_id(0) == 0)
    def _():
        zbuf[...] = jnp.zeros_like(zbuf)

        def tail(e):
            return xs_hbm.at[pl.ds(pl.multiple_of(pends_ref[e] - bm, bm), bm)]

        def zstart(e, carry):
            @pl.when(pcounts_ref[e] > 0)
            def _():
                pltpu.make_async_copy(zbuf, tail(e), sem).start()
            return carry

        def zwait(e, carry):
            @pl.when(pcounts_ref[e] > 0)
            def _():
                pltpu.make_async_copy(zbuf, tail(e), sem).wait()
            return carry

        lax.fori_loop(0, N_EXPERTS, zstart, 0)
        lax.fori_loop(0, N_EXPERTS, zwait, 0)

        def unused(j):
            return xs_hbm.at[pl.ds(pl.multiple_of(j * bm, bm), bm)]

        def ustart(j, carry):
            pltpu.make_async_copy(zbuf, unused(j), sem).start()
            return carry

        def uwait(j, carry):
            pltpu.make_async_copy(zbuf, unused(j), sem).wait()
            return carry

        first_unused = pends_ref[N_EXPERTS - 1] // bm
        lax.fori_loop(first_unused, xs_hbm.shape[0] // bm, ustart, 0)
        lax.fori_loop(first_unused, xs_hbm.shape[0] // bm, uwait, 0)

    def issue(t, carry):
        for k in range(TOP_K):
            pltpu.make_async_copy(h_ref.at[pl.ds(t, 1)], xs_hbm.at[pl.ds(dest_ref[k, t], 1)], sem).start()
        return carry

    lax.fori_loop(0, td, issue, 0)
    pltpu.make_async_copy(xs_hbm.at[pl.ds(0, TOP_K * td)], xs_hbm.at[pl.ds(0, TOP_K * td)], sem).wait()


def _dispatch(dest, h2, pends, pcounts, n_slots):
    n_tok, d = h2.shape
    td = DISPATCH_TOK
    grid_spec = pltpu.PrefetchScalarGridSpec(
        num_scalar_prefetch=2,
        grid=(n_tok // td,),
        in_specs=[
            pl.BlockSpec((TOP_K, td), lambda i, pe, pc: (0, i), memory_space=pltpu.SMEM),
            pl.BlockSpec((td, d), lambda i, pe, pc: (i, 0)),
        ],
        out_specs=pl.BlockSpec(memory_space=pl.ANY),
        scratch_shapes=[pltpu.VMEM((MOE_BM, d), F32), pltpu.SemaphoreType.DMA(())],
    )
    return pl.pallas_call(
        _dispatch_kernel,
        grid_spec=grid_spec,
        out_shape=jax.ShapeDtypeStruct((n_slots, d), F32),
        compiler_params=_cparams(("arbitrary",)),
        name="moe_dispatch",
    )(pends, pcounts, dest, h2)


def _gmm_kernel(blk_e_ref, n_used_ref, xs_ref, wgu_ref, wd_ref, ys_ref, wgu_bf, wd_bf):
    i = pl.program_id(0)
    ff = wd_ref.shape[0]

    @pl.when((i == 0) | (blk_e_ref[i] != blk_e_ref[jnp.maximum(i - 1, 0)]))
    def _():
        wgu_bf[...] = wgu_ref[...].astype(BF16)
        wd_bf[...] = wd_ref[...].astype(BF16)

    @pl.when(i < n_used_ref[0])
    def _():
        gu = jnp.dot(xs_ref[...].astype(BF16), wgu_bf[...], preferred_element_type=F32)
        act = (_silu(gu[:, :ff]) * gu[:, ff:]).astype(BF16)
        ys_ref[...] = jnp.dot(act, wd_bf[...], preferred_element_type=F32)

    @pl.when(i >= n_used_ref[0])
    def _():
        ys_ref[...] = jnp.zeros_like(ys_ref)


def _grouped_ffn(xs, blk_e, n_used, wgu, wd, layer):
    n_slots, d = xs.shape
    bm = MOE_BM
    ff2 = wgu.shape[3]
    ff = wd.shape[2]
    grid_spec = pltpu.PrefetchScalarGridSpec(
        num_scalar_prefetch=2,
        grid=(n_slots // bm,),
        in_specs=[
            pl.BlockSpec((bm, d), lambda i, be, nu: (jnp.minimum(i, nu[0] - 1), 0)),
            pl.BlockSpec((None, None, d, ff2), lambda i, be, nu: (layer, be[i], 0, 0)),
            pl.BlockSpec((None, None, ff, d), lambda i, be, nu: (layer, be[i], 0, 0)),
        ],
        out_specs=pl.BlockSpec((bm, d), lambda i, be, nu: (i, 0)),
        scratch_shapes=[pltpu.VMEM((d, ff2), BF16), pltpu.VMEM((ff, d), BF16)],
    )
    return pl.pallas_call(
        _gmm_kernel,
        grid_spec=grid_spec,
        out_shape=jax.ShapeDtypeStruct((n_slots, d), F32),
        compiler_params=_cparams(("arbitrary",)),
        name="moe_grouped_ffn",
    )(blk_e, n_used, xs, wgu, wd)


def _combine_kernel(dest_ref, gates_ref, ys_hbm, x1_ref, h2_ref, gate_ref, lng_ref, lnb_ref,
                    swgu_ref, swd_ref, o_ref, buf, sem, *, dn_alpha):
    tc = x1_ref.shape[0]
    ff = swd_ref.shape[0]

    def issue(t, carry):
        for k in range(TOP_K):
            pltpu.make_async_copy(ys_hbm.at[pl.ds(dest_ref[k, t], 1)], buf.at[k, pl.ds(t, 1)], sem).start()
        return carry

    lax.fori_loop(0, tc, issue, 0)

    gu = jnp.dot(h2_ref[...].astype(BF16), swgu_ref[...], preferred_element_type=F32)
    act = (_silu(gu[:, :ff]) * gu[:, ff:]).astype(BF16)
    y = jnp.dot(act, swd_ref[...], preferred_element_type=F32)

    for k in range(TOP_K):
        pltpu.make_async_copy(ys_hbm.at[pl.ds(0, tc)], buf.at[k], sem).wait()
    for k in range(TOP_K):
        y = y + gates_ref[:, k:k + 1] * buf[k]
    o_ref[...] = _layer_norm(dn_alpha * x1_ref[...] + gate_ref[...] * y, lng_ref[...], lnb_ref[...])


def _combine(dest, gates_t, ys, x1, h2, mod, ln_g, ln_b, swgu, swd, n_rows, n_lat, seq, ctx_row,
             dn_alpha):
    d = x1.shape[1]
    tc = DISPATCH_TOK
    const = lambda i: (0, 0)
    return pl.pallas_call(
        functools.partial(_combine_kernel, dn_alpha=dn_alpha),
        grid=(n_rows // tc,),
        in_specs=[
            pl.BlockSpec((TOP_K, tc), lambda i: (0, i), memory_space=pltpu.SMEM),
            pl.BlockSpec((tc, TOP_K), lambda i: (i, 0)),
            pl.BlockSpec(memory_space=pl.ANY),
            pl.BlockSpec((tc, d), lambda i: (i, 0)),
            pl.BlockSpec((tc, d), lambda i: (i, 0)),
            _mod_spec(5, tc, n_lat, seq, ctx_row, d),
            pl.BlockSpec((1, d), const), pl.BlockSpec((1, d), const),
            pl.BlockSpec(swgu.shape, const),
            pl.BlockSpec(swd.shape, const),
        ],
        out_specs=pl.BlockSpec((tc, d), lambda i: (i, 0)),
        out_shape=jax.ShapeDtypeStruct((n_rows, d), F32),
        scratch_shapes=[pltpu.VMEM((TOP_K, tc, d), F32), pltpu.SemaphoreType.DMA(())],
        compiler_params=_cparams(("arbitrary",)),
        name="moe_combine",
    )(dest, gates_t, ys, x1, h2, mod, ln_g.reshape(1, d), ln_b.reshape(1, d),
      swgu.astype(BF16), swd.astype(BF16))


def _moe(x1, h2, eidx, gates, rank, counts, mod, ln_g, ln_b, wgu, wd, layer, swgu, swd,
         n_rows, n_lat, seq, ctx_row, dn_alpha):
    bm = MOE_BM
    counts = counts.reshape(N_EXPERTS).astype(I32)
    pcounts = (counts + bm - 1) // bm * bm
    pends = jnp.cumsum(pcounts)
    pstarts = pends - pcounts
    dest = _slots(pstarts, eidx, rank)
    n_blk = (n_rows * TOP_K + N_EXPERTS * (bm - 1) + bm - 1) // bm
    blk_start = jnp.arange(n_blk, dtype=I32) * bm
    blk_e = jnp.minimum(jnp.sum((pends[None, :] <= blk_start[:, None]).astype(I32), axis=1),
                        N_EXPERTS - 1)
    n_used = (pends[-1] // bm).astype(I32).reshape(1)
    xs = _dispatch(dest, h2, pends, pcounts, n_blk * bm)
    ys = _grouped_ffn(xs, blk_e, n_used, wgu, wd, layer)
    return _combine(dest, gates.T, ys, x1, h2, mod, ln_g, ln_b, swgu, swd,
                    n_rows, n_lat, seq, ctx_row, dn_alpha)


def kernel(x, c, ctx, c_ctx, ada_w, ada_b, ln_g, ln_b, attn_w_qkv, attn_q_norm, attn_k_norm, attn_w_o, conv_w_in, conv_taps, conv_w_out, router_w, router_bias, exp_w_gate_up, exp_w_down, shared_w_gate_up, shared_w_down):
    b, seq, d = x.shape
    cl = ctx.shape[1]
    depth = ada_w.shape[0]
    n_lat = b * seq
    n_ctx = b * cl
    dn_alpha = (2 * depth) ** 0.25
    assert depth == 2 and b < MOD_ROWS
    assert seq % ROW_TILE == 0 and n_ctx % ROW_TILE == 0 and seq % GRID_W == 0

    cond = jnp.zeros((MOD_ROWS, d), F32).at[:b].set(c).at[b].set(c_ctx)
    mod = _modulation(cond, ada_w, ada_b)
    xall = jnp.concatenate([x.reshape(n_lat, d), ctx.reshape(n_ctx, d)], axis=0)
    n_all = n_lat + n_ctx

    q, k, v = _qkv_project(xall, mod[0], attn_w_qkv[0], attn_q_norm[0], attn_k_norm[0], n_lat, seq, b)
    o = _attention(q, k, v, b, seq, cl, n_lat)
    x1, h2, eidx, gates, rank, counts = _post_mixer(
        o, xall, mod[0], ln_g[0, 0], ln_b[0, 0], attn_w_o[0], router_w[0], router_bias[0],
        n_all, n_lat, seq, b, dn_alpha)
    xall = _moe(x1, h2, eidx, gates, rank, counts, mod[0], ln_g[0, 1], ln_b[0, 1],
                exp_w_gate_up, exp_w_down, 0, shared_w_gate_up[0], shared_w_down[0],
                n_all, n_lat, seq, b, dn_alpha)

    zin = _in_project(xall, mod[1], conv_w_in[0], n_lat, n_lat, seq, b)
    a = _short_conv(zin, conv_taps[0], b, seq, d)
    x1, h2, eidx, gates, rank, counts = _post_mixer(
        a, xall, mod[1], ln_g[1, 0], ln_b[1, 0], conv_w_out[0], router_w[1], router_bias[1],
        n_lat, n_lat, seq, b, dn_alpha)
    out = _moe(x1, h2, eidx, gates, rank, counts, mod[1], ln_g[1, 1], ln_b[1, 1],
               exp_w_gate_up, exp_w_down, 1, shared_w_gate_up[1], shared_w_down[1],
               n_lat, n_lat, seq, b, dn_alpha)
    return out.reshape(b, seq, d)
```

```python
import functools

import jax
import jax.numpy as jnp
from jax import lax
from jax.experimental import pallas as pl
from jax.experimental.pallas import tpu as pltpu

F32 = jnp.float32
BF16 = jnp.bfloat16
I32 = jnp.int32

N_HEADS = 8
N_KV_HEADS = 2
HEAD_DIM = 128
KV_GROUP = N_HEADS // N_KV_HEADS
GRID_W = 64
ROPE_THETA = 10000.0
N_EXPERTS = 256
TOP_K = 8
N_GROUPS = 8
TOPK_GROUPS = 4
PER_GROUP = N_EXPERTS // N_GROUPS
ROUTED_SCALE = 2.5
LN_EPS = 1e-5
QK_EPS = 1e-6
N_MOD = 6
MOD_ROWS = 16

LANES = 128
SUBLANES = 8
VMEM_LIMIT = 56 * 1024 * 1024

ROW_TILE = 512
ATTN_TQ = 256
ATTN_TK = 512
MOE_BM = 256
DISPATCH_TOK = 512
CONV_LANES = 128

HIGHEST = lax.Precision.HIGHEST


def _cparams(sem):
    return pltpu.CompilerParams(dimension_semantics=sem, vmem_limit_bytes=VMEM_LIMIT)


def _silu(v):
    return v * jax.nn.sigmoid(v)


def _mod_kernel(c_ref, w_ref, b_ref, o_ref):
    s = _silu(c_ref[...])
    o_ref[...] = jnp.dot(s, w_ref[...], precision=HIGHEST, preferred_element_type=F32) + b_ref[...]


def _modulation(cond, ada_w, ada_b):
    depth, d, nd = ada_w.shape
    tn = 1536
    out = pl.pallas_call(
        _mod_kernel,
        grid=(depth, nd // tn),
        in_specs=[
            pl.BlockSpec((MOD_ROWS, d), lambda l, j: (0, 0)),
            pl.BlockSpec((None, d, tn), lambda l, j: (l, 0, j)),
            pl.BlockSpec((None, 1, tn), lambda l, j: (l, 0, j)),
        ],
        out_specs=pl.BlockSpec((None, MOD_ROWS, tn), lambda l, j: (l, 0, j)),
        out_shape=jax.ShapeDtypeStruct((depth, MOD_ROWS, nd), F32),
        compiler_params=_cparams(("arbitrary", "arbitrary")),
        name="adaln_modulation",
    )(cond, ada_w, ada_b.reshape(depth, 1, nd))
    return out.reshape(depth, MOD_ROWS * N_MOD, 1, d)


def _mod_spec(comp, tm, n_lat, seq, ctx_row, d):
    def index(i, *_):
        row0 = i * tm
        r = jnp.where(row0 < n_lat, row0 // seq, ctx_row)
        return (r * N_MOD + comp, 0, 0)

    return pl.BlockSpec((None, 1, d), index)


def _qkv_kernel(x_ref, shift_ref, scale_ref, w_ref, qg_ref, kg_ref, cos_ref, sin_ref,
                q_ref, k_ref, v_ref):
    h = (x_ref[...] * (1.0 + scale_ref[...]) + shift_ref[...]).astype(BF16)
    qkv = jnp.dot(h, w_ref[...], preferred_element_type=F32)
    cos = cos_ref[...]
    sin = sin_ref[...]
    hq = N_HEADS * HEAD_DIM
    kd = N_KV_HEADS * HEAD_DIM

    def norm_rope(t, g, post):
        t = t * lax.rsqrt(jnp.mean(t * t, axis=-1, keepdims=True) + QK_EPS) * g
        t = t * cos + pltpu.roll(t, HEAD_DIM // 2, axis=1) * sin
        return (t * post).astype(BF16)

    for hd in range(N_HEADS):
        sl = slice(hd * HEAD_DIM, (hd + 1) * HEAD_DIM)
        q_ref[:, sl] = norm_rope(qkv[:, sl], qg_ref[...], HEAD_DIM ** -0.5)
    for hd in range(N_KV_HEADS):
        sl = slice(hd * HEAD_DIM, (hd + 1) * HEAD_DIM)
        k_ref[:, sl] = norm_rope(qkv[:, hq + hd * HEAD_DIM: hq + (hd + 1) * HEAD_DIM], kg_ref[...], 1.0)
    v_ref[...] = qkv[:, hq + kd:].astype(BF16)


def _rope_tables(seq, tm):
    rows = seq // GRID_W
    row = jnp.repeat(jnp.arange(rows, dtype=F32), GRID_W)
    col = jnp.tile(jnp.arange(GRID_W, dtype=F32), rows)
    axis_dim = HEAD_DIM // 2
    freqs = ROPE_THETA ** (-jnp.arange(0, axis_dim, 2, dtype=F32) / axis_dim)
    ang = jnp.concatenate([row[:, None] * freqs, col[:, None] * freqs], axis=-1)
    cos = jnp.concatenate([jnp.cos(ang), jnp.cos(ang)], axis=-1)
    sin = jnp.concatenate([-jnp.sin(ang), jnp.sin(ang)], axis=-1)
    cos = jnp.concatenate([cos, jnp.ones((tm, HEAD_DIM), F32)], axis=0)
    sin = jnp.concatenate([sin, jnp.zeros((tm, HEAD_DIM), F32)], axis=0)
    return cos.reshape(seq // tm + 1, tm, HEAD_DIM), sin.reshape(seq // tm + 1, tm, HEAD_DIM)


def _qkv_project(xall, mod, w_qkv, q_g, k_g, n_lat, seq, ctx_row):
    t, d = xall.shape
    tm = ROW_TILE
    hq = N_HEADS * HEAD_DIM
    kd = N_KV_HEADS * HEAD_DIM
    perm = jnp.concatenate([jnp.arange(0, HEAD_DIM, 2), jnp.arange(1, HEAD_DIM, 2)])
    cols = jnp.concatenate([hd * HEAD_DIM + perm for hd in range(N_HEADS + N_KV_HEADS)]
                           + [jnp.arange(hq + kd, hq + 2 * kd)])
    w = w_qkv[:, cols].astype(BF16)
    cos, sin = _rope_tables(seq, tm)
    n_pos = seq // tm

    def pos_index(i):
        row0 = i * tm
        return (jnp.where(row0 < n_lat, (row0 % seq) // tm, n_pos), 0, 0)

    const = lambda i: (0, 0)
    return pl.pallas_call(
        _qkv_kernel,
        grid=(t // tm,),
        in_specs=[
            pl.BlockSpec((tm, d), lambda i: (i, 0)),
            _mod_spec(0, tm, n_lat, seq, ctx_row, d),
            _mod_spec(1, tm, n_lat, seq, ctx_row, d),
            pl.BlockSpec(w.shape, const),
            pl.BlockSpec((1, HEAD_DIM), const),
            pl.BlockSpec((1, HEAD_DIM), const),
            pl.BlockSpec((None, tm, HEAD_DIM), pos_index),
            pl.BlockSpec((None, tm, HEAD_DIM), pos_index),
        ],
        out_specs=[
            pl.BlockSpec((tm, hq), lambda i: (i, 0)),
            pl.BlockSpec((tm, kd), lambda i: (i, 0)),
            pl.BlockSpec((tm, kd), lambda i: (i, 0)),
        ],
        out_shape=[
            jax.ShapeDtypeStruct((t, hq), BF16),
            jax.ShapeDtypeStruct((t, kd), BF16),
            jax.ShapeDtypeStruct((t, kd), BF16),
        ],
        compiler_params=_cparams(("parallel",)),
        name="qkv_norm_rope",
    )(xall, mod, mod, w, q_g[perm].reshape(1, HEAD_DIM), k_g[perm].reshape(1, HEAD_DIM), cos, sin)


def _attn_kernel(*refs, n_lat_chunks, tk):
    if n_lat_chunks:
        q_ref, kc_ref, vc_ref, kl_ref, vl_ref, o_ref = refs
    else:
        q_ref, kc_ref, vc_ref, o_ref = refs
    tq = q_ref.shape[0]
    q = jnp.concatenate([q_ref[:, h * HEAD_DIM:(h + 1) * HEAD_DIM] for h in range(KV_GROUP)], axis=0)
    rows = KV_GROUP * tq

    def chunk(k, v, m, l, acc):
        s = lax.dot_general(q, k, (((1,), (1,)), ((), ())), preferred_element_type=F32)
        m_new = jnp.maximum(m, jnp.max(s, axis=-1, keepdims=True))
        p = jnp.exp(s - m_new)
        a = jnp.exp(m - m_new)
        l = a * l + jnp.sum(p, axis=-1, keepdims=True)
        acc = a * acc + jnp.dot(p.astype(BF16), v, preferred_element_type=F32)
        return m_new, l, acc

    m = jnp.full((rows, 1), -jnp.inf, F32)
    l = jnp.zeros((rows, 1), F32)
    acc = jnp.zeros((rows, HEAD_DIM), F32)
    m, l, acc = chunk(kc_ref[...], vc_ref[...], m, l, acc)
    for c in range(n_lat_chunks):
        m, l, acc = chunk(kl_ref[c * tk:(c + 1) * tk, :], vl_ref[c * tk:(c + 1) * tk, :], m, l, acc)
    o = (acc / l).astype(BF16)
    for h in range(KV_GROUP):
        o_ref[:, h * HEAD_DIM:(h + 1) * HEAD_DIM] = o[h * tq:(h + 1) * tq]


def _attention(q, k, v, b, seq, cl, n_lat):
    t = q.shape[0]
    gw = KV_GROUP * HEAD_DIM
    tq = min(ATTN_TQ, seq)
    tk = min(ATTN_TK, seq)
    nq = seq // tq
    ctx_blk0 = n_lat // cl
    hq = N_HEADS * HEAD_DIM

    ctx_kv = pl.BlockSpec((cl, HEAD_DIM), lambda bi, g, qi: (ctx_blk0 + bi, g))
    lat_kv = pl.BlockSpec((seq, HEAD_DIM), lambda bi, g, qi: (bi, g))
    lat_q = pl.BlockSpec((tq, gw), lambda bi, g, qi: (bi * nq + qi, g))
    o_lat = pl.pallas_call(
        functools.partial(_attn_kernel, n_lat_chunks=seq // tk, tk=tk),
        grid=(b, N_KV_HEADS, nq),
        in_specs=[lat_q, ctx_kv, ctx_kv, lat_kv, lat_kv],
        out_specs=lat_q,
        out_shape=jax.ShapeDtypeStruct((n_lat, hq), BF16),
        compiler_params=_cparams(("parallel", "parallel", "arbitrary")),
        name="attention_latent",
    )(q, k, v, k, v)

    ctx_q = pl.BlockSpec((cl, gw), lambda bi, g: (ctx_blk0 + bi, g))
    ctx_kv2 = pl.BlockSpec((cl, HEAD_DIM), lambda bi, g: (ctx_blk0 + bi, g))
    o_ctx = pl.pallas_call(
        functools.partial(_attn_kernel, n_lat_chunks=0, tk=tk),
        grid=(b, N_KV_HEADS),
        in_specs=[ctx_q, ctx_kv2, ctx_kv2],
        out_specs=pl.BlockSpec((cl, gw), lambda bi, g: (bi, g)),
        out_shape=jax.ShapeDtypeStruct((t - n_lat, hq), BF16),
        compiler_params=_cparams(("parallel", "parallel")),
        name="attention_context",
    )(q, k, v)
    return jnp.concatenate([o_lat, o_ctx], axis=0)


def _in_proj_kernel(x_ref, shift_ref, scale_ref, w_ref, o_ref):
    h = (x_ref[...] * (1.0 + scale_ref[...]) + shift_ref[...]).astype(BF16)
    o_ref[...] = jnp.dot(h, w_ref[...], preferred_element_type=F32)


def _in_project(xall, mod, w_in, n_rows, n_lat, seq, ctx_row):
    d = xall.shape[1]
    n_out = w_in.shape[1]
    tm = ROW_TILE
    return pl.pallas_call(
        _in_proj_kernel,
        grid=(n_rows // tm,),
        in_specs=[
            pl.BlockSpec((tm, d), lambda i: (i, 0)),
            _mod_spec(0, tm, n_lat, seq, ctx_row, d),
            _mod_spec(1, tm, n_lat, seq, ctx_row, d),
            pl.BlockSpec(w_in.shape, lambda i: (0, 0)),
        ],
        out_specs=pl.BlockSpec((tm, n_out), lambda i: (i, 0)),
        out_shape=jax.ShapeDtypeStruct((n_rows, n_out), F32),
        compiler_params=_cparams(("parallel",)),
        name="conv_in_proj",
    )(xall, mod, mod, w_in.astype(BF16))


def _conv_kernel(bg_ref, cg_ref, v_ref, taps_ref, o_ref):
    u = cg_ref[...] * v_ref[...]
    n = u.shape[0]
    pos = lax.broadcasted_iota(I32, u.shape, 0)
    prev = jnp.where(pos == 0, 0.0, pltpu.roll(u, 1, axis=0))
    nxt = jnp.where(pos == n - 1, 0.0, pltpu.roll(u, n - 1, axis=0))
    conv = prev * taps_ref[0:1, :] + u * taps_ref[1:2, :] + nxt * taps_ref[2:3, :]
    o_ref[...] = (bg_ref[...] * conv).astype(BF16)


def _short_conv(zin, taps, n_seqs, seq, d):
    tc = CONV_LANES
    nj = d // tc
    return pl.pallas_call(
        _conv_kernel,
        grid=(n_seqs, nj),
        in_specs=[
            pl.BlockSpec((seq, tc), lambda s, j: (s, j)),
            pl.BlockSpec((seq, tc), lambda s, j: (s, nj + j)),
            pl.BlockSpec((seq, tc), lambda s, j: (s, 2 * nj + j)),
            pl.BlockSpec((taps.shape[0], tc), lambda s, j: (0, j)),
        ],
        out_specs=pl.BlockSpec((seq, tc), lambda s, j: (s, j)),
        out_shape=jax.ShapeDtypeStruct((n_seqs * seq, d), BF16),
        compiler_params=_cparams(("parallel", "parallel")),
        name="short_conv",
    )(zin, zin, zin, taps)


def _layer_norm(z, g, b):
    mu = jnp.mean(z, axis=-1, keepdims=True)
    zc = z - mu
    var = jnp.mean(zc * zc, axis=-1, keepdims=True)
    return zc * lax.rsqrt(var + LN_EPS) * g + b


def _post_kernel(a_ref, x_ref, gate_ref, lng_ref, lnb_ref, shift_ref, scale_ref, w_ref,
                 rwt_ref, rb_ref, tri_ref,
                 x1_ref, h2_ref, eidx_ref, gates_ref, rank_ref, cnt_ref, *, dn_alpha):
    tm = a_ref.shape[0]
    d = x_ref.shape[1]
    y = jnp.dot(a_ref[...], w_ref[...], preferred_element_type=F32)
    x1 = _layer_norm(dn_alpha * x_ref[...] + gate_ref[...] * y, lng_ref[...], lnb_ref[...])
    x1_ref[...] = x1
    h2 = x1 * (1.0 + scale_ref[...]) + shift_ref[...]
    h2_ref[...] = h2

    logits = lax.dot_general(rwt_ref[...], h2, (((1,), (1,)), ((), ())),
                             precision=HIGHEST, preferred_element_type=F32)
    scores = jax.nn.sigmoid(logits)
    biased = scores + rb_ref[...]
    neg = -jnp.inf
    big = jnp.int32(1 << 30)
    row = lax.broadcasted_iota(I32, (N_EXPERTS, tm), 0)

    def argmax_rows(vals, idx):
        mx = jnp.max(vals, axis=0, keepdims=True)
        return mx, jnp.min(jnp.where(vals == mx, idx, big), axis=0, keepdims=True)

    gs = []
    grp_row = lax.broadcasted_iota(I32, (PER_GROUP, tm), 0)
    for g in range(N_GROUPS):
        bg = biased[g * PER_GROUP:(g + 1) * PER_GROUP]
        ig = grp_row + g * PER_GROUP
        m1, i1 = argmax_rows(bg, ig)
        m2 = jnp.max(jnp.where(ig == i1, neg, bg), axis=0, keepdims=True)
        gs.append(m1 + m2)
    gsc = jnp.concatenate(gs, axis=0)
    grow = lax.broadcasted_iota(I32, (N_GROUPS, tm), 0)
    gsel = jnp.zeros((N_GROUPS, tm), F32)
    for _ in range(TOPK_GROUPS):
        _, gi = argmax_rows(gsc, grow)
        hit = grow == gi
        gsel = jnp.where(hit, 1.0, gsel)
        gsc = jnp.where(hit, neg, gsc)
    cur = jnp.concatenate(
        [jnp.where(gsel[g:g + 1] > 0.0, biased[g * PER_GROUP:(g + 1) * PER_GROUP], neg)
         for g in range(N_GROUPS)], axis=0)

    onehot = jnp.zeros((N_EXPERTS, tm), F32)
    idxs, gvals = [], []
    for _ in range(TOP_K):
        _, ei = argmax_rows(cur, row)
        hit = row == ei
        gvals.append(jnp.sum(jnp.where(hit, scores, 0.0), axis=0, keepdims=True))
        idxs.append(ei)
        onehot = jnp.where(hit, 1.0, onehot)
        cur = jnp.where(hit, neg, cur)
    gv = jnp.concatenate(gvals, axis=0)
    gates_ref[...] = gv / jnp.sum(gv, axis=0, keepdims=True) * ROUTED_SCALE
    eidx_ref[...] = jnp.concatenate(idxs, axis=0)

    @pl.when(pl.program_id(0) == 0)
    def _():
        cnt_ref[...] = jnp.zeros_like(cnt_ref)

    prefix = jnp.dot(onehot.astype(BF16), tri_ref[...], preferred_element_type=F32)
    pos = prefix + cnt_ref[...]
    rank_ref[...] = jnp.concatenate(
        [jnp.sum(jnp.where(row == ei, pos, 0.0), axis=0, keepdims=True) for ei in idxs],
        axis=0).astype(I32)
    cnt_ref[...] = cnt_ref[...] + jnp.sum(onehot, axis=1, keepdims=True)


def _post_mixer(a, xall, mod, ln_g, ln_b, w, router_w, router_bias, n_rows, n_lat, seq, ctx_row,
                dn_alpha):
    d = xall.shape[1]
    tm = ROW_TILE
    tri =(lax.broadcasted_iota(I32, (tm, tm), 0) < lax.broadcasted_iota(I32, (tm, tm), 1)).astype(BF16)
    const = lambda i: (0, 0)
    row_blk = pl.BlockSpec((tm, d), lambda i: (i, 0))
    k_blk = pl.BlockSpec((TOP_K, tm), lambda i: (0, i))
    return pl.pallas_call(
        functools.partial(_post_kernel, dn_alpha=dn_alpha),
        grid=(n_rows // tm,),
        in_specs=[
            row_blk, row_blk,
            _mod_spec(2, tm, n_lat, seq, ctx_row, d),
            pl.BlockSpec((1, d), const), pl.BlockSpec((1, d), const),
            _mod_spec(3, tm, n_lat, seq, ctx_row, d),
            _mod_spec(4, tm, n_lat, seq, ctx_row, d),
            pl.BlockSpec(w.shape, const),
            pl.BlockSpec((N_EXPERTS, d), const),
            pl.BlockSpec((N_EXPERTS, 1), const),
            pl.BlockSpec((tm, tm), const),
        ],
        out_specs=[
            row_blk, row_blk,
            k_blk, k_blk, k_blk,
            pl.BlockSpec((N_EXPERTS, 1), const),
        ],
        out_shape=[
            jax.ShapeDtypeStruct((n_rows, d), F32),
            jax.ShapeDtypeStruct((n_rows, d), F32),
            jax.ShapeDtypeStruct((TOP_K, n_rows), I32),
            jax.ShapeDtypeStruct((TOP_K, n_rows), F32),
            jax.ShapeDtypeStruct((TOP_K, n_rows), I32),
            jax.ShapeDtypeStruct((N_EXPERTS, 1), F32),
        ],
        compiler_params=_cparams(("arbitrary",)),
        name="post_mixer_router",
    )(a, xall, mod, ln_g.reshape(1, d), ln_b.reshape(1, d), mod, mod, w.astype(BF16),
      router_w.T, router_bias.reshape(N_EXPERTS, 1), tri)


def _slots_kernel(pstart_ref, eidx_ref, rank_ref, dest_ref):
    e = eidx_ref[...]

    def pick(i, acc):
        return jnp.where(e == i, pstart_ref[i], acc)

    dest_ref[...] = lax.fori_loop(0, N_EXPERTS, pick, jnp.zeros_like(e)) + rank_ref[...]


def _slots(pstarts, eidx, rank):
    n_tok = eidx.shape[1]
    tn = 2048 if n_tok % 2048 == 0 else ROW_TILE
    blk = pl.BlockSpec((TOP_K, tn), lambda i, ps: (0, i))
    return pl.pallas_call(
        _slots_kernel,
        grid_spec=pltpu.PrefetchScalarGridSpec(
            num_scalar_prefetch=1, grid=(n_tok // tn,), in_specs=[blk, blk], out_specs=blk),
        out_shape=jax.ShapeDtypeStruct((TOP_K, n_tok), I32),
        compiler_params=_cparams(("arbitrary",)),
        name="moe_slots",
    )(pstarts, eidx, rank)


def _dispatch_kernel(pends_ref, pcounts_ref, dest_ref, h_ref, xs_hbm, zbuf, sem):
    td = dest_ref.shape[1]
    bm = zbuf.shape[0]

    @pl.when(pl.program_id(0) == 0)
    def _():
        zbuf[...] = jnp.zeros_like(zbuf)

        def tail(e):
            return xs_hbm.at[pl.ds(pl.multiple_of(pends_ref[e] - bm, bm), bm)]

        def zstart(e, carry):
            @pl.when(pcounts_ref[e] > 0)
            def _():
                pltpu.make_async_copy(zbuf, tail(e), sem).start()
            return carry

        def zwait(e, carry):
            @pl.when(pcounts_ref[e] > 0)
            def _():
                pltpu.make_async_copy(zbuf, tail(e), sem).wait()
            return carry

        lax.fori_loop(0, N_EXPERTS, zstart, 0)
        lax.fori_loop(0, N_EXPERTS, zwait, 0)

        def unused(j):
            return xs_hbm.at[pl.ds(pl.multiple_of(j * bm, bm), bm)]

        def ustart(j, carry):
            pltpu.make_async_copy(zbuf, unused(j), sem).start()
            return carry

        def uwait(j, carry):
            pltpu.make_async_copy(zbuf, unused(j), sem).wait()
            return carry

        first_unused = pends_ref[N_EXPERTS - 1] // bm
        lax.fori_loop(first_unused, xs_hbm.shape[0] // bm, ustart, 0)
        lax.fori_loop(first_unused, xs_hbm.shape[0] // bm, uwait, 0)

    def issue(t, carry):
        for k in range(TOP_K):
            pltpu.make_async_copy(h_ref.at[pl.ds(t, 1)], xs_hbm.at[pl.ds(dest_ref[k, t], 1)], sem).start()
        return carry

    lax.fori_loop(0, td, issue, 0)
    pltpu.make_async_copy(xs_hbm.at[pl.ds(0, TOP_K * td)], xs_hbm.at[pl.ds(0, TOP_K * td)], sem).wait()


def _dispatch(dest, h2, pends, pcounts, n_slots):
    n_tok, d = h2.shape
    td = DISPATCH_TOK
    grid_spec = pltpu.PrefetchScalarGridSpec(
        num_scalar_prefetch=2,
        grid=(n_tok // td,),
        in_specs=[
            pl.BlockSpec((TOP_K, td), lambda i, pe, pc: (0, i), memory_space=pltpu.SMEM),
            pl.BlockSpec((td, d), lambda i, pe, pc: (i, 0)),
        ],
        out_specs=pl.BlockSpec(memory_space=pl.ANY),
        scratch_shapes=[pltpu.VMEM((MOE_BM, d), h2.dtype), pltpu.SemaphoreType.DMA(())],
    )
    return pl.pallas_call(
        _dispatch_kernel,
        grid_spec=grid_spec,
        out_shape=jax.ShapeDtypeStruct((n_slots, d), h2.dtype),
        compiler_params=_cparams(("arbitrary",)),
        name="moe_dispatch",
    )(pends, pcounts, dest, h2)


def _gmm_kernel(blk0_ref, nblk_ref, n_used_ref, xs_hbm, wgu_ref, wd_ref, ys_hbm,
                xbuf, ybuf, wgu_bf, wd_bf, sem_in, sem_out):
    e = pl.program_id(0)
    bm = xbuf.shape[1]
    ff = wd_ref.shape[0]
    n_used = n_used_ref[0]

    def rows(b):
        return pl.ds(pl.multiple_of(b * bm, bm), bm)

    def x_copy(b, slot):
        return pltpu.make_async_copy(xs_hbm.at[rows(b)], xbuf.at[slot], sem_in.at[slot])

    def y_copy(b, slot):
        return pltpu.make_async_copy(ybuf.at[slot], ys_hbm.at[rows(b)], sem_out.at[slot])

    @pl.when(e == 0)
    def _():
        x_copy(0, 0).start()

    nb = nblk_ref[e]

    @pl.when(nb > 0)
    def _():
        wgu_bf[...] = wgu_ref[...].astype(BF16)
        wd_bf[...] = wd_ref[...].astype(BF16)

    def block(j, carry):
        b = blk0_ref[e] + j
        slot = b % 2
        x_copy(b, slot).wait()

        @pl.when(b + 1 < n_used)
        def _():
            x_copy(b + 1, 1 - slot).start()

        @pl.when(b >= 2)
        def _():
            y_copy(b - 2, slot).wait()

        gu = jnp.dot(xbuf[slot].astype(BF16), wgu_bf[...], preferred_element_type=F32)
        act = (_silu(gu[:, :ff]) * gu[:, ff:]).astype(BF16)
        ybuf[slot] = jnp.dot(act, wd_bf[...], preferred_element_type=F32)
        y_copy(b, slot).start()
        return carry

    lax.fori_loop(0, nb, block, 0)

    @pl.when(e == pl.num_programs(0) - 1)
    def _():
        for back in (2, 1):
            @pl.when(n_used >= back)
            def _():
                y_copy(n_used - back, (n_used - back) % 2).wait()

        ybuf[0] = jnp.zeros(ybuf.shape[1:], ybuf.dtype)
        n_total = ys_hbm.shape[0] // bm

        def zstart(b, carry):
            y_copy(b, 0).start()
            return carry

        def zwait(b, carry):
            y_copy(b, 0).wait()
            return carry

        lax.fori_loop(n_used, n_total, zstart, 0)
        lax.fori_loop(n_used, n_total, zwait, 0)


def _grouped_ffn(xs, blk0, nblk, n_used, wgu, wd, layer):
    n_slots = xs.shape[0]
    bm = MOE_BM
    n_exp, d, ff2 = wgu.shape[1:]
    ff = wd.shape[2]
    grid_spec = pltpu.PrefetchScalarGridSpec(
        num_scalar_prefetch=3,
        grid=(n_exp,),
        in_specs=[
            pl.BlockSpec(memory_space=pl.ANY),
            pl.BlockSpec((None, None, d, ff2), lambda e, b0, nb, nu: (layer, e, 0, 0)),
            pl.BlockSpec((None, None, ff, d), lambda e, b0, nb, nu: (layer, e, 0, 0)),
        ],
        out_specs=pl.BlockSpec(memory_space=pl.ANY),
        scratch_shapes=[
            pltpu.VMEM((2, bm, d), F32), pltpu.VMEM((2, bm, d), F32),
            pltpu.VMEM((d, ff2), BF16), pltpu.VMEM((ff, d), BF16),
            pltpu.SemaphoreType.DMA((2,)), pltpu.SemaphoreType.DMA((2,)),
        ],
    )
    return pl.pallas_call(
        _gmm_kernel,
        grid_spec=grid_spec,
        out_shape=jax.ShapeDtypeStruct((n_slots, d), F32),
        compiler_params=_cparams(("arbitrary",)),
        name="moe_grouped_ffn",
    )(blk0, nblk, n_used, xs, wgu, wd)


def _combine_kernel(dest_ref, gates_ref, ys_hbm, x1_ref, h2_ref, gate_ref, lng_ref, lnb_ref,
                    swgu_ref, swd_ref, o_ref, buf, sem, *, dn_alpha):
    tc = x1_ref.shape[0]
    ff = swd_ref.shape[0]

    def issue(t, carry):
        for k in range(TOP_K):
            pltpu.make_async_copy(ys_hbm.at[pl.ds(dest_ref[k, t], 1)], buf.at[k, pl.ds(t, 1)], sem).start()
        return carry

    lax.fori_loop(0, tc, issue, 0)

    gu = jnp.dot(h2_ref[...].astype(BF16), swgu_ref[...], preferred_element_type=F32)
    act = (_silu(gu[:, :ff]) * gu[:, ff:]).astype(BF16)
    y = jnp.dot(act, swd_ref[...], preferred_element_type=F32)

    for k in range(TOP_K):
        pltpu.make_async_copy(ys_hbm.at[pl.ds(0, tc)], buf.at[k], sem).wait()
    for k in range(TOP_K):
        y = y + gates_ref[:, k:k + 1] * buf[k]
    o_ref[...] = _layer_norm(dn_alpha * x1_ref[...] + gate_ref[...] * y, lng_ref[...], lnb_ref[...])


def _combine(dest, gates_t, ys, x1, h2, mod, ln_g, ln_b, swgu, swd, n_rows, n_lat, seq, ctx_row,
             dn_alpha):
    d = x1.shape[1]
    tc = DISPATCH_TOK
    const = lambda i: (0, 0)
    return pl.pallas_call(
        functools.partial(_combine_kernel, dn_alpha=dn_alpha),
        grid=(n_rows // tc,),
        in_specs=[
            pl.BlockSpec((TOP_K, tc), lambda i: (0, i), memory_space=pltpu.SMEM),
            pl.BlockSpec((tc, TOP_K), lambda i: (i, 0)),
            pl.BlockSpec(memory_space=pl.ANY),
            pl.BlockSpec((tc, d), lambda i: (i, 0)),
            pl.BlockSpec((tc, d), lambda i: (i, 0)),
            _mod_spec(5, tc, n_lat, seq, ctx_row, d),
            pl.BlockSpec((1, d), const), pl.BlockSpec((1, d), const),
            pl.BlockSpec(swgu.shape, const),
            pl.BlockSpec(swd.shape, const),
        ],
        out_specs=pl.BlockSpec((tc, d), lambda i: (i, 0)),
        out_shape=jax.ShapeDtypeStruct((n_rows, d), F32),
        scratch_shapes=[pltpu.VMEM((TOP_K, tc, d), F32), pltpu.SemaphoreType.DMA(())],
        compiler_params=_cparams(("arbitrary",)),
        name="moe_combine",
    )(dest, gates_t, ys, x1, h2, mod, ln_g.reshape(1, d), ln_b.reshape(1, d),
      swgu.astype(BF16), swd.astype(BF16))


def _moe(x1, h2, eidx, gates, rank, counts, mod, ln_g, ln_b, wgu, wd, layer, swgu, swd,
         n_rows, n_lat, seq, ctx_row, dn_alpha):
    bm = MOE_BM
    counts = counts.reshape(N_EXPERTS).astype(I32)
    pcounts = (counts + bm - 1) // bm * bm
    pends = jnp.cumsum(pcounts)
    pstarts = pends - pcounts
    dest = _slots(pstarts, eidx, rank)
    n_blk = (n_rows * TOP_K + N_EXPERTS * (bm - 1) + bm - 1) // bm
    n_used = (pends[-1] // bm).astype(I32).reshape(1)
    xs = _dispatch(dest, h2, pends, pcounts, n_blk * bm)
    ys = _grouped_ffn(xs, pstarts // bm, pcounts // bm, n_used, wgu, wd, layer)
    return _combine(dest, gates.T, ys, x1, h2, mod, ln_g, ln_b, swgu, swd,
                    n_rows, n_lat, seq, ctx_row, dn_alpha)


def kernel(x, c, ctx, c_ctx, ada_w, ada_b, ln_g, ln_b, attn_w_qkv, attn_q_norm, attn_k_norm, attn_w_o, conv_w_in, conv_taps, conv_w_out, router_w, router_bias, exp_w_gate_up, exp_w_down, shared_w_gate_up, shared_w_down):
    b, seq, d = x.shape
    cl = ctx.shape[1]
    depth = ada_w.shape[0]
    n_lat = b * seq
    n_ctx = b * cl
    dn_alpha = (2 * depth) ** 0.25
    assert depth == 2 and b < MOD_ROWS
    assert seq % ROW_TILE == 0 and n_ctx % ROW_TILE == 0 and seq % GRID_W == 0

    cond = jnp.zeros((MOD_ROWS, d), F32).at[:b].set(c).at[b].set(c_ctx)
    mod = _modulation(cond, ada_w, ada_b)
    xall = jnp.concatenate([x.reshape(n_lat, d), ctx.reshape(n_ctx, d)], axis=0)
    n_all = n_lat + n_ctx

    q, k, v = _qkv_project(xall, mod[0], attn_w_qkv[0], attn_q_norm[0], attn_k_norm[0], n_lat, seq, b)
    o = _attention(q, k, v, b, seq, cl, n_lat)
    x1, h2, eidx, gates, rank, counts = _post_mixer(
        o, xall, mod[0], ln_g[0, 0], ln_b[0, 0], attn_w_o[0], router_w[0], router_bias[0],
        n_all, n_lat, seq, b, dn_alpha)
    xall = _moe(x1, h2, eidx, gates, rank, counts, mod[0], ln_g[0, 1], ln_b[0, 1],
                exp_w_gate_up, exp_w_down, 0, shared_w_gate_up[0], shared_w_down[0],
                n_all, n_lat, seq, b, dn_alpha)

    zin = _in_project(xall, mod[1], conv_w_in[0], n_lat, n_lat, seq, b)
    a = _short_conv(zin, conv_taps[0], b, seq, d)
    x1, h2, eidx, gates, rank, counts = _post_mixer(
        a, xall, mod[1], ln_g[1, 0], ln_b[1, 0], conv_w_out[0], router_w[1], router_bias[1],
        n_lat, n_lat, seq, b, dn_alpha)
    out = _moe(x1, h2, eidx, gates, rank, counts, mod[1], ln_g[1, 1], ln_b[1, 1],
               exp_w_gate_up, exp_w_down, 1, shared_w_gate_up[1], shared_w_down[1],
               n_lat, n_lat, seq, b, dn_alpha)
    return out.reshape(b, seq, d)
```

```python
import functools

import jax
import jax.numpy as jnp
from jax import lax
from jax.experimental import pallas as pl
from jax.experimental.pallas import tpu as pltpu

F32 = jnp.float32
BF16 = jnp.bfloat16
I32 = jnp.int32

N_HEADS = 8
N_KV_HEADS = 2
HEAD_DIM = 128
KV_GROUP = N_HEADS // N_KV_HEADS
GRID_W = 64
ROPE_THETA = 10000.0
N_EXPERTS = 256
TOP_K = 8
N_GROUPS = 8
TOPK_GROUPS = 4
PER_GROUP = N_EXPERTS // N_GROUPS
ROUTED_SCALE = 2.5
LN_EPS = 1e-5
QK_EPS = 1e-6
N_MOD = 6
MOD_ROWS = 16

LANES = 128
SUBLANES = 8
VMEM_LIMIT = 56 * 1024 * 1024

ROW_TILE = 512
ATTN_TQ = 256
ATTN_TK = 512
MOE_BM = 256
GMM_IN_SLOTS = 4
GMM_OUT_SLOTS = 3
DISPATCH_TOK = 512
CONV_LANES = 128

HIGHEST = lax.Precision.HIGHEST


def _cparams(sem):
    return pltpu.CompilerParams(dimension_semantics=sem, vmem_limit_bytes=VMEM_LIMIT)


def _silu(v):
    return v * jax.nn.sigmoid(v)


def _mod_kernel(c_ref, w_ref, b_ref, o_ref):
    s = _silu(c_ref[...])
    o_ref[...] = jnp.dot(s, w_ref[...], precision=HIGHEST, preferred_element_type=F32) + b_ref[...]


def _modulation(cond, ada_w, ada_b):
    depth, d, nd = ada_w.shape
    tn = 1536
    out = pl.pallas_call(
        _mod_kernel,
        grid=(depth, nd // tn),
        in_specs=[
            pl.BlockSpec((MOD_ROWS, d), lambda l, j: (0, 0)),
            pl.BlockSpec((None, d, tn), lambda l, j: (l, 0, j)),
            pl.BlockSpec((None, 1, tn), lambda l, j: (l, 0, j)),
        ],
        out_specs=pl.BlockSpec((None, MOD_ROWS, tn), lambda l, j: (l, 0, j)),
        out_shape=jax.ShapeDtypeStruct((depth, MOD_ROWS, nd), F32),
        compiler_params=_cparams(("arbitrary", "arbitrary")),
        name="adaln_modulation",
    )(cond, ada_w, ada_b.reshape(depth, 1, nd))
    return out.reshape(depth, MOD_ROWS * N_MOD, 1, d)


def _mod_spec(comp, tm, n_lat, seq, ctx_row, d):
    def index(i, *_):
        row0 = i * tm
        r = jnp.where(row0 < n_lat, row0 // seq, ctx_row)
        return (r * N_MOD + comp, 0, 0)

    return pl.BlockSpec((None, 1, d), index)


def _qkv_kernel(x_ref, shift_ref, scale_ref, w_ref, qg_ref, kg_ref, cos_ref, sin_ref,
                q_ref, k_ref, v_ref):
    h = (x_ref[...] * (1.0 + scale_ref[...]) + shift_ref[...]).astype(BF16)
    qkv = jnp.dot(h, w_ref[...], preferred_element_type=F32)
    cos = cos_ref[...]
    sin = sin_ref[...]
    hq = N_HEADS * HEAD_DIM
    kd = N_KV_HEADS * HEAD_DIM

    def norm_rope(t, g, post):
        t = t * lax.rsqrt(jnp.mean(t * t, axis=-1, keepdims=True) + QK_EPS) * g
        t = t * cos + pltpu.roll(t, HEAD_DIM // 2, axis=1) * sin
        return (t * post).astype(BF16)

    for hd in range(N_HEADS):
        sl = slice(hd * HEAD_DIM, (hd + 1) * HEAD_DIM)
        q_ref[:, sl] = norm_rope(qkv[:, sl], qg_ref[...], HEAD_DIM ** -0.5)
    for hd in range(N_KV_HEADS):
        sl = slice(hd * HEAD_DIM, (hd + 1) * HEAD_DIM)
        k_ref[:, sl] = norm_rope(qkv[:, hq + hd * HEAD_DIM: hq + (hd + 1) * HEAD_DIM], kg_ref[...], 1.0)
    v_ref[...] = qkv[:, hq + kd:].astype(BF16)


def _rope_tables(seq, tm):
    rows = seq // GRID_W
    row = jnp.repeat(jnp.arange(rows, dtype=F32), GRID_W)
    col = jnp.tile(jnp.arange(GRID_W, dtype=F32), rows)
    axis_dim = HEAD_DIM // 2
    freqs = ROPE_THETA ** (-jnp.arange(0, axis_dim, 2, dtype=F32) / axis_dim)
    ang = jnp.concatenate([row[:, None] * freqs, col[:, None] * freqs], axis=-1)
    cos = jnp.concatenate([jnp.cos(ang), jnp.cos(ang)], axis=-1)
    sin = jnp.concatenate([-jnp.sin(ang), jnp.sin(ang)], axis=-1)
    cos = jnp.concatenate([cos, jnp.ones((tm, HEAD_DIM), F32)], axis=0)
    sin = jnp.concatenate([sin, jnp.zeros((tm, HEAD_DIM), F32)], axis=0)
    return cos.reshape(seq // tm + 1, tm, HEAD_DIM), sin.reshape(seq // tm + 1, tm, HEAD_DIM)


def _qkv_project(xall, mod, w_qkv, q_g, k_g, n_lat, seq, ctx_row):
    t, d = xall.shape
    tm = ROW_TILE
    hq = N_HEADS * HEAD_DIM
    kd = N_KV_HEADS * HEAD_DIM
    perm = jnp.concatenate([jnp.arange(0, HEAD_DIM, 2), jnp.arange(1, HEAD_DIM, 2)])
    cols = jnp.concatenate([hd * HEAD_DIM + perm for hd in range(N_HEADS + N_KV_HEADS)]
                           + [jnp.arange(hq + kd, hq + 2 * kd)])
    w = w_qkv[:, cols].astype(BF16)
    cos, sin = _rope_tables(seq, tm)
    n_pos = seq // tm

    def pos_index(i):
        row0 = i * tm
        return (jnp.where(row0 < n_lat, (row0 % seq) // tm, n_pos), 0, 0)

    const = lambda i: (0, 0)
    return pl.pallas_call(
        _qkv_kernel,
        grid=(t // tm,),
        in_specs=[
            pl.BlockSpec((tm, d), lambda i: (i, 0)),
            _mod_spec(0, tm, n_lat, seq, ctx_row, d),
            _mod_spec(1, tm, n_lat, seq, ctx_row, d),
            pl.BlockSpec(w.shape, const),
            pl.BlockSpec((1, HEAD_DIM), const),
            pl.BlockSpec((1, HEAD_DIM), const),
            pl.BlockSpec((None, tm, HEAD_DIM), pos_index),
            pl.BlockSpec((None, tm, HEAD_DIM), pos_index),
        ],
        out_specs=[
            pl.BlockSpec((tm, hq), lambda i: (i, 0)),
            pl.BlockSpec((tm, kd), lambda i: (i, 0)),
            pl.BlockSpec((tm, kd), lambda i: (i, 0)),
        ],
        out_shape=[
            jax.ShapeDtypeStruct((t, hq), BF16),
            jax.ShapeDtypeStruct((t, kd), BF16),
            jax.ShapeDtypeStruct((t, kd), BF16),
        ],
        compiler_params=_cparams(("parallel",)),
        name="qkv_norm_rope",
    )(xall, mod, mod, w, q_g[perm].reshape(1, HEAD_DIM), k_g[perm].reshape(1, HEAD_DIM), cos, sin)


def _attn_kernel(*refs, n_lat_chunks, tk):
    if n_lat_chunks:
        q_ref, kc_ref, vc_ref, kl_ref, vl_ref, o_ref = refs
    else:
        q_ref, kc_ref, vc_ref, o_ref = refs
    tq = q_ref.shape[0]
    q = jnp.concatenate([q_ref[:, h * HEAD_DIM:(h + 1) * HEAD_DIM] for h in range(KV_GROUP)], axis=0)
    rows = KV_GROUP * tq

    def chunk(k, v, m, l, acc):
        s = lax.dot_general(q, k, (((1,), (1,)), ((), ())), preferred_element_type=F32)
        m_new = jnp.maximum(m, jnp.max(s, axis=-1, keepdims=True))
        p = jnp.exp(s - m_new)
        a = jnp.exp(m - m_new)
        l = a * l + jnp.sum(p, axis=-1, keepdims=True)
        acc = a * acc + jnp.dot(p.astype(BF16), v, preferred_element_type=F32)
        return m_new, l, acc

    m = jnp.full((rows, 1), -jnp.inf, F32)
    l = jnp.zeros((rows, 1), F32)
    acc = jnp.zeros((rows, HEAD_DIM), F32)
    m, l, acc = chunk(kc_ref[...], vc_ref[...], m, l, acc)
    for c in range(n_lat_chunks):
        m, l, acc = chunk(kl_ref[c * tk:(c + 1) * tk, :], vl_ref[c * tk:(c + 1) * tk, :], m, l, acc)
    o = (acc / l).astype(BF16)
    for h in range(KV_GROUP):
        o_ref[:, h * HEAD_DIM:(h + 1) * HEAD_DIM] = o[h * tq:(h + 1) * tq]


def _attention(q, k, v, b, seq, cl, n_lat):
    t = q.shape[0]
    gw = KV_GROUP * HEAD_DIM
    tq = min(ATTN_TQ, seq)
    tk = min(ATTN_TK, seq)
    nq = seq // tq
    ctx_blk0 = n_lat // cl
    hq = N_HEADS * HEAD_DIM

    ctx_kv = pl.BlockSpec((cl, HEAD_DIM), lambda bi, g, qi: (ctx_blk0 + bi, g))
    lat_kv = pl.BlockSpec((seq, HEAD_DIM), lambda bi, g, qi: (bi, g))
    lat_q = pl.BlockSpec((tq, gw), lambda bi, g, qi: (bi * nq + qi, g))
    o_lat = pl.pallas_call(
        functools.partial(_attn_kernel, n_lat_chunks=seq // tk, tk=tk),
        grid=(b, N_KV_HEADS, nq),
        in_specs=[lat_q, ctx_kv, ctx_kv, lat_kv, lat_kv],
        out_specs=lat_q,
        out_shape=jax.ShapeDtypeStruct((n_lat, hq), BF16),
        compiler_params=_cparams(("parallel", "parallel", "arbitrary")),
        name="attention_latent",
    )(q, k, v, k, v)

    ctx_q = pl.BlockSpec((cl, gw), lambda bi, g: (ctx_blk0 + bi, g))
    ctx_kv2 = pl.BlockSpec((cl, HEAD_DIM), lambda bi, g: (ctx_blk0 + bi, g))
    o_ctx = pl.pallas_call(
        functools.partial(_attn_kernel, n_lat_chunks=0, tk=tk),
        grid=(b, N_KV_HEADS),
        in_specs=[ctx_q, ctx_kv2, ctx_kv2],
        out_specs=pl.BlockSpec((cl, gw), lambda bi, g: (bi, g)),
        out_shape=jax.ShapeDtypeStruct((t - n_lat, hq), BF16),
        compiler_params=_cparams(("parallel", "parallel")),
        name="attention_context",
    )(q, k, v)
    return jnp.concatenate([o_lat, o_ctx], axis=0)


def _in_proj_kernel(x_ref, shift_ref, scale_ref, w_ref, o_ref):
    h = (x_ref[...] * (1.0 + scale_ref[...]) + shift_ref[...]).astype(BF16)
    o_ref[...] = jnp.dot(h, w_ref[...], preferred_element_type=F32)


def _in_project(xall, mod, w_in, n_rows, n_lat, seq, ctx_row):
    d = xall.shape[1]
    n_out = w_in.shape[1]
    tm = ROW_TILE
    return pl.pallas_call(
        _in_proj_kernel,
        grid=(n_rows // tm,),
        in_specs=[
            pl.BlockSpec((tm, d), lambda i: (i, 0)),
            _mod_spec(0, tm, n_lat, seq, ctx_row, d),
            _mod_spec(1, tm, n_lat, seq, ctx_row, d),
            pl.BlockSpec(w_in.shape, lambda i: (0, 0)),
        ],
        out_specs=pl.BlockSpec((tm, n_out), lambda i: (i, 0)),
        out_shape=jax.ShapeDtypeStruct((n_rows, n_out), F32),
        compiler_params=_cparams(("parallel",)),
        name="conv_in_proj",
    )(xall, mod, mod, w_in.astype(BF16))


def _conv_kernel(bg_ref, cg_ref, v_ref, taps_ref, o_ref):
    u = cg_ref[...] * v_ref[...]
    n = u.shape[0]
    pos = lax.broadcasted_iota(I32, u.shape, 0)
    prev = jnp.where(pos == 0, 0.0, pltpu.roll(u, 1, axis=0))
    nxt = jnp.where(pos == n - 1, 0.0, pltpu.roll(u, n - 1, axis=0))
    conv = prev * taps_ref[0:1, :] + u * taps_ref[1:2, :] + nxt * taps_ref[2:3, :]
    o_ref[...] = (bg_ref[...] * conv).astype(BF16)


def _short_conv(zin, taps, n_seqs, seq, d):
    tc = CONV_LANES
    nj = d // tc
    return pl.pallas_call(
        _conv_kernel,
        grid=(n_seqs, nj),
        in_specs=[
            pl.BlockSpec((seq, tc), lambda s, j: (s, j)),
            pl.BlockSpec((seq, tc), lambda s, j: (s, nj + j)),
            pl.BlockSpec((seq, tc), lambda s, j: (s, 2 * nj + j)),
            pl.BlockSpec((taps.shape[0], tc), lambda s, j: (0, j)),
        ],
        out_specs=pl.BlockSpec((seq, tc), lambda s, j: (s, j)),
        out_shape=jax.ShapeDtypeStruct((n_seqs * seq, d), BF16),
        compiler_params=_cparams(("parallel", "parallel")),
        name="short_conv",
    )(zin, zin, zin, taps)


def _layer_norm(z, g, b):
    mu = jnp.mean(z, axis=-1, keepdims=True)
    zc = z - mu
    var = jnp.mean(zc * zc, axis=-1, keepdims=True)
    return zc * lax.rsqrt(var + LN_EPS) * g + b


def _post_kernel(a_ref, x_ref, gate_ref, lng_ref, lnb_ref, shift_ref, scale_ref, w_ref,
                 rwt_ref, rb_ref, tri_ref,
                 x1_ref, h2_ref, eidx_ref, gates_ref, rank_ref, cnt_ref, *, dn_alpha):
    tm = a_ref.shape[0]
    d = x_ref.shape[1]
    y = jnp.dot(a_ref[...], w_ref[...], preferred_element_type=F32)
    x1 = _layer_norm(dn_alpha * x_ref[...] + gate_ref[...] * y, lng_ref[...], lnb_ref[...])
    x1_ref[...] = x1
    h2 = x1 * (1.0 + scale_ref[...]) + shift_ref[...]
    h2_ref[...] = h2

    logits = lax.dot_general(rwt_ref[...], h2, (((1,), (1,)), ((), ())),
                             precision=HIGHEST, preferred_element_type=F32)
    scores = jax.nn.sigmoid(logits)
    biased = scores + rb_ref[...]
    neg = -jnp.inf
    big = jnp.int32(1 << 30)
    row = lax.broadcasted_iota(I32, (N_EXPERTS, tm), 0)

    def argmax_rows(vals, idx):
        mx = jnp.max(vals, axis=0, keepdims=True)
        return mx, jnp.min(jnp.where(vals == mx, idx, big), axis=0, keepdims=True)

    gs = []
    grp_row = lax.broadcasted_iota(I32, (PER_GROUP, tm), 0)
    for g in range(N_GROUPS):
        bg = biased[g * PER_GROUP:(g + 1) * PER_GROUP]
        ig = grp_row + g * PER_GROUP
        m1, i1 = argmax_rows(bg, ig)
        m2 = jnp.max(jnp.where(ig == i1, neg, bg), axis=0, keepdims=True)
        gs.append(m1 + m2)
    gsc = jnp.concatenate(gs, axis=0)
    grow = lax.broadcasted_iota(I32, (N_GROUPS, tm), 0)
    gsel = jnp.zeros((N_GROUPS, tm), F32)
    for _ in range(TOPK_GROUPS):
        _, gi = argmax_rows(gsc, grow)
        hit = grow == gi
        gsel = jnp.where(hit, 1.0, gsel)
        gsc = jnp.where(hit, neg, gsc)
    cur = jnp.concatenate(
        [jnp.where(gsel[g:g + 1] > 0.0, biased[g * PER_GROUP:(g + 1) * PER_GROUP], neg)
         for g in range(N_GROUPS)], axis=0)

    onehot = jnp.zeros((N_EXPERTS, tm), F32)
    idxs, gvals = [], []
    for _ in range(TOP_K):
        _, ei = argmax_rows(cur, row)
        hit = row == ei
        gvals.append(jnp.sum(jnp.where(hit, scores, 0.0), axis=0, keepdims=True))
        idxs.append(ei)
        onehot = jnp.where(hit, 1.0, onehot)
        cur = jnp.where(hit, neg, cur)
    gv = jnp.concatenate(gvals, axis=0)
    gates_ref[...] = gv / jnp.sum(gv, axis=0, keepdims=True) * ROUTED_SCALE
    eidx_ref[...] = jnp.concatenate(idxs, axis=0)

    @pl.when(pl.program_id(0) == 0)
    def _():
        cnt_ref[...] = jnp.zeros_like(cnt_ref)

    prefix = jnp.dot(onehot.astype(BF16), tri_ref[...], preferred_element_type=F32)
    pos = prefix + cnt_ref[...]
    rank_ref[...] = jnp.concatenate(
        [jnp.sum(jnp.where(row == ei, pos, 0.0), axis=0, keepdims=True) for ei in idxs],
        axis=0).astype(I32)
    cnt_ref[...] = cnt_ref[...] + jnp.sum(onehot, axis=1, keepdims=True)


def _post_mixer(a, xall, mod, ln_g, ln_b, w, router_w, router_bias, n_rows, n_lat, seq, ctx_row,
                dn_alpha):
    d = xall.shape[1]
    tm = ROW_TILE
    tri =(lax.broadcasted_iota(I32, (tm, tm), 0) < lax.broadcasted_iota(I32, (tm, tm), 1)).astype(BF16)
    const = lambda i: (0, 0)
    row_blk = pl.BlockSpec((tm, d), lambda i: (i, 0))
    k_blk = pl.BlockSpec((TOP_K, tm), lambda i: (0, i))
    return pl.pallas_call(
        functools.partial(_post_kernel, dn_alpha=dn_alpha),
        grid=(n_rows // tm,),
        in_specs=[
            row_blk, row_blk,
            _mod_spec(2, tm, n_lat, seq, ctx_row, d),
            pl.BlockSpec((1, d), const), pl.BlockSpec((1, d), const),
            _mod_spec(3, tm, n_lat, seq, ctx_row, d),
            _mod_spec(4, tm, n_lat, seq, ctx_row, d),
            pl.BlockSpec(w.shape, const),
            pl.BlockSpec((N_EXPERTS, d), const),
            pl.BlockSpec((N_EXPERTS, 1), const),
            pl.BlockSpec((tm, tm), const),
        ],
        out_specs=[
            row_blk, row_blk,
            k_blk, k_blk, k_blk,
            pl.BlockSpec((N_EXPERTS, 1), const),
        ],
        out_shape=[
            jax.ShapeDtypeStruct((n_rows, d), F32),
            jax.ShapeDtypeStruct((n_rows, d), F32),
            jax.ShapeDtypeStruct((TOP_K, n_rows), I32),
            jax.ShapeDtypeStruct((TOP_K, n_rows), F32),
            jax.ShapeDtypeStruct((TOP_K, n_rows), I32),
            jax.ShapeDtypeStruct((N_EXPERTS, 1), F32),
        ],
        compiler_params=_cparams(("arbitrary",)),
        name="post_mixer_router",
    )(a, xall, mod, ln_g.reshape(1, d), ln_b.reshape(1, d), mod, mod, w.astype(BF16),
      router_w.T, router_bias.reshape(N_EXPERTS, 1), tri)


def _slots_kernel(pstart_ref, eidx_ref, rank_ref, dest_ref):
    e = eidx_ref[...]

    def pick(i, acc):
        return jnp.where(e == i, pstart_ref[i], acc)

    dest_ref[...] = lax.fori_loop(0, N_EXPERTS, pick, jnp.zeros_like(e)) + rank_ref[...]


def _slots(pstarts, eidx, rank):
    n_tok = eidx.shape[1]
    tn = 2048 if n_tok % 2048 == 0 else ROW_TILE
    blk = pl.BlockSpec((TOP_K, tn), lambda i, ps: (0, i))
    return pl.pallas_call(
        _slots_kernel,
        grid_spec=pltpu.PrefetchScalarGridSpec(
            num_scalar_prefetch=1, grid=(n_tok // tn,), in_specs=[blk, blk], out_specs=blk),
        out_shape=jax.ShapeDtypeStruct((TOP_K, n_tok), I32),
        compiler_params=_cparams(("arbitrary",)),
        name="moe_slots",
    )(pstarts, eidx, rank)


def _dispatch_kernel(pends_ref, pcounts_ref, dest_ref, h_ref, xs_hbm, zbuf, sem):
    td = dest_ref.shape[1]
    bm = zbuf.shape[0]

    @pl.when(pl.program_id(0) == 0)
    def _():
        zbuf[...] = jnp.zeros_like(zbuf)

        def tail(e):
            return xs_hbm.at[pl.ds(pl.multiple_of(pends_ref[e] - bm, bm), bm)]

        def zstart(e, carry):
            @pl.when(pcounts_ref[e] > 0)
            def _():
                pltpu.make_async_copy(zbuf, tail(e), sem).start()
            return carry

        def zwait(e, carry):
            @pl.when(pcounts_ref[e] > 0)
            def _():
                pltpu.make_async_copy(zbuf, tail(e), sem).wait()
            return carry

        lax.fori_loop(0, N_EXPERTS, zstart, 0)
        lax.fori_loop(0, N_EXPERTS, zwait, 0)

        def unused(j):
            return xs_hbm.at[pl.ds(pl.multiple_of(j * bm, bm), bm)]

        def ustart(j, carry):
            pltpu.make_async_copy(zbuf, unused(j), sem).start()
            return carry

        def uwait(j, carry):
            pltpu.make_async_copy(zbuf, unused(j), sem).wait()
            return carry

        first_unused = pends_ref[N_EXPERTS - 1] // bm
        lax.fori_loop(first_unused, xs_hbm.shape[0] // bm, ustart, 0)
        lax.fori_loop(first_unused, xs_hbm.shape[0] // bm, uwait, 0)

    def issue(t, carry):
        for k in range(TOP_K):
            pltpu.make_async_copy(h_ref.at[pl.ds(t, 1)], xs_hbm.at[pl.ds(dest_ref[k, t], 1)], sem).start()
        return carry

    lax.fori_loop(0, td, issue, 0)
    pltpu.make_async_copy(xs_hbm.at[pl.ds(0, TOP_K * td)], xs_hbm.at[pl.ds(0, TOP_K * td)], sem).wait()


def _dispatch(dest, h2, pends, pcounts, n_slots):
    n_tok, d = h2.shape
    td = DISPATCH_TOK
    grid_spec = pltpu.PrefetchScalarGridSpec(
        num_scalar_prefetch=2,
        grid=(n_tok // td,),
        in_specs=[
            pl.BlockSpec((TOP_K, td), lambda i, pe, pc: (0, i), memory_space=pltpu.SMEM),
            pl.BlockSpec((td, d), lambda i, pe, pc: (i, 0)),
        ],
        out_specs=pl.BlockSpec(memory_space=pl.ANY),
        scratch_shapes=[pltpu.VMEM((MOE_BM, d), h2.dtype), pltpu.SemaphoreType.DMA(())],
    )
    return pl.pallas_call(
        _dispatch_kernel,
        grid_spec=grid_spec,
        out_shape=jax.ShapeDtypeStruct((n_slots, d), h2.dtype),
        compiler_params=_cparams(("arbitrary",)),
        name="moe_dispatch",
    )(pends, pcounts, dest, h2)


def _gmm_kernel(blk0_ref, nblk_ref, n_used_ref, xs_hbm, wgu_ref, wd_ref, ys_hbm,
                xbuf, ybuf, wgu_bf, wd_bf, sem_in, sem_out):
    e = pl.program_id(0)
    n_in, bm = xbuf.shape[:2]
    n_out = ybuf.shape[0]
    ff = wd_ref.shape[0]
    n_used = n_used_ref[0]

    def rows(b):
        return pl.ds(pl.multiple_of(b * bm, bm), bm)

    def x_copy(b):
        return pltpu.make_async_copy(xs_hbm.at[rows(b)], xbuf.at[b % n_in], sem_in.at[b % n_in])

    def y_copy(b, slot):
        return pltpu.make_async_copy(ybuf.at[slot], ys_hbm.at[rows(b)], sem_out.at[slot])

    @pl.when(e == 0)
    def _():
        for b in range(n_in - 1):
            @pl.when(b < n_used)
            def _():
                x_copy(b).start()

    nb = nblk_ref[e]

    @pl.when(nb > 0)
    def _():
        wgu_bf[...] = wgu_ref[...].astype(BF16)
        wd_bf[...] = wd_ref[...].astype(BF16)

    def block(j, carry):
        b = blk0_ref[e] + j
        slot = b % n_out
        x_copy(b).wait()

        @pl.when(b + n_in - 1 < n_used)
        def _():
            x_copy(b + n_in - 1).start()

        @pl.when(b >= n_out)
        def _():
            y_copy(b - n_out, slot).wait()

        gu = jnp.dot(xbuf[b % n_in].astype(BF16), wgu_bf[...], preferred_element_type=F32)
        act = (_silu(gu[:, :ff]) * gu[:, ff:]).astype(BF16)
        ybuf[slot] = jnp.dot(act, wd_bf[...], preferred_element_type=F32)
        y_copy(b, slot).start()
        return carry

    lax.fori_loop(0, nb, block, 0)

    @pl.when(e == pl.num_programs(0) - 1)
    def _():
        for back in range(n_out, 0, -1):
            @pl.when(n_used >= back)
            def _():
                y_copy(n_used - back, (n_used - back) % n_out).wait()

        ybuf[0] = jnp.zeros(ybuf.shape[1:], ybuf.dtype)
        n_total = ys_hbm.shape[0] // bm

        def zstart(b, carry):
            y_copy(b, 0).start()
            return carry

        def zwait(b, carry):
            y_copy(b, 0).wait()
            return carry

        lax.fori_loop(n_used, n_total, zstart, 0)
        lax.fori_loop(n_used, n_total, zwait, 0)


def _grouped_ffn(xs, blk0, nblk, n_used, wgu, wd, layer):
    n_slots = xs.shape[0]
    bm = MOE_BM
    n_exp, d, ff2 = wgu.shape[1:]
    ff = wd.shape[2]
    grid_spec = pltpu.PrefetchScalarGridSpec(
        num_scalar_prefetch=3,
        grid=(n_exp,),
        in_specs=[
            pl.BlockSpec(memory_space=pl.ANY),
            pl.BlockSpec((None, None, d, ff2), lambda e, b0, nb, nu: (layer, e, 0, 0)),
            pl.BlockSpec((None, None, ff, d), lambda e, b0, nb, nu: (layer, e, 0, 0)),
        ],
        out_specs=pl.BlockSpec(memory_space=pl.ANY),
        scratch_shapes=[
            pltpu.VMEM((GMM_IN_SLOTS, bm, d), F32), pltpu.VMEM((GMM_OUT_SLOTS, bm, d), F32),
            pltpu.VMEM((d, ff2), BF16), pltpu.VMEM((ff, d), BF16),
            pltpu.SemaphoreType.DMA((GMM_IN_SLOTS,)), pltpu.SemaphoreType.DMA((GMM_OUT_SLOTS,)),
        ],
    )
    return pl.pallas_call(
        _gmm_kernel,
        grid_spec=grid_spec,
        out_shape=jax.ShapeDtypeStruct((n_slots, d), F32),
        compiler_params=_cparams(("arbitrary",)),
        name="moe_grouped_ffn",
    )(blk0, nblk, n_used, xs, wgu, wd)


def _combine_kernel(dest_ref, gates_ref, ys_hbm, x1_ref, h2_ref, gate_ref, lng_ref, lnb_ref,
                    swgu_ref, swd_ref, o_ref, buf, sem, *, dn_alpha):
    tc = x1_ref.shape[0]
    ff = swd_ref.shape[0]

    def issue(t, carry):
        for k in range(TOP_K):
            pltpu.make_async_copy(ys_hbm.at[pl.ds(dest_ref[k, t], 1)], buf.at[k, pl.ds(t, 1)], sem).start()
        return carry

    lax.fori_loop(0, tc, issue, 0)

    gu = jnp.dot(h2_ref[...].astype(BF16), swgu_ref[...], preferred_element_type=F32)
    act = (_silu(gu[:, :ff]) * gu[:, ff:]).astype(BF16)
    y = jnp.dot(act, swd_ref[...], preferred_element_type=F32)

    for k in range(TOP_K):
        pltpu.make_async_copy(ys_hbm.at[pl.ds(0, tc)], buf.at[k], sem).wait()
    for k in range(TOP_K):
        y = y + gates_ref[:, k:k + 1] * buf[k]
    o_ref[...] = _layer_norm(dn_alpha * x1_ref[...] + gate_ref[...] * y, lng_ref[...], lnb_ref[...])


def _combine(dest, gates_t, ys, x1, h2, mod, ln_g, ln_b, swgu, swd, n_rows, n_lat, seq, ctx_row,
             dn_alpha):
    d = x1.shape[1]
    tc = DISPATCH_TOK
    const = lambda i: (0, 0)
    return pl.pallas_call(
        functools.partial(_combine_kernel, dn_alpha=dn_alpha),
        grid=(n_rows // tc,),
        in_specs=[
            pl.BlockSpec((TOP_K, tc), lambda i: (0, i), memory_space=pltpu.SMEM),
            pl.BlockSpec((tc, TOP_K), lambda i: (i, 0)),
            pl.BlockSpec(memory_space=pl.ANY),
            pl.BlockSpec((tc, d), lambda i: (i, 0)),
            pl.BlockSpec((tc, d), lambda i: (i, 0)),
            _mod_spec(5, tc, n_lat, seq, ctx_row, d),
            pl.BlockSpec((1, d), const), pl.BlockSpec((1, d), const),
            pl.BlockSpec(swgu.shape, const),
            pl.BlockSpec(swd.shape, const),
        ],
        out_specs=pl.BlockSpec((tc, d), lambda i: (i, 0)),
        out_shape=jax.ShapeDtypeStruct((n_rows, d), F32),
        scratch_shapes=[pltpu.VMEM((TOP_K, tc, d), F32), pltpu.SemaphoreType.DMA(())],
        compiler_params=_cparams(("arbitrary",)),
        name="moe_combine",
    )(dest, gates_t, ys, x1, h2, mod, ln_g.reshape(1, d), ln_b.reshape(1, d),
      swgu.astype(BF16), swd.astype(BF16))


def _moe(x1, h2, eidx, gates, rank, counts, mod, ln_g, ln_b, wgu, wd, layer, swgu, swd,
         n_rows, n_lat, seq, ctx_row, dn_alpha):
    bm = MOE_BM
    counts = counts.reshape(N_EXPERTS).astype(I32)
    pcounts = (counts + bm - 1) // bm * bm
    pends = jnp.cumsum(pcounts)
    pstarts = pends - pcounts
    dest = _slots(pstarts, eidx, rank)
    n_blk = (n_rows * TOP_K + N_EXPERTS * (bm - 1) + bm - 1) // bm
    n_used = (pends[-1] // bm).astype(I32).reshape(1)
    xs = _dispatch(dest, h2, pends, pcounts, n_blk * bm)
    ys = _grouped_ffn(xs, pstarts // bm, pcounts // bm, n_used, wgu, wd, layer)
    return _combine(dest, gates.T, ys, x1, h2, mod, ln_g, ln_b, swgu, swd,
                    n_rows, n_lat, seq, ctx_row, dn_alpha)


def kernel(x, c, ctx, c_ctx, ada_w, ada_b, ln_g, ln_b, attn_w_qkv, attn_q_norm, attn_k_norm, attn_w_o, conv_w_in, conv_taps, conv_w_out, router_w, router_bias, exp_w_gate_up, exp_w_down, shared_w_gate_up, shared_w_down):
    b, seq, d = x.shape
    cl = ctx.shape[1]
    depth = ada_w.shape[0]
    n_lat = b * seq
    n_ctx = b * cl
    dn_alpha = (2 * depth) ** 0.25
    assert depth == 2 and b < MOD_ROWS
    assert seq % ROW_TILE == 0 and n_ctx % ROW_TILE == 0 and seq % GRID_W == 0

    cond = jnp.zeros((MOD_ROWS, d), F32).at[:b].set(c).at[b].set(c_ctx)
    mod = _modulation(cond, ada_w, ada_b)
    xall = jnp.concatenate([x.reshape(n_lat, d), ctx.reshape(n_ctx, d)], axis=0)
    n_all = n_lat + n_ctx

    q, k, v = _qkv_project(xall, mod[0], attn_w_qkv[0], attn_q_norm[0], attn_k_norm[0], n_lat, seq, b)
    o = _attention(q, k, v, b, seq, cl, n_lat)
    x1, h2, eidx, gates, rank, counts = _post_mixer(
        o, xall, mod[0], ln_g[0, 0], ln_b[0, 0], attn_w_o[0], router_w[0], router_bias[0],
        n_all, n_lat, seq, b, dn_alpha)
    xall = _moe(x1, h2, eidx, gates, rank, counts, mod[0], ln_g[0, 1], ln_b[0, 1],
                exp_w_gate_up, exp_w_down, 0, shared_w_gate_up[0], shared_w_down[0],
                n_all, n_lat, seq, b, dn_alpha)

    zin = _in_project(xall, mod[1], conv_w_in[0], n_lat, n_lat, seq, b)
    a = _short_conv(zin, conv_taps[0], b, seq, d)
    x1, h2, eidx, gates, rank, counts = _post_mixer(
        a, xall, mod[1], ln_g[1, 0], ln_b[1, 0], conv_w_out[0], router_w[1], router_bias[1],
        n_lat, n_lat, seq, b, dn_alpha)
    out = _moe(x1, h2, eidx, gates, rank, counts, mod[1], ln_g[1, 1], ln_b[1, 1],
               exp_w_gate_up, exp_w_down, 1, shared_w_gate_up[1], shared_w_down[1],
               n_lat, n_lat, seq, b, dn_alpha)
    return out.reshape(b, seq, d)
```

```python
import functools

import jax
import jax.numpy as jnp
from jax import lax
from jax.experimental import pallas as pl
from jax.experimental.pallas import tpu as pltpu

F32 = jnp.float32
BF16 = jnp.bfloat16
I32 = jnp.int32

N_HEADS = 8
N_KV_HEADS = 2
HEAD_DIM = 128
KV_GROUP = N_HEADS // N_KV_HEADS
GRID_W = 64
ROPE_THETA = 10000.0
N_EXPERTS = 256
TOP_K = 8
N_GROUPS = 8
TOPK_GROUPS = 4
PER_GROUP = N_EXPERTS // N_GROUPS
ROUTED_SCALE = 2.5
LN_EPS = 1e-5
QK_EPS = 1e-6
N_MOD = 6
MOD_ROWS = 16

LANES = 128
SUBLANES = 8
VMEM_LIMIT = 56 * 1024 * 1024

ROW_TILE = 512
ATTN_TQ = 256
ATTN_TK = 2048
MOE_BM = 256
GMM_IN_SLOTS = 4
GMM_OUT_SLOTS = 3
DISPATCH_TOK = 512
CONV_LANES = 128

HIGHEST = lax.Precision.HIGHEST
LOG2_E = 1.4426950408889634


def _cparams(sem):
    return pltpu.CompilerParams(dimension_semantics=sem, vmem_limit_bytes=VMEM_LIMIT)


def _silu(v):
    return v * jax.nn.sigmoid(v)


def _mod_kernel(c_ref, w_ref, b_ref, o_ref):
    s = _silu(c_ref[...])
    o_ref[...] = jnp.dot(s, w_ref[...], precision=HIGHEST, preferred_element_type=F32) + b_ref[...]


def _modulation(cond, ada_w, ada_b):
    depth, d, nd = ada_w.shape
    tn = 1536
    out = pl.pallas_call(
        _mod_kernel,
        grid=(depth, nd // tn),
        in_specs=[
            pl.BlockSpec((MOD_ROWS, d), lambda l, j: (0, 0)),
            pl.BlockSpec((None, d, tn), lambda l, j: (l, 0, j)),
            pl.BlockSpec((None, 1, tn), lambda l, j: (l, 0, j)),
        ],
        out_specs=pl.BlockSpec((None, MOD_ROWS, tn), lambda l, j: (l, 0, j)),
        out_shape=jax.ShapeDtypeStruct((depth, MOD_ROWS, nd), F32),
        compiler_params=_cparams(("arbitrary", "arbitrary")),
        name="adaln_modulation",
    )(cond, ada_w, ada_b.reshape(depth, 1, nd))
    return out.reshape(depth, MOD_ROWS * N_MOD, 1, d)


def _mod_spec(comp, tm, n_lat, seq, ctx_row, d):
    def index(i, *_):
        row0 = i * tm
        r = jnp.where(row0 < n_lat, row0 // seq, ctx_row)
        return (r * N_MOD + comp, 0, 0)

    return pl.BlockSpec((None, 1, d), index)


def _qkv_kernel(x_ref, shift_ref, scale_ref, w_ref, qg_ref, kg_ref, cos_ref, sin_ref,
                q_ref, k_ref, v_ref):
    h = (x_ref[...] * (1.0 + scale_ref[...]) + shift_ref[...]).astype(BF16)
    qkv = jnp.dot(h, w_ref[...], preferred_element_type=F32)
    cos = cos_ref[...]
    sin = sin_ref[...]
    hq = N_HEADS * HEAD_DIM
    kd = N_KV_HEADS * HEAD_DIM

    def norm_rope(t, g, post):
        t = t * lax.rsqrt(jnp.mean(t * t, axis=-1, keepdims=True) + QK_EPS) * g
        t = t * cos + pltpu.roll(t, HEAD_DIM // 2, axis=1) * sin
        return (t * post).astype(BF16)

    for hd in range(N_HEADS):
        sl = slice(hd * HEAD_DIM, (hd + 1) * HEAD_DIM)
        q_ref[:, sl] = norm_rope(qkv[:, sl], qg_ref[...], HEAD_DIM ** -0.5 * LOG2_E)
    for hd in range(N_KV_HEADS):
        sl = slice(hd * HEAD_DIM, (hd + 1) * HEAD_DIM)
        k_ref[:, sl] = norm_rope(qkv[:, hq + hd * HEAD_DIM: hq + (hd + 1) * HEAD_DIM], kg_ref[...], 1.0)
    v_ref[...] = qkv[:, hq + kd:].astype(BF16)


def _rope_tables(seq, tm):
    rows = seq // GRID_W
    row = jnp.repeat(jnp.arange(rows, dtype=F32), GRID_W)
    col = jnp.tile(jnp.arange(GRID_W, dtype=F32), rows)
    axis_dim = HEAD_DIM // 2
    freqs = ROPE_THETA ** (-jnp.arange(0, axis_dim, 2, dtype=F32) / axis_dim)
    ang = jnp.concatenate([row[:, None] * freqs, col[:, None] * freqs], axis=-1)
    cos = jnp.concatenate([jnp.cos(ang), jnp.cos(ang)], axis=-1)
    sin = jnp.concatenate([-jnp.sin(ang), jnp.sin(ang)], axis=-1)
    cos = jnp.concatenate([cos, jnp.ones((tm, HEAD_DIM), F32)], axis=0)
    sin = jnp.concatenate([sin, jnp.zeros((tm, HEAD_DIM), F32)], axis=0)
    return cos.reshape(seq // tm + 1, tm, HEAD_DIM), sin.reshape(seq // tm + 1, tm, HEAD_DIM)


def _qkv_project(xall, mod, w_qkv, q_g, k_g, n_lat, seq, ctx_row):
    t, d = xall.shape
    tm = ROW_TILE
    hq = N_HEADS * HEAD_DIM
    kd = N_KV_HEADS * HEAD_DIM
    perm = jnp.concatenate([jnp.arange(0, HEAD_DIM, 2), jnp.arange(1, HEAD_DIM, 2)])
    cols = jnp.concatenate([hd * HEAD_DIM + perm for hd in range(N_HEADS + N_KV_HEADS)]
                           + [jnp.arange(hq + kd, hq + 2 * kd)])
    w = w_qkv[:, cols].astype(BF16)
    cos, sin = _rope_tables(seq, tm)
    n_pos = seq // tm

    def pos_index(i):
        row0 = i * tm
        return (jnp.where(row0 < n_lat, (row0 % seq) // tm, n_pos), 0, 0)

    const = lambda i: (0, 0)
    return pl.pallas_call(
        _qkv_kernel,
        grid=(t // tm,),
        in_specs=[
            pl.BlockSpec((tm, d), lambda i: (i, 0)),
            _mod_spec(0, tm, n_lat, seq, ctx_row, d),
            _mod_spec(1, tm, n_lat, seq, ctx_row, d),
            pl.BlockSpec(w.shape, const),
            pl.BlockSpec((1, HEAD_DIM), const),
            pl.BlockSpec((1, HEAD_DIM), const),
            pl.BlockSpec((None, tm, HEAD_DIM), pos_index),
            pl.BlockSpec((None, tm, HEAD_DIM), pos_index),
        ],
        out_specs=[
            pl.BlockSpec((tm, hq), lambda i: (i, 0)),
            pl.BlockSpec((tm, kd), lambda i: (i, 0)),
            pl.BlockSpec((tm, kd), lambda i: (i, 0)),
        ],
        out_shape=[
            jax.ShapeDtypeStruct((t, hq), BF16),
            jax.ShapeDtypeStruct((t, kd), BF16),
            jax.ShapeDtypeStruct((t, kd), BF16),
        ],
        compiler_params=_cparams(("parallel",)),
        name="qkv_norm_rope",
    )(xall, mod, mod, w, q_g[perm].reshape(1, HEAD_DIM), k_g[perm].reshape(1, HEAD_DIM), cos, sin)


def _attn_kernel(*refs, n_lat_chunks, tk):
    if n_lat_chunks:
        q_ref, kc_ref, vc_ref, kl_ref, vl_ref, o_ref = refs
    else:
        q_ref, kc_ref, vc_ref, o_ref = refs
    tq = q_ref.shape[0]
    q = jnp.concatenate([q_ref[:, h * HEAD_DIM:(h + 1) * HEAD_DIM] for h in range(KV_GROUP)], axis=0)
    rows = KV_GROUP * tq

    def chunk(k, v, m, l, acc):
        s = lax.dot_general(q, k, (((1,), (1,)), ((), ())), preferred_element_type=F32)
        m_new = jnp.maximum(m, jnp.max(s, axis=-1, keepdims=True))
        p = jnp.exp2(s - m_new)
        a = jnp.exp2(m - m_new)
        l = a * l + jnp.sum(p, axis=-1, keepdims=True)
        acc = a * acc + jnp.dot(p.astype(BF16), v, preferred_element_type=F32)
        return m_new, l, acc

    m = jnp.full((rows, 1), -jnp.inf, F32)
    l = jnp.zeros((rows, 1), F32)
    acc = jnp.zeros((rows, HEAD_DIM), F32)
    m, l, acc = chunk(kc_ref[...], vc_ref[...], m, l, acc)
    for c in range(n_lat_chunks):
        m, l, acc = chunk(kl_ref[c * tk:(c + 1) * tk, :], vl_ref[c * tk:(c + 1) * tk, :], m, l, acc)
    o = (acc / l).astype(BF16)
    for h in range(KV_GROUP):
        o_ref[:, h * HEAD_DIM:(h + 1) * HEAD_DIM] = o[h * tq:(h + 1) * tq]


def _attention(q, k, v, b, seq, cl, n_lat):
    t = q.shape[0]
    gw = KV_GROUP * HEAD_DIM
    tq = min(ATTN_TQ, seq)
    tk = min(ATTN_TK, seq)
    nq = seq // tq
    ctx_blk0 = n_lat // cl
    hq = N_HEADS * HEAD_DIM

    ctx_kv = pl.BlockSpec((cl, HEAD_DIM), lambda bi, g, qi: (ctx_blk0 + bi, g))
    lat_kv = pl.BlockSpec((seq, HEAD_DIM), lambda bi, g, qi: (bi, g))
    lat_q = pl.BlockSpec((tq, gw), lambda bi, g, qi: (bi * nq + qi, g))
    o_lat = pl.pallas_call(
        functools.partial(_attn_kernel, n_lat_chunks=seq // tk, tk=tk),
        grid=(b, N_KV_HEADS, nq),
        in_specs=[lat_q, ctx_kv, ctx_kv, lat_kv, lat_kv],
        out_specs=lat_q,
        out_shape=jax.ShapeDtypeStruct((n_lat, hq), BF16),
        compiler_params=_cparams(("parallel", "parallel", "arbitrary")),
        name="attention_latent",
    )(q, k, v, k, v)

    ctx_q = pl.BlockSpec((cl, gw), lambda bi, g: (ctx_blk0 + bi, g))
    ctx_kv2 = pl.BlockSpec((cl, HEAD_DIM), lambda bi, g: (ctx_blk0 + bi, g))
    o_ctx = pl.pallas_call(
        functools.partial(_attn_kernel, n_lat_chunks=0, tk=tk),
        grid=(b, N_KV_HEADS),
        in_specs=[ctx_q, ctx_kv2, ctx_kv2],
        out_specs=pl.BlockSpec((cl, gw), lambda bi, g: (bi, g)),
        out_shape=jax.ShapeDtypeStruct((t - n_lat, hq), BF16),
        compiler_params=_cparams(("parallel", "parallel")),
        name="attention_context",
    )(q, k, v)
    return jnp.concatenate([o_lat, o_ctx], axis=0)


def _in_proj_kernel(x_ref, shift_ref, scale_ref, w_ref, o_ref):
    h = (x_ref[...] * (1.0 + scale_ref[...]) + shift_ref[...]).astype(BF16)
    o_ref[...] = jnp.dot(h, w_ref[...], preferred_element_type=F32)


def _in_project(xall, mod, w_in, n_rows, n_lat, seq, ctx_row):
    d = xall.shape[1]
    n_out = w_in.shape[1]
    tm = ROW_TILE
    return pl.pallas_call(
        _in_proj_kernel,
        grid=(n_rows // tm,),
        in_specs=[
            pl.BlockSpec((tm, d), lambda i: (i, 0)),
            _mod_spec(0, tm, n_lat, seq, ctx_row, d),
            _mod_spec(1, tm, n_lat, seq, ctx_row, d),
            pl.BlockSpec(w_in.shape, lambda i: (0, 0)),
        ],
        out_specs=pl.BlockSpec((tm, n_out), lambda i: (i, 0)),
        out_shape=jax.ShapeDtypeStruct((n_rows, n_out), F32),
        compiler_params=_cparams(("parallel",)),
        name="conv_in_proj",
    )(xall, mod, mod, w_in.astype(BF16))


def _conv_kernel(bg_ref, cg_ref, v_ref, taps_ref, o_ref):
    u = cg_ref[...] * v_ref[...]
    n = u.shape[0]
    pos = lax.broadcasted_iota(I32, u.shape, 0)
    prev = jnp.where(pos == 0, 0.0, pltpu.roll(u, 1, axis=0))
    nxt = jnp.where(pos == n - 1, 0.0, pltpu.roll(u, n - 1, axis=0))
    conv = prev * taps_ref[0:1, :] + u * taps_ref[1:2, :] + nxt * taps_ref[2:3, :]
    o_ref[...] = (bg_ref[...] * conv).astype(BF16)


def _short_conv(zin, taps, n_seqs, seq, d):
    tc = CONV_LANES
    nj = d // tc
    return pl.pallas_call(
        _conv_kernel,
        grid=(n_seqs, nj),
        in_specs=[
            pl.BlockSpec((seq, tc), lambda s, j: (s, j)),
            pl.BlockSpec((seq, tc), lambda s, j: (s, nj + j)),
            pl.BlockSpec((seq, tc), lambda s, j: (s, 2 * nj + j)),
            pl.BlockSpec((taps.shape[0], tc), lambda s, j: (0, j)),
        ],
        out_specs=pl.BlockSpec((seq, tc), lambda s, j: (s, j)),
        out_shape=jax.ShapeDtypeStruct((n_seqs * seq, d), BF16),
        compiler_params=_cparams(("parallel", "parallel")),
        name="short_conv",
    )(zin, zin, zin, taps)


def _layer_norm(z, g, b):
    mu = jnp.mean(z, axis=-1, keepdims=True)
    zc = z - mu
    var = jnp.mean(zc * zc, axis=-1, keepdims=True)
    return zc * lax.rsqrt(var + LN_EPS) * g + b


def _post_kernel(a_ref, x_ref, gate_ref, lng_ref, lnb_ref, shift_ref, scale_ref, w_ref,
                 rwh_ref, rwl_ref, rb_ref, tri_ref,
                 x1_ref, h2_ref, eidx_ref, gates_ref, rank_ref, cnt_ref, *, dn_alpha):
    tm = a_ref.shape[0]
    d = x_ref.shape[1]
    y = jnp.dot(a_ref[...], w_ref[...], preferred_element_type=F32)
    x1 = _layer_norm(dn_alpha * x_ref[...] + gate_ref[...] * y, lng_ref[...], lnb_ref[...])
    x1_ref[...] = x1
    h2 = x1 * (1.0 + scale_ref[...]) + shift_ref[...]
    h2_ref[...] = h2

    h_hi = h2.astype(BF16)
    h_lo = (h2 - h_hi.astype(F32)).astype(BF16)

    def nt_dot(w, h):
        return lax.dot_general(w, h, (((1,), (1,)), ((), ())), preferred_element_type=F32)

    logits = nt_dot(rwh_ref[...], h_hi) + (nt_dot(rwl_ref[...], h_hi) + nt_dot(rwh_ref[...], h_lo))
    scores = jax.nn.sigmoid(logits)
    biased = scores + rb_ref[...]
    neg = -jnp.inf
    big = jnp.int32(1 << 30)
    row = lax.broadcasted_iota(I32, (N_EXPERTS, tm), 0)

    def argmax_rows(vals, idx):
        mx = jnp.max(vals, axis=0, keepdims=True)
        return mx, jnp.min(jnp.where(vals == mx, idx, big), axis=0, keepdims=True)

    gs = []
    grp_row = lax.broadcasted_iota(I32, (PER_GROUP, tm), 0)
    for g in range(N_GROUPS):
        bg = biased[g * PER_GROUP:(g + 1) * PER_GROUP]
        ig = grp_row + g * PER_GROUP
        m1, i1 = argmax_rows(bg, ig)
        m2 = jnp.max(jnp.where(ig == i1, neg, bg), axis=0, keepdims=True)
        gs.append(m1 + m2)
    gsc = jnp.concatenate(gs, axis=0)
    grow = lax.broadcasted_iota(I32, (N_GROUPS, tm), 0)
    gsel = jnp.zeros((N_GROUPS, tm), F32)
    for _ in range(TOPK_GROUPS):
        _, gi = argmax_rows(gsc, grow)
        hit = grow == gi
        gsel = jnp.where(hit, 1.0, gsel)
        gsc = jnp.where(hit, neg, gsc)
    cur = jnp.concatenate(
        [jnp.where(gsel[g:g + 1] > 0.0, biased[g * PER_GROUP:(g + 1) * PER_GROUP], neg)
         for g in range(N_GROUPS)], axis=0)

    onehot = jnp.zeros((N_EXPERTS, tm), F32)
    idxs, gvals = [], []
    for _ in range(TOP_K):
        _, ei = argmax_rows(cur, row)
        hit = row == ei
        gvals.append(jnp.sum(jnp.where(hit, scores, 0.0), axis=0, keepdims=True))
        idxs.append(ei)
        onehot = jnp.where(hit, 1.0, onehot)
        cur = jnp.where(hit, neg, cur)
    gv = jnp.concatenate(gvals, axis=0)
    gates_ref[...] = gv / jnp.sum(gv, axis=0, keepdims=True) * ROUTED_SCALE
    eidx_ref[...] = jnp.concatenate(idxs, axis=0)

    @pl.when(pl.program_id(0) == 0)
    def _():
        cnt_ref[...] = jnp.zeros_like(cnt_ref)

    prefix = jnp.dot(onehot.astype(BF16), tri_ref[...], preferred_element_type=F32)
    pos = prefix + cnt_ref[...]
    rank_ref[...] = jnp.concatenate(
        [jnp.sum(jnp.where(row == ei, pos, 0.0), axis=0, keepdims=True) for ei in idxs],
        axis=0).astype(I32)
    cnt_ref[...] = cnt_ref[...] + jnp.sum(onehot, axis=1, keepdims=True)


def _post_mixer(a, xall, mod, ln_g, ln_b, w, router_w, router_bias, n_rows, n_lat, seq, ctx_row,
                dn_alpha):
    d = xall.shape[1]
    tm = ROW_TILE
    rwt = router_w.T
    rw_hi = rwt.astype(BF16)
    rw_lo = (rwt - rw_hi.astype(F32)).astype(BF16)
    tri = (lax.broadcasted_iota(I32, (tm, tm), 0) < lax.broadcasted_iota(I32, (tm, tm), 1)).astype(BF16)
    const = lambda i: (0, 0)
    row_blk = pl.BlockSpec((tm, d), lambda i: (i, 0))
    k_blk = pl.BlockSpec((TOP_K, tm), lambda i: (0, i))
    return pl.pallas_call(
        functools.partial(_post_kernel, dn_alpha=dn_alpha),
        grid=(n_rows // tm,),
        in_specs=[
            row_blk, row_blk,
            _mod_spec(2, tm, n_lat, seq, ctx_row, d),
            pl.BlockSpec((1, d), const), pl.BlockSpec((1, d), const),
            _mod_spec(3, tm, n_lat, seq, ctx_row, d),
            _mod_spec(4, tm, n_lat, seq, ctx_row, d),
            pl.BlockSpec(w.shape, const),
            pl.BlockSpec((N_EXPERTS, d), const),
            pl.BlockSpec((N_EXPERTS, d), const),
            pl.BlockSpec((N_EXPERTS, 1), const),
            pl.BlockSpec((tm, tm), const),
        ],
        out_specs=[
            row_blk, row_blk,
            k_blk, k_blk, k_blk,
            pl.BlockSpec((N_EXPERTS, 1), const),
        ],
        out_shape=[
            jax.ShapeDtypeStruct((n_rows, d), F32),
            jax.ShapeDtypeStruct((n_rows, d), F32),
            jax.ShapeDtypeStruct((TOP_K, n_rows), I32),
            jax.ShapeDtypeStruct((TOP_K, n_rows), F32),
            jax.ShapeDtypeStruct((TOP_K, n_rows), I32),
            jax.ShapeDtypeStruct((N_EXPERTS, 1), F32),
        ],
        compiler_params=_cparams(("arbitrary",)),
        name="post_mixer_router",
    )(a, xall, mod, ln_g.reshape(1, d), ln_b.reshape(1, d), mod, mod, w.astype(BF16),
      rw_hi, rw_lo, router_bias.reshape(N_EXPERTS, 1), tri)


def _slots_kernel(pstart_ref, eidx_ref, rank_ref, dest_ref):
    e = eidx_ref[...]

    def pick(i, acc):
        return jnp.where(e == i, pstart_ref[i], acc)

    dest_ref[...] = lax.fori_loop(0, N_EXPERTS, pick, jnp.zeros_like(e)) + rank_ref[...]


def _slots(pstarts, eidx, rank):
    n_tok = eidx.shape[1]
    tn = 2048 if n_tok % 2048 == 0 else ROW_TILE
    blk = pl.BlockSpec((TOP_K, tn), lambda i, ps: (0, i))
    return pl.pallas_call(
        _slots_kernel,
        grid_spec=pltpu.PrefetchScalarGridSpec(
            num_scalar_prefetch=1, grid=(n_tok // tn,), in_specs=[blk, blk], out_specs=blk),
        out_shape=jax.ShapeDtypeStruct((TOP_K, n_tok), I32),
        compiler_params=_cparams(("arbitrary",)),
        name="moe_slots",
    )(pstarts, eidx, rank)


def _dispatch_kernel(pends_ref, pcounts_ref, dest_ref, h_ref, xs_hbm, zbuf, sem):
    td = dest_ref.shape[1]
    bm = zbuf.shape[0]

    @pl.when(pl.program_id(0) == 0)
    def _():
        zbuf[...] = jnp.zeros_like(zbuf)

        def tail(e):
            return xs_hbm.at[pl.ds(pl.multiple_of(pends_ref[e] - bm, bm), bm)]

        def zstart(e, carry):
            @pl.when(pcounts_ref[e] > 0)
            def _():
                pltpu.make_async_copy(zbuf, tail(e), sem).start()
            return carry

        def zwait(e, carry):
            @pl.when(pcounts_ref[e] > 0)
            def _():
                pltpu.make_async_copy(zbuf, tail(e), sem).wait()
            return carry

        lax.fori_loop(0, N_EXPERTS, zstart, 0)
        lax.fori_loop(0, N_EXPERTS, zwait, 0)

        def unused(j):
            return xs_hbm.at[pl.ds(pl.multiple_of(j * bm, bm), bm)]

        def ustart(j, carry):
            pltpu.make_async_copy(zbuf, unused(j), sem).start()
            return carry

        def uwait(j, carry):
            pltpu.make_async_copy(zbuf, unused(j), sem).wait()
            return carry

        first_unused = pends_ref[N_EXPERTS - 1] // bm
        lax.fori_loop(first_unused, xs_hbm.shape[0] // bm, ustart, 0)
        lax.fori_loop(first_unused, xs_hbm.shape[0] // bm, uwait, 0)

    def issue(t, carry):
        for k in range(TOP_K):
            pltpu.make_async_copy(h_ref.at[pl.ds(t, 1)], xs_hbm.at[pl.ds(dest_ref[k, t], 1)], sem).start()
        return carry

    lax.fori_loop(0, td, issue, 0)
    pltpu.make_async_copy(xs_hbm.at[pl.ds(0, TOP_K * td)], xs_hbm.at[pl.ds(0, TOP_K * td)], sem).wait()


def _dispatch(dest, h2, pends, pcounts, n_slots):
    n_tok, d = h2.shape
    td = DISPATCH_TOK
    grid_spec = pltpu.PrefetchScalarGridSpec(
        num_scalar_prefetch=2,
        grid=(n_tok // td,),
        in_specs=[
            pl.BlockSpec((TOP_K, td), lambda i, pe, pc: (0, i), memory_space=pltpu.SMEM),
            pl.BlockSpec((td, d), lambda i, pe, pc: (i, 0)),
        ],
        out_specs=pl.BlockSpec(memory_space=pl.ANY),
        scratch_shapes=[pltpu.VMEM((MOE_BM, d), h2.dtype), pltpu.SemaphoreType.DMA(())],
    )
    return pl.pallas_call(
        _dispatch_kernel,
        grid_spec=grid_spec,
        out_shape=jax.ShapeDtypeStruct((n_slots, d), h2.dtype),
        compiler_params=_cparams(("arbitrary",)),
        name="moe_dispatch",
    )(pends, pcounts, dest, h2)


def _gmm_kernel(blk0_ref, nblk_ref, n_used_ref, xs_hbm, wgu_ref, wd_ref, ys_hbm,
                xbuf, ybuf, wgu_bf, wd_bf, sem_in, sem_out):
    e = pl.program_id(0)
    n_in, bm = xbuf.shape[:2]
    n_out = ybuf.shape[0]
    ff = wd_ref.shape[0]
    n_used = n_used_ref[0]

    def rows(b):
        return pl.ds(pl.multiple_of(b * bm, bm), bm)

    def x_copy(b):
        return pltpu.make_async_copy(xs_hbm.at[rows(b)], xbuf.at[b % n_in], sem_in.at[b % n_in])

    def y_copy(b, slot):
        return pltpu.make_async_copy(ybuf.at[slot], ys_hbm.at[rows(b)], sem_out.at[slot])

    @pl.when(e == 0)
    def _():
        for b in range(n_in - 1):
            @pl.when(b < n_used)
            def _():
                x_copy(b).start()

    nb = nblk_ref[e]

    @pl.when(nb > 0)
    def _():
        wgu_bf[...] = wgu_ref[...].astype(BF16)
        wd_bf[...] = wd_ref[...].astype(BF16)

    def block(j, carry):
        b = blk0_ref[e] + j
        slot = b % n_out
        x_copy(b).wait()

        @pl.when(b + n_in - 1 < n_used)
        def _():
            x_copy(b + n_in - 1).start()

        @pl.when(b >= n_out)
        def _():
            y_copy(b - n_out, slot).wait()

        gu = jnp.dot(xbuf[b % n_in].astype(BF16), wgu_bf[...], preferred_element_type=F32)
        act = (_silu(gu[:, :ff]) * gu[:, ff:]).astype(BF16)
        ybuf[slot] = jnp.dot(act, wd_bf[...], preferred_element_type=F32)
        y_copy(b, slot).start()
        return carry

    lax.fori_loop(0, nb, block, 0)

    @pl.when(e == pl.num_programs(0) - 1)
    def _():
        for back in range(n_out, 0, -1):
            @pl.when(n_used >= back)
            def _():
                y_copy(n_used - back, (n_used - back) % n_out).wait()

        ybuf[0] = jnp.zeros(ybuf.shape[1:], ybuf.dtype)
        n_total = ys_hbm.shape[0] // bm

        def zstart(b, carry):
            y_copy(b, 0).start()
            return carry

        def zwait(b, carry):
            y_copy(b, 0).wait()
            return carry

        lax.fori_loop(n_used, n_total, zstart, 0)
        lax.fori_loop(n_used, n_total, zwait, 0)


def _grouped_ffn(xs, blk0, nblk, n_used, wgu, wd, layer):
    n_slots = xs.shape[0]
    bm = MOE_BM
    n_exp, d, ff2 = wgu.shape[1:]
    ff = wd.shape[2]
    grid_spec = pltpu.PrefetchScalarGridSpec(
        num_scalar_prefetch=3,
        grid=(n_exp,),
        in_specs=[
            pl.BlockSpec(memory_space=pl.ANY),
            pl.BlockSpec((None, None, d, ff2), lambda e, b0, nb, nu: (layer, e, 0, 0)),
            pl.BlockSpec((None, None, ff, d), lambda e, b0, nb, nu: (layer, e, 0, 0)),
        ],
        out_specs=pl.BlockSpec(memory_space=pl.ANY),
        scratch_shapes=[
            pltpu.VMEM((GMM_IN_SLOTS, bm, d), F32), pltpu.VMEM((GMM_OUT_SLOTS, bm, d), F32),
            pltpu.VMEM((d, ff2), BF16), pltpu.VMEM((ff, d), BF16),
            pltpu.SemaphoreType.DMA((GMM_IN_SLOTS,)), pltpu.SemaphoreType.DMA((GMM_OUT_SLOTS,)),
        ],
    )
    return pl.pallas_call(
        _gmm_kernel,
        grid_spec=grid_spec,
        out_shape=jax.ShapeDtypeStruct((n_slots, d), F32),
        compiler_params=_cparams(("arbitrary",)),
        name="moe_grouped_ffn",
    )(blk0, nblk, n_used, xs, wgu, wd)


def _combine_kernel(dest_ref, gates_ref, ys_hbm, x1_ref, h2_ref, gate_ref, lng_ref, lnb_ref,
                    swgu_ref, swd_ref, o_ref, buf, sem, *, dn_alpha):
    tc = x1_ref.shape[0]
    ff = swd_ref.shape[0]

    def issue(t, carry):
        for k in range(TOP_K):
            pltpu.make_async_copy(ys_hbm.at[pl.ds(dest_ref[k, t], 1)], buf.at[k, pl.ds(t, 1)], sem).start()
        return carry

    lax.fori_loop(0, tc, issue, 0)

    gu = jnp.dot(h2_ref[...].astype(BF16), swgu_ref[...], preferred_element_type=F32)
    act = (_silu(gu[:, :ff]) * gu[:, ff:]).astype(BF16)
    y = jnp.dot(act, swd_ref[...], preferred_element_type=F32)

    for k in range(TOP_K):
        pltpu.make_async_copy(ys_hbm.at[pl.ds(0, tc)], buf.at[k], sem).wait()
    for k in range(TOP_K):
        y = y + gates_ref[:, k:k + 1] * buf[k]
    o_ref[...] = _layer_norm(dn_alpha * x1_ref[...] + gate_ref[...] * y, lng_ref[...], lnb_ref[...])


def _combine(dest, gates_t, ys, x1, h2, mod, ln_g, ln_b, swgu, swd, n_rows, n_lat, seq, ctx_row,
             dn_alpha):
    d = x1.shape[1]
    tc = DISPATCH_TOK
    const = lambda i: (0, 0)
    return pl.pallas_call(
        functools.partial(_combine_kernel, dn_alpha=dn_alpha),
        grid=(n_rows // tc,),
        in_specs=[
            pl.BlockSpec((TOP_K, tc), lambda i: (0, i), memory_space=pltpu.SMEM),
            pl.BlockSpec((tc, TOP_K), lambda i: (i, 0)),
            pl.BlockSpec(memory_space=pl.ANY),
            pl.BlockSpec((tc, d), lambda i: (i, 0)),
            pl.BlockSpec((tc, d), lambda i: (i, 0)),
            _mod_spec(5, tc, n_lat, seq, ctx_row, d),
            pl.BlockSpec((1, d), const), pl.BlockSpec((1, d), const),
            pl.BlockSpec(swgu.shape, const),
            pl.BlockSpec(swd.shape, const),
        ],
        out_specs=pl.BlockSpec((tc, d), lambda i: (i, 0)),
        out_shape=jax.ShapeDtypeStruct((n_rows, d), F32),
        scratch_shapes=[pltpu.VMEM((TOP_K, tc, d), F32), pltpu.SemaphoreType.DMA(())],
        compiler_params=_cparams(("arbitrary",)),
        name="moe_combine",
    )(dest, gates_t, ys, x1, h2, mod, ln_g.reshape(1, d), ln_b.reshape(1, d),
      swgu.astype(BF16), swd.astype(BF16))


def _moe(x1, h2, eidx, gates, rank, counts, mod, ln_g, ln_b, wgu, wd, layer, swgu, swd,
         n_rows, n_lat, seq, ctx_row, dn_alpha):
    bm = MOE_BM
    counts = counts.reshape(N_EXPERTS).astype(I32)
    pcounts = (counts + bm - 1) // bm * bm
    pends = jnp.cumsum(pcounts)
    pstarts = pends - pcounts
    dest = _slots(pstarts, eidx, rank)
    n_blk = (n_rows * TOP_K + N_EXPERTS * (bm - 1) + bm - 1) // bm
    n_used = (pends[-1] // bm).astype(I32).reshape(1)
    xs = _dispatch(dest, h2, pends, pcounts, n_blk * bm)
    ys = _grouped_ffn(xs, pstarts // bm, pcounts // bm, n_used, wgu, wd, layer)
    return _combine(dest, gates.T, ys, x1, h2, mod, ln_g, ln_b, swgu, swd,
                    n_rows, n_lat, seq, ctx_row, dn_alpha)


def kernel(x, c, ctx, c_ctx, ada_w, ada_b, ln_g, ln_b, attn_w_qkv, attn_q_norm, attn_k_norm, attn_w_o, conv_w_in, conv_taps, conv_w_out, router_w, router_bias, exp_w_gate_up, exp_w_down, shared_w_gate_up, shared_w_down):
    b, seq, d = x.shape
    cl = ctx.shape[1]
    depth = ada_w.shape[0]
    n_lat = b * seq
    n_ctx = b * cl
    dn_alpha = (2 * depth) ** 0.25
    assert depth == 2 and b < MOD_ROWS
    assert seq % ROW_TILE == 0 and n_ctx % ROW_TILE == 0 and seq % GRID_W == 0

    cond = jnp.zeros((MOD_ROWS, d), F32).at[:b].set(c).at[b].set(c_ctx)
    mod = _modulation(cond, ada_w, ada_b)
    xall = jnp.concatenate([x.reshape(n_lat, d), ctx.reshape(n_ctx, d)], axis=0)
    n_all = n_lat + n_ctx

    q, k, v = _qkv_project(xall, mod[0], attn_w_qkv[0], attn_q_norm[0], attn_k_norm[0], n_lat, seq, b)
    o = _attention(q, k, v, b, seq, cl, n_lat)
    x1, h2, eidx, gates, rank, counts = _post_mixer(
        o, xall, mod[0], ln_g[0, 0], ln_b[0, 0], attn_w_o[0], router_w[0], router_bias[0],
        n_all, n_lat, seq, b, dn_alpha)
    xall = _moe(x1, h2, eidx, gates, rank, counts, mod[0], ln_g[0, 1], ln_b[0, 1],
                exp_w_gate_up, exp_w_down, 0, shared_w_gate_up[0], shared_w_down[0],
                n_all, n_lat, seq, b, dn_alpha)

    zin = _in_project(xall, mod[1], conv_w_in[0], n_lat, n_lat, seq, b)
    a = _short_conv(zin, conv_taps[0], b, seq, d)
    x1, h2, eidx, gates, rank, counts = _post_mixer(
        a, xall, mod[1], ln_g[1, 0], ln_b[1, 0], conv_w_out[0], router_w[1], router_bias[1],
        n_lat, n_lat, seq, b, dn_alpha)
    out = _moe(x1, h2, eidx, gates, rank, counts, mod[1], ln_g[1, 1], ln_b[1, 1],
               exp_w_gate_up, exp_w_down, 1, shared_w_gate_up[1], shared_w_down[1],
               n_lat, n_lat, seq, b, dn_alpha)
    return out.reshape(b, seq, d)
```

```python
import functools

import jax
import jax.numpy as jnp
from jax import lax
from jax.experimental import pallas as pl
from jax.experimental.pallas import tpu as pltpu

F32 = jnp.float32
BF16 = jnp.bfloat16
I32 = jnp.int32

N_HEADS = 8
N_KV_HEADS = 2
HEAD_DIM = 128
KV_GROUP = N_HEADS // N_KV_HEADS
GRID_W = 64
ROPE_THETA = 10000.0
N_EXPERTS = 256
TOP_K = 8
N_GROUPS = 8
TOPK_GROUPS = 4
PER_GROUP = N_EXPERTS // N_GROUPS
ROUTED_SCALE = 2.5
LN_EPS = 1e-5
QK_EPS = 1e-6
N_MOD = 6
MOD_ROWS = 16

LANES = 128
SUBLANES = 8
VMEM_LIMIT = 56 * 1024 * 1024

ROW_TILE = 512
ATTN_TQ = 256
ATTN_TK = 2048
MOE_BM = 256
GMM_IN_SLOTS = 4
GMM_OUT_SLOTS = 3
DISPATCH_TOK = 512
CONV_LANES = 128

HIGHEST = lax.Precision.HIGHEST
LOG2_E = 1.4426950408889634


def _cparams(sem):
    return pltpu.CompilerParams(dimension_semantics=sem, vmem_limit_bytes=VMEM_LIMIT)


def _silu(v):
    return v * jax.nn.sigmoid(v)


def _mod_kernel(c_ref, w_ref, b_ref, o_ref):
    s = _silu(c_ref[...])
    o_ref[...] = jnp.dot(s, w_ref[...], precision=HIGHEST, preferred_element_type=F32) + b_ref[...]


def _modulation(cond, ada_w, ada_b):
    depth, d, nd = ada_w.shape
    tn = 1536
    out = pl.pallas_call(
        _mod_kernel,
        grid=(depth, nd // tn),
        in_specs=[
            pl.BlockSpec((MOD_ROWS, d), lambda l, j: (0, 0)),
            pl.BlockSpec((None, d, tn), lambda l, j: (l, 0, j)),
            pl.BlockSpec((None, 1, tn), lambda l, j: (l, 0, j)),
        ],
        out_specs=pl.BlockSpec((None, MOD_ROWS, tn), lambda l, j: (l, 0, j)),
        out_shape=jax.ShapeDtypeStruct((depth, MOD_ROWS, nd), F32),
        compiler_params=_cparams(("arbitrary", "arbitrary")),
        name="adaln_modulation",
    )(cond, ada_w, ada_b.reshape(depth, 1, nd))
    return out.reshape(depth, MOD_ROWS * N_MOD, 1, d)


def _mod_spec(comp, tm, n_lat, seq, ctx_row, d):
    def index(i, *_):
        row0 = i * tm
        r = jnp.where(row0 < n_lat, row0 // seq, ctx_row)
        return (r * N_MOD + comp, 0, 0)

    return pl.BlockSpec((None, 1, d), index)


def _qkv_kernel(x_ref, shift_ref, scale_ref, w_ref, qg_ref, kg_ref, cos_ref, sin_ref,
                q_ref, k_ref, v_ref):
    h = (x_ref[...] * (1.0 + scale_ref[...]) + shift_ref[...]).astype(BF16)
    qkv = jnp.dot(h, w_ref[...], preferred_element_type=F32)
    cos = cos_ref[...]
    sin = sin_ref[...]
    hq = N_HEADS * HEAD_DIM
    kd = N_KV_HEADS * HEAD_DIM

    def norm_rope(t, g, post):
        t = t * lax.rsqrt(jnp.mean(t * t, axis=-1, keepdims=True) + QK_EPS) * g
        t = t * cos + pltpu.roll(t, HEAD_DIM // 2, axis=1) * sin
        return (t * post).astype(BF16)

    for hd in range(N_HEADS):
        sl = slice(hd * HEAD_DIM, (hd + 1) * HEAD_DIM)
        q_ref[:, sl] = norm_rope(qkv[:, sl], qg_ref[...], HEAD_DIM ** -0.5 * LOG2_E)
    for hd in range(N_KV_HEADS):
        sl = slice(hd * HEAD_DIM, (hd + 1) * HEAD_DIM)
        k_ref[:, sl] = norm_rope(qkv[:, hq + hd * HEAD_DIM: hq + (hd + 1) * HEAD_DIM], kg_ref[...], 1.0)
    v_ref[...] = qkv[:, hq + kd:].astype(BF16)


def _rope_tables(seq, tm):
    rows = seq // GRID_W
    row = jnp.repeat(jnp.arange(rows, dtype=F32), GRID_W)
    col = jnp.tile(jnp.arange(GRID_W, dtype=F32), rows)
    axis_dim = HEAD_DIM // 2
    freqs = ROPE_THETA ** (-jnp.arange(0, axis_dim, 2, dtype=F32) / axis_dim)
    ang = jnp.concatenate([row[:, None] * freqs, col[:, None] * freqs], axis=-1)
    cos = jnp.concatenate([jnp.cos(ang), jnp.cos(ang)], axis=-1)
    sin = jnp.concatenate([-jnp.sin(ang), jnp.sin(ang)], axis=-1)
    cos = jnp.concatenate([cos, jnp.ones((tm, HEAD_DIM), F32)], axis=0)
    sin = jnp.concatenate([sin, jnp.zeros((tm, HEAD_DIM), F32)], axis=0)
    return cos.reshape(seq // tm + 1, tm, HEAD_DIM), sin.reshape(seq // tm + 1, tm, HEAD_DIM)


def _qkv_project(xall, mod, w_qkv, q_g, k_g, n_lat, seq, ctx_row):
    t, d = xall.shape
    tm = ROW_TILE
    hq = N_HEADS * HEAD_DIM
    kd = N_KV_HEADS * HEAD_DIM
    perm = jnp.concatenate([jnp.arange(0, HEAD_DIM, 2), jnp.arange(1, HEAD_DIM, 2)])
    cols = jnp.concatenate([hd * HEAD_DIM + perm for hd in range(N_HEADS + N_KV_HEADS)]
                           + [jnp.arange(hq + kd, hq + 2 * kd)])
    w = w_qkv[:, cols].astype(BF16)
    cos, sin = _rope_tables(seq, tm)
    n_pos = seq // tm

    def pos_index(i):
        row0 = i * tm
        return (jnp.where(row0 < n_lat, (row0 % seq) // tm, n_pos), 0, 0)

    const = lambda i: (0, 0)
    return pl.pallas_call(
        _qkv_kernel,
        grid=(t // tm,),
        in_specs=[
            pl.BlockSpec((tm, d), lambda i: (i, 0)),
            _mod_spec(0, tm, n_lat, seq, ctx_row, d),
            _mod_spec(1, tm, n_lat, seq, ctx_row, d),
            pl.BlockSpec(w.shape, const),
            pl.BlockSpec((1, HEAD_DIM), const),
            pl.BlockSpec((1, HEAD_DIM), const),
            pl.BlockSpec((None, tm, HEAD_DIM), pos_index),
            pl.BlockSpec((None, tm, HEAD_DIM), pos_index),
        ],
        out_specs=[
            pl.BlockSpec((tm, hq), lambda i: (i, 0)),
            pl.BlockSpec((tm, kd), lambda i: (i, 0)),
            pl.BlockSpec((tm, kd), lambda i: (i, 0)),
        ],
        out_shape=[
            jax.ShapeDtypeStruct((t, hq), BF16),
            jax.ShapeDtypeStruct((t, kd), BF16),
            jax.ShapeDtypeStruct((t, kd), BF16),
        ],
        compiler_params=_cparams(("parallel",)),
        name="qkv_norm_rope",
    )(xall, mod, mod, w, q_g[perm].reshape(1, HEAD_DIM), k_g[perm].reshape(1, HEAD_DIM), cos, sin)


def _attn_kernel(*refs, n_lat_chunks, tk):
    if n_lat_chunks:
        q_ref, kc_ref, vc_ref, kl_ref, vl_ref, o_ref = refs
    else:
        q_ref, kc_ref, vc_ref, o_ref = refs
    tq = q_ref.shape[0]
    q = jnp.concatenate([q_ref[:, h * HEAD_DIM:(h + 1) * HEAD_DIM] for h in range(KV_GROUP)], axis=0)
    rows = KV_GROUP * tq

    def chunk(k, v, m, l, acc):
        s = lax.dot_general(q, k, (((1,), (1,)), ((), ())), preferred_element_type=F32)
        m_new = jnp.maximum(m, jnp.max(s, axis=-1, keepdims=True))
        p = jnp.exp2(s - m_new)
        a = jnp.exp2(m - m_new)
        l = a * l + jnp.sum(p, axis=-1, keepdims=True)
        acc = a * acc + jnp.dot(p.astype(BF16), v, preferred_element_type=F32)
        return m_new, l, acc

    m = jnp.full((rows, 1), -jnp.inf, F32)
    l = jnp.zeros((rows, 1), F32)
    acc = jnp.zeros((rows, HEAD_DIM), F32)
    m, l, acc = chunk(kc_ref[...], vc_ref[...], m, l, acc)
    for c in range(n_lat_chunks):
        m, l, acc = chunk(kl_ref[c * tk:(c + 1) * tk, :], vl_ref[c * tk:(c + 1) * tk, :], m, l, acc)
    o = (acc / l).astype(BF16)
    for h in range(KV_GROUP):
        o_ref[:, h * HEAD_DIM:(h + 1) * HEAD_DIM] = o[h * tq:(h + 1) * tq]


def _attention(q, k, v, b, seq, cl, n_lat):
    t = q.shape[0]
    gw = KV_GROUP * HEAD_DIM
    tq = min(ATTN_TQ, seq)
    tk = min(ATTN_TK, seq)
    nq = seq // tq
    ctx_blk0 = n_lat // cl
    hq = N_HEADS * HEAD_DIM

    ctx_kv = pl.BlockSpec((cl, HEAD_DIM), lambda bi, g, qi: (ctx_blk0 + bi, g))
    lat_kv = pl.BlockSpec((seq, HEAD_DIM), lambda bi, g, qi: (bi, g))
    lat_q = pl.BlockSpec((tq, gw), lambda bi, g, qi: (bi * nq + qi, g))
    o_lat = pl.pallas_call(
        functools.partial(_attn_kernel, n_lat_chunks=seq // tk, tk=tk),
        grid=(b, N_KV_HEADS, nq),
        in_specs=[lat_q, ctx_kv, ctx_kv, lat_kv, lat_kv],
        out_specs=lat_q,
        out_shape=jax.ShapeDtypeStruct((n_lat, hq), BF16),
        compiler_params=_cparams(("parallel", "parallel", "arbitrary")),
        name="attention_latent",
    )(q, k, v, k, v)

    ctx_q = pl.BlockSpec((cl, gw), lambda bi, g: (ctx_blk0 + bi, g))
    ctx_kv2 = pl.BlockSpec((cl, HEAD_DIM), lambda bi, g: (ctx_blk0 + bi, g))
    o_ctx = pl.pallas_call(
        functools.partial(_attn_kernel, n_lat_chunks=0, tk=tk),
        grid=(b, N_KV_HEADS),
        in_specs=[ctx_q, ctx_kv2, ctx_kv2],
        out_specs=pl.BlockSpec((cl, gw), lambda bi, g: (bi, g)),
        out_shape=jax.ShapeDtypeStruct((t - n_lat, hq), BF16),
        compiler_params=_cparams(("parallel", "parallel")),
        name="attention_context",
    )(q, k, v)
    return jnp.concatenate([o_lat, o_ctx], axis=0)


def _in_proj_kernel(x_ref, shift_ref, scale_ref, w_ref, o_ref):
    h = (x_ref[...] * (1.0 + scale_ref[...]) + shift_ref[...]).astype(BF16)
    o_ref[...] = jnp.dot(h, w_ref[...], preferred_element_type=F32)


def _in_project(xall, mod, w_in, n_rows, n_lat, seq, ctx_row):
    d = xall.shape[1]
    n_out = w_in.shape[1]
    tm = ROW_TILE
    return pl.pallas_call(
        _in_proj_kernel,
        grid=(n_rows // tm,),
        in_specs=[
            pl.BlockSpec((tm, d), lambda i: (i, 0)),
            _mod_spec(0, tm, n_lat, seq, ctx_row, d),
            _mod_spec(1, tm, n_lat, seq, ctx_row, d),
            pl.BlockSpec(w_in.shape, lambda i: (0, 0)),
        ],
        out_specs=pl.BlockSpec((tm, n_out), lambda i: (i, 0)),
        out_shape=jax.ShapeDtypeStruct((n_rows, n_out), F32),
        compiler_params=_cparams(("parallel",)),
        name="conv_in_proj",
    )(xall, mod, mod, w_in.astype(BF16))


def _conv_kernel(bg_ref, cg_ref, v_ref, taps_ref, o_ref):
    u = cg_ref[...] * v_ref[...]
    n = u.shape[0]
    pos = lax.broadcasted_iota(I32, u.shape, 0)
    prev = jnp.where(pos == 0, 0.0, pltpu.roll(u, 1, axis=0))
    nxt = jnp.where(pos == n - 1, 0.0, pltpu.roll(u, n - 1, axis=0))
    conv = prev * taps_ref[0:1, :] + u * taps_ref[1:2, :] + nxt * taps_ref[2:3, :]
    o_ref[...] = (bg_ref[...] * conv).astype(BF16)


def _short_conv(zin, taps, n_seqs, seq, d):
    tc = CONV_LANES
    nj = d // tc
    return pl.pallas_call(
        _conv_kernel,
        grid=(n_seqs, nj),
        in_specs=[
            pl.BlockSpec((seq, tc), lambda s, j: (s, j)),
            pl.BlockSpec((seq, tc), lambda s, j: (s, nj + j)),
            pl.BlockSpec((seq, tc), lambda s, j: (s, 2 * nj + j)),
            pl.BlockSpec((taps.shape[0], tc), lambda s, j: (0, j)),
        ],
        out_specs=pl.BlockSpec((seq, tc), lambda s, j: (s, j)),
        out_shape=jax.ShapeDtypeStruct((n_seqs * seq, d), BF16),
        compiler_params=_cparams(("parallel", "parallel")),
        name="short_conv",
    )(zin, zin, zin, taps)


def _layer_norm(z, g, b):
    mu = jnp.mean(z, axis=-1, keepdims=True)
    zc = z - mu
    var = jnp.mean(zc * zc, axis=-1, keepdims=True)
    return zc * lax.rsqrt(var + LN_EPS) * g + b


def _post_kernel(a_ref, x_ref, gate_ref, lng_ref, lnb_ref, shift_ref, scale_ref, w_ref,
                 rwt_ref, rb_ref, tri_ref,
                 x1_ref, h2_ref, eidx_ref, gates_ref, rank_ref, cnt_ref, rwh_ref, rwl_ref, *, dn_alpha):
    tm = a_ref.shape[0]

    @pl.when(pl.program_id(0) == 0)
    def _():
        hi = rwt_ref[...].astype(BF16)
        rwh_ref[...] = hi
        rwl_ref[...] = (rwt_ref[...] - hi.astype(F32)).astype(BF16)

    d = x_ref.shape[1]
    y = jnp.dot(a_ref[...], w_ref[...], preferred_element_type=F32)
    x1 = _layer_norm(dn_alpha * x_ref[...] + gate_ref[...] * y, lng_ref[...], lnb_ref[...])
    x1_ref[...] = x1
    h2 = x1 * (1.0 + scale_ref[...]) + shift_ref[...]
    h2_ref[...] = h2

    h_hi = h2.astype(BF16)
    h_lo = (h2 - h_hi.astype(F32)).astype(BF16)

    def nt_dot(w, h):
        return lax.dot_general(w, h, (((1,), (1,)), ((), ())), preferred_element_type=F32)

    logits = nt_dot(rwh_ref[...], h_hi) + (nt_dot(rwl_ref[...], h_hi) + nt_dot(rwh_ref[...], h_lo))
    scores = jax.nn.sigmoid(logits)
    biased = scores + rb_ref[...]
    neg = -jnp.inf
    big = jnp.int32(1 << 30)
    row = lax.broadcasted_iota(I32, (N_EXPERTS, tm), 0)

    def argmax_rows(vals, idx):
        mx = jnp.max(vals, axis=0, keepdims=True)
        return mx, jnp.min(jnp.where(vals == mx, idx, big), axis=0, keepdims=True)

    gs = []
    grp_row = lax.broadcasted_iota(I32, (PER_GROUP, tm), 0)
    for g in range(N_GROUPS):
        bg = biased[g * PER_GROUP:(g + 1) * PER_GROUP]
        ig = grp_row + g * PER_GROUP
        m1, i1 = argmax_rows(bg, ig)
        m2 = jnp.max(jnp.where(ig == i1, neg, bg), axis=0, keepdims=True)
        gs.append(m1 + m2)
    gsc = jnp.concatenate(gs, axis=0)
    grow = lax.broadcasted_iota(I32, (N_GROUPS, tm), 0)
    gsel = jnp.zeros((N_GROUPS, tm), F32)
    for _ in range(TOPK_GROUPS):
        _, gi = argmax_rows(gsc, grow)
        hit = grow == gi
        gsel = jnp.where(hit, 1.0, gsel)
        gsc = jnp.where(hit, neg, gsc)
    cur = jnp.concatenate(
        [jnp.where(gsel[g:g + 1] > 0.0, biased[g * PER_GROUP:(g + 1) * PER_GROUP], neg)
         for g in range(N_GROUPS)], axis=0)

    onehot = jnp.zeros((N_EXPERTS, tm), F32)
    idxs, gvals = [], []
    for _ in range(TOP_K):
        _, ei = argmax_rows(cur, row)
        hit = row == ei
        gvals.append(jnp.sum(jnp.where(hit, scores, 0.0), axis=0, keepdims=True))
        idxs.append(ei)
        onehot = jnp.where(hit, 1.0, onehot)
        cur = jnp.where(hit, neg, cur)
    gv = jnp.concatenate(gvals, axis=0)
    gates_ref[...] = gv / jnp.sum(gv, axis=0, keepdims=True) * ROUTED_SCALE
    eidx_ref[...] = jnp.concatenate(idxs, axis=0)

    @pl.when(pl.program_id(0) == 0)
    def _():
        cnt_ref[...] = jnp.zeros_like(cnt_ref)

    prefix = jnp.dot(onehot.astype(BF16), tri_ref[...], preferred_element_type=F32)
    pos = prefix + cnt_ref[...]
    rank_ref[...] = jnp.concatenate(
        [jnp.sum(jnp.where(row == ei, pos, 0.0), axis=0, keepdims=True) for ei in idxs],
        axis=0).astype(I32)
    cnt_ref[...] = cnt_ref[...] + jnp.sum(onehot, axis=1, keepdims=True)


def _post_mixer(a, xall, mod, ln_g, ln_b, w, router_w, router_bias, n_rows, n_lat, seq, ctx_row,
                dn_alpha):
    d = xall.shape[1]
    tm = ROW_TILE
    tri = (lax.broadcasted_iota(I32, (tm, tm), 0) < lax.broadcasted_iota(I32, (tm, tm), 1)).astype(BF16)
    const = lambda i: (0, 0)
    row_blk = pl.BlockSpec((tm, d), lambda i: (i, 0))
    k_blk = pl.BlockSpec((TOP_K, tm), lambda i: (0, i))
    return pl.pallas_call(
        functools.partial(_post_kernel, dn_alpha=dn_alpha),
        grid=(n_rows // tm,),
        in_specs=[
            row_blk, row_blk,
            _mod_spec(2, tm, n_lat, seq, ctx_row, d),
            pl.BlockSpec((1, d), const), pl.BlockSpec((1, d), const),
            _mod_spec(3, tm, n_lat, seq, ctx_row, d),
            _mod_spec(4, tm, n_lat, seq, ctx_row, d),
            pl.BlockSpec(w.shape, const),
            pl.BlockSpec((N_EXPERTS, d), const),
            pl.BlockSpec((N_EXPERTS, 1), const),
            pl.BlockSpec((tm, tm), const),
        ],
        out_specs=[
            row_blk, row_blk,
            k_blk, k_blk, k_blk,
            pl.BlockSpec((N_EXPERTS, 1), const),
        ],
        out_shape=[
            jax.ShapeDtypeStruct((n_rows, d), F32),
            jax.ShapeDtypeStruct((n_rows, d), F32),
            jax.ShapeDtypeStruct((TOP_K, n_rows), I32),
            jax.ShapeDtypeStruct((TOP_K, n_rows), F32),
            jax.ShapeDtypeStruct((TOP_K, n_rows), I32),
            jax.ShapeDtypeStruct((N_EXPERTS, 1), F32),
        ],
        scratch_shapes=[pltpu.VMEM((N_EXPERTS, d), BF16), pltpu.VMEM((N_EXPERTS, d), BF16)],
        compiler_params=_cparams(("arbitrary",)),
        name="post_mixer_router",
    )(a, xall, mod, ln_g.reshape(1, d), ln_b.reshape(1, d), mod, mod, w.astype(BF16),
      router_w.T, router_bias.reshape(N_EXPERTS, 1), tri)


def _slots_kernel(pstart_ref, eidx_ref, rank_ref, dest_ref):
    e = eidx_ref[...]

    def pick(i, acc):
        return jnp.where(e == i, pstart_ref[i], acc)

    dest_ref[...] = lax.fori_loop(0, N_EXPERTS, pick, jnp.zeros_like(e)) + rank_ref[...]


def _slots(pstarts, eidx, rank):
    n_tok = eidx.shape[1]
    tn = 2048 if n_tok % 2048 == 0 else ROW_TILE
    blk = pl.BlockSpec((TOP_K, tn), lambda i, ps: (0, i))
    return pl.pallas_call(
        _slots_kernel,
        grid_spec=pltpu.PrefetchScalarGridSpec(
            num_scalar_prefetch=1, grid=(n_tok // tn,), in_specs=[blk, blk], out_specs=blk),
        out_shape=jax.ShapeDtypeStruct((TOP_K, n_tok), I32),
        compiler_params=_cparams(("arbitrary",)),
        name="moe_slots",
    )(pstarts, eidx, rank)


def _dispatch_kernel(pends_ref, pcounts_ref, dest_ref, h_ref, xs_hbm, zbuf, sem):
    td = dest_ref.shape[1]
    bm = zbuf.shape[0]

    @pl.when(pl.program_id(0) == 0)
    def _():
        zbuf[...] = jnp.zeros_like(zbuf)

        def tail(e):
            return xs_hbm.at[pl.ds(pl.multiple_of(pends_ref[e] - bm, bm), bm)]

        def zstart(e, carry):
            @pl.when(pcounts_ref[e] > 0)
            def _():
                pltpu.make_async_copy(zbuf, tail(e), sem).start()
            return carry

        def zwait(e, carry):
            @pl.when(pcounts_ref[e] > 0)
            def _():
                pltpu.make_async_copy(zbuf, tail(e), sem).wait()
            return carry

        lax.fori_loop(0, N_EXPERTS, zstart, 0)
        lax.fori_loop(0, N_EXPERTS, zwait, 0)

        def unused(j):
            return xs_hbm.at[pl.ds(pl.multiple_of(j * bm, bm), bm)]

        def ustart(j, carry):
            pltpu.make_async_copy(zbuf, unused(j), sem).start()
            return carry

        def uwait(j, carry):
            pltpu.make_async_copy(zbuf, unused(j), sem).wait()
            return carry

        first_unused = pends_ref[N_EXPERTS - 1] // bm
        lax.fori_loop(first_unused, xs_hbm.shape[0] // bm, ustart, 0)
        lax.fori_loop(first_unused, xs_hbm.shape[0] // bm, uwait, 0)

    def issue(t, carry):
        for k in range(TOP_K):
            pltpu.make_async_copy(h_ref.at[pl.ds(t, 1)], xs_hbm.at[pl.ds(dest_ref[k, t], 1)], sem).start()
        return carry

    lax.fori_loop(0, td, issue, 0)
    pltpu.make_async_copy(xs_hbm.at[pl.ds(0, TOP_K * td)], xs_hbm.at[pl.ds(0, TOP_K * td)], sem).wait()


def _dispatch(dest, h2, pends, pcounts, n_slots):
    n_tok, d = h2.shape
    td = DISPATCH_TOK
    grid_spec = pltpu.PrefetchScalarGridSpec(
        num_scalar_prefetch=2,
        grid=(n_tok // td,),
        in_specs=[
            pl.BlockSpec((TOP_K, td), lambda i, pe, pc: (0, i), memory_space=pltpu.SMEM),
            pl.BlockSpec((td, d), lambda i, pe, pc: (i, 0)),
        ],
        out_specs=pl.BlockSpec(memory_space=pl.ANY),
        scratch_shapes=[pltpu.VMEM((MOE_BM, d), h2.dtype), pltpu.SemaphoreType.DMA(())],
    )
    return pl.pallas_call(
        _dispatch_kernel,
        grid_spec=grid_spec,
        out_shape=jax.ShapeDtypeStruct((n_slots, d), h2.dtype),
        compiler_params=_cparams(("arbitrary",)),
        name="moe_dispatch",
    )(pends, pcounts, dest, h2)


def _gmm_kernel(blk0_ref, nblk_ref, n_used_ref, xs_hbm, wgu_ref, wd_ref, ys_hbm,
                xbuf, ybuf, wgu_bf, wd_bf, sem_in, sem_out):
    e = pl.program_id(0)
    n_in, bm = xbuf.shape[:2]
    n_out = ybuf.shape[0]
    ff = wd_ref.shape[0]
    n_used = n_used_ref[0]

    def rows(b):
        return pl.ds(pl.multiple_of(b * bm, bm), bm)

    def x_copy(b):
        return pltpu.make_async_copy(xs_hbm.at[rows(b)], xbuf.at[b % n_in], sem_in.at[b % n_in])

    def y_copy(b, slot):
        return pltpu.make_async_copy(ybuf.at[slot], ys_hbm.at[rows(b)], sem_out.at[slot])

    @pl.when(e == 0)
    def _():
        for b in range(n_in - 1):
            @pl.when(b < n_used)
            def _():
                x_copy(b).start()

    nb = nblk_ref[e]

    @pl.when(nb > 0)
    def _():
        wgu_bf[...] = wgu_ref[...].astype(BF16)
        wd_bf[...] = wd_ref[...].astype(BF16)

    def block(j, carry):
        b = blk0_ref[e] + j
        slot = b % n_out
        x_copy(b).wait()

        @pl.when(b + n_in - 1 < n_used)
        def _():
            x_copy(b + n_in - 1).start()

        @pl.when(b >= n_out)
        def _():
            y_copy(b - n_out, slot).wait()

        gu = jnp.dot(xbuf[b % n_in].astype(BF16), wgu_bf[...], preferred_element_type=F32)
        act = (_silu(gu[:, :ff]) * gu[:, ff:]).astype(BF16)
        ybuf[slot] = jnp.dot(act, wd_bf[...], preferred_element_type=F32)
        y_copy(b, slot).start()
        return carry

    lax.fori_loop(0, nb, block, 0)

    @pl.when(e == pl.num_programs(0) - 1)
    def _():
        for back in range(n_out, 0, -1):
            @pl.when(n_used >= back)
            def _():
                y_copy(n_used - back, (n_used - back) % n_out).wait()

        ybuf[0] = jnp.zeros(ybuf.shape[1:], ybuf.dtype)
        n_total = ys_hbm.shape[0] // bm

        def zstart(b, carry):
            y_copy(b, 0).start()
            return carry

        def zwait(b, carry):
            y_copy(b, 0).wait()
            return carry

        lax.fori_loop(n_used, n_total, zstart, 0)
        lax.fori_loop(n_used, n_total, zwait, 0)


def _grouped_ffn(xs, blk0, nblk, n_used, wgu, wd, layer):
    n_slots = xs.shape[0]
    bm = MOE_BM
    n_exp, d, ff2 = wgu.shape[1:]
    ff = wd.shape[2]
    grid_spec = pltpu.PrefetchScalarGridSpec(
        num_scalar_prefetch=3,
        grid=(n_exp,),
        in_specs=[
            pl.BlockSpec(memory_space=pl.ANY),
            pl.BlockSpec((None, None, d, ff2), lambda e, b0, nb, nu: (layer, e, 0, 0)),
            pl.BlockSpec((None, None, ff, d), lambda e, b0, nb, nu: (layer, e, 0, 0)),
        ],
        out_specs=pl.BlockSpec(memory_space=pl.ANY),
        scratch_shapes=[
            pltpu.VMEM((GMM_IN_SLOTS, bm, d), F32), pltpu.VMEM((GMM_OUT_SLOTS, bm, d), F32),
            pltpu.VMEM((d, ff2), BF16), pltpu.VMEM((ff, d), BF16),
            pltpu.SemaphoreType.DMA((GMM_IN_SLOTS,)), pltpu.SemaphoreType.DMA((GMM_OUT_SLOTS,)),
        ],
    )
    return pl.pallas_call(
        _gmm_kernel,
        grid_spec=grid_spec,
        out_shape=jax.ShapeDtypeStruct((n_slots, d), F32),
        compiler_params=_cparams(("arbitrary",)),
        name="moe_grouped_ffn",
    )(blk0, nblk, n_used, xs, wgu, wd)


def _combine_kernel(dest_ref, gates_ref, ys_hbm, x1_ref, h2_ref, gate_ref, lng_ref, lnb_ref,
                    swgu_ref, swd_ref, o_ref, buf, sem, *, dn_alpha):
    tc = x1_ref.shape[0]
    ff = swd_ref.shape[0]

    def issue(t, carry):
        for k in range(TOP_K):
            pltpu.make_async_copy(ys_hbm.at[pl.ds(dest_ref[k, t], 1)], buf.at[k, pl.ds(t, 1)], sem).start()
        return carry

    lax.fori_loop(0, tc, issue, 0)

    gu = jnp.dot(h2_ref[...].astype(BF16), swgu_ref[...], preferred_element_type=F32)
    act = (_silu(gu[:, :ff]) * gu[:, ff:]).astype(BF16)
    y = jnp.dot(act, swd_ref[...], preferred_element_type=F32)

    for k in range(TOP_K):
        pltpu.make_async_copy(ys_hbm.at[pl.ds(0, tc)], buf.at[k], sem).wait()
    for k in range(TOP_K):
        y = y + gates_ref[:, k:k + 1] * buf[k]
    o_ref[...] = _layer_norm(dn_alpha * x1_ref[...] + gate_ref[...] * y, lng_ref[...], lnb_ref[...])


def _combine(dest, gates_t, ys, x1, h2, mod, ln_g, ln_b, swgu, swd, n_rows, n_lat, seq, ctx_row,
             dn_alpha):
    d = x1.shape[1]
    tc = DISPATCH_TOK
    const = lambda i: (0, 0)
    return pl.pallas_call(
        functools.partial(_combine_kernel, dn_alpha=dn_alpha),
        grid=(n_rows // tc,),
        in_specs=[
            pl.BlockSpec((TOP_K, tc), lambda i: (0, i), memory_space=pltpu.SMEM),
            pl.BlockSpec((tc, TOP_K), lambda i: (i, 0)),
            pl.BlockSpec(memory_space=pl.ANY),
            pl.BlockSpec((tc, d), lambda i: (i, 0)),
            pl.BlockSpec((tc, d), lambda i: (i, 0)),
            _mod_spec(5, tc, n_lat, seq, ctx_row, d),
            pl.BlockSpec((1, d), const), pl.BlockSpec((1, d), const),
            pl.BlockSpec(swgu.shape, const),
            pl.BlockSpec(swd.shape, const),
        ],
        out_specs=pl.BlockSpec((tc, d), lambda i: (i, 0)),
        out_shape=jax.ShapeDtypeStruct((n_rows, d), F32),
        scratch_shapes=[pltpu.VMEM((TOP_K, tc, d), F32), pltpu.SemaphoreType.DMA(())],
        compiler_params=_cparams(("arbitrary",)),
        name="moe_combine",
    )(dest, gates_t, ys, x1, h2, mod, ln_g.reshape(1, d), ln_b.reshape(1, d),
      swgu.astype(BF16), swd.astype(BF16))


def _moe(x1, h2, eidx, gates, rank, counts, mod, ln_g, ln_b, wgu, wd, layer, swgu, swd,
         n_rows, n_lat, seq, ctx_row, dn_alpha):
    bm = MOE_BM
    counts = counts.reshape(N_EXPERTS).astype(I32)
    pcounts = (counts + bm - 1) // bm * bm
    pends = jnp.cumsum(pcounts)
    pstarts = pends - pcounts
    dest = _slots(pstarts, eidx, rank)
    n_blk = (n_rows * TOP_K + N_EXPERTS * (bm - 1) + bm - 1) // bm
    n_used = (pends[-1] // bm).astype(I32).reshape(1)
    xs = _dispatch(dest, h2, pends, pcounts, n_blk * bm)
    ys = _grouped_ffn(xs, pstarts // bm, pcounts // bm, n_used, wgu, wd, layer)
    return _combine(dest, gates.T, ys, x1, h2, mod, ln_g, ln_b, swgu, swd,
                    n_rows, n_lat, seq, ctx_row, dn_alpha)


def kernel(x, c, ctx, c_ctx, ada_w, ada_b, ln_g, ln_b, attn_w_qkv, attn_q_norm, attn_k_norm, attn_w_o, conv_w_in, conv_taps, conv_w_out, router_w, router_bias, exp_w_gate_up, exp_w_down, shared_w_gate_up, shared_w_down):
    b, seq, d = x.shape
    cl = ctx.shape[1]
    depth = ada_w.shape[0]
    n_lat = b * seq
    n_ctx = b * cl
    dn_alpha = (2 * depth) ** 0.25
    assert depth == 2 and b < MOD_ROWS
    assert seq % ROW_TILE == 0 and n_ctx % ROW_TILE == 0 and seq % GRID_W == 0

    cond = jnp.zeros((MOD_ROWS, d), F32).at[:b].set(c).at[b].set(c_ctx)
    mod = _modulation(cond, ada_w, ada_b)
    xall = jnp.concatenate([x.reshape(n_lat, d), ctx.reshape(n_ctx, d)], axis=0)
    n_all = n_lat + n_ctx

    q, k, v = _qkv_project(xall, mod[0], attn_w_qkv[0], attn_q_norm[0], attn_k_norm[0], n_lat, seq, b)
    o = _attention(q, k, v, b, seq, cl, n_lat)
    x1, h2, eidx, gates, rank, counts = _post_mixer(
        o, xall, mod[0], ln_g[0, 0], ln_b[0, 0], attn_w_o[0], router_w[0], router_bias[0],
        n_all, n_lat, seq, b, dn_alpha)
    xall = _moe(x1, h2, eidx, gates, rank, counts, mod[0], ln_g[0, 1], ln_b[0, 1],
                exp_w_gate_up, exp_w_down, 0, shared_w_gate_up[0], shared_w_down[0],
                n_all, n_lat, seq, b, dn_alpha)

    zin = _in_project(xall, mod[1], conv_w_in[0], n_lat, n_lat, seq, b)
    a = _short_conv(zin, conv_taps[0], b, seq, d)
    x1, h2, eidx, gates, rank, counts = _post_mixer(
        a, xall, mod[1], ln_g[1, 0], ln_b[1, 0], conv_w_out[0], router_w[1], router_bias[1],
        n_lat, n_lat, seq, b, dn_alpha)
    out = _moe(x1, h2, eidx, gates, rank, counts, mod[1], ln_g[1, 1], ln_b[1, 1],
               exp_w_gate_up, exp_w_down, 1, shared_w_gate_up[1], shared_w_down[1],
               n_lat, n_lat, seq, b, dn_alpha)
    return out.reshape(b, seq, d)
```

```python
import functools

import jax
import jax.numpy as jnp
from jax import lax
from jax.experimental import pallas as pl
from jax.experimental.pallas import tpu as pltpu
from jax.experimental.pallas import tpu_sc as plsc

F32 = jnp.float32
BF16 = jnp.bfloat16
I32 = jnp.int32

N_HEADS = 8
N_KV_HEADS = 2
HEAD_DIM = 128
KV_GROUP = N_HEADS // N_KV_HEADS
GRID_W = 64
ROPE_THETA = 10000.0
N_EXPERTS = 256
TOP_K = 8
N_GROUPS = 8
TOPK_GROUPS = 4
PER_GROUP = N_EXPERTS // N_GROUPS
ROUTED_SCALE = 2.5
LN_EPS = 1e-5
QK_EPS = 1e-6
N_MOD = 6
MOD_ROWS = 16

LANES = 128
SUBLANES = 8
VMEM_LIMIT = 56 * 1024 * 1024

ROW_TILE = 512
ATTN_TQ = 256
ATTN_TK = 2048
MOE_BM = 256
GMM_IN_SLOTS = 4
GMM_OUT_SLOTS = 3
DISPATCH_TOK = 512
COMBINE_GATHERED_TOK = 256
SC_GATHER_ROWS = 64
SC_SHARE_PERCENT = 40
CONV_LANES = 128

HIGHEST = lax.Precision.HIGHEST
LOG2_E = 1.4426950408889634


def _cparams(sem):
    return pltpu.CompilerParams(dimension_semantics=sem, vmem_limit_bytes=VMEM_LIMIT)


def _silu(v):
    return v * jax.nn.sigmoid(v)


def _mod_kernel(c_ref, w_ref, b_ref, o_ref):
    s = _silu(c_ref[...])
    o_ref[...] = jnp.dot(s, w_ref[...], precision=HIGHEST, preferred_element_type=F32) + b_ref[...]


def _modulation(cond, ada_w, ada_b):
    depth, d, nd = ada_w.shape
    tn = 1536
    out = pl.pallas_call(
        _mod_kernel,
        grid=(depth, nd // tn),
        in_specs=[
            pl.BlockSpec((MOD_ROWS, d), lambda l, j: (0, 0)),
            pl.BlockSpec((None, d, tn), lambda l, j: (l, 0, j)),
            pl.BlockSpec((None, 1, tn), lambda l, j: (l, 0, j)),
        ],
        out_specs=pl.BlockSpec((None, MOD_ROWS, tn), lambda l, j: (l, 0, j)),
        out_shape=jax.ShapeDtypeStruct((depth, MOD_ROWS, nd), F32),
        compiler_params=_cparams(("arbitrary", "arbitrary")),
        name="adaln_modulation",
    )(cond, ada_w, ada_b.reshape(depth, 1, nd))
    return out.reshape(depth, MOD_ROWS * N_MOD, 1, d)


def _mod_spec(comp, tm, n_lat, seq, ctx_row, d, blk_off=0):
    def index(i, *_):
        row0 = (i + blk_off) * tm
        r = jnp.where(row0 < n_lat, row0 // seq, ctx_row)
        return (r * N_MOD + comp, 0, 0)

    return pl.BlockSpec((None, 1, d), index)


def _qkv_kernel(x_ref, shift_ref, scale_ref, w_ref, qg_ref, kg_ref, cos_ref, sin_ref,
                q_ref, k_ref, v_ref):
    h = (x_ref[...] * (1.0 + scale_ref[...]) + shift_ref[...]).astype(BF16)
    qkv = jnp.dot(h, w_ref[...], preferred_element_type=F32)
    cos = cos_ref[...]
    sin = sin_ref[...]
    hq = N_HEADS * HEAD_DIM
    kd = N_KV_HEADS * HEAD_DIM

    def norm_rope(t, g, post):
        t = t * lax.rsqrt(jnp.mean(t * t, axis=-1, keepdims=True) + QK_EPS) * g
        t = t * cos + pltpu.roll(t, HEAD_DIM // 2, axis=1) * sin
        return (t * post).astype(BF16)

    for hd in range(N_HEADS):
        sl = slice(hd * HEAD_DIM, (hd + 1) * HEAD_DIM)
        q_ref[:, sl] = norm_rope(qkv[:, sl], qg_ref[...], HEAD_DIM ** -0.5 * LOG2_E)
    for hd in range(N_KV_HEADS):
        sl = slice(hd * HEAD_DIM, (hd + 1) * HEAD_DIM)
        k_ref[:, sl] = norm_rope(qkv[:, hq + hd * HEAD_DIM: hq + (hd + 1) * HEAD_DIM], kg_ref[...], 1.0)
    v_ref[...] = qkv[:, hq + kd:].astype(BF16)


def _rope_tables(seq, tm):
    rows = seq // GRID_W
    row = jnp.repeat(jnp.arange(rows, dtype=F32), GRID_W)
    col = jnp.tile(jnp.arange(GRID_W, dtype=F32), rows)
    axis_dim = HEAD_DIM // 2
    freqs = ROPE_THETA ** (-jnp.arange(0, axis_dim, 2, dtype=F32) / axis_dim)
    ang = jnp.concatenate([row[:, None] * freqs, col[:, None] * freqs], axis=-1)
    cos = jnp.concatenate([jnp.cos(ang), jnp.cos(ang)], axis=-1)
    sin = jnp.concatenate([-jnp.sin(ang), jnp.sin(ang)], axis=-1)
    cos = jnp.concatenate([cos, jnp.ones((tm, HEAD_DIM), F32)], axis=0)
    sin = jnp.concatenate([sin, jnp.zeros((tm, HEAD_DIM), F32)], axis=0)
    return cos.reshape(seq // tm + 1, tm, HEAD_DIM), sin.reshape(seq // tm + 1, tm, HEAD_DIM)


def _qkv_project(xall, mod, w_qkv, q_g, k_g, n_lat, seq, ctx_row):
    t, d = xall.shape
    tm = ROW_TILE
    hq = N_HEADS * HEAD_DIM
    kd = N_KV_HEADS * HEAD_DIM
    perm = jnp.concatenate([jnp.arange(0, HEAD_DIM, 2), jnp.arange(1, HEAD_DIM, 2)])
    cols = jnp.concatenate([hd * HEAD_DIM + perm for hd in range(N_HEADS + N_KV_HEADS)]
                           + [jnp.arange(hq + kd, hq + 2 * kd)])
    w = w_qkv[:, cols].astype(BF16)
    cos, sin = _rope_tables(seq, tm)
    n_pos = seq // tm

    def pos_index(i):
        row0 = i * tm
        return (jnp.where(row0 < n_lat, (row0 % seq) // tm, n_pos), 0, 0)

    const = lambda i: (0, 0)
    return pl.pallas_call(
        _qkv_kernel,
        grid=(t // tm,),
        in_specs=[
            pl.BlockSpec((tm, d), lambda i: (i, 0)),
            _mod_spec(0, tm, n_lat, seq, ctx_row, d),
            _mod_spec(1, tm, n_lat, seq, ctx_row, d),
            pl.BlockSpec(w.shape, const),
            pl.BlockSpec((1, HEAD_DIM), const),
            pl.BlockSpec((1, HEAD_DIM), const),
            pl.BlockSpec((None, tm, HEAD_DIM), pos_index),
            pl.BlockSpec((None, tm, HEAD_DIM), pos_index),
        ],
        out_specs=[
            pl.BlockSpec((tm, hq), lambda i: (i, 0)),
            pl.BlockSpec((tm, kd), lambda i: (i, 0)),
            pl.BlockSpec((tm, kd), lambda i: (i, 0)),
        ],
        out_shape=[
            jax.ShapeDtypeStruct((t, hq), BF16),
            jax.ShapeDtypeStruct((t, kd), BF16),
            jax.ShapeDtypeStruct((t, kd), BF16),
        ],
        compiler_params=_cparams(("parallel",)),
        name="qkv_norm_rope",
    )(xall, mod, mod, w, q_g[perm].reshape(1, HEAD_DIM), k_g[perm].reshape(1, HEAD_DIM), cos, sin)


def _attn_kernel(*refs, n_lat_chunks, tk):
    if n_lat_chunks:
        q_ref, kc_ref, vc_ref, kl_ref, vl_ref, o_ref = refs
    else:
        q_ref, kc_ref, vc_ref, o_ref = refs
    tq = q_ref.shape[0]
    q = jnp.concatenate([q_ref[:, h * HEAD_DIM:(h + 1) * HEAD_DIM] for h in range(KV_GROUP)], axis=0)
    rows = KV_GROUP * tq

    def chunk(k, v, m, l, acc):
        s = lax.dot_general(q, k, (((1,), (1,)), ((), ())), preferred_element_type=F32)
        m_new = jnp.maximum(m, jnp.max(s, axis=-1, keepdims=True))
        p = jnp.exp2(s - m_new)
        a = jnp.exp2(m - m_new)
        l = a * l + jnp.sum(p, axis=-1, keepdims=True)
        acc = a * acc + jnp.dot(p.astype(BF16), v, preferred_element_type=F32)
        return m_new, l, acc

    m = jnp.full((rows, 1), -jnp.inf, F32)
    l = jnp.zeros((rows, 1), F32)
    acc = jnp.zeros((rows, HEAD_DIM), F32)
    m, l, acc = chunk(kc_ref[...], vc_ref[...], m, l, acc)
    for c in range(n_lat_chunks):
        m, l, acc = chunk(kl_ref[c * tk:(c + 1) * tk, :], vl_ref[c * tk:(c + 1) * tk, :], m, l, acc)
    o = (acc / l).astype(BF16)
    for h in range(KV_GROUP):
        o_ref[:, h * HEAD_DIM:(h + 1) * HEAD_DIM] = o[h * tq:(h + 1) * tq]


def _attention(q, k, v, b, seq, cl, n_lat):
    t = q.shape[0]
    gw = KV_GROUP * HEAD_DIM
    tq = min(ATTN_TQ, seq)
    tk = min(ATTN_TK, seq)
    nq = seq // tq
    ctx_blk0 = n_lat // cl
    hq = N_HEADS * HEAD_DIM

    ctx_kv = pl.BlockSpec((cl, HEAD_DIM), lambda bi, g, qi: (ctx_blk0 + bi, g))
    lat_kv = pl.BlockSpec((seq, HEAD_DIM), lambda bi, g, qi: (bi, g))
    lat_q = pl.BlockSpec((tq, gw), lambda bi, g, qi: (bi * nq + qi, g))
    o_lat = pl.pallas_call(
        functools.partial(_attn_kernel, n_lat_chunks=seq // tk, tk=tk),
        grid=(b, N_KV_HEADS, nq),
        in_specs=[lat_q, ctx_kv, ctx_kv, lat_kv, lat_kv],
        out_specs=lat_q,
        out_shape=jax.ShapeDtypeStruct((n_lat, hq), BF16),
        compiler_params=_cparams(("parallel", "parallel", "arbitrary")),
        name="attention_latent",
    )(q, k, v, k, v)

    ctx_q = pl.BlockSpec((cl, gw), lambda bi, g: (ctx_blk0 + bi, g))
    ctx_kv2 = pl.BlockSpec((cl, HEAD_DIM), lambda bi, g: (ctx_blk0 + bi, g))
    o_ctx = pl.pallas_call(
        functools.partial(_attn_kernel, n_lat_chunks=0, tk=tk),
        grid=(b, N_KV_HEADS),
        in_specs=[ctx_q, ctx_kv2, ctx_kv2],
        out_specs=pl.BlockSpec((cl, gw), lambda bi, g: (bi, g)),
        out_shape=jax.ShapeDtypeStruct((t - n_lat, hq), BF16),
        compiler_params=_cparams(("parallel", "parallel")),
        name="attention_context",
    )(q, k, v)
    return jnp.concatenate([o_lat, o_ctx], axis=0)


def _in_proj_kernel(x_ref, shift_ref, scale_ref, w_ref, o_ref):
    h = (x_ref[...] * (1.0 + scale_ref[...]) + shift_ref[...]).astype(BF16)
    o_ref[...] = jnp.dot(h, w_ref[...], preferred_element_type=F32)


def _in_project(xall, mod, w_in, n_rows, n_lat, seq, ctx_row):
    d = xall.shape[1]
    n_out = w_in.shape[1]
    tm = ROW_TILE
    return pl.pallas_call(
        _in_proj_kernel,
        grid=(n_rows // tm,),
        in_specs=[
            pl.BlockSpec((tm, d), lambda i: (i, 0)),
            _mod_spec(0, tm, n_lat, seq, ctx_row, d),
            _mod_spec(1, tm, n_lat, seq, ctx_row, d),
            pl.BlockSpec(w_in.shape, lambda i: (0, 0)),
        ],
        out_specs=pl.BlockSpec((tm, n_out), lambda i: (i, 0)),
        out_shape=jax.ShapeDtypeStruct((n_rows, n_out), F32),
        compiler_params=_cparams(("parallel",)),
        name="conv_in_proj",
    )(xall, mod, mod, w_in.astype(BF16))


def _conv_kernel(bg_ref, cg_ref, v_ref, taps_ref, o_ref):
    u = cg_ref[...] * v_ref[...]
    n = u.shape[0]
    pos = lax.broadcasted_iota(I32, u.shape, 0)
    prev = jnp.where(pos == 0, 0.0, pltpu.roll(u, 1, axis=0))
    nxt = jnp.where(pos == n - 1, 0.0, pltpu.roll(u, n - 1, axis=0))
    conv = prev * taps_ref[0:1, :] + u * taps_ref[1:2, :] + nxt * taps_ref[2:3, :]
    o_ref[...] = (bg_ref[...] * conv).astype(BF16)


def _short_conv(zin, taps, n_seqs, seq, d):
    tc = CONV_LANES
    nj = d // tc
    return pl.pallas_call(
        _conv_kernel,
        grid=(n_seqs, nj),
        in_specs=[
            pl.BlockSpec((seq, tc), lambda s, j: (s, j)),
            pl.BlockSpec((seq, tc), lambda s, j: (s, nj + j)),
            pl.BlockSpec((seq, tc), lambda s, j: (s, 2 * nj + j)),
            pl.BlockSpec((taps.shape[0], tc), lambda s, j: (0, j)),
        ],
        out_specs=pl.BlockSpec((seq, tc), lambda s, j: (s, j)),
        out_shape=jax.ShapeDtypeStruct((n_seqs * seq, d), BF16),
        compiler_params=_cparams(("parallel", "parallel")),
        name="short_conv",
    )(zin, zin, zin, taps)


def _layer_norm(z, g, b):
    mu = jnp.mean(z, axis=-1, keepdims=True)
    zc = z - mu
    var = jnp.mean(zc * zc, axis=-1, keepdims=True)
    return zc * lax.rsqrt(var + LN_EPS) * g + b


def _post_kernel(a_ref, x_ref, gate_ref, lng_ref, lnb_ref, shift_ref, scale_ref, w_ref,
                 rwt_ref, rb_ref, tri_ref,
                 x1_ref, h2_ref, eidx_ref, gates_ref, rank_ref, cnt_ref, rwh_ref, rwl_ref, *, dn_alpha):
    tm = a_ref.shape[0]

    @pl.when(pl.program_id(0) == 0)
    def _():
        hi = rwt_ref[...].astype(BF16)
        rwh_ref[...] = hi
        rwl_ref[...] = (rwt_ref[...] - hi.astype(F32)).astype(BF16)

    d = x_ref.shape[1]
    y = jnp.dot(a_ref[...], w_ref[...], preferred_element_type=F32)
    x1 = _layer_norm(dn_alpha * x_ref[...] + gate_ref[...] * y, lng_ref[...], lnb_ref[...])
    x1_ref[...] = x1
    h2 = x1 * (1.0 + scale_ref[...]) + shift_ref[...]
    h2_ref[...] = h2

    h_hi = h2.astype(BF16)
    h_lo = (h2 - h_hi.astype(F32)).astype(BF16)

    def nt_dot(w, h):
        return lax.dot_general(w, h, (((1,), (1,)), ((), ())), preferred_element_type=F32)

    logits = nt_dot(rwh_ref[...], h_hi) + (nt_dot(rwl_ref[...], h_hi) + nt_dot(rwh_ref[...], h_lo))
    scores = jax.nn.sigmoid(logits)
    biased = scores + rb_ref[...]
    neg = -jnp.inf
    big = jnp.int32(1 << 30)
    row = lax.broadcasted_iota(I32, (N_EXPERTS, tm), 0)

    def argmax_rows(vals, idx):
        mx = jnp.max(vals, axis=0, keepdims=True)
        return mx, jnp.min(jnp.where(vals == mx, idx, big), axis=0, keepdims=True)

    gs = []
    grp_row = lax.broadcasted_iota(I32, (PER_GROUP, tm), 0)
    for g in range(N_GROUPS):
        bg = biased[g * PER_GROUP:(g + 1) * PER_GROUP]
        ig = grp_row + g * PER_GROUP
        m1, i1 = argmax_rows(bg, ig)
        m2 = jnp.max(jnp.where(ig == i1, neg, bg), axis=0, keepdims=True)
        gs.append(m1 + m2)
    gsc = jnp.concatenate(gs, axis=0)
    grow = lax.broadcasted_iota(I32, (N_GROUPS, tm), 0)
    gsel = jnp.zeros((N_GROUPS, tm), F32)
    for _ in range(TOPK_GROUPS):
        _, gi = argmax_rows(gsc, grow)
        hit = grow == gi
        gsel = jnp.where(hit, 1.0, gsel)
        gsc = jnp.where(hit, neg, gsc)
    cur = jnp.concatenate(
        [jnp.where(gsel[g:g + 1] > 0.0, biased[g * PER_GROUP:(g + 1) * PER_GROUP], neg)
         for g in range(N_GROUPS)], axis=0)

    onehot = jnp.zeros((N_EXPERTS, tm), F32)
    idxs, gvals = [], []
    for _ in range(TOP_K):
        _, ei = argmax_rows(cur, row)
        hit = row == ei
        gvals.append(jnp.sum(jnp.where(hit, scores, 0.0), axis=0, keepdims=True))
        idxs.append(ei)
        onehot = jnp.where(hit, 1.0, onehot)
        cur = jnp.where(hit, neg, cur)
    gv = jnp.concatenate(gvals, axis=0)
    gates_ref[...] = gv / jnp.sum(gv, axis=0, keepdims=True) * ROUTED_SCALE
    eidx_ref[...] = jnp.concatenate(idxs, axis=0)

    @pl.when(pl.program_id(0) == 0)
    def _():
        cnt_ref[...] = jnp.zeros_like(cnt_ref)

    prefix = jnp.dot(onehot.astype(BF16), tri_ref[...], preferred_element_type=F32)
    pos = prefix + cnt_ref[...]
    rank_ref[...] = jnp.concatenate(
        [jnp.sum(jnp.where(row == ei, pos, 0.0), axis=0, keepdims=True) for ei in idxs],
        axis=0).astype(I32)
    cnt_ref[...] = cnt_ref[...] + jnp.sum(onehot, axis=1, keepdims=True)


def _post_mixer(a, xall, mod, ln_g, ln_b, w, router_w, router_bias, n_rows, n_lat, seq, ctx_row,
                dn_alpha):
    d = xall.shape[1]
    tm = ROW_TILE
    tri = (lax.broadcasted_iota(I32, (tm, tm), 0) < lax.broadcasted_iota(I32, (tm, tm), 1)).astype(BF16)
    const = lambda i: (0, 0)
    row_blk = pl.BlockSpec((tm, d), lambda i: (i, 0))
    k_blk = pl.BlockSpec((TOP_K, tm), lambda i: (0, i))
    return pl.pallas_call(
        functools.partial(_post_kernel, dn_alpha=dn_alpha),
        grid=(n_rows // tm,),
        in_specs=[
            row_blk, row_blk,
            _mod_spec(2, tm, n_lat, seq, ctx_row, d),
            pl.BlockSpec((1, d), const), pl.BlockSpec((1, d), const),
            _mod_spec(3, tm, n_lat, seq, ctx_row, d),
            _mod_spec(4, tm, n_lat, seq, ctx_row, d),
            pl.BlockSpec(w.shape, const),
            pl.BlockSpec((N_EXPERTS, d), const),
            pl.BlockSpec((N_EXPERTS, 1), const),
            pl.BlockSpec((tm, tm), const),
        ],
        out_specs=[
            row_blk, row_blk,
            k_blk, k_blk, k_blk,
            pl.BlockSpec((N_EXPERTS, 1), const),
        ],
        out_shape=[
            jax.ShapeDtypeStruct((n_rows, d), F32),
            jax.ShapeDtypeStruct((n_rows, d), F32),
            jax.ShapeDtypeStruct((TOP_K, n_rows), I32),
            jax.ShapeDtypeStruct((TOP_K, n_rows), F32),
            jax.ShapeDtypeStruct((TOP_K, n_rows), I32),
            jax.ShapeDtypeStruct((N_EXPERTS, 1), F32),
        ],
        scratch_shapes=[pltpu.VMEM((N_EXPERTS, d), BF16), pltpu.VMEM((N_EXPERTS, d), BF16)],
        compiler_params=_cparams(("arbitrary",)),
        name="post_mixer_router",
    )(a, xall, mod, ln_g.reshape(1, d), ln_b.reshape(1, d), mod, mod, w.astype(BF16),
      router_w.T, router_bias.reshape(N_EXPERTS, 1), tri)


def _slots_kernel(pstart_ref, eidx_ref, rank_ref, dest_ref):
    e = eidx_ref[...]

    def pick(i, acc):
        return jnp.where(e == i, pstart_ref[i], acc)

    dest_ref[...] = lax.fori_loop(0, N_EXPERTS, pick, jnp.zeros_like(e)) + rank_ref[...]


def _slots(pstarts, eidx, rank):
    n_tok = eidx.shape[1]
    tn = 2048 if n_tok % 2048 == 0 else ROW_TILE
    blk = pl.BlockSpec((TOP_K, tn), lambda i, ps: (0, i))
    return pl.pallas_call(
        _slots_kernel,
        grid_spec=pltpu.PrefetchScalarGridSpec(
            num_scalar_prefetch=1, grid=(n_tok // tn,), in_specs=[blk, blk], out_specs=blk),
        out_shape=jax.ShapeDtypeStruct((TOP_K, n_tok), I32),
        compiler_params=_cparams(("arbitrary",)),
        name="moe_slots",
    )(pstarts, eidx, rank)


def _dispatch_kernel(pends_ref, pcounts_ref, dest_ref, h_ref, xs_hbm, zbuf, sem):
    td = dest_ref.shape[1]
    bm = zbuf.shape[0]

    @pl.when(pl.program_id(0) == 0)
    def _():
        zbuf[...] = jnp.zeros_like(zbuf)

        def tail(e):
            return xs_hbm.at[pl.ds(pl.multiple_of(pends_ref[e] - bm, bm), bm)]

        def zstart(e, carry):
            @pl.when(pcounts_ref[e] > 0)
            def _():
                pltpu.make_async_copy(zbuf, tail(e), sem).start()
            return carry

        def zwait(e, carry):
            @pl.when(pcounts_ref[e] > 0)
            def _():
                pltpu.make_async_copy(zbuf, tail(e), sem).wait()
            return carry

        lax.fori_loop(0, N_EXPERTS, zstart, 0)
        lax.fori_loop(0, N_EXPERTS, zwait, 0)

        def unused(j):
            return xs_hbm.at[pl.ds(pl.multiple_of(j * bm, bm), bm)]

        def ustart(j, carry):
            pltpu.make_async_copy(zbuf, unused(j), sem).start()
            return carry

        def uwait(j, carry):
            pltpu.make_async_copy(zbuf, unused(j), sem).wait()
            return carry

        first_unused = pends_ref[N_EXPERTS - 1] // bm
        lax.fori_loop(first_unused, xs_hbm.shape[0] // bm, ustart, 0)
        lax.fori_loop(first_unused, xs_hbm.shape[0] // bm, uwait, 0)

    def issue(t, carry):
        for k in range(TOP_K):
            pltpu.make_async_copy(h_ref.at[pl.ds(t, 1)], xs_hbm.at[pl.ds(dest_ref[k, t], 1)], sem).start()
        return carry

    lax.fori_loop(0, td, issue, 0)
    pltpu.make_async_copy(xs_hbm.at[pl.ds(0, TOP_K * td)], xs_hbm.at[pl.ds(0, TOP_K * td)], sem).wait()


def _dispatch(dest, h2, pends, pcounts, n_slots):
    n_tok, d = h2.shape
    td = DISPATCH_TOK
    grid_spec = pltpu.PrefetchScalarGridSpec(
        num_scalar_prefetch=2,
        grid=(n_tok // td,),
        in_specs=[
            pl.BlockSpec((TOP_K, td), lambda i, pe, pc: (0, i), memory_space=pltpu.SMEM),
            pl.BlockSpec((td, d), lambda i, pe, pc: (i, 0)),
        ],
        out_specs=pl.BlockSpec(memory_space=pl.ANY),
        scratch_shapes=[pltpu.VMEM((MOE_BM, d), h2.dtype), pltpu.SemaphoreType.DMA(())],
    )
    return pl.pallas_call(
        _dispatch_kernel,
        grid_spec=grid_spec,
        out_shape=jax.ShapeDtypeStruct((n_slots, d), h2.dtype),
        compiler_params=_cparams(("arbitrary",)),
        name="moe_dispatch",
    )(pends, pcounts, dest, h2)


def _gmm_kernel(blk0_ref, nblk_ref, n_used_ref, xs_hbm, wgu_ref, wd_ref, ys_hbm,
                xbuf, ybuf, wgu_bf, wd_bf, sem_in, sem_out):
    e = pl.program_id(0)
    n_in, bm = xbuf.shape[:2]
    n_out = ybuf.shape[0]
    ff = wd_ref.shape[0]
    n_used = n_used_ref[0]

    def rows(b):
        return pl.ds(pl.multiple_of(b * bm, bm), bm)

    def x_copy(b):
        return pltpu.make_async_copy(xs_hbm.at[rows(b)], xbuf.at[b % n_in], sem_in.at[b % n_in])

    def y_copy(b, slot):
        return pltpu.make_async_copy(ybuf.at[slot], ys_hbm.at[rows(b)], sem_out.at[slot])

    @pl.when(e == 0)
    def _():
        for b in range(n_in - 1):
            @pl.when(b < n_used)
            def _():
                x_copy(b).start()

    nb = nblk_ref[e]

    @pl.when(nb > 0)
    def _():
        wgu_bf[...] = wgu_ref[...].astype(BF16)
        wd_bf[...] = wd_ref[...].astype(BF16)

    def block(j, carry):
        b = blk0_ref[e] + j
        slot = b % n_out
        x_copy(b).wait()

        @pl.when(b + n_in - 1 < n_used)
        def _():
            x_copy(b + n_in - 1).start()

        @pl.when(b >= n_out)
        def _():
            y_copy(b - n_out, slot).wait()

        gu = jnp.dot(xbuf[b % n_in].astype(BF16), wgu_bf[...], preferred_element_type=F32)
        act = (_silu(gu[:, :ff]) * gu[:, ff:]).astype(BF16)
        ybuf[slot] = jnp.dot(act, wd_bf[...], preferred_element_type=F32)
        y_copy(b, slot).start()
        return carry

    lax.fori_loop(0, nb, block, 0)

    @pl.when(e == pl.num_programs(0) - 1)
    def _():
        for back in range(n_out, 0, -1):
            @pl.when(n_used >= back)
            def _():
                y_copy(n_used - back, (n_used - back) % n_out).wait()

        ybuf[0] = jnp.zeros(ybuf.shape[1:], ybuf.dtype)
        n_total = ys_hbm.shape[0] // bm

        def zstart(b, carry):
            y_copy(b, 0).start()
            return carry

        def zwait(b, carry):
            y_copy(b, 0).wait()
            return carry

        lax.fori_loop(n_used, n_total, zstart, 0)
        lax.fori_loop(n_used, n_total, zwait, 0)


def _grouped_ffn(xs, blk0, nblk, n_used, wgu, wd, layer):
    n_slots = xs.shape[0]
    bm = MOE_BM
    n_exp, d, ff2 = wgu.shape[1:]
    ff = wd.shape[2]
    grid_spec = pltpu.PrefetchScalarGridSpec(
        num_scalar_prefetch=3,
        grid=(n_exp,),
        in_specs=[
            pl.BlockSpec(memory_space=pl.ANY),
            pl.BlockSpec((None, None, d, ff2), lambda e, b0, nb, nu: (layer, e, 0, 0)),
            pl.BlockSpec((None, None, ff, d), lambda e, b0, nb, nu: (layer, e, 0, 0)),
        ],
        out_specs=pl.BlockSpec(memory_space=pl.ANY),
        scratch_shapes=[
            pltpu.VMEM((GMM_IN_SLOTS, bm, d), F32), pltpu.VMEM((GMM_OUT_SLOTS, bm, d), F32),
            pltpu.VMEM((d, ff2), BF16), pltpu.VMEM((ff, d), BF16),
            pltpu.SemaphoreType.DMA((GMM_IN_SLOTS,)), pltpu.SemaphoreType.DMA((GMM_OUT_SLOTS,)),
        ],
    )
    return pl.pallas_call(
        _gmm_kernel,
        grid_spec=grid_spec,
        out_shape=jax.ShapeDtypeStruct((n_slots, d), F32),
        compiler_params=_cparams(("arbitrary",)),
        name="moe_grouped_ffn",
    )(blk0, nblk, n_used, xs, wgu, wd)


def _shared_expert(h2_ref, swgu_ref, swd_ref):
    ff = swd_ref.shape[0]
    gu = jnp.dot(h2_ref[...].astype(BF16), swgu_ref[...], preferred_element_type=F32)
    act = (_silu(gu[:, :ff]) * gu[:, ff:]).astype(BF16)
    return jnp.dot(act, swd_ref[...], preferred_element_type=F32)


def _combine_kernel(dest_ref, gates_ref, ys_hbm, x1_ref, h2_ref, gate_ref, lng_ref, lnb_ref,
                    swgu_ref, swd_ref, o_ref, buf, sem, *, dn_alpha):
    tc = x1_ref.shape[0]

    def issue(t, carry):
        for k in range(TOP_K):
            pltpu.make_async_copy(ys_hbm.at[pl.ds(dest_ref[k, t], 1)], buf.at[k, pl.ds(t, 1)], sem).start()
        return carry

    lax.fori_loop(0, tc, issue, 0)

    y = _shared_expert(h2_ref, swgu_ref, swd_ref)

    for k in range(TOP_K):
        pltpu.make_async_copy(ys_hbm.at[pl.ds(0, tc)], buf.at[k], sem).wait()
    for k in range(TOP_K):
        y = y + gates_ref[:, k:k + 1] * buf[k]
    o_ref[...] = _layer_norm(dn_alpha * x1_ref[...] + gate_ref[...] * y, lng_ref[...], lnb_ref[...])


def _combine_gathered_kernel(gates_ref, g_ref, x1_ref, h2_ref, gate_ref, lng_ref, lnb_ref,
                             swgu_ref, swd_ref, prev_ref, o_ref, *, dn_alpha):
    del prev_ref
    y = _shared_expert(h2_ref, swgu_ref, swd_ref)
    for k in range(TOP_K):
        y = y + gates_ref[:, k:k + 1] * g_ref[k]
    o_ref[...] = _layer_norm(dn_alpha * x1_ref[...] + gate_ref[...] * y, lng_ref[...], lnb_ref[...])


def _sc_gather_rows(table, idx):
    n = idx.shape[0]
    d = table.shape[1]
    mesh = plsc.VectorSubcoreMesh(core_axis_name="c", subcore_axis_name="s")
    n_workers = mesh.num_cores * mesh.num_subcores
    ch = SC_GATHER_ROWS
    per_w = n // n_workers
    assert n % (n_workers * ch) == 0

    @functools.partial(
        pl.kernel, mesh=mesh, out_type=jax.ShapeDtypeStruct((n, d), table.dtype),
        scratch_types=[pltpu.VMEM((ch,), I32), pltpu.VMEM((ch, d), table.dtype), pltpu.SemaphoreType.DMA],
        name="moe_sc_gather")
    def gather(table_hbm, idx_hbm, out_hbm, idx_v, rows_v, sem):
        wid = lax.axis_index("s") * mesh.num_cores + lax.axis_index("c")

        @pl.loop(0, per_w // ch)
        def _(i):
            base = wid * per_w + i * ch
            pltpu.sync_copy(idx_hbm.at[pl.ds(base, ch)], idx_v)
            pltpu.async_copy(table_hbm.at[idx_v], rows_v, sem).wait()
            pltpu.sync_copy(rows_v, out_hbm.at[pl.ds(base, ch)])

    return gather(table, idx)


def _combine(dest, gates_t, ys, x1, h2, mod, ln_g, ln_b, swgu, swd, n_rows, n_lat, seq, ctx_row,
             dn_alpha):
    d = x1.shape[1]
    tc = DISPATCH_TOK
    const = lambda i: (0, 0)
    n_blocks = n_rows // tc
    n_sc_blocks = (n_blocks * SC_SHARE_PERCENT) // 100
    n_tc = (n_blocks - n_sc_blocks) * tc
    n_sc = n_rows - n_tc
    gathered = _sc_gather_rows(ys, dest[:, n_tc:].reshape(TOP_K * n_sc)).reshape(TOP_K, n_sc, d)
    swgu_bf, swd_bf = swgu.astype(BF16), swd.astype(BF16)
    out_tc = pl.pallas_call(
        functools.partial(_combine_kernel, dn_alpha=dn_alpha),
        grid=(n_tc // tc,),
        in_specs=[
            pl.BlockSpec((TOP_K, tc), lambda i: (0, i), memory_space=pltpu.SMEM),
            pl.BlockSpec((tc, TOP_K), lambda i: (i, 0)),
            pl.BlockSpec(memory_space=pl.ANY),
            pl.BlockSpec((tc, d), lambda i: (i, 0)),
            pl.BlockSpec((tc, d), lambda i: (i, 0)),
            _mod_spec(5, tc, n_lat, seq, ctx_row, d),
            pl.BlockSpec((1, d), const), pl.BlockSpec((1, d), const),
            pl.BlockSpec(swgu.shape, const),
            pl.BlockSpec(swd.shape, const),
        ],
        out_specs=pl.BlockSpec((tc, d), lambda i: (i, 0)),
        out_shape=jax.ShapeDtypeStruct((n_rows, d), F32),
        scratch_shapes=[pltpu.VMEM((TOP_K, tc, d), F32), pltpu.SemaphoreType.DMA(())],
        compiler_params=_cparams(("arbitrary",)),
        name="moe_combine",
    )(dest, gates_t, ys, x1, h2, mod, ln_g.reshape(1, d), ln_b.reshape(1, d), swgu_bf, swd_bf)

    tl = COMBINE_GATHERED_TOK
    off = n_tc // tl
    row_blk = pl.BlockSpec((tl, d), lambda i: (i + off, 0))
    return pl.pallas_call(
        functools.partial(_combine_gathered_kernel, dn_alpha=dn_alpha),
        grid=(n_sc // tl,),
        in_specs=[
            pl.BlockSpec((tl, TOP_K), lambda i: (i + off, 0)),
            pl.BlockSpec((TOP_K, tl, d), lambda i: (0, i, 0)),
            row_blk, row_blk,
            _mod_spec(5, tl, n_lat, seq, ctx_row, d, blk_off=off),
            pl.BlockSpec((1, d), const), pl.BlockSpec((1, d), const),
            pl.BlockSpec(swgu.shape, const),
            pl.BlockSpec(swd.shape, const),
            pl.BlockSpec(memory_space=pl.ANY),
        ],
        out_specs=row_blk,
        out_shape=jax.ShapeDtypeStruct((n_rows, d), F32),
        input_output_aliases={9: 0},
        compiler_params=_cparams(("parallel",)),
        name="moe_combine_gathered",
    )(gates_t, gathered, x1, h2, mod, ln_g.reshape(1, d), ln_b.reshape(1, d), swgu_bf, swd_bf, out_tc)


def _moe(x1, h2, eidx, gates, rank, counts, mod, ln_g, ln_b, wgu, wd, layer, swgu, swd,
         n_rows, n_lat, seq, ctx_row, dn_alpha):
    bm = MOE_BM
    counts = counts.reshape(N_EXPERTS).astype(I32)
    pcounts = (counts + bm - 1) // bm * bm
    pends = jnp.cumsum(pcounts)
    pstarts = pends - pcounts
    dest = _slots(pstarts, eidx, rank)
    n_blk = (n_rows * TOP_K + N_EXPERTS * (bm - 1) + bm - 1) // bm
    n_used = (pends[-1] // bm).astype(I32).reshape(1)
    xs = _dispatch(dest, h2, pends, pcounts, n_blk * bm)
    ys = _grouped_ffn(xs, pstarts // bm, pcounts // bm, n_used, wgu, wd, layer)
    return _combine(dest, gates.T, ys, x1, h2, mod, ln_g, ln_b, swgu, swd,
                    n_rows, n_lat, seq, ctx_row, dn_alpha)


def kernel(x, c, ctx, c_ctx, ada_w, ada_b, ln_g, ln_b, attn_w_qkv, attn_q_norm, attn_k_norm, attn_w_o, conv_w_in, conv_taps, conv_w_out, router_w, router_bias, exp_w_gate_up, exp_w_down, shared_w_gate_up, shared_w_down):
    b, seq, d = x.shape
    cl = ctx.shape[1]
    depth = ada_w.shape[0]
    n_lat = b * seq
    n_ctx = b * cl
    dn_alpha = (2 * depth) ** 0.25
    assert depth == 2 and b < MOD_ROWS
    assert seq % ROW_TILE == 0 and n_ctx % ROW_TILE == 0 and seq % GRID_W == 0

    cond = jnp.zeros((MOD_ROWS, d), F32).at[:b].set(c).at[b].set(c_ctx)
    mod = _modulation(cond, ada_w, ada_b)
    xall = jnp.concatenate([x.reshape(n_lat, d), ctx.reshape(n_ctx, d)], axis=0)
    n_all = n_lat + n_ctx

    q, k, v = _qkv_project(xall, mod[0], attn_w_qkv[0], attn_q_norm[0], attn_k_norm[0], n_lat, seq, b)
    o = _attention(q, k, v, b, seq, cl, n_lat)
    x1, h2, eidx, gates, rank, counts = _post_mixer(
        o, xall, mod[0], ln_g[0, 0], ln_b[0, 0], attn_w_o[0], router_w[0], router_bias[0],
        n_all, n_lat, seq, b, dn_alpha)
    xall = _moe(x1, h2, eidx, gates, rank, counts, mod[0], ln_g[0, 1], ln_b[0, 1],
                exp_w_gate_up, exp_w_down, 0, shared_w_gate_up[0], shared_w_down[0],
                n_all, n_lat, seq, b, dn_alpha)

    zin = _in_project(xall, mod[1], conv_w_in[0], n_lat, n_lat, seq, b)
    a = _short_conv(zin, conv_taps[0], b, seq, d)
    x1, h2, eidx, gates, rank, counts = _post_mixer(
        a, xall, mod[1], ln_g[1, 0], ln_b[1, 0], conv_w_out[0], router_w[1], router_bias[1],
        n_lat, n_lat, seq, b, dn_alpha)
    out = _moe(x1, h2, eidx, gates, rank, counts, mod[1], ln_g[1, 1], ln_b[1, 1],
               exp_w_gate_up, exp_w_down, 1, shared_w_gate_up[1], shared_w_down[1],
               n_lat, n_lat, seq, b, dn_alpha)
    return out.reshape(b, seq, d)
```

```python
import functools

import jax
import jax.numpy as jnp
from jax import lax
from jax.experimental import pallas as pl
from jax.experimental.pallas import tpu as pltpu
from jax.experimental.pallas import tpu_sc as plsc

F32 = jnp.float32
BF16 = jnp.bfloat16
I32 = jnp.int32

N_HEADS = 8
N_KV_HEADS = 2
HEAD_DIM = 128
KV_GROUP = N_HEADS // N_KV_HEADS
GRID_W = 64
ROPE_THETA = 10000.0
N_EXPERTS = 256
TOP_K = 8
N_GROUPS = 8
TOPK_GROUPS = 4
PER_GROUP = N_EXPERTS // N_GROUPS
ROUTED_SCALE = 2.5
LN_EPS = 1e-5
QK_EPS = 1e-6
N_MOD = 6
MOD_ROWS = 16

LANES = 128
SUBLANES = 8
VMEM_LIMIT = 56 * 1024 * 1024

ROW_TILE = 512
ATTN_TQ = 256
ATTN_TK = 2048
MOE_BM = 256
GMM_IN_SLOTS = 4
GMM_OUT_SLOTS = 3
DISPATCH_TOK = 512
COMBINE_GATHERED_TOK = 256
SC_GATHER_ROWS = 32
SC_SHARE_PERCENT = 65
CONV_LANES = 128

HIGHEST = lax.Precision.HIGHEST
LOG2_E = 1.4426950408889634


def _cparams(sem):
    return pltpu.CompilerParams(dimension_semantics=sem, vmem_limit_bytes=VMEM_LIMIT)


def _silu(v):
    return v * jax.nn.sigmoid(v)


def _mod_kernel(c_ref, w_ref, b_ref, o_ref):
    s = _silu(c_ref[...])
    o_ref[...] = jnp.dot(s, w_ref[...], precision=HIGHEST, preferred_element_type=F32) + b_ref[...]


def _modulation(cond, ada_w, ada_b):
    depth, d, nd = ada_w.shape
    tn = 1536
    out = pl.pallas_call(
        _mod_kernel,
        grid=(depth, nd // tn),
        in_specs=[
            pl.BlockSpec((MOD_ROWS, d), lambda l, j: (0, 0)),
            pl.BlockSpec((None, d, tn), lambda l, j: (l, 0, j)),
            pl.BlockSpec((None, 1, tn), lambda l, j: (l, 0, j)),
        ],
        out_specs=pl.BlockSpec((None, MOD_ROWS, tn), lambda l, j: (l, 0, j)),
        out_shape=jax.ShapeDtypeStruct((depth, MOD_ROWS, nd), F32),
        compiler_params=_cparams(("arbitrary", "arbitrary")),
        name="adaln_modulation",
    )(cond, ada_w, ada_b.reshape(depth, 1, nd))
    return out.reshape(depth, MOD_ROWS * N_MOD, 1, d)


def _mod_spec(comp, tm, n_lat, seq, ctx_row, d, blk_off=0):
    def index(i, *_):
        row0 = (i + blk_off) * tm
        r = jnp.where(row0 < n_lat, row0 // seq, ctx_row)
        return (r * N_MOD + comp, 0, 0)

    return pl.BlockSpec((None, 1, d), index)


def _qkv_kernel(x_ref, shift_ref, scale_ref, w_ref, qg_ref, kg_ref, cos_ref, sin_ref,
                q_ref, k_ref, v_ref):
    h = (x_ref[...] * (1.0 + scale_ref[...]) + shift_ref[...]).astype(BF16)
    qkv = jnp.dot(h, w_ref[...], preferred_element_type=F32)
    cos = cos_ref[...]
    sin = sin_ref[...]
    hq = N_HEADS * HEAD_DIM
    kd = N_KV_HEADS * HEAD_DIM

    def norm_rope(t, g, post):
        t = t * lax.rsqrt(jnp.mean(t * t, axis=-1, keepdims=True) + QK_EPS) * g
        t = t * cos + pltpu.roll(t, HEAD_DIM // 2, axis=1) * sin
        return (t * post).astype(BF16)

    for hd in range(N_HEADS):
        sl = slice(hd * HEAD_DIM, (hd + 1) * HEAD_DIM)
        q_ref[:, sl] = norm_rope(qkv[:, sl], qg_ref[...], HEAD_DIM ** -0.5 * LOG2_E)
    for hd in range(N_KV_HEADS):
        sl = slice(hd * HEAD_DIM, (hd + 1) * HEAD_DIM)
        k_ref[:, sl] = norm_rope(qkv[:, hq + hd * HEAD_DIM: hq + (hd + 1) * HEAD_DIM], kg_ref[...], 1.0)
    v_ref[...] = qkv[:, hq + kd:].astype(BF16)


def _rope_tables(seq, tm):
    rows = seq // GRID_W
    row = jnp.repeat(jnp.arange(rows, dtype=F32), GRID_W)
    col = jnp.tile(jnp.arange(GRID_W, dtype=F32), rows)
    axis_dim = HEAD_DIM // 2
    freqs = ROPE_THETA ** (-jnp.arange(0, axis_dim, 2, dtype=F32) / axis_dim)
    ang = jnp.concatenate([row[:, None] * freqs, col[:, None] * freqs], axis=-1)
    cos = jnp.concatenate([jnp.cos(ang), jnp.cos(ang)], axis=-1)
    sin = jnp.concatenate([-jnp.sin(ang), jnp.sin(ang)], axis=-1)
    cos = jnp.concatenate([cos, jnp.ones((tm, HEAD_DIM), F32)], axis=0)
    sin = jnp.concatenate([sin, jnp.zeros((tm, HEAD_DIM), F32)], axis=0)
    return cos.reshape(seq // tm + 1, tm, HEAD_DIM), sin.reshape(seq // tm + 1, tm, HEAD_DIM)


def _qkv_project(xall, mod, w_qkv, q_g, k_g, n_lat, seq, ctx_row):
    t, d = xall.shape
    tm = ROW_TILE
    hq = N_HEADS * HEAD_DIM
    kd = N_KV_HEADS * HEAD_DIM
    perm = jnp.concatenate([jnp.arange(0, HEAD_DIM, 2), jnp.arange(1, HEAD_DIM, 2)])
    cols = jnp.concatenate([hd * HEAD_DIM + perm for hd in range(N_HEADS + N_KV_HEADS)]
                           + [jnp.arange(hq + kd, hq + 2 * kd)])
    w = w_qkv[:, cols].astype(BF16)
    cos, sin = _rope_tables(seq, tm)
    n_pos = seq // tm

    def pos_index(i):
        row0 = i * tm
        return (jnp.where(row0 < n_lat, (row0 % seq) // tm, n_pos), 0, 0)

    const = lambda i: (0, 0)
    return pl.pallas_call(
        _qkv_kernel,
        grid=(t // tm,),
        in_specs=[
            pl.BlockSpec((tm, d), lambda i: (i, 0)),
            _mod_spec(0, tm, n_lat, seq, ctx_row, d),
            _mod_spec(1, tm, n_lat, seq, ctx_row, d),
            pl.BlockSpec(w.shape, const),
            pl.BlockSpec((1, HEAD_DIM), const),
            pl.BlockSpec((1, HEAD_DIM), const),
            pl.BlockSpec((None, tm, HEAD_DIM), pos_index),
            pl.BlockSpec((None, tm, HEAD_DIM), pos_index),
        ],
        out_specs=[
            pl.BlockSpec((tm, hq), lambda i: (i, 0)),
            pl.BlockSpec((tm, kd), lambda i: (i, 0)),
            pl.BlockSpec((tm, kd), lambda i: (i, 0)),
        ],
        out_shape=[
            jax.ShapeDtypeStruct((t, hq), BF16),
            jax.ShapeDtypeStruct((t, kd), BF16),
            jax.ShapeDtypeStruct((t, kd), BF16),
        ],
        compiler_params=_cparams(("parallel",)),
        name="qkv_norm_rope",
    )(xall, mod, mod, w, q_g[perm].reshape(1, HEAD_DIM), k_g[perm].reshape(1, HEAD_DIM), cos, sin)


def _attn_kernel(*refs, n_lat_chunks, tk):
    if n_lat_chunks:
        q_ref, kc_ref, vc_ref, kl_ref, vl_ref, o_ref = refs
    else:
        q_ref, kc_ref, vc_ref, o_ref = refs
    tq = q_ref.shape[0]
    q = jnp.concatenate([q_ref[:, h * HEAD_DIM:(h + 1) * HEAD_DIM] for h in range(KV_GROUP)], axis=0)
    rows = KV_GROUP * tq

    def chunk(k, v, m, l, acc):
        s = lax.dot_general(q, k, (((1,), (1,)), ((), ())), preferred_element_type=F32)
        m_new = jnp.maximum(m, jnp.max(s, axis=-1, keepdims=True))
        p = jnp.exp2(s - m_new)
        a = jnp.exp2(m - m_new)
        l = a * l + jnp.sum(p, axis=-1, keepdims=True)
        acc = a * acc + jnp.dot(p.astype(BF16), v, preferred_element_type=F32)
        return m_new, l, acc

    m = jnp.full((rows, 1), -jnp.inf, F32)
    l = jnp.zeros((rows, 1), F32)
    acc = jnp.zeros((rows, HEAD_DIM), F32)
    m, l, acc = chunk(kc_ref[...], vc_ref[...], m, l, acc)
    for c in range(n_lat_chunks):
        m, l, acc = chunk(kl_ref[c * tk:(c + 1) * tk, :], vl_ref[c * tk:(c + 1) * tk, :], m, l, acc)
    o = (acc / l).astype(BF16)
    for h in range(KV_GROUP):
        o_ref[:, h * HEAD_DIM:(h + 1) * HEAD_DIM] = o[h * tq:(h + 1) * tq]


def _attention(q, k, v, b, seq, cl, n_lat):
    t = q.shape[0]
    gw = KV_GROUP * HEAD_DIM
    tq = min(ATTN_TQ, seq)
    tk = min(ATTN_TK, seq)
    nq = seq // tq
    ctx_blk0 = n_lat // cl
    hq = N_HEADS * HEAD_DIM

    ctx_kv = pl.BlockSpec((cl, HEAD_DIM), lambda bi, g, qi: (ctx_blk0 + bi, g))
    lat_kv = pl.BlockSpec((seq, HEAD_DIM), lambda bi, g, qi: (bi, g))
    lat_q = pl.BlockSpec((tq, gw), lambda bi, g, qi: (bi * nq + qi, g))
    o_lat = pl.pallas_call(
        functools.partial(_attn_kernel, n_lat_chunks=seq // tk, tk=tk),
        grid=(b, N_KV_HEADS, nq),
        in_specs=[lat_q, ctx_kv, ctx_kv, lat_kv, lat_kv],
        out_specs=lat_q,
        out_shape=jax.ShapeDtypeStruct((n_lat, hq), BF16),
        compiler_params=_cparams(("parallel", "parallel", "arbitrary")),
        name="attention_latent",
    )(q, k, v, k, v)

    ctx_q = pl.BlockSpec((cl, gw), lambda bi, g: (ctx_blk0 + bi, g))
    ctx_kv2 = pl.BlockSpec((cl, HEAD_DIM), lambda bi, g: (ctx_blk0 + bi, g))
    o_ctx = pl.pallas_call(
        functools.partial(_attn_kernel, n_lat_chunks=0, tk=tk),
        grid=(b, N_KV_HEADS),
        in_specs=[ctx_q, ctx_kv2, ctx_kv2],
        out_specs=pl.BlockSpec((cl, gw), lambda bi, g: (bi, g)),
        out_shape=jax.ShapeDtypeStruct((t - n_lat, hq), BF16),
        compiler_params=_cparams(("parallel", "parallel")),
        name="attention_context",
    )(q, k, v)
    return jnp.concatenate([o_lat, o_ctx], axis=0)


def _in_proj_kernel(x_ref, shift_ref, scale_ref, w_ref, o_ref):
    h = (x_ref[...] * (1.0 + scale_ref[...]) + shift_ref[...]).astype(BF16)
    o_ref[...] = jnp.dot(h, w_ref[...], preferred_element_type=F32)


def _in_project(xall, mod, w_in, n_rows, n_lat, seq, ctx_row):
    d = xall.shape[1]
    n_out = w_in.shape[1]
    tm = ROW_TILE
    return pl.pallas_call(
        _in_proj_kernel,
        grid=(n_rows // tm,),
        in_specs=[
            pl.BlockSpec((tm, d), lambda i: (i, 0)),
            _mod_spec(0, tm, n_lat, seq, ctx_row, d),
            _mod_spec(1, tm, n_lat, seq, ctx_row, d),
            pl.BlockSpec(w_in.shape, lambda i: (0, 0)),
        ],
        out_specs=pl.BlockSpec((tm, n_out), lambda i: (i, 0)),
        out_shape=jax.ShapeDtypeStruct((n_rows, n_out), F32),
        compiler_params=_cparams(("parallel",)),
        name="conv_in_proj",
    )(xall, mod, mod, w_in.astype(BF16))


def _conv_kernel(bg_ref, cg_ref, v_ref, taps_ref, o_ref):
    u = cg_ref[...] * v_ref[...]
    n = u.shape[0]
    pos = lax.broadcasted_iota(I32, u.shape, 0)
    prev = jnp.where(pos == 0, 0.0, pltpu.roll(u, 1, axis=0))
    nxt = jnp.where(pos == n - 1, 0.0, pltpu.roll(u, n - 1, axis=0))
    conv = prev * taps_ref[0:1, :] + u * taps_ref[1:2, :] + nxt * taps_ref[2:3, :]
    o_ref[...] = (bg_ref[...] * conv).astype(BF16)


def _short_conv(zin, taps, n_seqs, seq, d):
    tc = CONV_LANES
    nj = d // tc
    return pl.pallas_call(
        _conv_kernel,
        grid=(n_seqs, nj),
        in_specs=[
            pl.BlockSpec((seq, tc), lambda s, j: (s, j)),
            pl.BlockSpec((seq, tc), lambda s, j: (s, nj + j)),
            pl.BlockSpec((seq, tc), lambda s, j: (s, 2 * nj + j)),
            pl.BlockSpec((taps.shape[0], tc), lambda s, j: (0, j)),
        ],
        out_specs=pl.BlockSpec((seq, tc), lambda s, j: (s, j)),
        out_shape=jax.ShapeDtypeStruct((n_seqs * seq, d), BF16),
        compiler_params=_cparams(("parallel", "parallel")),
        name="short_conv",
    )(zin, zin, zin, taps)


def _layer_norm(z, g, b):
    mu = jnp.mean(z, axis=-1, keepdims=True)
    zc = z - mu
    var = jnp.mean(zc * zc, axis=-1, keepdims=True)
    return zc * lax.rsqrt(var + LN_EPS) * g + b


def _post_kernel(a_ref, x_ref, gate_ref, lng_ref, lnb_ref, shift_ref, scale_ref, w_ref,
                 rwt_ref, rb_ref, tri_ref,
                 x1_ref, h2_ref, eidx_ref, gates_ref, rank_ref, cnt_ref, rwh_ref, rwl_ref, *, dn_alpha):
    tm = a_ref.shape[0]

    @pl.when(pl.program_id(0) == 0)
    def _():
        hi = rwt_ref[...].astype(BF16)
        rwh_ref[...] = hi
        rwl_ref[...] = (rwt_ref[...] - hi.astype(F32)).astype(BF16)

    d = x_ref.shape[1]
    y = jnp.dot(a_ref[...], w_ref[...], preferred_element_type=F32)
    x1 = _layer_norm(dn_alpha * x_ref[...] + gate_ref[...] * y, lng_ref[...], lnb_ref[...])
    x1_ref[...] = x1
    h2 = x1 * (1.0 + scale_ref[...]) + shift_ref[...]
    h2_ref[...] = h2

    h_hi = h2.astype(BF16)
    h_lo = (h2 - h_hi.astype(F32)).astype(BF16)

    def nt_dot(w, h):
        return lax.dot_general(w, h, (((1,), (1,)), ((), ())), preferred_element_type=F32)

    logits = nt_dot(rwh_ref[...], h_hi) + (nt_dot(rwl_ref[...], h_hi) + nt_dot(rwh_ref[...], h_lo))
    scores = jax.nn.sigmoid(logits)
    biased = scores + rb_ref[...]
    neg = -jnp.inf
    big = jnp.int32(1 << 30)
    row = lax.broadcasted_iota(I32, (N_EXPERTS, tm), 0)

    def argmax_rows(vals, idx):
        mx = jnp.max(vals, axis=0, keepdims=True)
        return mx, jnp.min(jnp.where(vals == mx, idx, big), axis=0, keepdims=True)

    gs = []
    grp_row = lax.broadcasted_iota(I32, (PER_GROUP, tm), 0)
    for g in range(N_GROUPS):
        bg = biased[g * PER_GROUP:(g + 1) * PER_GROUP]
        ig = grp_row + g * PER_GROUP
        m1, i1 = argmax_rows(bg, ig)
        m2 = jnp.max(jnp.where(ig == i1, neg, bg), axis=0, keepdims=True)
        gs.append(m1 + m2)
    gsc = jnp.concatenate(gs, axis=0)
    grow = lax.broadcasted_iota(I32, (N_GROUPS, tm), 0)
    gsel = jnp.zeros((N_GROUPS, tm), F32)
    for _ in range(TOPK_GROUPS):
        _, gi = argmax_rows(gsc, grow)
        hit = grow == gi
        gsel = jnp.where(hit, 1.0, gsel)
        gsc = jnp.where(hit, neg, gsc)
    cur = jnp.concatenate(
        [jnp.where(gsel[g:g + 1] > 0.0, biased[g * PER_GROUP:(g + 1) * PER_GROUP], neg)
         for g in range(N_GROUPS)], axis=0)

    onehot = jnp.zeros((N_EXPERTS, tm), F32)
    idxs, gvals = [], []
    for _ in range(TOP_K):
        _, ei = argmax_rows(cur, row)
        hit = row == ei
        gvals.append(jnp.sum(jnp.where(hit, scores, 0.0), axis=0, keepdims=True))
        idxs.append(ei)
        onehot = jnp.where(hit, 1.0, onehot)
        cur = jnp.where(hit, neg, cur)
    gv = jnp.concatenate(gvals, axis=0)
    gates_ref[...] = gv / jnp.sum(gv, axis=0, keepdims=True) * ROUTED_SCALE
    eidx_ref[...] = jnp.concatenate(idxs, axis=0)

    @pl.when(pl.program_id(0) == 0)
    def _():
        cnt_ref[...] = jnp.zeros_like(cnt_ref)

    prefix = jnp.dot(onehot.astype(BF16), tri_ref[...], preferred_element_type=F32)
    pos = prefix + cnt_ref[...]
    rank_ref[...] = jnp.concatenate(
        [jnp.sum(jnp.where(row == ei, pos, 0.0), axis=0, keepdims=True) for ei in idxs],
        axis=0).astype(I32)
    cnt_ref[...] = cnt_ref[...] + jnp.sum(onehot, axis=1, keepdims=True)


def _post_mixer(a, xall, mod, ln_g, ln_b, w, router_w, router_bias, n_rows, n_lat, seq, ctx_row,
                dn_alpha):
    d = xall.shape[1]
    tm = ROW_TILE
    tri = (lax.broadcasted_iota(I32, (tm, tm), 0) < lax.broadcasted_iota(I32, (tm, tm), 1)).astype(BF16)
    const = lambda i: (0, 0)
    row_blk = pl.BlockSpec((tm, d), lambda i: (i, 0))
    k_blk = pl.BlockSpec((TOP_K, tm), lambda i: (0, i))
    return pl.pallas_call(
        functools.partial(_post_kernel, dn_alpha=dn_alpha),
        grid=(n_rows // tm,),
        in_specs=[
            row_blk, row_blk,
            _mod_spec(2, tm, n_lat, seq, ctx_row, d),
            pl.BlockSpec((1, d), const), pl.BlockSpec((1, d), const),
            _mod_spec(3, tm, n_lat, seq, ctx_row, d),
            _mod_spec(4, tm, n_lat, seq, ctx_row, d),
            pl.BlockSpec(w.shape, const),
            pl.BlockSpec((N_EXPERTS, d), const),
            pl.BlockSpec((N_EXPERTS, 1), const),
            pl.BlockSpec((tm, tm), const),
        ],
        out_specs=[
            row_blk, row_blk,
            k_blk, k_blk, k_blk,
            pl.BlockSpec((N_EXPERTS, 1), const),
        ],
        out_shape=[
            jax.ShapeDtypeStruct((n_rows, d), F32),
            jax.ShapeDtypeStruct((n_rows, d), F32),
            jax.ShapeDtypeStruct((TOP_K, n_rows), I32),
            jax.ShapeDtypeStruct((TOP_K, n_rows), F32),
            jax.ShapeDtypeStruct((TOP_K, n_rows), I32),
            jax.ShapeDtypeStruct((N_EXPERTS, 1), F32),
        ],
        scratch_shapes=[pltpu.VMEM((N_EXPERTS, d), BF16), pltpu.VMEM((N_EXPERTS, d), BF16)],
        compiler_params=_cparams(("arbitrary",)),
        name="post_mixer_router",
    )(a, xall, mod, ln_g.reshape(1, d), ln_b.reshape(1, d), mod, mod, w.astype(BF16),
      router_w.T, router_bias.reshape(N_EXPERTS, 1), tri)


def _slots_kernel(pstart_ref, eidx_ref, rank_ref, dest_ref):
    e = eidx_ref[...]

    def pick(i, acc):
        return jnp.where(e == i, pstart_ref[i], acc)

    dest_ref[...] = lax.fori_loop(0, N_EXPERTS, pick, jnp.zeros_like(e)) + rank_ref[...]


def _slots(pstarts, eidx, rank):
    n_tok = eidx.shape[1]
    tn = 2048 if n_tok % 2048 == 0 else ROW_TILE
    blk = pl.BlockSpec((TOP_K, tn), lambda i, ps: (0, i))
    return pl.pallas_call(
        _slots_kernel,
        grid_spec=pltpu.PrefetchScalarGridSpec(
            num_scalar_prefetch=1, grid=(n_tok // tn,), in_specs=[blk, blk], out_specs=blk),
        out_shape=jax.ShapeDtypeStruct((TOP_K, n_tok), I32),
        compiler_params=_cparams(("arbitrary",)),
        name="moe_slots",
    )(pstarts, eidx, rank)


def _dispatch_kernel(pends_ref, pcounts_ref, dest_ref, h_ref, xs_hbm, zbuf, sem):
    td = dest_ref.shape[1]
    bm = zbuf.shape[0]

    @pl.when(pl.program_id(0) == 0)
    def _():
        zbuf[...] = jnp.zeros_like(zbuf)

        def tail(e):
            return xs_hbm.at[pl.ds(pl.multiple_of(pends_ref[e] - bm, bm), bm)]

        def zstart(e, carry):
            @pl.when(pcounts_ref[e] > 0)
            def _():
                pltpu.make_async_copy(zbuf, tail(e), sem).start()
            return carry

        def zwait(e, carry):
            @pl.when(pcounts_ref[e] > 0)
            def _():
                pltpu.make_async_copy(zbuf, tail(e), sem).wait()
            return carry

        lax.fori_loop(0, N_EXPERTS, zstart, 0)
        lax.fori_loop(0, N_EXPERTS, zwait, 0)

        def unused(j):
            return xs_hbm.at[pl.ds(pl.multiple_of(j * bm, bm), bm)]

        def ustart(j, carry):
            pltpu.make_async_copy(zbuf, unused(j), sem).start()
            return carry

        def uwait(j, carry):
            pltpu.make_async_copy(zbuf, unused(j), sem).wait()
            return carry

        first_unused = pends_ref[N_EXPERTS - 1] // bm
        lax.fori_loop(first_unused, xs_hbm.shape[0] // bm, ustart, 0)
        lax.fori_loop(first_unused, xs_hbm.shape[0] // bm, uwait, 0)

    def issue(t, carry):
        for k in range(TOP_K):
            pltpu.make_async_copy(h_ref.at[pl.ds(t, 1)], xs_hbm.at[pl.ds(dest_ref[k, t], 1)], sem).start()
        return carry

    lax.fori_loop(0, td, issue, 0)
    pltpu.make_async_copy(xs_hbm.at[pl.ds(0, TOP_K * td)], xs_hbm.at[pl.ds(0, TOP_K * td)], sem).wait()


def _dispatch(dest, h2, pends, pcounts, n_slots):
    n_tok, d = h2.shape
    td = DISPATCH_TOK
    grid_spec = pltpu.PrefetchScalarGridSpec(
        num_scalar_prefetch=2,
        grid=(n_tok // td,),
        in_specs=[
            pl.BlockSpec((TOP_K, td), lambda i, pe, pc: (0, i), memory_space=pltpu.SMEM),
            pl.BlockSpec((td, d), lambda i, pe, pc: (i, 0)),
        ],
        out_specs=pl.BlockSpec(memory_space=pl.ANY),
        scratch_shapes=[pltpu.VMEM((MOE_BM, d), h2.dtype), pltpu.SemaphoreType.DMA(())],
    )
    return pl.pallas_call(
        _dispatch_kernel,
        grid_spec=grid_spec,
        out_shape=jax.ShapeDtypeStruct((n_slots, d), h2.dtype),
        compiler_params=_cparams(("arbitrary",)),
        name="moe_dispatch",
    )(pends, pcounts, dest, h2)


def _gmm_kernel(blk0_ref, nblk_ref, n_used_ref, xs_hbm, wgu_ref, wd_ref, ys_hbm,
                xbuf, ybuf, wgu_bf, wd_bf, sem_in, sem_out):
    e = pl.program_id(0)
    n_in, bm = xbuf.shape[:2]
    n_out = ybuf.shape[0]
    ff = wd_ref.shape[0]
    n_used = n_used_ref[0]

    def rows(b):
        return pl.ds(pl.multiple_of(b * bm, bm), bm)

    def x_copy(b):
        return pltpu.make_async_copy(xs_hbm.at[rows(b)], xbuf.at[b % n_in], sem_in.at[b % n_in])

    def y_copy(b, slot):
        return pltpu.make_async_copy(ybuf.at[slot], ys_hbm.at[rows(b)], sem_out.at[slot])

    @pl.when(e == 0)
    def _():
        for b in range(n_in - 1):
            @pl.when(b < n_used)
            def _():
                x_copy(b).start()

    nb = nblk_ref[e]

    @pl.when(nb > 0)
    def _():
        wgu_bf[...] = wgu_ref[...].astype(BF16)
        wd_bf[...] = wd_ref[...].astype(BF16)

    def block(j, carry):
        b = blk0_ref[e] + j
        slot = b % n_out
        x_copy(b).wait()

        @pl.when(b + n_in - 1 < n_used)
        def _():
            x_copy(b + n_in - 1).start()

        @pl.when(b >= n_out)
        def _():
            y_copy(b - n_out, slot).wait()

        gu = jnp.dot(xbuf[b % n_in].astype(BF16), wgu_bf[...], preferred_element_type=F32)
        act = (_silu(gu[:, :ff]) * gu[:, ff:]).astype(BF16)
        ybuf[slot] = jnp.dot(act, wd_bf[...], preferred_element_type=F32)
        y_copy(b, slot).start()
        return carry

    lax.fori_loop(0, nb, block, 0)

    @pl.when(e == pl.num_programs(0) - 1)
    def _():
        for back in range(n_out, 0, -1):
            @pl.when(n_used >= back)
            def _():
                y_copy(n_used - back, (n_used - back) % n_out).wait()

        ybuf[0] = jnp.zeros(ybuf.shape[1:], ybuf.dtype)
        n_total = ys_hbm.shape[0] // bm

        def zstart(b, carry):
            y_copy(b, 0).start()
            return carry

        def zwait(b, carry):
            y_copy(b, 0).wait()
            return carry

        lax.fori_loop(n_used, n_total, zstart, 0)
        lax.fori_loop(n_used, n_total, zwait, 0)


def _grouped_ffn(xs, blk0, nblk, n_used, wgu, wd, layer):
    n_slots = xs.shape[0]
    bm = MOE_BM
    n_exp, d, ff2 = wgu.shape[1:]
    ff = wd.shape[2]
    grid_spec = pltpu.PrefetchScalarGridSpec(
        num_scalar_prefetch=3,
        grid=(n_exp,),
        in_specs=[
            pl.BlockSpec(memory_space=pl.ANY),
            pl.BlockSpec((None, None, d, ff2), lambda e, b0, nb, nu: (layer, e, 0, 0)),
            pl.BlockSpec((None, None, ff, d), lambda e, b0, nb, nu: (layer, e, 0, 0)),
        ],
        out_specs=pl.BlockSpec(memory_space=pl.ANY),
        scratch_shapes=[
            pltpu.VMEM((GMM_IN_SLOTS, bm, d), F32), pltpu.VMEM((GMM_OUT_SLOTS, bm, d), F32),
            pltpu.VMEM((d, ff2), BF16), pltpu.VMEM((ff, d), BF16),
            pltpu.SemaphoreType.DMA((GMM_IN_SLOTS,)), pltpu.SemaphoreType.DMA((GMM_OUT_SLOTS,)),
        ],
    )
    return pl.pallas_call(
        _gmm_kernel,
        grid_spec=grid_spec,
        out_shape=jax.ShapeDtypeStruct((n_slots, d), F32),
        compiler_params=_cparams(("arbitrary",)),
        name="moe_grouped_ffn",
    )(blk0, nblk, n_used, xs, wgu, wd)


def _shared_expert(h2_ref, swgu_ref, swd_ref):
    ff = swd_ref.shape[0]
    gu = jnp.dot(h2_ref[...].astype(BF16), swgu_ref[...], preferred_element_type=F32)
    act = (_silu(gu[:, :ff]) * gu[:, ff:]).astype(BF16)
    return jnp.dot(act, swd_ref[...], preferred_element_type=F32)


def _combine_kernel(dest_ref, gates_ref, ys_hbm, x1_ref, h2_ref, gate_ref, lng_ref, lnb_ref,
                    swgu_ref, swd_ref, o_ref, buf, sem, *, dn_alpha):
    tc = x1_ref.shape[0]

    def issue(t, carry):
        for k in range(TOP_K):
            pltpu.make_async_copy(ys_hbm.at[pl.ds(dest_ref[k, t], 1)], buf.at[k, pl.ds(t, 1)], sem).start()
        return carry

    lax.fori_loop(0, tc, issue, 0)

    y = _shared_expert(h2_ref, swgu_ref, swd_ref)

    for k in range(TOP_K):
        pltpu.make_async_copy(ys_hbm.at[pl.ds(0, tc)], buf.at[k], sem).wait()
    for k in range(TOP_K):
        y = y + gates_ref[:, k:k + 1] * buf[k]
    o_ref[...] = _layer_norm(dn_alpha * x1_ref[...] + gate_ref[...] * y, lng_ref[...], lnb_ref[...])


def _combine_gathered_kernel(gates_ref, g_ref, x1_ref, h2_ref, gate_ref, lng_ref, lnb_ref,
                             swgu_ref, swd_ref, prev_ref, o_ref, *, dn_alpha):
    del prev_ref
    y = _shared_expert(h2_ref, swgu_ref, swd_ref)
    for k in range(TOP_K):
        y = y + gates_ref[:, k:k + 1] * g_ref[k]
    o_ref[...] = _layer_norm(dn_alpha * x1_ref[...] + gate_ref[...] * y, lng_ref[...], lnb_ref[...])


def _sc_gather_rows(table, idx):
    n = idx.shape[0]
    d = table.shape[1]
    mesh = plsc.VectorSubcoreMesh(core_axis_name="c", subcore_axis_name="s")
    n_workers = mesh.num_cores * mesh.num_subcores
    ch = SC_GATHER_ROWS
    per_w = n // n_workers
    n_ch = per_w // ch
    assert n % (n_workers * ch * 2) == 0

    @functools.partial(
        pl.kernel, mesh=mesh, out_type=jax.ShapeDtypeStruct((n, d), table.dtype),
        scratch_types=[pltpu.VMEM((ch,), I32), pltpu.VMEM((ch,), I32),
                       pltpu.VMEM((ch, d), table.dtype), pltpu.VMEM((ch, d), table.dtype),
                       pltpu.SemaphoreType.DMA, pltpu.SemaphoreType.DMA],
        name="moe_sc_gather")
    def gather(table_hbm, idx_hbm, out_hbm, idx_a, idx_b, rows_a, rows_b, sem_a, sem_b):
        wid = lax.axis_index("s") * mesh.num_cores + lax.axis_index("c")

        def chunk(i):
            return pl.ds(wid * per_w + i * ch, ch)

        def fetch(i, idx_v, rows_v, sem):
            pltpu.sync_copy(idx_hbm.at[chunk(i)], idx_v)
            pltpu.make_async_copy(table_hbm.at[idx_v], rows_v, sem).start()

        def drain(i, idx_v, rows_v, sem):
            pltpu.make_async_copy(table_hbm.at[idx_v], rows_v, sem).wait()
            pltpu.sync_copy(rows_v, out_hbm.at[chunk(i)])

        fetch(0, idx_a, rows_a, sem_a)

        @pl.loop(0, n_ch // 2)
        def _(j):
            i = 2 * j
            fetch(i + 1, idx_b, rows_b, sem_b)
            drain(i, idx_a, rows_a, sem_a)

            @pl.when(i + 2 < n_ch)
            def _():
                fetch(i + 2, idx_a, rows_a, sem_a)

            drain(i + 1, idx_b, rows_b, sem_b)

    return gather(table, idx)


def _combine(dest, gates_t, ys, x1, h2, mod, ln_g, ln_b, swgu, swd, n_rows, n_lat, seq, ctx_row,
             dn_alpha):
    d = x1.shape[1]
    tc = DISPATCH_TOK
    const = lambda i: (0, 0)
    n_blocks = n_rows // tc
    n_sc_blocks = (n_blocks * SC_SHARE_PERCENT) // 100
    n_tc = (n_blocks - n_sc_blocks) * tc
    n_sc = n_rows - n_tc
    gathered = _sc_gather_rows(ys, dest[:, n_tc:].reshape(TOP_K * n_sc)).reshape(TOP_K, n_sc, d)
    swgu_bf, swd_bf = swgu.astype(BF16), swd.astype(BF16)
    out_tc = pl.pallas_call(
        functools.partial(_combine_kernel, dn_alpha=dn_alpha),
        grid=(n_tc // tc,),
        in_specs=[
            pl.BlockSpec((TOP_K, tc), lambda i: (0, i), memory_space=pltpu.SMEM),
            pl.BlockSpec((tc, TOP_K), lambda i: (i, 0)),
            pl.BlockSpec(memory_space=pl.ANY),
            pl.BlockSpec((tc, d), lambda i: (i, 0)),
            pl.BlockSpec((tc, d), lambda i: (i, 0)),
            _mod_spec(5, tc, n_lat, seq, ctx_row, d),
            pl.BlockSpec((1, d), const), pl.BlockSpec((1, d), const),
            pl.BlockSpec(swgu.shape, const),
            pl.BlockSpec(swd.shape, const),
        ],
        out_specs=pl.BlockSpec((tc, d), lambda i: (i, 0)),
        out_shape=jax.ShapeDtypeStruct((n_rows, d), F32),
        scratch_shapes=[pltpu.VMEM((TOP_K, tc, d), F32), pltpu.SemaphoreType.DMA(())],
        compiler_params=_cparams(("arbitrary",)),
        name="moe_combine",
    )(dest, gates_t, ys, x1, h2, mod, ln_g.reshape(1, d), ln_b.reshape(1, d), swgu_bf, swd_bf)

    tl = COMBINE_GATHERED_TOK
    off = n_tc // tl
    row_blk = pl.BlockSpec((tl, d), lambda i: (i + off, 0))
    return pl.pallas_call(
        functools.partial(_combine_gathered_kernel, dn_alpha=dn_alpha),
        grid=(n_sc // tl,),
        in_specs=[
            pl.BlockSpec((tl, TOP_K), lambda i: (i + off, 0)),
            pl.BlockSpec((TOP_K, tl, d), lambda i: (0, i, 0)),
            row_blk, row_blk,
            _mod_spec(5, tl, n_lat, seq, ctx_row, d, blk_off=off),
            pl.BlockSpec((1, d), const), pl.BlockSpec((1, d), const),
            pl.BlockSpec(swgu.shape, const),
            pl.BlockSpec(swd.shape, const),
            pl.BlockSpec(memory_space=pl.ANY),
        ],
        out_specs=row_blk,
        out_shape=jax.ShapeDtypeStruct((n_rows, d), F32),
        input_output_aliases={9: 0},
        compiler_params=_cparams(("parallel",)),
        name="moe_combine_gathered",
    )(gates_t, gathered, x1, h2, mod, ln_g.reshape(1, d), ln_b.reshape(1, d), swgu_bf, swd_bf, out_tc)


def _moe(x1, h2, eidx, gates, rank, counts, mod, ln_g, ln_b, wgu, wd, layer, swgu, swd,
         n_rows, n_lat, seq, ctx_row, dn_alpha):
    bm = MOE_BM
    counts = counts.reshape(N_EXPERTS).astype(I32)
    pcounts = (counts + bm - 1) // bm * bm
    pends = jnp.cumsum(pcounts)
    pstarts = pends - pcounts
    dest = _slots(pstarts, eidx, rank)
    n_blk = (n_rows * TOP_K + N_EXPERTS * (bm - 1) + bm - 1) // bm
    n_used = (pends[-1] // bm).astype(I32).reshape(1)
    xs = _dispatch(dest, h2, pends, pcounts, n_blk * bm)
    ys = _grouped_ffn(xs, pstarts // bm, pcounts // bm, n_used, wgu, wd, layer)
    return _combine(dest, gates.T, ys, x1, h2, mod, ln_g, ln_b, swgu, swd,
                    n_rows, n_lat, seq, ctx_row, dn_alpha)


def kernel(x, c, ctx, c_ctx, ada_w, ada_b, ln_g, ln_b, attn_w_qkv, attn_q_norm, attn_k_norm, attn_w_o, conv_w_in, conv_taps, conv_w_out, router_w, router_bias, exp_w_gate_up, exp_w_down, shared_w_gate_up, shared_w_down):
    b, seq, d = x.shape
    cl = ctx.shape[1]
    depth = ada_w.shape[0]
    n_lat = b * seq
    n_ctx = b * cl
    dn_alpha = (2 * depth) ** 0.25
    assert depth == 2 and b < MOD_ROWS
    assert seq % ROW_TILE == 0 and n_ctx % ROW_TILE == 0 and seq % GRID_W == 0

    cond = jnp.zeros((MOD_ROWS, d), F32).at[:b].set(c).at[b].set(c_ctx)
    mod = _modulation(cond, ada_w, ada_b)
    xall = jnp.concatenate([x.reshape(n_lat, d), ctx.reshape(n_ctx, d)], axis=0)
    n_all = n_lat + n_ctx

    q, k, v = _qkv_project(xall, mod[0], attn_w_qkv[0], attn_q_norm[0], attn_k_norm[0], n_lat, seq, b)
    o = _attention(q, k, v, b, seq, cl, n_lat)
    x1, h2, eidx, gates, rank, counts = _post_mixer(
        o, xall, mod[0], ln_g[0, 0], ln_b[0, 0], attn_w_o[0], router_w[0], router_bias[0],
        n_all, n_lat, seq, b, dn_alpha)
    xall = _moe(x1, h2, eidx, gates, rank, counts, mod[0], ln_g[0, 1], ln_b[0, 1],
                exp_w_gate_up, exp_w_down, 0, shared_w_gate_up[0], shared_w_down[0],
                n_all, n_lat, seq, b, dn_alpha)

    zin = _in_project(xall, mod[1], conv_w_in[0], n_lat, n_lat, seq, b)
    a = _short_conv(zin, conv_taps[0], b, seq, d)
    x1, h2, eidx, gates, rank, counts = _post_mixer(
        a, xall, mod[1], ln_g[1, 0], ln_b[1, 0], conv_w_out[0], router_w[1], router_bias[1],
        n_lat, n_lat, seq, b, dn_alpha)
    out = _moe(x1, h2, eidx, gates, rank, counts, mod[1], ln_g[1, 1], ln_b[1, 1],
               exp_w_gate_up, exp_w_down, 1, shared_w_gate_up[1], shared_w_down[1],
               n_lat, n_lat, seq, b, dn_alpha)
    return out.reshape(b, seq, d)
```

```python
import functools

import jax
import jax.numpy as jnp
from jax import lax
from jax.experimental import pallas as pl
from jax.experimental.pallas import tpu as pltpu
from jax.experimental.pallas import tpu_sc as plsc

F32 = jnp.float32
BF16 = jnp.bfloat16
I32 = jnp.int32

N_HEADS = 8
N_KV_HEADS = 2
HEAD_DIM = 128
KV_GROUP = N_HEADS // N_KV_HEADS
GRID_W = 64
ROPE_THETA = 10000.0
N_EXPERTS = 256
TOP_K = 8
N_GROUPS = 8
TOPK_GROUPS = 4
PER_GROUP = N_EXPERTS // N_GROUPS
ROUTED_SCALE = 2.5
LN_EPS = 1e-5
QK_EPS = 1e-6
N_MOD = 6
MOD_ROWS = 16

LANES = 128
SUBLANES = 8
VMEM_LIMIT = 56 * 1024 * 1024

ROW_TILE = 512
ATTN_TQ = 256
ATTN_TK = 2048
MOE_BM = 256
GMM_IN_SLOTS = 4
GMM_OUT_SLOTS = 3
DISPATCH_TOK = 512
COMBINE_GATHERED_TOK = 256
SC_GATHER_ROWS = 32
SC_SHARE_PERCENT = 65
SC_DISPATCH_PERCENT = 50
CONV_LANES = 128

HIGHEST = lax.Precision.HIGHEST
LOG2_E = 1.4426950408889634


def _cparams(sem):
    return pltpu.CompilerParams(dimension_semantics=sem, vmem_limit_bytes=VMEM_LIMIT)


def _silu(v):
    return v * jax.nn.sigmoid(v)


def _mod_kernel(c_ref, w_ref, b_ref, o_ref):
    s = _silu(c_ref[...])
    o_ref[...] = jnp.dot(s, w_ref[...], precision=HIGHEST, preferred_element_type=F32) + b_ref[...]


def _modulation(cond, ada_w, ada_b):
    depth, d, nd = ada_w.shape
    tn = 1536
    out = pl.pallas_call(
        _mod_kernel,
        grid=(depth, nd // tn),
        in_specs=[
            pl.BlockSpec((MOD_ROWS, d), lambda l, j: (0, 0)),
            pl.BlockSpec((None, d, tn), lambda l, j: (l, 0, j)),
            pl.BlockSpec((None, 1, tn), lambda l, j: (l, 0, j)),
        ],
        out_specs=pl.BlockSpec((None, MOD_ROWS, tn), lambda l, j: (l, 0, j)),
        out_shape=jax.ShapeDtypeStruct((depth, MOD_ROWS, nd), F32),
        compiler_params=_cparams(("arbitrary", "arbitrary")),
        name="adaln_modulation",
    )(cond, ada_w, ada_b.reshape(depth, 1, nd))
    return out.reshape(depth, MOD_ROWS * N_MOD, 1, d)


def _mod_spec(comp, tm, n_lat, seq, ctx_row, d, blk_off=0):
    def index(i, *_):
        row0 = (i + blk_off) * tm
        r = jnp.where(row0 < n_lat, row0 // seq, ctx_row)
        return (r * N_MOD + comp, 0, 0)

    return pl.BlockSpec((None, 1, d), index)


def _qkv_kernel(x_ref, shift_ref, scale_ref, w_ref, qg_ref, kg_ref, cos_ref, sin_ref,
                q_ref, k_ref, v_ref):
    h = (x_ref[...] * (1.0 + scale_ref[...]) + shift_ref[...]).astype(BF16)
    qkv = jnp.dot(h, w_ref[...], preferred_element_type=F32)
    cos = cos_ref[...]
    sin = sin_ref[...]
    hq = N_HEADS * HEAD_DIM
    kd = N_KV_HEADS * HEAD_DIM

    def norm_rope(t, g, post):
        t = t * lax.rsqrt(jnp.mean(t * t, axis=-1, keepdims=True) + QK_EPS) * g
        t = t * cos + pltpu.roll(t, HEAD_DIM // 2, axis=1) * sin
        return (t * post).astype(BF16)

    for hd in range(N_HEADS):
        sl = slice(hd * HEAD_DIM, (hd + 1) * HEAD_DIM)
        q_ref[:, sl] = norm_rope(qkv[:, sl], qg_ref[...], HEAD_DIM ** -0.5 * LOG2_E)
    for hd in range(N_KV_HEADS):
        sl = slice(hd * HEAD_DIM, (hd + 1) * HEAD_DIM)
        k_ref[:, sl] = norm_rope(qkv[:, hq + hd * HEAD_DIM: hq + (hd + 1) * HEAD_DIM], kg_ref[...], 1.0)
    v_ref[...] = qkv[:, hq + kd:].astype(BF16)


def _rope_tables(seq, tm):
    rows = seq // GRID_W
    row = jnp.repeat(jnp.arange(rows, dtype=F32), GRID_W)
    col = jnp.tile(jnp.arange(GRID_W, dtype=F32), rows)
    axis_dim = HEAD_DIM // 2
    freqs = ROPE_THETA ** (-jnp.arange(0, axis_dim, 2, dtype=F32) / axis_dim)
    ang = jnp.concatenate([row[:, None] * freqs, col[:, None] * freqs], axis=-1)
    cos = jnp.concatenate([jnp.cos(ang), jnp.cos(ang)], axis=-1)
    sin = jnp.concatenate([-jnp.sin(ang), jnp.sin(ang)], axis=-1)
    cos = jnp.concatenate([cos, jnp.ones((tm, HEAD_DIM), F32)], axis=0)
    sin = jnp.concatenate([sin, jnp.zeros((tm, HEAD_DIM), F32)], axis=0)
    return cos.reshape(seq // tm + 1, tm, HEAD_DIM), sin.reshape(seq // tm + 1, tm, HEAD_DIM)


def _qkv_project(xall, mod, w_qkv, q_g, k_g, n_lat, seq, ctx_row):
    t, d = xall.shape
    tm = ROW_TILE
    hq = N_HEADS * HEAD_DIM
    kd = N_KV_HEADS * HEAD_DIM
    perm = jnp.concatenate([jnp.arange(0, HEAD_DIM, 2), jnp.arange(1, HEAD_DIM, 2)])
    cols = jnp.concatenate([hd * HEAD_DIM + perm for hd in range(N_HEADS + N_KV_HEADS)]
                           + [jnp.arange(hq + kd, hq + 2 * kd)])
    w = w_qkv[:, cols].astype(BF16)
    cos, sin = _rope_tables(seq, tm)
    n_pos = seq // tm

    def pos_index(i):
        row0 = i * tm
        return (jnp.where(row0 < n_lat, (row0 % seq) // tm, n_pos), 0, 0)

    const = lambda i: (0, 0)
    return pl.pallas_call(
        _qkv_kernel,
        grid=(t // tm,),
        in_specs=[
            pl.BlockSpec((tm, d), lambda i: (i, 0)),
            _mod_spec(0, tm, n_lat, seq, ctx_row, d),
            _mod_spec(1, tm, n_lat, seq, ctx_row, d),
            pl.BlockSpec(w.shape, const),
            pl.BlockSpec((1, HEAD_DIM), const),
            pl.BlockSpec((1, HEAD_DIM), const),
            pl.BlockSpec((None, tm, HEAD_DIM), pos_index),
            pl.BlockSpec((None, tm, HEAD_DIM), pos_index),
        ],
        out_specs=[
            pl.BlockSpec((tm, hq), lambda i: (i, 0)),
            pl.BlockSpec((tm, kd), lambda i: (i, 0)),
            pl.BlockSpec((tm, kd), lambda i: (i, 0)),
        ],
        out_shape=[
            jax.ShapeDtypeStruct((t, hq), BF16),
            jax.ShapeDtypeStruct((t, kd), BF16),
            jax.ShapeDtypeStruct((t, kd), BF16),
        ],
        compiler_params=_cparams(("parallel",)),
        name="qkv_norm_rope",
    )(xall, mod, mod, w, q_g[perm].reshape(1, HEAD_DIM), k_g[perm].reshape(1, HEAD_DIM), cos, sin)


def _attn_kernel(*refs, n_lat_chunks, tk):
    if n_lat_chunks:
        q_ref, kc_ref, vc_ref, kl_ref, vl_ref, o_ref = refs
    else:
        q_ref, kc_ref, vc_ref, o_ref = refs
    tq = q_ref.shape[0]
    q = jnp.concatenate([q_ref[:, h * HEAD_DIM:(h + 1) * HEAD_DIM] for h in range(KV_GROUP)], axis=0)
    rows = KV_GROUP * tq

    def chunk(k, v, m, l, acc):
        s = lax.dot_general(q, k, (((1,), (1,)), ((), ())), preferred_element_type=F32)
        m_new = jnp.maximum(m, jnp.max(s, axis=-1, keepdims=True))
        p = jnp.exp2(s - m_new)
        a = jnp.exp2(m - m_new)
        l = a * l + jnp.sum(p, axis=-1, keepdims=True)
        acc = a * acc + jnp.dot(p.astype(BF16), v, preferred_element_type=F32)
        return m_new, l, acc

    m = jnp.full((rows, 1), -jnp.inf, F32)
    l = jnp.zeros((rows, 1), F32)
    acc = jnp.zeros((rows, HEAD_DIM), F32)
    m, l, acc = chunk(kc_ref[...], vc_ref[...], m, l, acc)
    for c in range(n_lat_chunks):
        m, l, acc = chunk(kl_ref[c * tk:(c + 1) * tk, :], vl_ref[c * tk:(c + 1) * tk, :], m, l, acc)
    o = (acc / l).astype(BF16)
    for h in range(KV_GROUP):
        o_ref[:, h * HEAD_DIM:(h + 1) * HEAD_DIM] = o[h * tq:(h + 1) * tq]


def _attention(q, k, v, b, seq, cl, n_lat):
    t = q.shape[0]
    gw = KV_GROUP * HEAD_DIM
    tq = min(ATTN_TQ, seq)
    tk = min(ATTN_TK, seq)
    nq = seq // tq
    ctx_blk0 = n_lat // cl
    hq = N_HEADS * HEAD_DIM

    ctx_kv = pl.BlockSpec((cl, HEAD_DIM), lambda bi, g, qi: (ctx_blk0 + bi, g))
    lat_kv = pl.BlockSpec((seq, HEAD_DIM), lambda bi, g, qi: (bi, g))
    lat_q = pl.BlockSpec((tq, gw), lambda bi, g, qi: (bi * nq + qi, g))
    o_lat = pl.pallas_call(
        functools.partial(_attn_kernel, n_lat_chunks=seq // tk, tk=tk),
        grid=(b, N_KV_HEADS, nq),
        in_specs=[lat_q, ctx_kv, ctx_kv, lat_kv, lat_kv],
        out_specs=lat_q,
        out_shape=jax.ShapeDtypeStruct((n_lat, hq), BF16),
        compiler_params=_cparams(("parallel", "parallel", "arbitrary")),
        name="attention_latent",
    )(q, k, v, k, v)

    ctx_q = pl.BlockSpec((cl, gw), lambda bi, g: (ctx_blk0 + bi, g))
    ctx_kv2 = pl.BlockSpec((cl, HEAD_DIM), lambda bi, g: (ctx_blk0 + bi, g))
    o_ctx = pl.pallas_call(
        functools.partial(_attn_kernel, n_lat_chunks=0, tk=tk),
        grid=(b, N_KV_HEADS),
        in_specs=[ctx_q, ctx_kv2, ctx_kv2],
        out_specs=pl.BlockSpec((cl, gw), lambda bi, g: (bi, g)),
        out_shape=jax.ShapeDtypeStruct((t - n_lat, hq), BF16),
        compiler_params=_cparams(("parallel", "parallel")),
        name="attention_context",
    )(q, k, v)
    return jnp.concatenate([o_lat, o_ctx], axis=0)


def _in_proj_kernel(x_ref, shift_ref, scale_ref, w_ref, o_ref):
    h = (x_ref[...] * (1.0 + scale_ref[...]) + shift_ref[...]).astype(BF16)
    o_ref[...] = jnp.dot(h, w_ref[...], preferred_element_type=F32)


def _in_project(xall, mod, w_in, n_rows, n_lat, seq, ctx_row):
    d = xall.shape[1]
    n_out = w_in.shape[1]
    tm = ROW_TILE
    return pl.pallas_call(
        _in_proj_kernel,
        grid=(n_rows // tm,),
        in_specs=[
            pl.BlockSpec((tm, d), lambda i: (i, 0)),
            _mod_spec(0, tm, n_lat, seq, ctx_row, d),
            _mod_spec(1, tm, n_lat, seq, ctx_row, d),
            pl.BlockSpec(w_in.shape, lambda i: (0, 0)),
        ],
        out_specs=pl.BlockSpec((tm, n_out), lambda i: (i, 0)),
        out_shape=jax.ShapeDtypeStruct((n_rows, n_out), F32),
        compiler_params=_cparams(("parallel",)),
        name="conv_in_proj",
    )(xall, mod, mod, w_in.astype(BF16))


def _conv_kernel(bg_ref, cg_ref, v_ref, taps_ref, o_ref):
    u = cg_ref[...] * v_ref[...]
    n = u.shape[0]
    pos = lax.broadcasted_iota(I32, u.shape, 0)
    prev = jnp.where(pos == 0, 0.0, pltpu.roll(u, 1, axis=0))
    nxt = jnp.where(pos == n - 1, 0.0, pltpu.roll(u, n - 1, axis=0))
    conv = prev * taps_ref[0:1, :] + u * taps_ref[1:2, :] + nxt * taps_ref[2:3, :]
    o_ref[...] = (bg_ref[...] * conv).astype(BF16)


def _short_conv(zin, taps, n_seqs, seq, d):
    tc = CONV_LANES
    nj = d // tc
    return pl.pallas_call(
        _conv_kernel,
        grid=(n_seqs, nj),
        in_specs=[
            pl.BlockSpec((seq, tc), lambda s, j: (s, j)),
            pl.BlockSpec((seq, tc), lambda s, j: (s, nj + j)),
            pl.BlockSpec((seq, tc), lambda s, j: (s, 2 * nj + j)),
            pl.BlockSpec((taps.shape[0], tc), lambda s, j: (0, j)),
        ],
        out_specs=pl.BlockSpec((seq, tc), lambda s, j: (s, j)),
        out_shape=jax.ShapeDtypeStruct((n_seqs * seq, d), BF16),
        compiler_params=_cparams(("parallel", "parallel")),
        name="short_conv",
    )(zin, zin, zin, taps)


def _layer_norm(z, g, b):
    mu = jnp.mean(z, axis=-1, keepdims=True)
    zc = z - mu
    var = jnp.mean(zc * zc, axis=-1, keepdims=True)
    return zc * lax.rsqrt(var + LN_EPS) * g + b


def _post_kernel(a_ref, x_ref, gate_ref, lng_ref, lnb_ref, shift_ref, scale_ref, w_ref,
                 rwt_ref, rb_ref, tri_ref,
                 x1_ref, h2_ref, eidx_ref, gates_ref, rank_ref, cnt_ref, rwh_ref, rwl_ref, *,
                 dn_alpha, split_blk):
    tm = a_ref.shape[0]

    @pl.when(pl.program_id(0) == 0)
    def _():
        hi = rwt_ref[...].astype(BF16)
        rwh_ref[...] = hi
        rwl_ref[...] = (rwt_ref[...] - hi.astype(F32)).astype(BF16)

    d = x_ref.shape[1]
    y = jnp.dot(a_ref[...], w_ref[...], preferred_element_type=F32)
    x1 = _layer_norm(dn_alpha * x_ref[...] + gate_ref[...] * y, lng_ref[...], lnb_ref[...])
    x1_ref[...] = x1
    h2 = x1 * (1.0 + scale_ref[...]) + shift_ref[...]
    h2_ref[...] = h2

    h_hi = h2.astype(BF16)
    h_lo = (h2 - h_hi.astype(F32)).astype(BF16)

    def nt_dot(w, h):
        return lax.dot_general(w, h, (((1,), (1,)), ((), ())), preferred_element_type=F32)

    logits = nt_dot(rwh_ref[...], h_hi) + (nt_dot(rwl_ref[...], h_hi) + nt_dot(rwh_ref[...], h_lo))
    scores = jax.nn.sigmoid(logits)
    biased = scores + rb_ref[...]
    neg = -jnp.inf
    big = jnp.int32(1 << 30)
    row = lax.broadcasted_iota(I32, (N_EXPERTS, tm), 0)

    def argmax_rows(vals, idx):
        mx = jnp.max(vals, axis=0, keepdims=True)
        return mx, jnp.min(jnp.where(vals == mx, idx, big), axis=0, keepdims=True)

    gs = []
    grp_row = lax.broadcasted_iota(I32, (PER_GROUP, tm), 0)
    for g in range(N_GROUPS):
        bg = biased[g * PER_GROUP:(g + 1) * PER_GROUP]
        ig = grp_row + g * PER_GROUP
        m1, i1 = argmax_rows(bg, ig)
        m2 = jnp.max(jnp.where(ig == i1, neg, bg), axis=0, keepdims=True)
        gs.append(m1 + m2)
    gsc = jnp.concatenate(gs, axis=0)
    grow = lax.broadcasted_iota(I32, (N_GROUPS, tm), 0)
    gsel = jnp.zeros((N_GROUPS, tm), F32)
    for _ in range(TOPK_GROUPS):
        _, gi = argmax_rows(gsc, grow)
        hit = grow == gi
        gsel = jnp.where(hit, 1.0, gsel)
        gsc = jnp.where(hit, neg, gsc)
    cur = jnp.concatenate(
        [jnp.where(gsel[g:g + 1] > 0.0, biased[g * PER_GROUP:(g + 1) * PER_GROUP], neg)
         for g in range(N_GROUPS)], axis=0)

    onehot = jnp.zeros((N_EXPERTS, tm), F32)
    idxs, gvals = [], []
    for _ in range(TOP_K):
        _, ei = argmax_rows(cur, row)
        hit = row == ei
        gvals.append(jnp.sum(jnp.where(hit, scores, 0.0), axis=0, keepdims=True))
        idxs.append(ei)
        onehot = jnp.where(hit, 1.0, onehot)
        cur = jnp.where(hit, neg, cur)
    gv = jnp.concatenate(gvals, axis=0)
    gates_ref[...] = gv / jnp.sum(gv, axis=0, keepdims=True) * ROUTED_SCALE
    eidx_ref[...] = jnp.concatenate(idxs, axis=0)

    @pl.when((pl.program_id(0) == 0) | (pl.program_id(0) == split_blk))
    def _():
        cnt_ref[...] = jnp.zeros_like(cnt_ref)

    prefix = jnp.dot(onehot.astype(BF16), tri_ref[...], preferred_element_type=F32)
    pos = prefix + cnt_ref[...]
    rank_ref[...] = jnp.concatenate(
        [jnp.sum(jnp.where(row == ei, pos, 0.0), axis=0, keepdims=True) for ei in idxs],
        axis=0).astype(I32)
    cnt_ref[...] = cnt_ref[...] + jnp.sum(onehot, axis=1, keepdims=True)


def _post_mixer(a, xall, mod, ln_g, ln_b, w, router_w, router_bias, n_rows, n_lat, seq, ctx_row,
                dn_alpha, split_blk):
    d = xall.shape[1]
    tm = ROW_TILE
    tri = (lax.broadcasted_iota(I32, (tm, tm), 0) < lax.broadcasted_iota(I32, (tm, tm), 1)).astype(BF16)
    const = lambda i: (0, 0)
    row_blk = pl.BlockSpec((tm, d), lambda i: (i, 0))
    k_blk = pl.BlockSpec((TOP_K, tm), lambda i: (0, i))
    return pl.pallas_call(
        functools.partial(_post_kernel, dn_alpha=dn_alpha, split_blk=split_blk),
        grid=(n_rows // tm,),
        in_specs=[
            row_blk, row_blk,
            _mod_spec(2, tm, n_lat, seq, ctx_row, d),
            pl.BlockSpec((1, d), const), pl.BlockSpec((1, d), const),
            _mod_spec(3, tm, n_lat, seq, ctx_row, d),
            _mod_spec(4, tm, n_lat, seq, ctx_row, d),
            pl.BlockSpec(w.shape, const),
            pl.BlockSpec((N_EXPERTS, d), const),
            pl.BlockSpec((N_EXPERTS, 1), const),
            pl.BlockSpec((tm, tm), const),
        ],
        out_specs=[
            row_blk, row_blk,
            k_blk, k_blk, k_blk,
            pl.BlockSpec((None, N_EXPERTS, 1), lambda i: (jnp.where(i >= split_blk, 1, 0), 0, 0)),
        ],
        out_shape=[
            jax.ShapeDtypeStruct((n_rows, d), F32),
            jax.ShapeDtypeStruct((n_rows, d), F32),
            jax.ShapeDtypeStruct((TOP_K, n_rows), I32),
            jax.ShapeDtypeStruct((TOP_K, n_rows), F32),
            jax.ShapeDtypeStruct((TOP_K, n_rows), I32),
            jax.ShapeDtypeStruct((2, N_EXPERTS, 1), F32),
        ],
        scratch_shapes=[pltpu.VMEM((N_EXPERTS, d), BF16), pltpu.VMEM((N_EXPERTS, d), BF16)],
        compiler_params=_cparams(("arbitrary",)),
        name="post_mixer_router",
    )(a, xall, mod, ln_g.reshape(1, d), ln_b.reshape(1, d), mod, mod, w.astype(BF16),
      router_w.T, router_bias.reshape(N_EXPERTS, 1), tri)


def _slots_kernel(pstart_ref, eidx_ref, rank_ref, dest_ref):
    e = eidx_ref[...]

    def pick(i, acc):
        return jnp.where(e == i, pstart_ref[i], acc)

    dest_ref[...] = lax.fori_loop(0, N_EXPERTS, pick, jnp.zeros_like(e)) + rank_ref[...]


def _slots(pstarts, eidx, rank):
    n_tok = eidx.shape[1]
    tn = 2048 if n_tok % 2048 == 0 else ROW_TILE
    blk = pl.BlockSpec((TOP_K, tn), lambda i, ps: (0, i))
    return pl.pallas_call(
        _slots_kernel,
        grid_spec=pltpu.PrefetchScalarGridSpec(
            num_scalar_prefetch=1, grid=(n_tok // tn,), in_specs=[blk, blk], out_specs=blk),
        out_shape=jax.ShapeDtypeStruct((TOP_K, n_tok), I32),
        compiler_params=_cparams(("arbitrary",)),
        name="moe_slots",
    )(pstarts, eidx, rank)


def _dispatch_kernel(pends_ref, pcounts_ref, dest_ref, h_ref, xs_hbm, zbuf, sem):
    td = dest_ref.shape[1]
    bm = zbuf.shape[0]

    @pl.when(pl.program_id(0) == 0)
    def _():
        zbuf[...] = jnp.zeros_like(zbuf)

        def tail(e):
            return xs_hbm.at[pl.ds(pl.multiple_of(pends_ref[e] - bm, bm), bm)]

        def zstart(e, carry):
            @pl.when(pcounts_ref[e] > 0)
            def _():
                pltpu.make_async_copy(zbuf, tail(e), sem).start()
            return carry

        def zwait(e, carry):
            @pl.when(pcounts_ref[e] > 0)
            def _():
                pltpu.make_async_copy(zbuf, tail(e), sem).wait()
            return carry

        lax.fori_loop(0, N_EXPERTS, zstart, 0)
        lax.fori_loop(0, N_EXPERTS, zwait, 0)

        def unused(j):
            return xs_hbm.at[pl.ds(pl.multiple_of(j * bm, bm), bm)]

        def ustart(j, carry):
            pltpu.make_async_copy(zbuf, unused(j), sem).start()
            return carry

        def uwait(j, carry):
            pltpu.make_async_copy(zbuf, unused(j), sem).wait()
            return carry

        first_unused = pends_ref[N_EXPERTS - 1] // bm
        lax.fori_loop(first_unused, xs_hbm.shape[0] // bm, ustart, 0)
        lax.fori_loop(first_unused, xs_hbm.shape[0] // bm, uwait, 0)

    def issue(t, carry):
        for k in range(TOP_K):
            pltpu.make_async_copy(h_ref.at[pl.ds(t, 1)], xs_hbm.at[pl.ds(dest_ref[k, t], 1)], sem).start()
        return carry

    lax.fori_loop(0, td, issue, 0)
    pltpu.make_async_copy(xs_hbm.at[pl.ds(0, TOP_K * td)], xs_hbm.at[pl.ds(0, TOP_K * td)], sem).wait()


def _dispatch(dest, h2, pends, pcounts, n_slots):
    n_tok = dest.shape[1]
    d = h2.shape[1]
    td = DISPATCH_TOK
    grid_spec = pltpu.PrefetchScalarGridSpec(
        num_scalar_prefetch=2,
        grid=(n_tok // td,),
        in_specs=[
            pl.BlockSpec((TOP_K, td), lambda i, pe, pc: (0, i), memory_space=pltpu.SMEM),
            pl.BlockSpec((td, d), lambda i, pe, pc: (i, 0)),
        ],
        out_specs=pl.BlockSpec(memory_space=pl.ANY),
        scratch_shapes=[pltpu.VMEM((MOE_BM, d), h2.dtype), pltpu.SemaphoreType.DMA(())],
    )
    return pl.pallas_call(
        _dispatch_kernel,
        grid_spec=grid_spec,
        out_shape=jax.ShapeDtypeStruct((n_slots, d), h2.dtype),
        compiler_params=_cparams(("arbitrary",)),
        name="moe_dispatch",
    )(pends, pcounts, dest, h2)


def _gmm_kernel(gb0_ref, nblk_ref, u0_ref, n_used_ref, xa_hbm, xb_hbm, wgu_ref, wd_ref, ys_hbm,
                xbuf, ybuf, wgu_bf, wd_bf, sem_in, sem_out):
    g = pl.program_id(0)
    n_in, bm = xbuf.shape[:2]
    n_out = ybuf.shape[0]
    ff = wd_ref.shape[0]
    n_used = n_used_ref[0]
    rows_a = xa_hbm.shape[0]

    def x_copy(src_hbm, row0, b):
        rows = pl.ds(row0 if isinstance(row0, int) else pl.multiple_of(row0, bm), bm)
        return pltpu.make_async_copy(src_hbm.at[rows], xbuf.at[b % n_in], sem_in.at[b % n_in])

    def x_start(b):
        u0 = u0_ref[b]

        @pl.when(u0 < rows_a)
        def _():
            x_copy(xa_hbm, u0, b).start()

        @pl.when(u0 >= rows_a)
        def _():
            x_copy(xb_hbm, u0 - rows_a, b).start()

    def x_wait(b):
        x_copy(xa_hbm, 0, b).wait()

    def y_copy(b, slot):
        rows = pl.ds(pl.multiple_of(u0_ref[b], bm), bm)
        return pltpu.make_async_copy(ybuf.at[slot], ys_hbm.at[rows], sem_out.at[slot])

    @pl.when(g == 0)
    def _():
        for b in range(n_in - 1):
            @pl.when(b < n_used)
            def _():
                x_start(b)

    nb = nblk_ref[g]

    @pl.when(g % 2 == 0)
    def _():
        wgu_bf[...] = wgu_ref[...].astype(BF16)
        wd_bf[...] = wd_ref[...].astype(BF16)

    def block(j, carry):
        b = gb0_ref[g] + j
        slot = b % n_out
        x_wait(b)

        @pl.when(b + n_in - 1 < n_used)
        def _():
            x_start(b + n_in - 1)

        @pl.when(b >= n_out)
        def _():
            y_copy(b - n_out, slot).wait()

        gu = jnp.dot(xbuf[b % n_in].astype(BF16), wgu_bf[...], preferred_element_type=F32)
        act = (_silu(gu[:, :ff]) * gu[:, ff:]).astype(BF16)
        ybuf[slot] = jnp.dot(act, wd_bf[...], preferred_element_type=F32)
        y_copy(b, slot).start()
        return carry

    lax.fori_loop(0, nb, block, 0)

    @pl.when(g == pl.num_programs(0) - 1)
    def _():
        for back in range(n_out, 0, -1):
            @pl.when(n_used >= back)
            def _():
                y_copy(n_used - back, (n_used - back) % n_out).wait()


def _grouped_ffn(xs_a, xs_b, gb0, nblk, u0_blk, n_used, wgu, wd, layer):
    n_slots = xs_a.shape[0] + xs_b.shape[0]
    bm = MOE_BM
    n_exp, d, ff2 = wgu.shape[1:]
    ff = wd.shape[2]
    grid_spec = pltpu.PrefetchScalarGridSpec(
        num_scalar_prefetch=4,
        grid=(2 * n_exp,),
        in_specs=[
            pl.BlockSpec(memory_space=pl.ANY),
            pl.BlockSpec(memory_space=pl.ANY),
            pl.BlockSpec((None, None, d, ff2), lambda g, *_: (layer, g // 2, 0, 0)),
            pl.BlockSpec((None, None, ff, d), lambda g, *_: (layer, g // 2, 0, 0)),
        ],
        out_specs=pl.BlockSpec(memory_space=pl.ANY),
        scratch_shapes=[
            pltpu.VMEM((GMM_IN_SLOTS, bm, d), F32), pltpu.VMEM((GMM_OUT_SLOTS, bm, d), F32),
            pltpu.VMEM((d, ff2), BF16), pltpu.VMEM((ff, d), BF16),
            pltpu.SemaphoreType.DMA((GMM_IN_SLOTS,)), pltpu.SemaphoreType.DMA((GMM_OUT_SLOTS,)),
        ],
    )
    return pl.pallas_call(
        _gmm_kernel,
        grid_spec=grid_spec,
        out_shape=jax.ShapeDtypeStruct((n_slots, d), F32),
        compiler_params=_cparams(("arbitrary",)),
        name="moe_grouped_ffn",
    )(gb0, nblk, u0_blk, n_used, xs_a, xs_b, wgu, wd)


def _shared_expert(h2_ref, swgu_ref, swd_ref):
    ff = swd_ref.shape[0]
    gu = jnp.dot(h2_ref[...].astype(BF16), swgu_ref[...], preferred_element_type=F32)
    act = (_silu(gu[:, :ff]) * gu[:, ff:]).astype(BF16)
    return jnp.dot(act, swd_ref[...], preferred_element_type=F32)


def _combine_kernel(dest_ref, gates_ref, ys_hbm, x1_ref, h2_ref, gate_ref, lng_ref, lnb_ref,
                    swgu_ref, swd_ref, o_ref, buf, sem, *, dn_alpha):
    tc = x1_ref.shape[0]

    def issue(t, carry):
        for k in range(TOP_K):
            pltpu.make_async_copy(ys_hbm.at[pl.ds(dest_ref[k, t], 1)], buf.at[k, pl.ds(t, 1)], sem).start()
        return carry

    lax.fori_loop(0, tc, issue, 0)

    y = _shared_expert(h2_ref, swgu_ref, swd_ref)

    for k in range(TOP_K):
        pltpu.make_async_copy(ys_hbm.at[pl.ds(0, tc)], buf.at[k], sem).wait()
    for k in range(TOP_K):
        y = y + gates_ref[:, k:k + 1] * buf[k]
    o_ref[...] = _layer_norm(dn_alpha * x1_ref[...] + gate_ref[...] * y, lng_ref[...], lnb_ref[...])


def _combine_gathered_kernel(gates_ref, g_ref, x1_ref, h2_ref, gate_ref, lng_ref, lnb_ref,
                             swgu_ref, swd_ref, prev_ref, o_ref, *, dn_alpha):
    del prev_ref
    y = _shared_expert(h2_ref, swgu_ref, swd_ref)
    for k in range(TOP_K):
        y = y + gates_ref[:, k:k + 1] * g_ref[k]
    o_ref[...] = _layer_norm(dn_alpha * x1_ref[...] + gate_ref[...] * y, lng_ref[...], lnb_ref[...])


def _sc_gather_rows(table, idx):
    n = idx.shape[0]
    d = table.shape[1]
    mesh = plsc.VectorSubcoreMesh(core_axis_name="c", subcore_axis_name="s")
    n_workers = mesh.num_cores * mesh.num_subcores
    ch = SC_GATHER_ROWS
    per_w = n // n_workers
    n_ch = per_w // ch
    assert n % (n_workers * ch * 2) == 0

    @functools.partial(
        pl.kernel, mesh=mesh, out_type=jax.ShapeDtypeStruct((n, d), table.dtype),
        scratch_types=[pltpu.VMEM((ch,), I32), pltpu.VMEM((ch,), I32),
                       pltpu.VMEM((ch, d), table.dtype), pltpu.VMEM((ch, d), table.dtype),
                       pltpu.SemaphoreType.DMA, pltpu.SemaphoreType.DMA],
        name="moe_sc_gather")
    def gather(table_hbm, idx_hbm, out_hbm, idx_a, idx_b, rows_a, rows_b, sem_a, sem_b):
        wid = lax.axis_index("s") * mesh.num_cores + lax.axis_index("c")

        def chunk(i):
            return pl.ds(wid * per_w + i * ch, ch)

        def fetch(i, idx_v, rows_v, sem):
            pltpu.sync_copy(idx_hbm.at[chunk(i)], idx_v)
            pltpu.make_async_copy(table_hbm.at[idx_v], rows_v, sem).start()

        def drain(i, idx_v, rows_v, sem):
            pltpu.make_async_copy(table_hbm.at[idx_v], rows_v, sem).wait()
            pltpu.sync_copy(rows_v, out_hbm.at[chunk(i)])

        fetch(0, idx_a, rows_a, sem_a)

        @pl.loop(0, n_ch // 2)
        def _(j):
            i = 2 * j
            fetch(i + 1, idx_b, rows_b, sem_b)
            drain(i, idx_a, rows_a, sem_a)

            @pl.when(i + 2 < n_ch)
            def _():
                fetch(i + 2, idx_a, rows_a, sem_a)

            drain(i + 1, idx_b, rows_b, sem_b)

    return gather(table, idx)


def _sc_scatter_rows(rows, first_row, idx, n_out):
    n_k, n = idx.shape
    d = rows.shape[1]
    mesh = plsc.VectorSubcoreMesh(core_axis_name="c", subcore_axis_name="s")
    n_workers = mesh.num_cores * mesh.num_subcores
    ch = SC_GATHER_ROWS
    per_w = n // n_workers
    assert n % (n_workers * ch) == 0

    @functools.partial(
        pl.kernel, mesh=mesh, out_type=jax.ShapeDtypeStruct((n_out, d), rows.dtype),
        scratch_types=[pltpu.VMEM((ch,), I32), pltpu.VMEM((ch, d), rows.dtype)],
        name="moe_sc_scatter")
    def scatter(rows_hbm, idx_hbm, out_hbm, idx_v, rows_v):
        wid = lax.axis_index("s") * mesh.num_cores + lax.axis_index("c")

        @pl.loop(0, per_w // ch)
        def _(i):
            base = wid * per_w + i * ch
            pltpu.sync_copy(rows_hbm.at[pl.ds(first_row + base, ch)], rows_v)
            for k in range(n_k):
                pltpu.sync_copy(idx_hbm.at[k, pl.ds(base, ch)], idx_v)
                pltpu.sync_copy(rows_v, out_hbm.at[idx_v])

    return scatter(rows, idx)


def _combine(dest, gates_t, ys, x1, h2, mod, ln_g, ln_b, swgu, swd, n_rows, n_lat, seq, ctx_row,
             dn_alpha):
    d = x1.shape[1]
    tc = DISPATCH_TOK
    const = lambda i: (0, 0)
    n_blocks = n_rows // tc
    n_sc_blocks = (n_blocks * SC_SHARE_PERCENT) // 100
    n_tc = (n_blocks - n_sc_blocks) * tc
    n_sc = n_rows - n_tc
    gathered = _sc_gather_rows(ys, dest[:, n_tc:].reshape(TOP_K * n_sc)).reshape(TOP_K, n_sc, d)
    swgu_bf, swd_bf = swgu.astype(BF16), swd.astype(BF16)
    out_tc = pl.pallas_call(
        functools.partial(_combine_kernel, dn_alpha=dn_alpha),
        grid=(n_tc // tc,),
        in_specs=[
            pl.BlockSpec((TOP_K, tc), lambda i: (0, i), memory_space=pltpu.SMEM),
            pl.BlockSpec((tc, TOP_K), lambda i: (i, 0)),
            pl.BlockSpec(memory_space=pl.ANY),
            pl.BlockSpec((tc, d), lambda i: (i, 0)),
            pl.BlockSpec((tc, d), lambda i: (i, 0)),
            _mod_spec(5, tc, n_lat, seq, ctx_row, d),
            pl.BlockSpec((1, d), const), pl.BlockSpec((1, d), const),
            pl.BlockSpec(swgu.shape, const),
            pl.BlockSpec(swd.shape, const),
        ],
        out_specs=pl.BlockSpec((tc, d), lambda i: (i, 0)),
        out_shape=jax.ShapeDtypeStruct((n_rows, d), F32),
        scratch_shapes=[pltpu.VMEM((TOP_K, tc, d), F32), pltpu.SemaphoreType.DMA(())],
        compiler_params=_cparams(("arbitrary",)),
        name="moe_combine",
    )(dest, gates_t, ys, x1, h2, mod, ln_g.reshape(1, d), ln_b.reshape(1, d), swgu_bf, swd_bf)

    tl = COMBINE_GATHERED_TOK
    off = n_tc // tl
    row_blk = pl.BlockSpec((tl, d), lambda i: (i + off, 0))
    return pl.pallas_call(
        functools.partial(_combine_gathered_kernel, dn_alpha=dn_alpha),
        grid=(n_sc // tl,),
        in_specs=[
            pl.BlockSpec((tl, TOP_K), lambda i: (i + off, 0)),
            pl.BlockSpec((TOP_K, tl, d), lambda i: (0, i, 0)),
            row_blk, row_blk,
            _mod_spec(5, tl, n_lat, seq, ctx_row, d, blk_off=off),
            pl.BlockSpec((1, d), const), pl.BlockSpec((1, d), const),
            pl.BlockSpec(swgu.shape, const),
            pl.BlockSpec(swd.shape, const),
            pl.BlockSpec(memory_space=pl.ANY),
        ],
        out_specs=row_blk,
        out_shape=jax.ShapeDtypeStruct((n_rows, d), F32),
        input_output_aliases={9: 0},
        compiler_params=_cparams(("parallel",)),
        name="moe_combine_gathered",
    )(gates_t, gathered, x1, h2, mod, ln_g.reshape(1, d), ln_b.reshape(1, d), swgu_bf, swd_bf, out_tc)


def _split_block(n_rows):
    n_blocks = n_rows // ROW_TILE
    sc_blocks = (n_blocks * SC_DISPATCH_PERCENT) // 100 // 2 * 2
    return n_blocks - sc_blocks


def _max_slots(n_tok, bm):
    return (n_tok * TOP_K + N_EXPERTS * (bm - 1) + bm - 1) // bm * bm


def _moe(x1, h2, eidx, gates, rank, counts, mod, ln_g, ln_b, wgu, wd, layer, swgu, swd,
         n_rows, n_lat, seq, ctx_row, dn_alpha, split_blk):
    bm = MOE_BM
    t0 = split_blk * ROW_TILE
    rows_a, rows_b = _max_slots(t0, bm), _max_slots(n_rows - t0, bm)
    cnt = counts.reshape(2, N_EXPERTS).astype(I32)
    pcnt = (cnt + bm - 1) // bm * bm
    pend = jnp.cumsum(pcnt, axis=1)
    ustart = pend - pcnt + jnp.array([[0], [rows_a]], I32)
    dest = jnp.concatenate([_slots(ustart[0], eidx[:, :t0], rank[:, :t0]),
                            _slots(ustart[1], eidx[:, t0:], rank[:, t0:])], axis=1)

    nblk = (pcnt // bm).T.reshape(2 * N_EXPERTS)
    gb0 = jnp.cumsum(nblk) - nblk
    u0_grp = ustart.T.reshape(2 * N_EXPERTS)
    blk = jnp.arange((rows_a + rows_b) // bm, dtype=I32)
    grp = jnp.sum((gb0[None, :] <= blk[:, None]).astype(I32), axis=1) - 1
    u0_blk = u0_grp[grp] + (blk - gb0[grp]) * bm
    n_used = jnp.sum(nblk).astype(I32).reshape(1)

    xs_a = _dispatch(dest[:, :t0], h2, pend[0], pcnt[0], rows_a)
    xs_b = _sc_scatter_rows(h2, t0, dest[:, t0:] - rows_a, rows_b)
    ys = _grouped_ffn(xs_a, xs_b, gb0, nblk, u0_blk, n_used, wgu, wd, layer)
    return _combine(dest, gates.T, ys, x1, h2, mod, ln_g, ln_b, swgu, swd,
                    n_rows, n_lat, seq, ctx_row, dn_alpha)


def kernel(x, c, ctx, c_ctx, ada_w, ada_b, ln_g, ln_b, attn_w_qkv, attn_q_norm, attn_k_norm, attn_w_o, conv_w_in, conv_taps, conv_w_out, router_w, router_bias, exp_w_gate_up, exp_w_down, shared_w_gate_up, shared_w_down):
    b, seq, d = x.shape
    cl = ctx.shape[1]
    depth = ada_w.shape[0]
    n_lat = b * seq
    n_ctx = b * cl
    dn_alpha = (2 * depth) ** 0.25
    assert depth == 2 and b < MOD_ROWS
    assert seq % ROW_TILE == 0 and n_ctx % ROW_TILE == 0 and seq % GRID_W == 0

    cond = jnp.zeros((MOD_ROWS, d), F32).at[:b].set(c).at[b].set(c_ctx)
    mod = _modulation(cond, ada_w, ada_b)
    xall = jnp.concatenate([x.reshape(n_lat, d), ctx.reshape(n_ctx, d)], axis=0)
    n_all = n_lat + n_ctx

    q, k, v = _qkv_project(xall, mod[0], attn_w_qkv[0], attn_q_norm[0], attn_k_norm[0], n_lat, seq, b)
    o = _attention(q, k, v, b, seq, cl, n_lat)
    x1, h2, eidx, gates, rank, counts = _post_mixer(
        o, xall, mod[0], ln_g[0, 0], ln_b[0, 0], attn_w_o[0], router_w[0], router_bias[0],
        n_all, n_lat, seq, b, dn_alpha, _split_block(n_all))
    xall = _moe(x1, h2, eidx, gates, rank, counts, mod[0], ln_g[0, 1], ln_b[0, 1],
                exp_w_gate_up, exp_w_down, 0, shared_w_gate_up[0], shared_w_down[0],
                n_all, n_lat, seq, b, dn_alpha, _split_block(n_all))

    zin = _in_project(xall, mod[1], conv_w_in[0], n_lat, n_lat, seq, b)
    a = _short_conv(zin, conv_taps[0], b, seq, d)
    x1, h2, eidx, gates, rank, counts = _post_mixer(
        a, xall, mod[1], ln_g[1, 0], ln_b[1, 0], conv_w_out[0], router_w[1], router_bias[1],
        n_lat, n_lat, seq, b, dn_alpha, _split_block(n_lat))
    out = _moe(x1, h2, eidx, gates, rank, counts, mod[1], ln_g[1, 1], ln_b[1, 1],
               exp_w_gate_up, exp_w_down, 1, shared_w_gate_up[1], shared_w_down[1],
               n_lat, n_lat, seq, b, dn_alpha, _split_block(n_lat))
    return out.reshape(b, seq, d)
```

```python
import functools

import jax
import jax.numpy as jnp
from jax import lax
from jax.experimental import pallas as pl
from jax.experimental.pallas import tpu as pltpu
from jax.experimental.pallas import tpu_sc as plsc

F32 = jnp.float32
BF16 = jnp.bfloat16
I32 = jnp.int32

N_HEADS = 8
N_KV_HEADS = 2
HEAD_DIM = 128
KV_GROUP = N_HEADS // N_KV_HEADS
GRID_W = 64
ROPE_THETA = 10000.0
N_EXPERTS = 256
TOP_K = 8
N_GROUPS = 8
TOPK_GROUPS = 4
PER_GROUP = N_EXPERTS // N_GROUPS
ROUTED_SCALE = 2.5
LN_EPS = 1e-5
QK_EPS = 1e-6
N_MOD = 6
MOD_ROWS = 16

LANES = 128
SUBLANES = 8
VMEM_LIMIT = 56 * 1024 * 1024

ROW_TILE = 512
ATTN_TQ = 256
ATTN_TK = 2048
MOE_BM = 256
GMM_IN_SLOTS = 4
GMM_OUT_SLOTS = 3
DISPATCH_TOK = 512
COMBINE_GATHERED_TOK = 256
SC_GATHER_ROWS = 32
SC_SHARE_PERCENT = 65
SC_DISPATCH_PERCENT = 75
CONV_LANES = 128

HIGHEST = lax.Precision.HIGHEST
LOG2_E = 1.4426950408889634


def _cparams(sem):
    return pltpu.CompilerParams(dimension_semantics=sem, vmem_limit_bytes=VMEM_LIMIT)


def _silu(v):
    return v * jax.nn.sigmoid(v)


def _mod_kernel(c_ref, w_ref, b_ref, o_ref):
    s = _silu(c_ref[...])
    o_ref[...] = jnp.dot(s, w_ref[...], precision=HIGHEST, preferred_element_type=F32) + b_ref[...]


def _modulation(cond, ada_w, ada_b):
    depth, d, nd = ada_w.shape
    tn = 1536
    out = pl.pallas_call(
        _mod_kernel,
        grid=(depth, nd // tn),
        in_specs=[
            pl.BlockSpec((MOD_ROWS, d), lambda l, j: (0, 0)),
            pl.BlockSpec((None, d, tn), lambda l, j: (l, 0, j)),
            pl.BlockSpec((None, 1, tn), lambda l, j: (l, 0, j)),
        ],
        out_specs=pl.BlockSpec((None, MOD_ROWS, tn), lambda l, j: (l, 0, j)),
        out_shape=jax.ShapeDtypeStruct((depth, MOD_ROWS, nd), F32),
        compiler_params=_cparams(("arbitrary", "arbitrary")),
        name="adaln_modulation",
    )(cond, ada_w, ada_b.reshape(depth, 1, nd))
    return out.reshape(depth, MOD_ROWS * N_MOD, 1, d)


def _mod_spec(comp, tm, n_lat, seq, ctx_row, d, blk_off=0):
    def index(i, *_):
        row0 = (i + blk_off) * tm
        r = jnp.where(row0 < n_lat, row0 // seq, ctx_row)
        return (r * N_MOD + comp, 0, 0)

    return pl.BlockSpec((None, 1, d), index)


def _qkv_kernel(x_ref, shift_ref, scale_ref, w_ref, qg_ref, kg_ref, cos_ref, sin_ref,
                q_ref, k_ref, v_ref):
    h = (x_ref[...] * (1.0 + scale_ref[...]) + shift_ref[...]).astype(BF16)
    qkv = jnp.dot(h, w_ref[...], preferred_element_type=F32)
    cos = cos_ref[...]
    sin = sin_ref[...]
    hq = N_HEADS * HEAD_DIM
    kd = N_KV_HEADS * HEAD_DIM

    def norm_rope(t, g, post):
        t = t * lax.rsqrt(jnp.mean(t * t, axis=-1, keepdims=True) + QK_EPS) * g
        t = t * cos + pltpu.roll(t, HEAD_DIM // 2, axis=1) * sin
        return (t * post).astype(BF16)

    for hd in range(N_HEADS):
        sl = slice(hd * HEAD_DIM, (hd + 1) * HEAD_DIM)
        q_ref[:, sl] = norm_rope(qkv[:, sl], qg_ref[...], HEAD_DIM ** -0.5 * LOG2_E)
    for hd in range(N_KV_HEADS):
        sl = slice(hd * HEAD_DIM, (hd + 1) * HEAD_DIM)
        k_ref[:, sl] = norm_rope(qkv[:, hq + hd * HEAD_DIM: hq + (hd + 1) * HEAD_DIM], kg_ref[...], 1.0)
    v_ref[...] = qkv[:, hq + kd:].astype(BF16)


def _rope_tables(seq, tm):
    rows = seq // GRID_W
    row = jnp.repeat(jnp.arange(rows, dtype=F32), GRID_W)
    col = jnp.tile(jnp.arange(GRID_W, dtype=F32), rows)
    axis_dim = HEAD_DIM // 2
    freqs = ROPE_THETA ** (-jnp.arange(0, axis_dim, 2, dtype=F32) / axis_dim)
    ang = jnp.concatenate([row[:, None] * freqs, col[:, None] * freqs], axis=-1)
    cos = jnp.concatenate([jnp.cos(ang), jnp.cos(ang)], axis=-1)
    sin = jnp.concatenate([-jnp.sin(ang), jnp.sin(ang)], axis=-1)
    cos = jnp.concatenate([cos, jnp.ones((tm, HEAD_DIM), F32)], axis=0)
    sin = jnp.concatenate([sin, jnp.zeros((tm, HEAD_DIM), F32)], axis=0)
    return cos.reshape(seq // tm + 1, tm, HEAD_DIM), sin.reshape(seq // tm + 1, tm, HEAD_DIM)


def _qkv_project(xall, mod, w_qkv, q_g, k_g, n_lat, seq, ctx_row):
    t, d = xall.shape
    tm = ROW_TILE
    hq = N_HEADS * HEAD_DIM
    kd = N_KV_HEADS * HEAD_DIM
    perm = jnp.concatenate([jnp.arange(0, HEAD_DIM, 2), jnp.arange(1, HEAD_DIM, 2)])
    cols = jnp.concatenate([hd * HEAD_DIM + perm for hd in range(N_HEADS + N_KV_HEADS)]
                           + [jnp.arange(hq + kd, hq + 2 * kd)])
    w = w_qkv[:, cols].astype(BF16)
    cos, sin = _rope_tables(seq, tm)
    n_pos = seq // tm

    def pos_index(i):
        row0 = i * tm
        return (jnp.where(row0 < n_lat, (row0 % seq) // tm, n_pos), 0, 0)

    const = lambda i: (0, 0)
    return pl.pallas_call(
        _qkv_kernel,
        grid=(t // tm,),
        in_specs=[
            pl.BlockSpec((tm, d), lambda i: (i, 0)),
            _mod_spec(0, tm, n_lat, seq, ctx_row, d),
            _mod_spec(1, tm, n_lat, seq, ctx_row, d),
            pl.BlockSpec(w.shape, const),
            pl.BlockSpec((1, HEAD_DIM), const),
            pl.BlockSpec((1, HEAD_DIM), const),
            pl.BlockSpec((None, tm, HEAD_DIM), pos_index),
            pl.BlockSpec((None, tm, HEAD_DIM), pos_index),
        ],
        out_specs=[
            pl.BlockSpec((tm, hq), lambda i: (i, 0)),
            pl.BlockSpec((tm, kd), lambda i: (i, 0)),
            pl.BlockSpec((tm, kd), lambda i: (i, 0)),
        ],
        out_shape=[
            jax.ShapeDtypeStruct((t, hq), BF16),
            jax.ShapeDtypeStruct((t, kd), BF16),
            jax.ShapeDtypeStruct((t, kd), BF16),
        ],
        compiler_params=_cparams(("parallel",)),
        name="qkv_norm_rope",
    )(xall, mod, mod, w, q_g[perm].reshape(1, HEAD_DIM), k_g[perm].reshape(1, HEAD_DIM), cos, sin)


def _attn_kernel(*refs, n_lat_chunks, tk):
    if n_lat_chunks:
        q_ref, kc_ref, vc_ref, kl_ref, vl_ref, o_ref = refs
    else:
        q_ref, kc_ref, vc_ref, o_ref = refs
    tq = q_ref.shape[0]
    q = jnp.concatenate([q_ref[:, h * HEAD_DIM:(h + 1) * HEAD_DIM] for h in range(KV_GROUP)], axis=0)
    rows = KV_GROUP * tq

    def chunk(k, v, m, l, acc):
        s = lax.dot_general(q, k, (((1,), (1,)), ((), ())), preferred_element_type=F32)
        m_new = jnp.maximum(m, jnp.max(s, axis=-1, keepdims=True))
        p = jnp.exp2(s - m_new)
        a = jnp.exp2(m - m_new)
        l = a * l + jnp.sum(p, axis=-1, keepdims=True)
        acc = a * acc + jnp.dot(p.astype(BF16), v, preferred_element_type=F32)
        return m_new, l, acc

    m = jnp.full((rows, 1), -jnp.inf, F32)
    l = jnp.zeros((rows, 1), F32)
    acc = jnp.zeros((rows, HEAD_DIM), F32)
    m, l, acc = chunk(kc_ref[...], vc_ref[...], m, l, acc)
    for c in range(n_lat_chunks):
        m, l, acc = chunk(kl_ref[c * tk:(c + 1) * tk, :], vl_ref[c * tk:(c + 1) * tk, :], m, l, acc)
    o = (acc / l).astype(BF16)
    for h in range(KV_GROUP):
        o_ref[:, h * HEAD_DIM:(h + 1) * HEAD_DIM] = o[h * tq:(h + 1) * tq]


def _attention(q, k, v, b, seq, cl, n_lat):
    t = q.shape[0]
    gw = KV_GROUP * HEAD_DIM
    tq = min(ATTN_TQ, seq)
    tk = min(ATTN_TK, seq)
    nq = seq // tq
    ctx_blk0 = n_lat // cl
    hq = N_HEADS * HEAD_DIM

    ctx_kv = pl.BlockSpec((cl, HEAD_DIM), lambda bi, g, qi: (ctx_blk0 + bi, g))
    lat_kv = pl.BlockSpec((seq, HEAD_DIM), lambda bi, g, qi: (bi, g))
    lat_q = pl.BlockSpec((tq, gw), lambda bi, g, qi: (bi * nq + qi, g))
    o_lat = pl.pallas_call(
        functools.partial(_attn_kernel, n_lat_chunks=seq // tk, tk=tk),
        grid=(b, N_KV_HEADS, nq),
        in_specs=[lat_q, ctx_kv, ctx_kv, lat_kv, lat_kv],
        out_specs=lat_q,
        out_shape=jax.ShapeDtypeStruct((n_lat, hq), BF16),
        compiler_params=_cparams(("parallel", "parallel", "arbitrary")),
        name="attention_latent",
    )(q, k, v, k, v)

    ctx_q = pl.BlockSpec((cl, gw), lambda bi, g: (ctx_blk0 + bi, g))
    ctx_kv2 = pl.BlockSpec((cl, HEAD_DIM), lambda bi, g: (ctx_blk0 + bi, g))
    o_ctx = pl.pallas_call(
        functools.partial(_attn_kernel, n_lat_chunks=0, tk=tk),
        grid=(b, N_KV_HEADS),
        in_specs=[ctx_q, ctx_kv2, ctx_kv2],
        out_specs=pl.BlockSpec((cl, gw), lambda bi, g: (bi, g)),
        out_shape=jax.ShapeDtypeStruct((t - n_lat, hq), BF16),
        compiler_params=_cparams(("parallel", "parallel")),
        name="attention_context",
    )(q, k, v)
    return jnp.concatenate([o_lat, o_ctx], axis=0)


def _in_proj_kernel(x_ref, shift_ref, scale_ref, w_ref, o_ref):
    h = (x_ref[...] * (1.0 + scale_ref[...]) + shift_ref[...]).astype(BF16)
    o_ref[...] = jnp.dot(h, w_ref[...], preferred_element_type=F32)


def _in_project(xall, mod, w_in, n_rows, n_lat, seq, ctx_row):
    d = xall.shape[1]
    n_out = w_in.shape[1]
    tm = ROW_TILE
    return pl.pallas_call(
        _in_proj_kernel,
        grid=(n_rows // tm,),
        in_specs=[
            pl.BlockSpec((tm, d), lambda i: (i, 0)),
            _mod_spec(0, tm, n_lat, seq, ctx_row, d),
            _mod_spec(1, tm, n_lat, seq, ctx_row, d),
            pl.BlockSpec(w_in.shape, lambda i: (0, 0)),
        ],
        out_specs=pl.BlockSpec((tm, n_out), lambda i: (i, 0)),
        out_shape=jax.ShapeDtypeStruct((n_rows, n_out), F32),
        compiler_params=_cparams(("parallel",)),
        name="conv_in_proj",
    )(xall, mod, mod, w_in.astype(BF16))


def _conv_kernel(bg_ref, cg_ref, v_ref, taps_ref, o_ref):
    u = cg_ref[...] * v_ref[...]
    n = u.shape[0]
    pos = lax.broadcasted_iota(I32, u.shape, 0)
    prev = jnp.where(pos == 0, 0.0, pltpu.roll(u, 1, axis=0))
    nxt = jnp.where(pos == n - 1, 0.0, pltpu.roll(u, n - 1, axis=0))
    conv = prev * taps_ref[0:1, :] + u * taps_ref[1:2, :] + nxt * taps_ref[2:3, :]
    o_ref[...] = (bg_ref[...] * conv).astype(BF16)


def _short_conv(zin, taps, n_seqs, seq, d):
    tc = CONV_LANES
    nj = d // tc
    return pl.pallas_call(
        _conv_kernel,
        grid=(n_seqs, nj),
        in_specs=[
            pl.BlockSpec((seq, tc), lambda s, j: (s, j)),
            pl.BlockSpec((seq, tc), lambda s, j: (s, nj + j)),
            pl.BlockSpec((seq, tc), lambda s, j: (s, 2 * nj + j)),
            pl.BlockSpec((taps.shape[0], tc), lambda s, j: (0, j)),
        ],
        out_specs=pl.BlockSpec((seq, tc), lambda s, j: (s, j)),
        out_shape=jax.ShapeDtypeStruct((n_seqs * seq, d), BF16),
        compiler_params=_cparams(("parallel", "parallel")),
        name="short_conv",
    )(zin, zin, zin, taps)


def _layer_norm(z, g, b):
    mu = jnp.mean(z, axis=-1, keepdims=True)
    zc = z - mu
    var = jnp.mean(zc * zc, axis=-1, keepdims=True)
    return zc * lax.rsqrt(var + LN_EPS) * g + b


def _post_kernel(a_ref, x_ref, gate_ref, lng_ref, lnb_ref, shift_ref, scale_ref, w_ref,
                 rwt_ref, rb_ref, tri_ref,
                 x1_ref, h2_ref, eidx_ref, gates_ref, rank_ref, cnt_ref, rwh_ref, rwl_ref, *,
                 dn_alpha, split_blk):
    tm = a_ref.shape[0]

    @pl.when(pl.program_id(0) == 0)
    def _():
        hi = rwt_ref[...].astype(BF16)
        rwh_ref[...] = hi
        rwl_ref[...] = (rwt_ref[...] - hi.astype(F32)).astype(BF16)

    d = x_ref.shape[1]
    y = jnp.dot(a_ref[...], w_ref[...], preferred_element_type=F32)
    x1 = _layer_norm(dn_alpha * x_ref[...] + gate_ref[...] * y, lng_ref[...], lnb_ref[...])
    x1_ref[...] = x1
    h2 = x1 * (1.0 + scale_ref[...]) + shift_ref[...]
    h2_ref[...] = h2

    h_hi = h2.astype(BF16)
    h_lo = (h2 - h_hi.astype(F32)).astype(BF16)

    def nt_dot(w, h):
        return lax.dot_general(w, h, (((1,), (1,)), ((), ())), preferred_element_type=F32)

    logits = nt_dot(rwh_ref[...], h_hi) + (nt_dot(rwl_ref[...], h_hi) + nt_dot(rwh_ref[...], h_lo))
    scores = jax.nn.sigmoid(logits)
    biased = scores + rb_ref[...]
    neg = -jnp.inf
    big = jnp.int32(1 << 30)
    row = lax.broadcasted_iota(I32, (N_EXPERTS, tm), 0)

    def argmax_rows(vals, idx):
        mx = jnp.max(vals, axis=0, keepdims=True)
        return mx, jnp.min(jnp.where(vals == mx, idx, big), axis=0, keepdims=True)

    gs = []
    grp_row = lax.broadcasted_iota(I32, (PER_GROUP, tm), 0)
    for g in range(N_GROUPS):
        bg = biased[g * PER_GROUP:(g + 1) * PER_GROUP]
        ig = grp_row + g * PER_GROUP
        m1, i1 = argmax_rows(bg, ig)
        m2 = jnp.max(jnp.where(ig == i1, neg, bg), axis=0, keepdims=True)
        gs.append(m1 + m2)
    gsc = jnp.concatenate(gs, axis=0)
    grow = lax.broadcasted_iota(I32, (N_GROUPS, tm), 0)
    gsel = jnp.zeros((N_GROUPS, tm), F32)
    for _ in range(TOPK_GROUPS):
        _, gi = argmax_rows(gsc, grow)
        hit = grow == gi
        gsel = jnp.where(hit, 1.0, gsel)
        gsc = jnp.where(hit, neg, gsc)
    cur = jnp.concatenate(
        [jnp.where(gsel[g:g + 1] > 0.0, biased[g * PER_GROUP:(g + 1) * PER_GROUP], neg)
         for g in range(N_GROUPS)], axis=0)

    onehot = jnp.zeros((N_EXPERTS, tm), F32)
    idxs, gvals = [], []
    for _ in range(TOP_K):
        _, ei = argmax_rows(cur, row)
        hit = row == ei
        gvals.append(jnp.sum(jnp.where(hit, scores, 0.0), axis=0, keepdims=True))
        idxs.append(ei)
        onehot = jnp.where(hit, 1.0, onehot)
        cur = jnp.where(hit, neg, cur)
    gv = jnp.concatenate(gvals, axis=0)
    gates_ref[...] = gv / jnp.sum(gv, axis=0, keepdims=True) * ROUTED_SCALE
    eidx_ref[...] = jnp.concatenate(idxs, axis=0)

    @pl.when((pl.program_id(0) == 0) | (pl.program_id(0) == split_blk))
    def _():
        cnt_ref[...] = jnp.zeros_like(cnt_ref)

    prefix = jnp.dot(onehot.astype(BF16), tri_ref[...], preferred_element_type=F32)
    pos = prefix + cnt_ref[...]
    rank_ref[...] = jnp.concatenate(
        [jnp.sum(jnp.where(row == ei, pos, 0.0), axis=0, keepdims=True) for ei in idxs],
        axis=0).astype(I32)
    cnt_ref[...] = cnt_ref[...] + jnp.sum(onehot, axis=1, keepdims=True)


def _post_mixer(a, xall, mod, ln_g, ln_b, w, router_w, router_bias, n_rows, n_lat, seq, ctx_row,
                dn_alpha, split_blk):
    d = xall.shape[1]
    tm = ROW_TILE
    tri = (lax.broadcasted_iota(I32, (tm, tm), 0) < lax.broadcasted_iota(I32, (tm, tm), 1)).astype(BF16)
    const = lambda i: (0, 0)
    row_blk = pl.BlockSpec((tm, d), lambda i: (i, 0))
    k_blk = pl.BlockSpec((TOP_K, tm), lambda i: (0, i))
    return pl.pallas_call(
        functools.partial(_post_kernel, dn_alpha=dn_alpha, split_blk=split_blk),
        grid=(n_rows // tm,),
        in_specs=[
            row_blk, row_blk,
            _mod_spec(2, tm, n_lat, seq, ctx_row, d),
            pl.BlockSpec((1, d), const), pl.BlockSpec((1, d), const),
            _mod_spec(3, tm, n_lat, seq, ctx_row, d),
            _mod_spec(4, tm, n_lat, seq, ctx_row, d),
            pl.BlockSpec(w.shape, const),
            pl.BlockSpec((N_EXPERTS, d), const),
            pl.BlockSpec((N_EXPERTS, 1), const),
            pl.BlockSpec((tm, tm), const),
        ],
        out_specs=[
            row_blk, row_blk,
            k_blk, k_blk, k_blk,
            pl.BlockSpec((None, N_EXPERTS, 1), lambda i: (jnp.where(i >= split_blk, 1, 0), 0, 0)),
        ],
        out_shape=[
            jax.ShapeDtypeStruct((n_rows, d), F32),
            jax.ShapeDtypeStruct((n_rows, d), F32),
            jax.ShapeDtypeStruct((TOP_K, n_rows), I32),
            jax.ShapeDtypeStruct((TOP_K, n_rows), F32),
            jax.ShapeDtypeStruct((TOP_K, n_rows), I32),
            jax.ShapeDtypeStruct((2, N_EXPERTS, 1), F32),
        ],
        scratch_shapes=[pltpu.VMEM((N_EXPERTS, d), BF16), pltpu.VMEM((N_EXPERTS, d), BF16)],
        compiler_params=_cparams(("arbitrary",)),
        name="post_mixer_router",
    )(a, xall, mod, ln_g.reshape(1, d), ln_b.reshape(1, d), mod, mod, w.astype(BF16),
      router_w.T, router_bias.reshape(N_EXPERTS, 1), tri)


def _slots_kernel(pstart_ref, eidx_ref, rank_ref, dest_ref):
    e = eidx_ref[...]

    def pick(i, acc):
        return jnp.where(e == i, pstart_ref[i], acc)

    dest_ref[...] = lax.fori_loop(0, N_EXPERTS, pick, jnp.zeros_like(e)) + rank_ref[...]


def _slots(pstarts, eidx, rank):
    n_tok = eidx.shape[1]
    tn = 2048 if n_tok % 2048 == 0 else ROW_TILE
    blk = pl.BlockSpec((TOP_K, tn), lambda i, ps: (0, i))
    return pl.pallas_call(
        _slots_kernel,
        grid_spec=pltpu.PrefetchScalarGridSpec(
            num_scalar_prefetch=1, grid=(n_tok // tn,), in_specs=[blk, blk], out_specs=blk),
        out_shape=jax.ShapeDtypeStruct((TOP_K, n_tok), I32),
        compiler_params=_cparams(("arbitrary",)),
        name="moe_slots",
    )(pstarts, eidx, rank)


def _dispatch_kernel(pends_ref, pcounts_ref, dest_ref, h_ref, xs_hbm, zbuf, sem):
    td = dest_ref.shape[1]
    bm = zbuf.shape[0]

    @pl.when(pl.program_id(0) == 0)
    def _():
        zbuf[...] = jnp.zeros_like(zbuf)

        def tail(e):
            return xs_hbm.at[pl.ds(pl.multiple_of(pends_ref[e] - bm, bm), bm)]

        def zstart(e, carry):
            @pl.when(pcounts_ref[e] > 0)
            def _():
                pltpu.make_async_copy(zbuf, tail(e), sem).start()
            return carry

        def zwait(e, carry):
            @pl.when(pcounts_ref[e] > 0)
            def _():
                pltpu.make_async_copy(zbuf, tail(e), sem).wait()
            return carry

        lax.fori_loop(0, N_EXPERTS, zstart, 0)
        lax.fori_loop(0, N_EXPERTS, zwait, 0)

        def unused(j):
            return xs_hbm.at[pl.ds(pl.multiple_of(j * bm, bm), bm)]

        def ustart(j, carry):
            pltpu.make_async_copy(zbuf, unused(j), sem).start()
            return carry

        def uwait(j, carry):
            pltpu.make_async_copy(zbuf, unused(j), sem).wait()
            return carry

        first_unused = pends_ref[N_EXPERTS - 1] // bm
        lax.fori_loop(first_unused, xs_hbm.shape[0] // bm, ustart, 0)
        lax.fori_loop(first_unused, xs_hbm.shape[0] // bm, uwait, 0)

    def issue(t, carry):
        for k in range(TOP_K):
            pltpu.make_async_copy(h_ref.at[pl.ds(t, 1)], xs_hbm.at[pl.ds(dest_ref[k, t], 1)], sem).start()
        return carry

    lax.fori_loop(0, td, issue, 0)
    pltpu.make_async_copy(xs_hbm.at[pl.ds(0, TOP_K * td)], xs_hbm.at[pl.ds(0, TOP_K * td)], sem).wait()


def _dispatch(dest, h2, pends, pcounts, n_slots):
    n_tok = dest.shape[1]
    d = h2.shape[1]
    td = DISPATCH_TOK
    grid_spec = pltpu.PrefetchScalarGridSpec(
        num_scalar_prefetch=2,
        grid=(n_tok // td,),
        in_specs=[
            pl.BlockSpec((TOP_K, td), lambda i, pe, pc: (0, i), memory_space=pltpu.SMEM),
            pl.BlockSpec((td, d), lambda i, pe, pc: (i, 0)),
        ],
        out_specs=pl.BlockSpec(memory_space=pl.ANY),
        scratch_shapes=[pltpu.VMEM((MOE_BM, d), h2.dtype), pltpu.SemaphoreType.DMA(())],
    )
    return pl.pallas_call(
        _dispatch_kernel,
        grid_spec=grid_spec,
        out_shape=jax.ShapeDtypeStruct((n_slots, d), h2.dtype),
        compiler_params=_cparams(("arbitrary",)),
        name="moe_dispatch",
    )(pends, pcounts, dest, h2)


def _gmm_kernel(gb0_ref, nblk_ref, u0_ref, n_used_ref, xa_hbm, xb_hbm, wgu_ref, wd_ref, ys_hbm,
                xbuf, ybuf, wgu_bf, wd_bf, sem_in, sem_out):
    g = pl.program_id(0)
    n_in, bm = xbuf.shape[:2]
    n_out = ybuf.shape[0]
    ff = wd_ref.shape[0]
    n_used = n_used_ref[0]
    rows_a = xa_hbm.shape[0]

    def x_copy(src_hbm, row0, b):
        rows = pl.ds(row0 if isinstance(row0, int) else pl.multiple_of(row0, bm), bm)
        return pltpu.make_async_copy(src_hbm.at[rows], xbuf.at[b % n_in], sem_in.at[b % n_in])

    def x_start(b):
        u0 = u0_ref[b]

        @pl.when(u0 < rows_a)
        def _():
            x_copy(xa_hbm, u0, b).start()

        @pl.when(u0 >= rows_a)
        def _():
            x_copy(xb_hbm, u0 - rows_a, b).start()

    def x_wait(b):
        x_copy(xa_hbm, 0, b).wait()

    def y_copy(b, slot):
        rows = pl.ds(pl.multiple_of(u0_ref[b], bm), bm)
        return pltpu.make_async_copy(ybuf.at[slot], ys_hbm.at[rows], sem_out.at[slot])

    @pl.when(g == 0)
    def _():
        for b in range(n_in - 1):
            @pl.when(b < n_used)
            def _():
                x_start(b)

    nb = nblk_ref[g]

    @pl.when(g % 2 == 0)
    def _():
        wgu_bf[...] = wgu_ref[...].astype(BF16)
        wd_bf[...] = wd_ref[...].astype(BF16)

    def block(j, carry):
        b = gb0_ref[g] + j
        slot = b % n_out
        x_wait(b)

        @pl.when(b + n_in - 1 < n_used)
        def _():
            x_start(b + n_in - 1)

        @pl.when(b >= n_out)
        def _():
            y_copy(b - n_out, slot).wait()

        gu = jnp.dot(xbuf[b % n_in].astype(BF16), wgu_bf[...], preferred_element_type=F32)
        act = (_silu(gu[:, :ff]) * gu[:, ff:]).astype(BF16)
        ybuf[slot] = jnp.dot(act, wd_bf[...], preferred_element_type=F32)
        y_copy(b, slot).start()
        return carry

    lax.fori_loop(0, nb, block, 0)

    @pl.when(g == pl.num_programs(0) - 1)
    def _():
        for back in range(n_out, 0, -1):
            @pl.when(n_used >= back)
            def _():
                y_copy(n_used - back, (n_used - back) % n_out).wait()


def _grouped_ffn(xs_a, xs_b, gb0, nblk, u0_blk, n_used, wgu, wd, layer):
    n_slots = xs_a.shape[0] + xs_b.shape[0]
    bm = MOE_BM
    n_exp, d, ff2 = wgu.shape[1:]
    ff = wd.shape[2]
    grid_spec = pltpu.PrefetchScalarGridSpec(
        num_scalar_prefetch=4,
        grid=(2 * n_exp,),
        in_specs=[
            pl.BlockSpec(memory_space=pl.ANY),
            pl.BlockSpec(memory_space=pl.ANY),
            pl.BlockSpec((None, None, d, ff2), lambda g, *_: (layer, g // 2, 0, 0)),
            pl.BlockSpec((None, None, ff, d), lambda g, *_: (layer, g // 2, 0, 0)),
        ],
        out_specs=pl.BlockSpec(memory_space=pl.ANY),
        scratch_shapes=[
            pltpu.VMEM((GMM_IN_SLOTS, bm, d), F32), pltpu.VMEM((GMM_OUT_SLOTS, bm, d), F32),
            pltpu.VMEM((d, ff2), BF16), pltpu.VMEM((ff, d), BF16),
            pltpu.SemaphoreType.DMA((GMM_IN_SLOTS,)), pltpu.SemaphoreType.DMA((GMM_OUT_SLOTS,)),
        ],
    )
    return pl.pallas_call(
        _gmm_kernel,
        grid_spec=grid_spec,
        out_shape=jax.ShapeDtypeStruct((n_slots, d), F32),
        compiler_params=_cparams(("arbitrary",)),
        name="moe_grouped_ffn",
    )(gb0, nblk, u0_blk, n_used, xs_a, xs_b, wgu, wd)


def _shared_expert(h2_ref, swgu_ref, swd_ref):
    ff = swd_ref.shape[0]
    gu = jnp.dot(h2_ref[...].astype(BF16), swgu_ref[...], preferred_element_type=F32)
    act = (_silu(gu[:, :ff]) * gu[:, ff:]).astype(BF16)
    return jnp.dot(act, swd_ref[...], preferred_element_type=F32)


def _combine_kernel(dest_ref, gates_ref, ys_hbm, x1_ref, h2_ref, gate_ref, lng_ref, lnb_ref,
                    swgu_ref, swd_ref, o_ref, buf, sem, *, dn_alpha):
    tc = x1_ref.shape[0]

    def issue(t, carry):
        for k in range(TOP_K):
            pltpu.make_async_copy(ys_hbm.at[pl.ds(dest_ref[k, t], 1)], buf.at[k, pl.ds(t, 1)], sem).start()
        return carry

    lax.fori_loop(0, tc, issue, 0)

    y = _shared_expert(h2_ref, swgu_ref, swd_ref)

    for k in range(TOP_K):
        pltpu.make_async_copy(ys_hbm.at[pl.ds(0, tc)], buf.at[k], sem).wait()
    for k in range(TOP_K):
        y = y + gates_ref[:, k:k + 1] * buf[k]
    o_ref[...] = _layer_norm(dn_alpha * x1_ref[...] + gate_ref[...] * y, lng_ref[...], lnb_ref[...])


def _combine_gathered_kernel(gates_ref, g_ref, x1_ref, h2_ref, gate_ref, lng_ref, lnb_ref,
                             swgu_ref, swd_ref, prev_ref, o_ref, *, dn_alpha):
    del prev_ref
    y = _shared_expert(h2_ref, swgu_ref, swd_ref)
    for k in range(TOP_K):
        y = y + gates_ref[:, k:k + 1] * g_ref[k]
    o_ref[...] = _layer_norm(dn_alpha * x1_ref[...] + gate_ref[...] * y, lng_ref[...], lnb_ref[...])


def _sc_gather_rows(table, idx):
    n = idx.shape[0]
    d = table.shape[1]
    mesh = plsc.VectorSubcoreMesh(core_axis_name="c", subcore_axis_name="s")
    n_workers = mesh.num_cores * mesh.num_subcores
    ch = SC_GATHER_ROWS
    per_w = n // n_workers
    n_ch = per_w // ch
    assert n % (n_workers * ch * 2) == 0

    @functools.partial(
        pl.kernel, mesh=mesh, out_type=jax.ShapeDtypeStruct((n, d), table.dtype),
        scratch_types=[pltpu.VMEM((ch,), I32), pltpu.VMEM((ch,), I32),
                       pltpu.VMEM((ch, d), table.dtype), pltpu.VMEM((ch, d), table.dtype),
                       pltpu.SemaphoreType.DMA, pltpu.SemaphoreType.DMA],
        name="moe_sc_gather")
    def gather(table_hbm, idx_hbm, out_hbm, idx_a, idx_b, rows_a, rows_b, sem_a, sem_b):
        wid = lax.axis_index("s") * mesh.num_cores + lax.axis_index("c")

        def chunk(i):
            return pl.ds(wid * per_w + i * ch, ch)

        def fetch(i, idx_v, rows_v, sem):
            pltpu.sync_copy(idx_hbm.at[chunk(i)], idx_v)
            pltpu.make_async_copy(table_hbm.at[idx_v], rows_v, sem).start()

        def drain(i, idx_v, rows_v, sem):
            pltpu.make_async_copy(table_hbm.at[idx_v], rows_v, sem).wait()
            pltpu.sync_copy(rows_v, out_hbm.at[chunk(i)])

        fetch(0, idx_a, rows_a, sem_a)

        @pl.loop(0, n_ch // 2)
        def _(j):
            i = 2 * j
            fetch(i + 1, idx_b, rows_b, sem_b)
            drain(i, idx_a, rows_a, sem_a)

            @pl.when(i + 2 < n_ch)
            def _():
                fetch(i + 2, idx_a, rows_a, sem_a)

            drain(i + 1, idx_b, rows_b, sem_b)

    return gather(table, idx)


def _sc_scatter_rows(rows, first_row, idx, n_out):
    n_k, n = idx.shape
    d = rows.shape[1]
    mesh = plsc.VectorSubcoreMesh(core_axis_name="c", subcore_axis_name="s")
    n_workers = mesh.num_cores * mesh.num_subcores
    ch = SC_GATHER_ROWS
    per_w = n // n_workers
    assert n % (n_workers * ch) == 0

    @functools.partial(
        pl.kernel, mesh=mesh, out_type=jax.ShapeDtypeStruct((n_out, d), rows.dtype),
        scratch_types=[pltpu.VMEM((ch,), I32), pltpu.VMEM((ch, d), rows.dtype)],
        name="moe_sc_scatter")
    def scatter(rows_hbm, idx_hbm, out_hbm, idx_v, rows_v):
        wid = lax.axis_index("s") * mesh.num_cores + lax.axis_index("c")

        @pl.loop(0, per_w // ch)
        def _(i):
            base = wid * per_w + i * ch
            pltpu.sync_copy(rows_hbm.at[pl.ds(first_row + base, ch)], rows_v)
            for k in range(n_k):
                pltpu.sync_copy(idx_hbm.at[k, pl.ds(base, ch)], idx_v)
                pltpu.sync_copy(rows_v, out_hbm.at[idx_v])

    return scatter(rows, idx)


def _combine(dest, gates_t, ys, x1, h2, mod, ln_g, ln_b, swgu, swd, n_rows, n_lat, seq, ctx_row,
             dn_alpha):
    d = x1.shape[1]
    tc = DISPATCH_TOK
    const = lambda i: (0, 0)
    n_blocks = n_rows // tc
    n_sc_blocks = (n_blocks * SC_SHARE_PERCENT) // 100
    n_tc = (n_blocks - n_sc_blocks) * tc
    n_sc = n_rows - n_tc
    gathered = _sc_gather_rows(ys, dest[:, n_tc:].reshape(TOP_K * n_sc)).reshape(TOP_K, n_sc, d)
    swgu_bf, swd_bf = swgu.astype(BF16), swd.astype(BF16)
    out_tc = pl.pallas_call(
        functools.partial(_combine_kernel, dn_alpha=dn_alpha),
        grid=(n_tc // tc,),
        in_specs=[
            pl.BlockSpec((TOP_K, tc), lambda i: (0, i), memory_space=pltpu.SMEM),
            pl.BlockSpec((tc, TOP_K), lambda i: (i, 0)),
            pl.BlockSpec(memory_space=pl.ANY),
            pl.BlockSpec((tc, d), lambda i: (i, 0)),
            pl.BlockSpec((tc, d), lambda i: (i, 0)),
            _mod_spec(5, tc, n_lat, seq, ctx_row, d),
            pl.BlockSpec((1, d), const), pl.BlockSpec((1, d), const),
            pl.BlockSpec(swgu.shape, const),
            pl.BlockSpec(swd.shape, const),
        ],
        out_specs=pl.BlockSpec((tc, d), lambda i: (i, 0)),
        out_shape=jax.ShapeDtypeStruct((n_rows, d), F32),
        scratch_shapes=[pltpu.VMEM((TOP_K, tc, d), F32), pltpu.SemaphoreType.DMA(())],
        compiler_params=_cparams(("arbitrary",)),
        name="moe_combine",
    )(dest, gates_t, ys, x1, h2, mod, ln_g.reshape(1, d), ln_b.reshape(1, d), swgu_bf, swd_bf)

    tl = COMBINE_GATHERED_TOK
    off = n_tc // tl
    row_blk = pl.BlockSpec((tl, d), lambda i: (i + off, 0))
    return pl.pallas_call(
        functools.partial(_combine_gathered_kernel, dn_alpha=dn_alpha),
        grid=(n_sc // tl,),
        in_specs=[
            pl.BlockSpec((tl, TOP_K), lambda i: (i + off, 0)),
            pl.BlockSpec((TOP_K, tl, d), lambda i: (0, i, 0)),
            row_blk, row_blk,
            _mod_spec(5, tl, n_lat, seq, ctx_row, d, blk_off=off),
            pl.BlockSpec((1, d), const), pl.BlockSpec((1, d), const),
            pl.BlockSpec(swgu.shape, const),
            pl.BlockSpec(swd.shape, const),
            pl.BlockSpec(memory_space=pl.ANY),
        ],
        out_specs=row_blk,
        out_shape=jax.ShapeDtypeStruct((n_rows, d), F32),
        input_output_aliases={9: 0},
        compiler_params=_cparams(("parallel",)),
        name="moe_combine_gathered",
    )(gates_t, gathered, x1, h2, mod, ln_g.reshape(1, d), ln_b.reshape(1, d), swgu_bf, swd_bf, out_tc)


def _split_block(n_rows):
    n_blocks = n_rows // ROW_TILE
    sc_blocks = (n_blocks * SC_DISPATCH_PERCENT) // 100 // 2 * 2
    return n_blocks - sc_blocks


def _max_slots(n_tok, bm):
    return (n_tok * TOP_K + N_EXPERTS * (bm - 1) + bm - 1) // bm * bm


def _moe(x1, h2, eidx, gates, rank, counts, mod, ln_g, ln_b, wgu, wd, layer, swgu, swd,
         n_rows, n_lat, seq, ctx_row, dn_alpha, split_blk):
    bm = MOE_BM
    t0 = split_blk * ROW_TILE
    rows_a, rows_b = _max_slots(t0, bm), _max_slots(n_rows - t0, bm)
    cnt = counts.reshape(2, N_EXPERTS).astype(I32)
    pcnt = (cnt + bm - 1) // bm * bm
    pend = jnp.cumsum(pcnt, axis=1)
    ustart = pend - pcnt + jnp.array([[0], [rows_a]], I32)
    dest = jnp.concatenate([_slots(ustart[0], eidx[:, :t0], rank[:, :t0]),
                            _slots(ustart[1], eidx[:, t0:], rank[:, t0:])], axis=1)

    nblk = (pcnt // bm).T.reshape(2 * N_EXPERTS)
    gb0 = jnp.cumsum(nblk) - nblk
    u0_grp = ustart.T.reshape(2 * N_EXPERTS)
    blk = jnp.arange((rows_a + rows_b) // bm, dtype=I32)
    grp = jnp.sum((gb0[None, :] <= blk[:, None]).astype(I32), axis=1) - 1
    u0_blk = u0_grp[grp] + (blk - gb0[grp]) * bm
    n_used = jnp.sum(nblk).astype(I32).reshape(1)

    xs_a = _dispatch(dest[:, :t0], h2, pend[0], pcnt[0], rows_a)
    xs_b = _sc_scatter_rows(h2, t0, dest[:, t0:] - rows_a, rows_b)
    ys = _grouped_ffn(xs_a, xs_b, gb0, nblk, u0_blk, n_used, wgu, wd, layer)
    return _combine(dest, gates.T, ys, x1, h2, mod, ln_g, ln_b, swgu, swd,
                    n_rows, n_lat, seq, ctx_row, dn_alpha)


def kernel(x, c, ctx, c_ctx, ada_w, ada_b, ln_g, ln_b, attn_w_qkv, attn_q_norm, attn_k_norm, attn_w_o, conv_w_in, conv_taps, conv_w_out, router_w, router_bias, exp_w_gate_up, exp_w_down, shared_w_gate_up, shared_w_down):
    b, seq, d = x.shape
    cl = ctx.shape[1]
    depth = ada_w.shape[0]
    n_lat = b * seq
    n_ctx = b * cl
    dn_alpha = (2 * depth) ** 0.25
    assert depth == 2 and b < MOD_ROWS
    assert seq % ROW_TILE == 0 and n_ctx % ROW_TILE == 0 and seq % GRID_W == 0

    cond = jnp.zeros((MOD_ROWS, d), F32).at[:b].set(c).at[b].set(c_ctx)
    mod = _modulation(cond, ada_w, ada_b)
    xall = jnp.concatenate([x.reshape(n_lat, d), ctx.reshape(n_ctx, d)], axis=0)
    n_all = n_lat + n_ctx

    q, k, v = _qkv_project(xall, mod[0], attn_w_qkv[0], attn_q_norm[0], attn_k_norm[0], n_lat, seq, b)
    o = _attention(q, k, v, b, seq, cl, n_lat)
    x1, h2, eidx, gates, rank, counts = _post_mixer(
        o, xall, mod[0], ln_g[0, 0], ln_b[0, 0], attn_w_o[0], router_w[0], router_bias[0],
        n_all, n_lat, seq, b, dn_alpha, _split_block(n_all))
    xall = _moe(x1, h2, eidx, gates, rank, counts, mod[0], ln_g[0, 1], ln_b[0, 1],
                exp_w_gate_up, exp_w_down, 0, shared_w_gate_up[0], shared_w_down[0],
                n_all, n_lat, seq, b, dn_alpha, _split_block(n_all))

    zin = _in_project(xall, mod[1], conv_w_in[0], n_lat, n_lat, seq, b)
    a = _short_conv(zin, conv_taps[0], b, seq, d)
    x1, h2, eidx, gates, rank, counts = _post_mixer(
        a, xall, mod[1], ln_g[1, 0], ln_b[1, 0], conv_w_out[0], router_w[1], router_bias[1],
        n_lat, n_lat, seq, b, dn_alpha, _split_block(n_lat))
    out = _moe(x1, h2, eidx, gates, rank, counts, mod[1], ln_g[1, 1], ln_b[1, 1],
               exp_w_gate_up, exp_w_down, 1, shared_w_gate_up[1], shared_w_down[1],
               n_lat, n_lat, seq, b, dn_alpha, _split_block(n_lat))
    return out.reshape(b, seq, d)
```

```python
import functools

import jax
import jax.numpy as jnp
from jax import lax
from jax.experimental import pallas as pl
from jax.experimental.pallas import tpu as pltpu
from jax.experimental.pallas import tpu_sc as plsc

F32 = jnp.float32
BF16 = jnp.bfloat16
I32 = jnp.int32

N_HEADS = 8
N_KV_HEADS = 2
HEAD_DIM = 128
KV_GROUP = N_HEADS // N_KV_HEADS
GRID_W = 64
ROPE_THETA = 10000.0
N_EXPERTS = 256
TOP_K = 8
N_GROUPS = 8
TOPK_GROUPS = 4
PER_GROUP = N_EXPERTS // N_GROUPS
ROUTED_SCALE = 2.5
LN_EPS = 1e-5
QK_EPS = 1e-6
N_MOD = 6
MOD_ROWS = 16

LANES = 128
SUBLANES = 8
VMEM_LIMIT = 56 * 1024 * 1024

ROW_TILE = 512
ATTN_TQ = 256
ATTN_TK = 2048
MOE_BM = 256
GMM_IN_SLOTS = 4
GMM_OUT_SLOTS = 3
DISPATCH_TOK = 512
COMBINE_GATHERED_TOK = 256
SC_GATHER_ROWS = 32
SC_SHARE_PERCENT = 65
SC_DISPATCH_PERCENT = 75
CONV_LANES = 128

HIGHEST = lax.Precision.HIGHEST
LOG2_E = 1.4426950408889634


def _cparams(sem):
    return pltpu.CompilerParams(dimension_semantics=sem, vmem_limit_bytes=VMEM_LIMIT)


def _silu(v):
    return v * jax.nn.sigmoid(v)


U32 = jnp.uint32
HI_MASK = 0xFFFF0000


def _pack_rows(v):
    half = v.shape[1] // 2
    lo = lax.bitcast_convert_type(v[:, :half].astype(BF16).astype(F32), U32)
    hi = lax.bitcast_convert_type(v[:, half:].astype(BF16).astype(F32), U32)
    return (lo >> 16) | (hi & U32(HI_MASK))


def _unpack_rows_bf16(w):
    lo = lax.bitcast_convert_type(w << 16, F32)
    hi = lax.bitcast_convert_type(w & U32(HI_MASK), F32)
    return jnp.concatenate([lo.astype(BF16), hi.astype(BF16)], axis=-1)


def _mod_kernel(c_ref, w_ref, b_ref, o_ref):
    s = _silu(c_ref[...])
    o_ref[...] = jnp.dot(s, w_ref[...], precision=HIGHEST, preferred_element_type=F32) + b_ref[...]


def _modulation(cond, ada_w, ada_b):
    depth, d, nd = ada_w.shape
    tn = 1536
    out = pl.pallas_call(
        _mod_kernel,
        grid=(depth, nd // tn),
        in_specs=[
            pl.BlockSpec((MOD_ROWS, d), lambda l, j: (0, 0)),
            pl.BlockSpec((None, d, tn), lambda l, j: (l, 0, j)),
            pl.BlockSpec((None, 1, tn), lambda l, j: (l, 0, j)),
        ],
        out_specs=pl.BlockSpec((None, MOD_ROWS, tn), lambda l, j: (l, 0, j)),
        out_shape=jax.ShapeDtypeStruct((depth, MOD_ROWS, nd), F32),
        compiler_params=_cparams(("arbitrary", "arbitrary")),
        name="adaln_modulation",
    )(cond, ada_w, ada_b.reshape(depth, 1, nd))
    return out.reshape(depth, MOD_ROWS * N_MOD, 1, d)


def _mod_spec(comp, tm, n_lat, seq, ctx_row, d, blk_off=0):
    def index(i, *_):
        row0 = (i + blk_off) * tm
        r = jnp.where(row0 < n_lat, row0 // seq, ctx_row)
        return (r * N_MOD + comp, 0, 0)

    return pl.BlockSpec((None, 1, d), index)


def _qkv_kernel(x_ref, shift_ref, scale_ref, w_ref, qg_ref, kg_ref, cos_ref, sin_ref,
                q_ref, k_ref, v_ref):
    h = (x_ref[...] * (1.0 + scale_ref[...]) + shift_ref[...]).astype(BF16)
    qkv = jnp.dot(h, w_ref[...], preferred_element_type=F32)
    cos = cos_ref[...]
    sin = sin_ref[...]
    hq = N_HEADS * HEAD_DIM
    kd = N_KV_HEADS * HEAD_DIM

    def norm_rope(t, g, post):
        t = t * lax.rsqrt(jnp.mean(t * t, axis=-1, keepdims=True) + QK_EPS) * g
        t = t * cos + pltpu.roll(t, HEAD_DIM // 2, axis=1) * sin
        return (t * post).astype(BF16)

    for hd in range(N_HEADS):
        sl = slice(hd * HEAD_DIM, (hd + 1) * HEAD_DIM)
        q_ref[:, sl] = norm_rope(qkv[:, sl], qg_ref[...], HEAD_DIM ** -0.5 * LOG2_E)
    for hd in range(N_KV_HEADS):
        sl = slice(hd * HEAD_DIM, (hd + 1) * HEAD_DIM)
        k_ref[:, sl] = norm_rope(qkv[:, hq + hd * HEAD_DIM: hq + (hd + 1) * HEAD_DIM], kg_ref[...], 1.0)
    v_ref[...] = qkv[:, hq + kd:].astype(BF16)


def _rope_tables(seq, tm):
    rows = seq // GRID_W
    row = jnp.repeat(jnp.arange(rows, dtype=F32), GRID_W)
    col = jnp.tile(jnp.arange(GRID_W, dtype=F32), rows)
    axis_dim = HEAD_DIM // 2
    freqs = ROPE_THETA ** (-jnp.arange(0, axis_dim, 2, dtype=F32) / axis_dim)
    ang = jnp.concatenate([row[:, None] * freqs, col[:, None] * freqs], axis=-1)
    cos = jnp.concatenate([jnp.cos(ang), jnp.cos(ang)], axis=-1)
    sin = jnp.concatenate([-jnp.sin(ang), jnp.sin(ang)], axis=-1)
    cos = jnp.concatenate([cos, jnp.ones((tm, HEAD_DIM), F32)], axis=0)
    sin = jnp.concatenate([sin, jnp.zeros((tm, HEAD_DIM), F32)], axis=0)
    return cos.reshape(seq // tm + 1, tm, HEAD_DIM), sin.reshape(seq // tm + 1, tm, HEAD_DIM)


def _qkv_project(xall, mod, w_qkv, q_g, k_g, n_lat, seq, ctx_row):
    t, d = xall.shape
    tm = ROW_TILE
    hq = N_HEADS * HEAD_DIM
    kd = N_KV_HEADS * HEAD_DIM
    perm = jnp.concatenate([jnp.arange(0, HEAD_DIM, 2), jnp.arange(1, HEAD_DIM, 2)])
    cols = jnp.concatenate([hd * HEAD_DIM + perm for hd in range(N_HEADS + N_KV_HEADS)]
                           + [jnp.arange(hq + kd, hq + 2 * kd)])
    w = w_qkv[:, cols].astype(BF16)
    cos, sin = _rope_tables(seq, tm)
    n_pos = seq // tm

    def pos_index(i):
        row0 = i * tm
        return (jnp.where(row0 < n_lat, (row0 % seq) // tm, n_pos), 0, 0)

    const = lambda i: (0, 0)
    return pl.pallas_call(
        _qkv_kernel,
        grid=(t // tm,),
        in_specs=[
            pl.BlockSpec((tm, d), lambda i: (i, 0)),
            _mod_spec(0, tm, n_lat, seq, ctx_row, d),
            _mod_spec(1, tm, n_lat, seq, ctx_row, d),
            pl.BlockSpec(w.shape, const),
            pl.BlockSpec((1, HEAD_DIM), const),
            pl.BlockSpec((1, HEAD_DIM), const),
            pl.BlockSpec((None, tm, HEAD_DIM), pos_index),
            pl.BlockSpec((None, tm, HEAD_DIM), pos_index),
        ],
        out_specs=[
            pl.BlockSpec((tm, hq), lambda i: (i, 0)),
            pl.BlockSpec((tm, kd), lambda i: (i, 0)),
            pl.BlockSpec((tm, kd), lambda i: (i, 0)),
        ],
        out_shape=[
            jax.ShapeDtypeStruct((t, hq), BF16),
            jax.ShapeDtypeStruct((t, kd), BF16),
            jax.ShapeDtypeStruct((t, kd), BF16),
        ],
        compiler_params=_cparams(("parallel",)),
        name="qkv_norm_rope",
    )(xall, mod, mod, w, q_g[perm].reshape(1, HEAD_DIM), k_g[perm].reshape(1, HEAD_DIM), cos, sin)


def _attn_kernel(*refs, n_lat_chunks, tk):
    if n_lat_chunks:
        q_ref, kc_ref, vc_ref, kl_ref, vl_ref, o_ref = refs
    else:
        q_ref, kc_ref, vc_ref, o_ref = refs
    tq = q_ref.shape[0]
    q = jnp.concatenate([q_ref[:, h * HEAD_DIM:(h + 1) * HEAD_DIM] for h in range(KV_GROUP)], axis=0)
    rows = KV_GROUP * tq

    def chunk(k, v, m, l, acc):
        s = lax.dot_general(q, k, (((1,), (1,)), ((), ())), preferred_element_type=F32)
        m_new = jnp.maximum(m, jnp.max(s, axis=-1, keepdims=True))
        p = jnp.exp2(s - m_new)
        a = jnp.exp2(m - m_new)
        l = a * l + jnp.sum(p, axis=-1, keepdims=True)
        acc = a * acc + jnp.dot(p.astype(BF16), v, preferred_element_type=F32)
        return m_new, l, acc

    m = jnp.full((rows, 1), -jnp.inf, F32)
    l = jnp.zeros((rows, 1), F32)
    acc = jnp.zeros((rows, HEAD_DIM), F32)
    m, l, acc = chunk(kc_ref[...], vc_ref[...], m, l, acc)
    for c in range(n_lat_chunks):
        m, l, acc = chunk(kl_ref[c * tk:(c + 1) * tk, :], vl_ref[c * tk:(c + 1) * tk, :], m, l, acc)
    o = (acc / l).astype(BF16)
    for h in range(KV_GROUP):
        o_ref[:, h * HEAD_DIM:(h + 1) * HEAD_DIM] = o[h * tq:(h + 1) * tq]


def _attention(q, k, v, b, seq, cl, n_lat):
    t = q.shape[0]
    gw = KV_GROUP * HEAD_DIM
    tq = min(ATTN_TQ, seq)
    tk = min(ATTN_TK, seq)
    nq = seq // tq
    ctx_blk0 = n_lat // cl
    hq = N_HEADS * HEAD_DIM

    ctx_kv = pl.BlockSpec((cl, HEAD_DIM), lambda bi, g, qi: (ctx_blk0 + bi, g))
    lat_kv = pl.BlockSpec((seq, HEAD_DIM), lambda bi, g, qi: (bi, g))
    lat_q = pl.BlockSpec((tq, gw), lambda bi, g, qi: (bi * nq + qi, g))
    o_lat = pl.pallas_call(
        functools.partial(_attn_kernel, n_lat_chunks=seq // tk, tk=tk),
        grid=(b, N_KV_HEADS, nq),
        in_specs=[lat_q, ctx_kv, ctx_kv, lat_kv, lat_kv],
        out_specs=lat_q,
        out_shape=jax.ShapeDtypeStruct((n_lat, hq), BF16),
        compiler_params=_cparams(("parallel", "parallel", "arbitrary")),
        name="attention_latent",
    )(q, k, v, k, v)

    ctx_q = pl.BlockSpec((cl, gw), lambda bi, g: (ctx_blk0 + bi, g))
    ctx_kv2 = pl.BlockSpec((cl, HEAD_DIM), lambda bi, g: (ctx_blk0 + bi, g))
    o_ctx = pl.pallas_call(
        functools.partial(_attn_kernel, n_lat_chunks=0, tk=tk),
        grid=(b, N_KV_HEADS),
        in_specs=[ctx_q, ctx_kv2, ctx_kv2],
        out_specs=pl.BlockSpec((cl, gw), lambda bi, g: (bi, g)),
        out_shape=jax.ShapeDtypeStruct((t - n_lat, hq), BF16),
        compiler_params=_cparams(("parallel", "parallel")),
        name="attention_context",
    )(q, k, v)
    return jnp.concatenate([o_lat, o_ctx], axis=0)


def _in_proj_kernel(x_ref, shift_ref, scale_ref, w_ref, o_ref):
    h = (x_ref[...] * (1.0 + scale_ref[...]) + shift_ref[...]).astype(BF16)
    o_ref[...] = jnp.dot(h, w_ref[...], preferred_element_type=F32)


def _in_project(xall, mod, w_in, n_rows, n_lat, seq, ctx_row):
    d = xall.shape[1]
    n_out = w_in.shape[1]
    tm = ROW_TILE
    return pl.pallas_call(
        _in_proj_kernel,
        grid=(n_rows // tm,),
        in_specs=[
            pl.BlockSpec((tm, d), lambda i: (i, 0)),
            _mod_spec(0, tm, n_lat, seq, ctx_row, d),
            _mod_spec(1, tm, n_lat, seq, ctx_row, d),
            pl.BlockSpec(w_in.shape, lambda i: (0, 0)),
        ],
        out_specs=pl.BlockSpec((tm, n_out), lambda i: (i, 0)),
        out_shape=jax.ShapeDtypeStruct((n_rows, n_out), F32),
        compiler_params=_cparams(("parallel",)),
        name="conv_in_proj",
    )(xall, mod, mod, w_in.astype(BF16))


def _conv_kernel(bg_ref, cg_ref, v_ref, taps_ref, o_ref):
    u = cg_ref[...] * v_ref[...]
    n = u.shape[0]
    pos = lax.broadcasted_iota(I32, u.shape, 0)
    prev = jnp.where(pos == 0, 0.0, pltpu.roll(u, 1, axis=0))
    nxt = jnp.where(pos == n - 1, 0.0, pltpu.roll(u, n - 1, axis=0))
    conv = prev * taps_ref[0:1, :] + u * taps_ref[1:2, :] + nxt * taps_ref[2:3, :]
    o_ref[...] = (bg_ref[...] * conv).astype(BF16)


def _short_conv(zin, taps, n_seqs, seq, d):
    tc = CONV_LANES
    nj = d // tc
    return pl.pallas_call(
        _conv_kernel,
        grid=(n_seqs, nj),
        in_specs=[
            pl.BlockSpec((seq, tc), lambda s, j: (s, j)),
            pl.BlockSpec((seq, tc), lambda s, j: (s, nj + j)),
            pl.BlockSpec((seq, tc), lambda s, j: (s, 2 * nj + j)),
            pl.BlockSpec((taps.shape[0], tc), lambda s, j: (0, j)),
        ],
        out_specs=pl.BlockSpec((seq, tc), lambda s, j: (s, j)),
        out_shape=jax.ShapeDtypeStruct((n_seqs * seq, d), BF16),
        compiler_params=_cparams(("parallel", "parallel")),
        name="short_conv",
    )(zin, zin, zin, taps)


def _layer_norm(z, g, b):
    mu = jnp.mean(z, axis=-1, keepdims=True)
    zc = z - mu
    var = jnp.mean(zc * zc, axis=-1, keepdims=True)
    return zc * lax.rsqrt(var + LN_EPS) * g + b


def _post_kernel(a_ref, x_ref, gate_ref, lng_ref, lnb_ref, shift_ref, scale_ref, w_ref,
                 rwt_ref, rb_ref, tri_ref,
                 x1_ref, h2_ref, eidx_ref, gates_ref, rank_ref, cnt_ref, rwh_ref, rwl_ref, *,
                 dn_alpha, split_blk):
    tm = a_ref.shape[0]

    @pl.when(pl.program_id(0) == 0)
    def _():
        hi = rwt_ref[...].astype(BF16)
        rwh_ref[...] = hi
        rwl_ref[...] = (rwt_ref[...] - hi.astype(F32)).astype(BF16)

    d = x_ref.shape[1]
    y = jnp.dot(a_ref[...], w_ref[...], preferred_element_type=F32)
    x1 = _layer_norm(dn_alpha * x_ref[...] + gate_ref[...] * y, lng_ref[...], lnb_ref[...])
    x1_ref[...] = x1
    h2 = x1 * (1.0 + scale_ref[...]) + shift_ref[...]
    h2_ref[...] = _pack_rows(h2)

    h_hi = h2.astype(BF16)
    h_lo = (h2 - h_hi.astype(F32)).astype(BF16)

    def nt_dot(w, h):
        return lax.dot_general(w, h, (((1,), (1,)), ((), ())), preferred_element_type=F32)

    logits = nt_dot(rwh_ref[...], h_hi) + (nt_dot(rwl_ref[...], h_hi) + nt_dot(rwh_ref[...], h_lo))
    scores = jax.nn.sigmoid(logits)
    biased = scores + rb_ref[...]
    neg = -jnp.inf
    big = jnp.int32(1 << 30)
    row = lax.broadcasted_iota(I32, (N_EXPERTS, tm), 0)

    def argmax_rows(vals, idx):
        mx = jnp.max(vals, axis=0, keepdims=True)
        return mx, jnp.min(jnp.where(vals == mx, idx, big), axis=0, keepdims=True)

    gs = []
    grp_row = lax.broadcasted_iota(I32, (PER_GROUP, tm), 0)
    for g in range(N_GROUPS):
        bg = biased[g * PER_GROUP:(g + 1) * PER_GROUP]
        ig = grp_row + g * PER_GROUP
        m1, i1 = argmax_rows(bg, ig)
        m2 = jnp.max(jnp.where(ig == i1, neg, bg), axis=0, keepdims=True)
        gs.append(m1 + m2)
    gsc = jnp.concatenate(gs, axis=0)
    grow = lax.broadcasted_iota(I32, (N_GROUPS, tm), 0)
    gsel = jnp.zeros((N_GROUPS, tm), F32)
    for _ in range(TOPK_GROUPS):
        _, gi = argmax_rows(gsc, grow)
        hit = grow == gi
        gsel = jnp.where(hit, 1.0, gsel)
        gsc = jnp.where(hit, neg, gsc)
    cur = jnp.concatenate(
        [jnp.where(gsel[g:g + 1] > 0.0, biased[g * PER_GROUP:(g + 1) * PER_GROUP], neg)
         for g in range(N_GROUPS)], axis=0)

    onehot = jnp.zeros((N_EXPERTS, tm), F32)
    idxs, gvals = [], []
    for _ in range(TOP_K):
        _, ei = argmax_rows(cur, row)
        hit = row == ei
        gvals.append(jnp.sum(jnp.where(hit, scores, 0.0), axis=0, keepdims=True))
        idxs.append(ei)
        onehot = jnp.where(hit, 1.0, onehot)
        cur = jnp.where(hit, neg, cur)
    gv = jnp.concatenate(gvals, axis=0)
    gates_ref[...] = gv / jnp.sum(gv, axis=0, keepdims=True) * ROUTED_SCALE
    eidx_ref[...] = jnp.concatenate(idxs, axis=0)

    @pl.when((pl.program_id(0) == 0) | (pl.program_id(0) == split_blk))
    def _():
        cnt_ref[...] = jnp.zeros_like(cnt_ref)

    prefix = jnp.dot(onehot.astype(BF16), tri_ref[...], preferred_element_type=F32)
    pos = prefix + cnt_ref[...]
    rank_ref[...] = jnp.concatenate(
        [jnp.sum(jnp.where(row == ei, pos, 0.0), axis=0, keepdims=True) for ei in idxs],
        axis=0).astype(I32)
    cnt_ref[...] = cnt_ref[...] + jnp.sum(onehot, axis=1, keepdims=True)


def _post_mixer(a, xall, mod, ln_g, ln_b, w, router_w, router_bias, n_rows, n_lat, seq, ctx_row,
                dn_alpha, split_blk):
    d = xall.shape[1]
    tm = ROW_TILE
    tri = (lax.broadcasted_iota(I32, (tm, tm), 0) < lax.broadcasted_iota(I32, (tm, tm), 1)).astype(BF16)
    const = lambda i: (0, 0)
    row_blk = pl.BlockSpec((tm, d), lambda i: (i, 0))
    k_blk = pl.BlockSpec((TOP_K, tm), lambda i: (0, i))
    return pl.pallas_call(
        functools.partial(_post_kernel, dn_alpha=dn_alpha, split_blk=split_blk),
        grid=(n_rows // tm,),
        in_specs=[
            row_blk, row_blk,
            _mod_spec(2, tm, n_lat, seq, ctx_row, d),
            pl.BlockSpec((1, d), const), pl.BlockSpec((1, d), const),
            _mod_spec(3, tm, n_lat, seq, ctx_row, d),
            _mod_spec(4, tm, n_lat, seq, ctx_row, d),
            pl.BlockSpec(w.shape, const),
            pl.BlockSpec((N_EXPERTS, d), const),
            pl.BlockSpec((N_EXPERTS, 1), const),
            pl.BlockSpec((tm, tm), const),
        ],
        out_specs=[
            row_blk, pl.BlockSpec((tm, d // 2), lambda i: (i, 0)),
            k_blk, k_blk, k_blk,
            pl.BlockSpec((None, N_EXPERTS, 1), lambda i: (jnp.where(i >= split_blk, 1, 0), 0, 0)),
        ],
        out_shape=[
            jax.ShapeDtypeStruct((n_rows, d), F32),
            jax.ShapeDtypeStruct((n_rows, d // 2), U32),
            jax.ShapeDtypeStruct((TOP_K, n_rows), I32),
            jax.ShapeDtypeStruct((TOP_K, n_rows), F32),
            jax.ShapeDtypeStruct((TOP_K, n_rows), I32),
            jax.ShapeDtypeStruct((2, N_EXPERTS, 1), F32),
        ],
        scratch_shapes=[pltpu.VMEM((N_EXPERTS, d), BF16), pltpu.VMEM((N_EXPERTS, d), BF16)],
        compiler_params=_cparams(("arbitrary",)),
        name="post_mixer_router",
    )(a, xall, mod, ln_g.reshape(1, d), ln_b.reshape(1, d), mod, mod, w.astype(BF16),
      router_w.T, router_bias.reshape(N_EXPERTS, 1), tri)


def _slots_kernel(pstart_ref, eidx_ref, rank_ref, dest_ref):
    e = eidx_ref[...]

    def pick(i, acc):
        return jnp.where(e == i, pstart_ref[i], acc)

    dest_ref[...] = lax.fori_loop(0, N_EXPERTS, pick, jnp.zeros_like(e)) + rank_ref[...]


def _slots(pstarts, eidx, rank):
    n_tok = eidx.shape[1]
    tn = 2048 if n_tok % 2048 == 0 else ROW_TILE
    blk = pl.BlockSpec((TOP_K, tn), lambda i, ps: (0, i))
    return pl.pallas_call(
        _slots_kernel,
        grid_spec=pltpu.PrefetchScalarGridSpec(
            num_scalar_prefetch=1, grid=(n_tok // tn,), in_specs=[blk, blk], out_specs=blk),
        out_shape=jax.ShapeDtypeStruct((TOP_K, n_tok), I32),
        compiler_params=_cparams(("arbitrary",)),
        name="moe_slots",
    )(pstarts, eidx, rank)


def _dispatch_kernel(dest_ref, h_ref, xs_hbm, sem):
    td = dest_ref.shape[1]

    def issue(t, carry):
        for k in range(TOP_K):
            pltpu.make_async_copy(h_ref.at[pl.ds(t, 1)], xs_hbm.at[pl.ds(dest_ref[k, t], 1)], sem).start()
        return carry

    lax.fori_loop(0, td, issue, 0)
    pltpu.make_async_copy(xs_hbm.at[pl.ds(0, TOP_K * td)], xs_hbm.at[pl.ds(0, TOP_K * td)], sem).wait()


def _dispatch(dest, h2, n_slots):
    n_tok = dest.shape[1]
    d = h2.shape[1]
    td = DISPATCH_TOK
    return pl.pallas_call(
        _dispatch_kernel,
        grid=(n_tok // td,),
        in_specs=[
            pl.BlockSpec((TOP_K, td), lambda i: (0, i), memory_space=pltpu.SMEM),
            pl.BlockSpec((td, d), lambda i: (i, 0)),
        ],
        out_specs=pl.BlockSpec(memory_space=pl.ANY),
        out_shape=jax.ShapeDtypeStruct((n_slots, d), h2.dtype),
        scratch_shapes=[pltpu.SemaphoreType.DMA(())],
        compiler_params=_cparams(("arbitrary",)),
        name="moe_dispatch",
    )(dest, h2)


def _gmm_kernel(gb0_ref, nblk_ref, u0_ref, n_used_ref, xa_hbm, xb_hbm, wgu_ref, wd_ref, ys_hbm,
                xbuf, ybuf, wgu_bf, wd_bf, sem_in, sem_out):
    g = pl.program_id(0)
    n_in, bm = xbuf.shape[:2]
    n_out = ybuf.shape[0]
    ff = wd_ref.shape[0]
    n_used = n_used_ref[0]
    rows_a = xa_hbm.shape[0]

    def x_copy(src_hbm, row0, b):
        rows = pl.ds(row0 if isinstance(row0, int) else pl.multiple_of(row0, bm), bm)
        return pltpu.make_async_copy(src_hbm.at[rows], xbuf.at[b % n_in], sem_in.at[b % n_in])

    def x_start(b):
        u0 = u0_ref[b]

        @pl.when(u0 < rows_a)
        def _():
            x_copy(xa_hbm, u0, b).start()

        @pl.when(u0 >= rows_a)
        def _():
            x_copy(xb_hbm, u0 - rows_a, b).start()

    def x_wait(b):
        x_copy(xa_hbm, 0, b).wait()

    def y_copy(b, slot):
        rows = pl.ds(pl.multiple_of(u0_ref[b], bm), bm)
        return pltpu.make_async_copy(ybuf.at[slot], ys_hbm.at[rows], sem_out.at[slot])

    @pl.when(g == 0)
    def _():
        for b in range(n_in - 1):
            @pl.when(b < n_used)
            def _():
                x_start(b)

    nb = nblk_ref[g]

    @pl.when(g % 2 == 0)
    def _():
        wgu_bf[...] = wgu_ref[...].astype(BF16)
        wd_bf[...] = wd_ref[...].astype(BF16)

    def block(j, carry):
        b = gb0_ref[g] + j
        slot = b % n_out
        x_wait(b)

        @pl.when(b + n_in - 1 < n_used)
        def _():
            x_start(b + n_in - 1)

        @pl.when(b >= n_out)
        def _():
            y_copy(b - n_out, slot).wait()

        gu = jnp.dot(_unpack_rows_bf16(xbuf[b % n_in]), wgu_bf[...], preferred_element_type=F32)
        act = (_silu(gu[:, :ff]) * gu[:, ff:]).astype(BF16)
        ybuf[slot] = jnp.dot(act, wd_bf[...], preferred_element_type=F32)
        y_copy(b, slot).start()
        return carry

    lax.fori_loop(0, nb, block, 0)

    @pl.when(g == pl.num_programs(0) - 1)
    def _():
        for back in range(n_out, 0, -1):
            @pl.when(n_used >= back)
            def _():
                y_copy(n_used - back, (n_used - back) % n_out).wait()


def _grouped_ffn(xs_a, xs_b, gb0, nblk, u0_blk, n_used, wgu, wd, layer):
    n_slots = xs_a.shape[0] + xs_b.shape[0]
    bm = MOE_BM
    n_exp, d, ff2 = wgu.shape[1:]
    ff = wd.shape[2]
    grid_spec = pltpu.PrefetchScalarGridSpec(
        num_scalar_prefetch=4,
        grid=(2 * n_exp,),
        in_specs=[
            pl.BlockSpec(memory_space=pl.ANY),
            pl.BlockSpec(memory_space=pl.ANY),
            pl.BlockSpec((None, None, d, ff2), lambda g, *_: (layer, g // 2, 0, 0)),
            pl.BlockSpec((None, None, ff, d), lambda g, *_: (layer, g // 2, 0, 0)),
        ],
        out_specs=pl.BlockSpec(memory_space=pl.ANY),
        scratch_shapes=[
            pltpu.VMEM((GMM_IN_SLOTS, bm, d // 2), U32), pltpu.VMEM((GMM_OUT_SLOTS, bm, d), F32),
            pltpu.VMEM((d, ff2), BF16), pltpu.VMEM((ff, d), BF16),
            pltpu.SemaphoreType.DMA((GMM_IN_SLOTS,)), pltpu.SemaphoreType.DMA((GMM_OUT_SLOTS,)),
        ],
    )
    return pl.pallas_call(
        _gmm_kernel,
        grid_spec=grid_spec,
        out_shape=jax.ShapeDtypeStruct((n_slots, d), F32),
        compiler_params=_cparams(("arbitrary",)),
        name="moe_grouped_ffn",
    )(gb0, nblk, u0_blk, n_used, xs_a, xs_b, wgu, wd)


def _shared_expert(h2_ref, swgu_ref, swd_ref):
    ff = swd_ref.shape[0]
    gu = jnp.dot(_unpack_rows_bf16(h2_ref[...]), swgu_ref[...], preferred_element_type=F32)
    act = (_silu(gu[:, :ff]) * gu[:, ff:]).astype(BF16)
    return jnp.dot(act, swd_ref[...], preferred_element_type=F32)


def _combine_kernel(dest_ref, gates_ref, ys_hbm, x1_ref, h2_ref, gate_ref, lng_ref, lnb_ref,
                    swgu_ref, swd_ref, o_ref, buf, sem, *, dn_alpha):
    tc = x1_ref.shape[0]

    def issue(t, carry):
        for k in range(TOP_K):
            pltpu.make_async_copy(ys_hbm.at[pl.ds(dest_ref[k, t], 1)], buf.at[k, pl.ds(t, 1)], sem).start()
        return carry

    lax.fori_loop(0, tc, issue, 0)

    y = _shared_expert(h2_ref, swgu_ref, swd_ref)

    for k in range(TOP_K):
        pltpu.make_async_copy(ys_hbm.at[pl.ds(0, tc)], buf.at[k], sem).wait()
    for k in range(TOP_K):
        y = y + gates_ref[:, k:k + 1] * buf[k]
    o_ref[...] = _layer_norm(dn_alpha * x1_ref[...] + gate_ref[...] * y, lng_ref[...], lnb_ref[...])


def _combine_gathered_kernel(gates_ref, g_ref, x1_ref, h2_ref, gate_ref, lng_ref, lnb_ref,
                             swgu_ref, swd_ref, prev_ref, o_ref, *, dn_alpha):
    del prev_ref
    y = _shared_expert(h2_ref, swgu_ref, swd_ref)
    for k in range(TOP_K):
        y = y + gates_ref[:, k:k + 1] * g_ref[k]
    o_ref[...] = _layer_norm(dn_alpha * x1_ref[...] + gate_ref[...] * y, lng_ref[...], lnb_ref[...])


def _sc_gather_rows(table, idx):
    n = idx.shape[0]
    d = table.shape[1]
    mesh = plsc.VectorSubcoreMesh(core_axis_name="c", subcore_axis_name="s")
    n_workers = mesh.num_cores * mesh.num_subcores
    ch = SC_GATHER_ROWS
    per_w = n // n_workers
    n_ch = per_w // ch
    assert n % (n_workers * ch * 2) == 0

    @functools.partial(
        pl.kernel, mesh=mesh, out_type=jax.ShapeDtypeStruct((n, d), table.dtype),
        scratch_types=[pltpu.VMEM((ch,), I32), pltpu.VMEM((ch,), I32),
                       pltpu.VMEM((ch, d), table.dtype), pltpu.VMEM((ch, d), table.dtype),
                       pltpu.SemaphoreType.DMA, pltpu.SemaphoreType.DMA],
        name="moe_sc_gather")
    def gather(table_hbm, idx_hbm, out_hbm, idx_a, idx_b, rows_a, rows_b, sem_a, sem_b):
        wid = lax.axis_index("s") * mesh.num_cores + lax.axis_index("c")

        def chunk(i):
            return pl.ds(wid * per_w + i * ch, ch)

        def fetch(i, idx_v, rows_v, sem):
            pltpu.sync_copy(idx_hbm.at[chunk(i)], idx_v)
            pltpu.make_async_copy(table_hbm.at[idx_v], rows_v, sem).start()

        def drain(i, idx_v, rows_v, sem):
            pltpu.make_async_copy(table_hbm.at[idx_v], rows_v, sem).wait()
            pltpu.sync_copy(rows_v, out_hbm.at[chunk(i)])

        fetch(0, idx_a, rows_a, sem_a)

        @pl.loop(0, n_ch // 2)
        def _(j):
            i = 2 * j
            fetch(i + 1, idx_b, rows_b, sem_b)
            drain(i, idx_a, rows_a, sem_a)

            @pl.when(i + 2 < n_ch)
            def _():
                fetch(i + 2, idx_a, rows_a, sem_a)

            drain(i + 1, idx_b, rows_b, sem_b)

    return gather(table, idx)


def _sc_scatter_rows(rows, first_row, idx, n_out):
    n_k, n = idx.shape
    d = rows.shape[1]
    mesh = plsc.VectorSubcoreMesh(core_axis_name="c", subcore_axis_name="s")
    n_workers = mesh.num_cores * mesh.num_subcores
    ch = SC_GATHER_ROWS
    per_w = n // n_workers
    assert n % (n_workers * ch) == 0

    @functools.partial(
        pl.kernel, mesh=mesh, out_type=jax.ShapeDtypeStruct((n_out, d), rows.dtype),
        scratch_types=[pltpu.VMEM((ch,), I32), pltpu.VMEM((ch, d), rows.dtype)],
        name="moe_sc_scatter")
    def scatter(rows_hbm, idx_hbm, out_hbm, idx_v, rows_v):
        wid = lax.axis_index("s") * mesh.num_cores + lax.axis_index("c")

        @pl.loop(0, per_w // ch)
        def _(i):
            base = wid * per_w + i * ch
            pltpu.sync_copy(rows_hbm.at[pl.ds(first_row + base, ch)], rows_v)
            for k in range(n_k):
                pltpu.sync_copy(idx_hbm.at[k, pl.ds(base, ch)], idx_v)
                pltpu.sync_copy(rows_v, out_hbm.at[idx_v])

    return scatter(rows, idx)


def _combine(dest, gates_t, ys, x1, h2, mod, ln_g, ln_b, swgu, swd, n_rows, n_lat, seq, ctx_row,
             dn_alpha):
    d = x1.shape[1]
    tc = DISPATCH_TOK
    const = lambda i: (0, 0)
    n_blocks = n_rows // tc
    n_sc_blocks = (n_blocks * SC_SHARE_PERCENT) // 100
    n_tc = (n_blocks - n_sc_blocks) * tc
    n_sc = n_rows - n_tc
    gathered = _sc_gather_rows(ys, dest[:, n_tc:].reshape(TOP_K * n_sc)).reshape(TOP_K, n_sc, d)
    swgu_bf, swd_bf = swgu.astype(BF16), swd.astype(BF16)
    out_tc = pl.pallas_call(
        functools.partial(_combine_kernel, dn_alpha=dn_alpha),
        grid=(n_tc // tc,),
        in_specs=[
            pl.BlockSpec((TOP_K, tc), lambda i: (0, i), memory_space=pltpu.SMEM),
            pl.BlockSpec((tc, TOP_K), lambda i: (i, 0)),
            pl.BlockSpec(memory_space=pl.ANY),
            pl.BlockSpec((tc, d), lambda i: (i, 0)),
            pl.BlockSpec((tc, d // 2), lambda i: (i, 0)),
            _mod_spec(5, tc, n_lat, seq, ctx_row, d),
            pl.BlockSpec((1, d), const), pl.BlockSpec((1, d), const),
            pl.BlockSpec(swgu.shape, const),
            pl.BlockSpec(swd.shape, const),
        ],
        out_specs=pl.BlockSpec((tc, d), lambda i: (i, 0)),
        out_shape=jax.ShapeDtypeStruct((n_rows, d), F32),
        scratch_shapes=[pltpu.VMEM((TOP_K, tc, d), F32), pltpu.SemaphoreType.DMA(())],
        compiler_params=_cparams(("arbitrary",)),
        name="moe_combine",
    )(dest, gates_t, ys, x1, h2, mod, ln_g.reshape(1, d), ln_b.reshape(1, d), swgu_bf, swd_bf)

    tl = COMBINE_GATHERED_TOK
    off = n_tc // tl
    row_blk = pl.BlockSpec((tl, d), lambda i: (i + off, 0))
    return pl.pallas_call(
        functools.partial(_combine_gathered_kernel, dn_alpha=dn_alpha),
        grid=(n_sc // tl,),
        in_specs=[
            pl.BlockSpec((tl, TOP_K), lambda i: (i + off, 0)),
            pl.BlockSpec((TOP_K, tl, d), lambda i: (0, i, 0)),
            row_blk, pl.BlockSpec((tl, d // 2), lambda i: (i + off, 0)),
            _mod_spec(5, tl, n_lat, seq, ctx_row, d, blk_off=off),
            pl.BlockSpec((1, d), const), pl.BlockSpec((1, d), const),
            pl.BlockSpec(swgu.shape, const),
            pl.BlockSpec(swd.shape, const),
            pl.BlockSpec(memory_space=pl.ANY),
        ],
        out_specs=row_blk,
        out_shape=jax.ShapeDtypeStruct((n_rows, d), F32),
        input_output_aliases={9: 0},
        compiler_params=_cparams(("parallel",)),
        name="moe_combine_gathered",
    )(gates_t, gathered, x1, h2, mod, ln_g.reshape(1, d), ln_b.reshape(1, d), swgu_bf, swd_bf, out_tc)


def _split_block(n_rows):
    n_blocks = n_rows // ROW_TILE
    sc_blocks = (n_blocks * SC_DISPATCH_PERCENT) // 100 // 2 * 2
    return n_blocks - sc_blocks


def _max_slots(n_tok, bm):
    return (n_tok * TOP_K + N_EXPERTS * (bm - 1) + bm - 1) // bm * bm


def _moe(x1, h2, eidx, gates, rank, counts, mod, ln_g, ln_b, wgu, wd, layer, swgu, swd,
         n_rows, n_lat, seq, ctx_row, dn_alpha, split_blk):
    bm = MOE_BM
    t0 = split_blk * ROW_TILE
    rows_a, rows_b = _max_slots(t0, bm), _max_slots(n_rows - t0, bm)
    cnt = counts.reshape(2, N_EXPERTS).astype(I32)
    pcnt = (cnt + bm - 1) // bm * bm
    pend = jnp.cumsum(pcnt, axis=1)
    ustart = pend - pcnt + jnp.array([[0], [rows_a]], I32)
    dest = jnp.concatenate([_slots(ustart[0], eidx[:, :t0], rank[:, :t0]),
                            _slots(ustart[1], eidx[:, t0:], rank[:, t0:])], axis=1)

    nblk = (pcnt // bm).T.reshape(2 * N_EXPERTS)
    gb0 = jnp.cumsum(nblk) - nblk
    u0_grp = ustart.T.reshape(2 * N_EXPERTS)
    blk = jnp.arange((rows_a + rows_b) // bm, dtype=I32)
    grp = jnp.sum((gb0[None, :] <= blk[:, None]).astype(I32), axis=1) - 1
    u0_blk = u0_grp[grp] + (blk - gb0[grp]) * bm
    n_used = jnp.sum(nblk).astype(I32).reshape(1)

    xs_a = _dispatch(dest[:, :t0], h2, rows_a)
    xs_b = _sc_scatter_rows(h2, t0, dest[:, t0:] - rows_a, rows_b)
    ys = _grouped_ffn(xs_a, xs_b, gb0, nblk, u0_blk, n_used, wgu, wd, layer)
    return _combine(dest, gates.T, ys, x1, h2, mod, ln_g, ln_b, swgu, swd,
                    n_rows, n_lat, seq, ctx_row, dn_alpha)


def kernel(x, c, ctx, c_ctx, ada_w, ada_b, ln_g, ln_b, attn_w_qkv, attn_q_norm, attn_k_norm, attn_w_o, conv_w_in, conv_taps, conv_w_out, router_w, router_bias, exp_w_gate_up, exp_w_down, shared_w_gate_up, shared_w_down):
    b, seq, d = x.shape
    cl = ctx.shape[1]
    depth = ada_w.shape[0]
    n_lat = b * seq
    n_ctx = b * cl
    dn_alpha = (2 * depth) ** 0.25
    assert depth == 2 and b < MOD_ROWS
    assert seq % ROW_TILE == 0 and n_ctx % ROW_TILE == 0 and seq % GRID_W == 0

    cond = jnp.zeros((MOD_ROWS, d), F32).at[:b].set(c).at[b].set(c_ctx)
    mod = _modulation(cond, ada_w, ada_b)
    xall = jnp.concatenate([x.reshape(n_lat, d), ctx.reshape(n_ctx, d)], axis=0)
    n_all = n_lat + n_ctx

    q, k, v = _qkv_project(xall, mod[0], attn_w_qkv[0], attn_q_norm[0], attn_k_norm[0], n_lat, seq, b)
    o = _attention(q, k, v, b, seq, cl, n_lat)
    x1, h2, eidx, gates, rank, counts = _post_mixer(
        o, xall, mod[0], ln_g[0, 0], ln_b[0, 0], attn_w_o[0], router_w[0], router_bias[0],
        n_all, n_lat, seq, b, dn_alpha, _split_block(n_all))
    xall = _moe(x1, h2, eidx, gates, rank, counts, mod[0], ln_g[0, 1], ln_b[0, 1],
                exp_w_gate_up, exp_w_down, 0, shared_w_gate_up[0], shared_w_down[0],
                n_all, n_lat, seq, b, dn_alpha, _split_block(n_all))

    zin = _in_project(xall, mod[1], conv_w_in[0], n_lat, n_lat, seq, b)
    a = _short_conv(zin, conv_taps[0], b, seq, d)
    x1, h2, eidx, gates, rank, counts = _post_mixer(
        a, xall, mod[1], ln_g[1, 0], ln_b[1, 0], conv_w_out[0], router_w[1], router_bias[1],
        n_lat, n_lat, seq, b, dn_alpha, _split_block(n_lat))
    out = _moe(x1, h2, eidx, gates, rank, counts, mod[1], ln_g[1, 1], ln_b[1, 1],
               exp_w_gate_up, exp_w_down, 1, shared_w_gate_up[1], shared_w_down[1],
               n_lat, n_lat, seq, b, dn_alpha, _split_block(n_lat))
    return out.reshape(b, seq, d)
```

```python
import functools

import jax
import jax.numpy as jnp
from jax import lax
from jax.experimental import pallas as pl
from jax.experimental.pallas import tpu as pltpu
from jax.experimental.pallas import tpu_sc as plsc

F32 = jnp.float32
BF16 = jnp.bfloat16
I32 = jnp.int32

N_HEADS = 8
N_KV_HEADS = 2
HEAD_DIM = 128
KV_GROUP = N_HEADS // N_KV_HEADS
GRID_W = 64
ROPE_THETA = 10000.0
N_EXPERTS = 256
TOP_K = 8
N_GROUPS = 8
TOPK_GROUPS = 4
PER_GROUP = N_EXPERTS // N_GROUPS
ROUTED_SCALE = 2.5
LN_EPS = 1e-5
QK_EPS = 1e-6
N_MOD = 6
MOD_ROWS = 16

LANES = 128
SUBLANES = 8
VMEM_LIMIT = 56 * 1024 * 1024

ROW_TILE = 512
ATTN_TQ = 256
ATTN_TK = 2048
MOE_BM = 256
GMM_IN_SLOTS = 4
GMM_OUT_SLOTS = 3
DISPATCH_TOK = 512
COMBINE_GATHERED_TOK = 256
SC_GATHER_ROWS = 32
SC_SHARE_PERCENT = 78
SC_DISPATCH_PERCENT = 75
CONV_LANES = 128

HIGHEST = lax.Precision.HIGHEST
LOG2_E = 1.4426950408889634


def _cparams(sem):
    return pltpu.CompilerParams(dimension_semantics=sem, vmem_limit_bytes=VMEM_LIMIT)


def _silu(v):
    return v * jax.nn.sigmoid(v)


U32 = jnp.uint32
HI_MASK = 0xFFFF0000


def _pack_rows(v):
    half = v.shape[1] // 2
    lo = lax.bitcast_convert_type(v[:, :half].astype(BF16).astype(F32), U32)
    hi = lax.bitcast_convert_type(v[:, half:].astype(BF16).astype(F32), U32)
    return (lo >> 16) | (hi & U32(HI_MASK))


def _add_weighted_rows(y, gate, packed):
    half = y.shape[1] // 2
    lo = lax.bitcast_convert_type(packed << 16, F32)
    hi = lax.bitcast_convert_type(packed & U32(HI_MASK), F32)
    return jnp.concatenate([y[:, :half] + gate * lo, y[:, half:] + gate * hi], axis=-1)


def _unpack_rows_bf16(w):
    lo = lax.bitcast_convert_type(w << 16, F32)
    hi = lax.bitcast_convert_type(w & U32(HI_MASK), F32)
    return jnp.concatenate([lo.astype(BF16), hi.astype(BF16)], axis=-1)


def _mod_kernel(c_ref, w_ref, b_ref, o_ref):
    s = _silu(c_ref[...])
    o_ref[...] = jnp.dot(s, w_ref[...], precision=HIGHEST, preferred_element_type=F32) + b_ref[...]


def _modulation(cond, ada_w, ada_b):
    depth, d, nd = ada_w.shape
    tn = 1536
    out = pl.pallas_call(
        _mod_kernel,
        grid=(depth, nd // tn),
        in_specs=[
            pl.BlockSpec((MOD_ROWS, d), lambda l, j: (0, 0)),
            pl.BlockSpec((None, d, tn), lambda l, j: (l, 0, j)),
            pl.BlockSpec((None, 1, tn), lambda l, j: (l, 0, j)),
        ],
        out_specs=pl.BlockSpec((None, MOD_ROWS, tn), lambda l, j: (l, 0, j)),
        out_shape=jax.ShapeDtypeStruct((depth, MOD_ROWS, nd), F32),
        compiler_params=_cparams(("arbitrary", "arbitrary")),
        name="adaln_modulation",
    )(cond, ada_w, ada_b.reshape(depth, 1, nd))
    return out.reshape(depth, MOD_ROWS * N_MOD, 1, d)


def _mod_spec(comp, tm, n_lat, seq, ctx_row, d, blk_off=0):
    def index(i, *_):
        row0 = (i + blk_off) * tm
        r = jnp.where(row0 < n_lat, row0 // seq, ctx_row)
        return (r * N_MOD + comp, 0, 0)

    return pl.BlockSpec((None, 1, d), index)


def _qkv_kernel(x_ref, shift_ref, scale_ref, w_ref, qg_ref, kg_ref, cos_ref, sin_ref,
                q_ref, k_ref, v_ref):
    h = (x_ref[...] * (1.0 + scale_ref[...]) + shift_ref[...]).astype(BF16)
    qkv = jnp.dot(h, w_ref[...], preferred_element_type=F32)
    cos = cos_ref[...]
    sin = sin_ref[...]
    hq = N_HEADS * HEAD_DIM
    kd = N_KV_HEADS * HEAD_DIM

    def norm_rope(t, g, post):
        t = t * lax.rsqrt(jnp.mean(t * t, axis=-1, keepdims=True) + QK_EPS) * g
        t = t * cos + pltpu.roll(t, HEAD_DIM // 2, axis=1) * sin
        return (t * post).astype(BF16)

    for hd in range(N_HEADS):
        sl = slice(hd * HEAD_DIM, (hd + 1) * HEAD_DIM)
        q_ref[:, sl] = norm_rope(qkv[:, sl], qg_ref[...], HEAD_DIM ** -0.5 * LOG2_E)
    for hd in range(N_KV_HEADS):
        sl = slice(hd * HEAD_DIM, (hd + 1) * HEAD_DIM)
        k_ref[:, sl] = norm_rope(qkv[:, hq + hd * HEAD_DIM: hq + (hd + 1) * HEAD_DIM], kg_ref[...], 1.0)
    v_ref[...] = qkv[:, hq + kd:].astype(BF16)


def _rope_tables(seq, tm):
    rows = seq // GRID_W
    row = jnp.repeat(jnp.arange(rows, dtype=F32), GRID_W)
    col = jnp.tile(jnp.arange(GRID_W, dtype=F32), rows)
    axis_dim = HEAD_DIM // 2
    freqs = ROPE_THETA ** (-jnp.arange(0, axis_dim, 2, dtype=F32) / axis_dim)
    ang = jnp.concatenate([row[:, None] * freqs, col[:, None] * freqs], axis=-1)
    cos = jnp.concatenate([jnp.cos(ang), jnp.cos(ang)], axis=-1)
    sin = jnp.concatenate([-jnp.sin(ang), jnp.sin(ang)], axis=-1)
    cos = jnp.concatenate([cos, jnp.ones((tm, HEAD_DIM), F32)], axis=0)
    sin = jnp.concatenate([sin, jnp.zeros((tm, HEAD_DIM), F32)], axis=0)
    return cos.reshape(seq // tm + 1, tm, HEAD_DIM), sin.reshape(seq // tm + 1, tm, HEAD_DIM)


def _qkv_project(xall, mod, w_qkv, q_g, k_g, n_lat, seq, ctx_row):
    t, d = xall.shape
    tm = ROW_TILE
    hq = N_HEADS * HEAD_DIM
    kd = N_KV_HEADS * HEAD_DIM
    perm = jnp.concatenate([jnp.arange(0, HEAD_DIM, 2), jnp.arange(1, HEAD_DIM, 2)])
    cols = jnp.concatenate([hd * HEAD_DIM + perm for hd in range(N_HEADS + N_KV_HEADS)]
                           + [jnp.arange(hq + kd, hq + 2 * kd)])
    w = w_qkv[:, cols].astype(BF16)
    cos, sin = _rope_tables(seq, tm)
    n_pos = seq // tm

    def pos_index(i):
        row0 = i * tm
        return (jnp.where(row0 < n_lat, (row0 % seq) // tm, n_pos), 0, 0)

    const = lambda i: (0, 0)
    return pl.pallas_call(
        _qkv_kernel,
        grid=(t // tm,),
        in_specs=[
            pl.BlockSpec((tm, d), lambda i: (i, 0)),
            _mod_spec(0, tm, n_lat, seq, ctx_row, d),
            _mod_spec(1, tm, n_lat, seq, ctx_row, d),
            pl.BlockSpec(w.shape, const),
            pl.BlockSpec((1, HEAD_DIM), const),
            pl.BlockSpec((1, HEAD_DIM), const),
            pl.BlockSpec((None, tm, HEAD_DIM), pos_index),
            pl.BlockSpec((None, tm, HEAD_DIM), pos_index),
        ],
        out_specs=[
            pl.BlockSpec((tm, hq), lambda i: (i, 0)),
            pl.BlockSpec((tm, kd), lambda i: (i, 0)),
            pl.BlockSpec((tm, kd), lambda i: (i, 0)),
        ],
        out_shape=[
            jax.ShapeDtypeStruct((t, hq), BF16),
            jax.ShapeDtypeStruct((t, kd), BF16),
            jax.ShapeDtypeStruct((t, kd), BF16),
        ],
        compiler_params=_cparams(("parallel",)),
        name="qkv_norm_rope",
    )(xall, mod, mod, w, q_g[perm].reshape(1, HEAD_DIM), k_g[perm].reshape(1, HEAD_DIM), cos, sin)


def _attn_kernel(*refs, n_lat_chunks, tk):
    if n_lat_chunks:
        q_ref, kc_ref, vc_ref, kl_ref, vl_ref, o_ref = refs
    else:
        q_ref, kc_ref, vc_ref, o_ref = refs
    tq = q_ref.shape[0]
    q = jnp.concatenate([q_ref[:, h * HEAD_DIM:(h + 1) * HEAD_DIM] for h in range(KV_GROUP)], axis=0)
    rows = KV_GROUP * tq

    def chunk(k, v, m, l, acc):
        s = lax.dot_general(q, k, (((1,), (1,)), ((), ())), preferred_element_type=F32)
        m_new = jnp.maximum(m, jnp.max(s, axis=-1, keepdims=True))
        p = jnp.exp2(s - m_new)
        a = jnp.exp2(m - m_new)
        l = a * l + jnp.sum(p, axis=-1, keepdims=True)
        acc = a * acc + jnp.dot(p.astype(BF16), v, preferred_element_type=F32)
        return m_new, l, acc

    m = jnp.full((rows, 1), -jnp.inf, F32)
    l = jnp.zeros((rows, 1), F32)
    acc = jnp.zeros((rows, HEAD_DIM), F32)
    m, l, acc = chunk(kc_ref[...], vc_ref[...], m, l, acc)
    for c in range(n_lat_chunks):
        m, l, acc = chunk(kl_ref[c * tk:(c + 1) * tk, :], vl_ref[c * tk:(c + 1) * tk, :], m, l, acc)
    o = (acc / l).astype(BF16)
    for h in range(KV_GROUP):
        o_ref[:, h * HEAD_DIM:(h + 1) * HEAD_DIM] = o[h * tq:(h + 1) * tq]


def _attention(q, k, v, b, seq, cl, n_lat):
    t = q.shape[0]
    gw = KV_GROUP * HEAD_DIM
    tq = min(ATTN_TQ, seq)
    tk = min(ATTN_TK, seq)
    nq = seq // tq
    ctx_blk0 = n_lat // cl
    hq = N_HEADS * HEAD_DIM

    ctx_kv = pl.BlockSpec((cl, HEAD_DIM), lambda bi, g, qi: (ctx_blk0 + bi, g))
    lat_kv = pl.BlockSpec((seq, HEAD_DIM), lambda bi, g, qi: (bi, g))
    lat_q = pl.BlockSpec((tq, gw), lambda bi, g, qi: (bi * nq + qi, g))
    o_lat = pl.pallas_call(
        functools.partial(_attn_kernel, n_lat_chunks=seq // tk, tk=tk),
        grid=(b, N_KV_HEADS, nq),
        in_specs=[lat_q, ctx_kv, ctx_kv, lat_kv, lat_kv],
        out_specs=lat_q,
        out_shape=jax.ShapeDtypeStruct((n_lat, hq), BF16),
        compiler_params=_cparams(("parallel", "parallel", "arbitrary")),
        name="attention_latent",
    )(q, k, v, k, v)

    ctx_q = pl.BlockSpec((cl, gw), lambda bi, g: (ctx_blk0 + bi, g))
    ctx_kv2 = pl.BlockSpec((cl, HEAD_DIM), lambda bi, g: (ctx_blk0 + bi, g))
    o_ctx = pl.pallas_call(
        functools.partial(_attn_kernel, n_lat_chunks=0, tk=tk),
        grid=(b, N_KV_HEADS),
        in_specs=[ctx_q, ctx_kv2, ctx_kv2],
        out_specs=pl.BlockSpec((cl, gw), lambda bi, g: (bi, g)),
        out_shape=jax.ShapeDtypeStruct((t - n_lat, hq), BF16),
        compiler_params=_cparams(("parallel", "parallel")),
        name="attention_context",
    )(q, k, v)
    return jnp.concatenate([o_lat, o_ctx], axis=0)


def _in_proj_kernel(x_ref, shift_ref, scale_ref, w_ref, o_ref):
    h = (x_ref[...] * (1.0 + scale_ref[...]) + shift_ref[...]).astype(BF16)
    o_ref[...] = jnp.dot(h, w_ref[...], preferred_element_type=F32)


def _in_project(xall, mod, w_in, n_rows, n_lat, seq, ctx_row):
    d = xall.shape[1]
    n_out = w_in.shape[1]
    tm = ROW_TILE
    return pl.pallas_call(
        _in_proj_kernel,
        grid=(n_rows // tm,),
        in_specs=[
            pl.BlockSpec((tm, d), lambda i: (i, 0)),
            _mod_spec(0, tm, n_lat, seq, ctx_row, d),
            _mod_spec(1, tm, n_lat, seq, ctx_row, d),
            pl.BlockSpec(w_in.shape, lambda i: (0, 0)),
        ],
        out_specs=pl.BlockSpec((tm, n_out), lambda i: (i, 0)),
        out_shape=jax.ShapeDtypeStruct((n_rows, n_out), F32),
        compiler_params=_cparams(("parallel",)),
        name="conv_in_proj",
    )(xall, mod, mod, w_in.astype(BF16))


def _conv_kernel(bg_ref, cg_ref, v_ref, taps_ref, o_ref):
    u = cg_ref[...] * v_ref[...]
    n = u.shape[0]
    pos = lax.broadcasted_iota(I32, u.shape, 0)
    prev = jnp.where(pos == 0, 0.0, pltpu.roll(u, 1, axis=0))
    nxt = jnp.where(pos == n - 1, 0.0, pltpu.roll(u, n - 1, axis=0))
    conv = prev * taps_ref[0:1, :] + u * taps_ref[1:2, :] + nxt * taps_ref[2:3, :]
    o_ref[...] = (bg_ref[...] * conv).astype(BF16)


def _short_conv(zin, taps, n_seqs, seq, d):
    tc = CONV_LANES
    nj = d // tc
    return pl.pallas_call(
        _conv_kernel,
        grid=(n_seqs, nj),
        in_specs=[
            pl.BlockSpec((seq, tc), lambda s, j: (s, j)),
            pl.BlockSpec((seq, tc), lambda s, j: (s, nj + j)),
            pl.BlockSpec((seq, tc), lambda s, j: (s, 2 * nj + j)),
            pl.BlockSpec((taps.shape[0], tc), lambda s, j: (0, j)),
        ],
        out_specs=pl.BlockSpec((seq, tc), lambda s, j: (s, j)),
        out_shape=jax.ShapeDtypeStruct((n_seqs * seq, d), BF16),
        compiler_params=_cparams(("parallel", "parallel")),
        name="short_conv",
    )(zin, zin, zin, taps)


def _layer_norm(z, g, b):
    mu = jnp.mean(z, axis=-1, keepdims=True)
    zc = z - mu
    var = jnp.mean(zc * zc, axis=-1, keepdims=True)
    return zc * lax.rsqrt(var + LN_EPS) * g + b


def _post_kernel(a_ref, x_ref, gate_ref, lng_ref, lnb_ref, shift_ref, scale_ref, w_ref,
                 rwt_ref, rb_ref, tri_ref,
                 x1_ref, h2_ref, eidx_ref, gates_ref, rank_ref, cnt_ref, rwh_ref, rwl_ref, *,
                 dn_alpha, split_blk):
    tm = a_ref.shape[0]

    @pl.when(pl.program_id(0) == 0)
    def _():
        hi = rwt_ref[...].astype(BF16)
        rwh_ref[...] = hi
        rwl_ref[...] = (rwt_ref[...] - hi.astype(F32)).astype(BF16)

    d = x_ref.shape[1]
    y = jnp.dot(a_ref[...], w_ref[...], preferred_element_type=F32)
    x1 = _layer_norm(dn_alpha * x_ref[...] + gate_ref[...] * y, lng_ref[...], lnb_ref[...])
    x1_ref[...] = x1
    h2 = x1 * (1.0 + scale_ref[...]) + shift_ref[...]
    h2_ref[...] = _pack_rows(h2)

    h_hi = h2.astype(BF16)
    h_lo = (h2 - h_hi.astype(F32)).astype(BF16)

    def nt_dot(w, h):
        return lax.dot_general(w, h, (((1,), (1,)), ((), ())), preferred_element_type=F32)

    logits = nt_dot(rwh_ref[...], h_hi) + (nt_dot(rwl_ref[...], h_hi) + nt_dot(rwh_ref[...], h_lo))
    scores = jax.nn.sigmoid(logits)
    biased = scores + rb_ref[...]
    neg = -jnp.inf
    big = jnp.int32(1 << 30)
    row = lax.broadcasted_iota(I32, (N_EXPERTS, tm), 0)

    def argmax_rows(vals, idx):
        mx = jnp.max(vals, axis=0, keepdims=True)
        return mx, jnp.min(jnp.where(vals == mx, idx, big), axis=0, keepdims=True)

    gs = []
    grp_row = lax.broadcasted_iota(I32, (PER_GROUP, tm), 0)
    for g in range(N_GROUPS):
        bg = biased[g * PER_GROUP:(g + 1) * PER_GROUP]
        ig = grp_row + g * PER_GROUP
        m1, i1 = argmax_rows(bg, ig)
        m2 = jnp.max(jnp.where(ig == i1, neg, bg), axis=0, keepdims=True)
        gs.append(m1 + m2)
    gsc = jnp.concatenate(gs, axis=0)
    grow = lax.broadcasted_iota(I32, (N_GROUPS, tm), 0)
    gsel = jnp.zeros((N_GROUPS, tm), F32)
    for _ in range(TOPK_GROUPS):
        _, gi = argmax_rows(gsc, grow)
        hit = grow == gi
        gsel = jnp.where(hit, 1.0, gsel)
        gsc = jnp.where(hit, neg, gsc)
    cur = jnp.concatenate(
        [jnp.where(gsel[g:g + 1] > 0.0, biased[g * PER_GROUP:(g + 1) * PER_GROUP], neg)
         for g in range(N_GROUPS)], axis=0)

    onehot = jnp.zeros((N_EXPERTS, tm), F32)
    idxs, gvals = [], []
    for _ in range(TOP_K):
        _, ei = argmax_rows(cur, row)
        hit = row == ei
        gvals.append(jnp.sum(jnp.where(hit, scores, 0.0), axis=0, keepdims=True))
        idxs.append(ei)
        onehot = jnp.where(hit, 1.0, onehot)
        cur = jnp.where(hit, neg, cur)
    gv = jnp.concatenate(gvals, axis=0)
    gates_ref[...] = gv / jnp.sum(gv, axis=0, keepdims=True) * ROUTED_SCALE
    eidx_ref[...] = jnp.concatenate(idxs, axis=0)

    @pl.when((pl.program_id(0) == 0) | (pl.program_id(0) == split_blk))
    def _():
        cnt_ref[...] = jnp.zeros_like(cnt_ref)

    prefix = jnp.dot(onehot.astype(BF16), tri_ref[...], preferred_element_type=F32)
    pos = prefix + cnt_ref[...]
    rank_ref[...] = jnp.concatenate(
        [jnp.sum(jnp.where(row == ei, pos, 0.0), axis=0, keepdims=True) for ei in idxs],
        axis=0).astype(I32)
    cnt_ref[...] = cnt_ref[...] + jnp.sum(onehot, axis=1, keepdims=True)


def _post_mixer(a, xall, mod, ln_g, ln_b, w, router_w, router_bias, n_rows, n_lat, seq, ctx_row,
                dn_alpha, split_blk):
    d = xall.shape[1]
    tm = ROW_TILE
    tri = (lax.broadcasted_iota(I32, (tm, tm), 0) < lax.broadcasted_iota(I32, (tm, tm), 1)).astype(BF16)
    const = lambda i: (0, 0)
    row_blk = pl.BlockSpec((tm, d), lambda i: (i, 0))
    k_blk = pl.BlockSpec((TOP_K, tm), lambda i: (0, i))
    return pl.pallas_call(
        functools.partial(_post_kernel, dn_alpha=dn_alpha, split_blk=split_blk),
        grid=(n_rows // tm,),
        in_specs=[
            row_blk, row_blk,
            _mod_spec(2, tm, n_lat, seq, ctx_row, d),
            pl.BlockSpec((1, d), const), pl.BlockSpec((1, d), const),
            _mod_spec(3, tm, n_lat, seq, ctx_row, d),
            _mod_spec(4, tm, n_lat, seq, ctx_row, d),
            pl.BlockSpec(w.shape, const),
            pl.BlockSpec((N_EXPERTS, d), const),
            pl.BlockSpec((N_EXPERTS, 1), const),
            pl.BlockSpec((tm, tm), const),
        ],
        out_specs=[
            row_blk, pl.BlockSpec((tm, d // 2), lambda i: (i, 0)),
            k_blk, k_blk, k_blk,
            pl.BlockSpec((None, N_EXPERTS, 1), lambda i: (jnp.where(i >= split_blk, 1, 0), 0, 0)),
        ],
        out_shape=[
            jax.ShapeDtypeStruct((n_rows, d), F32),
            jax.ShapeDtypeStruct((n_rows, d // 2), U32),
            jax.ShapeDtypeStruct((TOP_K, n_rows), I32),
            jax.ShapeDtypeStruct((TOP_K, n_rows), F32),
            jax.ShapeDtypeStruct((TOP_K, n_rows), I32),
            jax.ShapeDtypeStruct((2, N_EXPERTS, 1), F32),
        ],
        scratch_shapes=[pltpu.VMEM((N_EXPERTS, d), BF16), pltpu.VMEM((N_EXPERTS, d), BF16)],
        compiler_params=_cparams(("arbitrary",)),
        name="post_mixer_router",
    )(a, xall, mod, ln_g.reshape(1, d), ln_b.reshape(1, d), mod, mod, w.astype(BF16),
      router_w.T, router_bias.reshape(N_EXPERTS, 1), tri)


def _slots_kernel(pstart_ref, eidx_ref, rank_ref, dest_ref):
    e = eidx_ref[...]

    def pick(i, acc):
        return jnp.where(e == i, pstart_ref[i], acc)

    dest_ref[...] = lax.fori_loop(0, N_EXPERTS, pick, jnp.zeros_like(e)) + rank_ref[...]


def _slots(pstarts, eidx, rank):
    n_tok = eidx.shape[1]
    tn = 2048 if n_tok % 2048 == 0 else ROW_TILE
    blk = pl.BlockSpec((TOP_K, tn), lambda i, ps: (0, i))
    return pl.pallas_call(
        _slots_kernel,
        grid_spec=pltpu.PrefetchScalarGridSpec(
            num_scalar_prefetch=1, grid=(n_tok // tn,), in_specs=[blk, blk], out_specs=blk),
        out_shape=jax.ShapeDtypeStruct((TOP_K, n_tok), I32),
        compiler_params=_cparams(("arbitrary",)),
        name="moe_slots",
    )(pstarts, eidx, rank)


def _dispatch_kernel(dest_ref, h_ref, xs_hbm, sem):
    td = dest_ref.shape[1]

    def issue(t, carry):
        for k in range(TOP_K):
            pltpu.make_async_copy(h_ref.at[pl.ds(t, 1)], xs_hbm.at[pl.ds(dest_ref[k, t], 1)], sem).start()
        return carry

    lax.fori_loop(0, td, issue, 0)
    pltpu.make_async_copy(xs_hbm.at[pl.ds(0, TOP_K * td)], xs_hbm.at[pl.ds(0, TOP_K * td)], sem).wait()


def _dispatch(dest, h2, n_slots):
    n_tok = dest.shape[1]
    d = h2.shape[1]
    td = DISPATCH_TOK
    return pl.pallas_call(
        _dispatch_kernel,
        grid=(n_tok // td,),
        in_specs=[
            pl.BlockSpec((TOP_K, td), lambda i: (0, i), memory_space=pltpu.SMEM),
            pl.BlockSpec((td, d), lambda i: (i, 0)),
        ],
        out_specs=pl.BlockSpec(memory_space=pl.ANY),
        out_shape=jax.ShapeDtypeStruct((n_slots, d), h2.dtype),
        scratch_shapes=[pltpu.SemaphoreType.DMA(())],
        compiler_params=_cparams(("arbitrary",)),
        name="moe_dispatch",
    )(dest, h2)


def _gmm_kernel(gb0_ref, nblk_ref, u0_ref, n_used_ref, xa_hbm, xb_hbm, wgu_ref, wd_ref, ys_hbm,
                xbuf, ybuf, wgu_bf, wd_bf, sem_in, sem_out):
    g = pl.program_id(0)
    n_in, bm = xbuf.shape[:2]
    n_out = ybuf.shape[0]
    ff = wd_ref.shape[0]
    n_used = n_used_ref[0]
    rows_a = xa_hbm.shape[0]

    def x_copy(src_hbm, row0, b):
        rows = pl.ds(row0 if isinstance(row0, int) else pl.multiple_of(row0, bm), bm)
        return pltpu.make_async_copy(src_hbm.at[rows], xbuf.at[b % n_in], sem_in.at[b % n_in])

    def x_start(b):
        u0 = u0_ref[b]

        @pl.when(u0 < rows_a)
        def _():
            x_copy(xa_hbm, u0, b).start()

        @pl.when(u0 >= rows_a)
        def _():
            x_copy(xb_hbm, u0 - rows_a, b).start()

    def x_wait(b):
        x_copy(xa_hbm, 0, b).wait()

    def y_copy(b, slot):
        rows = pl.ds(pl.multiple_of(u0_ref[b], bm), bm)
        return pltpu.make_async_copy(ybuf.at[slot], ys_hbm.at[rows], sem_out.at[slot])

    @pl.when(g == 0)
    def _():
        for b in range(n_in - 1):
            @pl.when(b < n_used)
            def _():
                x_start(b)

    nb = nblk_ref[g]

    @pl.when(g % 2 == 0)
    def _():
        wgu_bf[...] = wgu_ref[...].astype(BF16)
        wd_bf[...] = wd_ref[...].astype(BF16)

    def block(j, carry):
        b = gb0_ref[g] + j
        slot = b % n_out
        x_wait(b)

        @pl.when(b + n_in - 1 < n_used)
        def _():
            x_start(b + n_in - 1)

        @pl.when(b >= n_out)
        def _():
            y_copy(b - n_out, slot).wait()

        gu = jnp.dot(_unpack_rows_bf16(xbuf[b % n_in]), wgu_bf[...], preferred_element_type=F32)
        act = (_silu(gu[:, :ff]) * gu[:, ff:]).astype(BF16)
        ybuf[slot] = _pack_rows(jnp.dot(act, wd_bf[...], preferred_element_type=F32))
        y_copy(b, slot).start()
        return carry

    lax.fori_loop(0, nb, block, 0)

    @pl.when(g == pl.num_programs(0) - 1)
    def _():
        for back in range(n_out, 0, -1):
            @pl.when(n_used >= back)
            def _():
                y_copy(n_used - back, (n_used - back) % n_out).wait()


def _grouped_ffn(xs_a, xs_b, gb0, nblk, u0_blk, n_used, wgu, wd, layer):
    n_slots = xs_a.shape[0] + xs_b.shape[0]
    bm = MOE_BM
    n_exp, d, ff2 = wgu.shape[1:]
    ff = wd.shape[2]
    grid_spec = pltpu.PrefetchScalarGridSpec(
        num_scalar_prefetch=4,
        grid=(2 * n_exp,),
        in_specs=[
            pl.BlockSpec(memory_space=pl.ANY),
            pl.BlockSpec(memory_space=pl.ANY),
            pl.BlockSpec((None, None, d, ff2), lambda g, *_: (layer, g // 2, 0, 0)),
            pl.BlockSpec((None, None, ff, d), lambda g, *_: (layer, g // 2, 0, 0)),
        ],
        out_specs=pl.BlockSpec(memory_space=pl.ANY),
        scratch_shapes=[
            pltpu.VMEM((GMM_IN_SLOTS, bm, d // 2), U32), pltpu.VMEM((GMM_OUT_SLOTS, bm, d // 2), U32),
            pltpu.VMEM((d, ff2), BF16), pltpu.VMEM((ff, d), BF16),
            pltpu.SemaphoreType.DMA((GMM_IN_SLOTS,)), pltpu.SemaphoreType.DMA((GMM_OUT_SLOTS,)),
        ],
    )
    return pl.pallas_call(
        _gmm_kernel,
        grid_spec=grid_spec,
        out_shape=jax.ShapeDtypeStruct((n_slots, d // 2), U32),
        compiler_params=_cparams(("arbitrary",)),
        name="moe_grouped_ffn",
    )(gb0, nblk, u0_blk, n_used, xs_a, xs_b, wgu, wd)


def _shared_expert(h2_ref, swgu_ref, swd_ref):
    ff = swd_ref.shape[0]
    gu = jnp.dot(_unpack_rows_bf16(h2_ref[...]), swgu_ref[...], preferred_element_type=F32)
    act = (_silu(gu[:, :ff]) * gu[:, ff:]).astype(BF16)
    return jnp.dot(act, swd_ref[...], preferred_element_type=F32)


def _combine_kernel(dest_ref, gates_ref, ys_hbm, x1_ref, h2_ref, gate_ref, lng_ref, lnb_ref,
                    swgu_ref, swd_ref, o_ref, buf, sem, *, dn_alpha):
    tc = x1_ref.shape[0]

    def issue(t, carry):
        for k in range(TOP_K):
            pltpu.make_async_copy(ys_hbm.at[pl.ds(dest_ref[k, t], 1)], buf.at[k, pl.ds(t, 1)], sem).start()
        return carry

    lax.fori_loop(0, tc, issue, 0)

    y = _shared_expert(h2_ref, swgu_ref, swd_ref)

    for k in range(TOP_K):
        pltpu.make_async_copy(ys_hbm.at[pl.ds(0, tc)], buf.at[k], sem).wait()
    for k in range(TOP_K):
        y = _add_weighted_rows(y, gates_ref[:, k:k + 1], buf[k])
    o_ref[...] = _layer_norm(dn_alpha * x1_ref[...] + gate_ref[...] * y, lng_ref[...], lnb_ref[...])


def _combine_gathered_kernel(gates_ref, g_ref, x1_ref, h2_ref, gate_ref, lng_ref, lnb_ref,
                             swgu_ref, swd_ref, prev_ref, o_ref, *, dn_alpha):
    del prev_ref
    y = _shared_expert(h2_ref, swgu_ref, swd_ref)
    for k in range(TOP_K):
        y = _add_weighted_rows(y, gates_ref[:, k:k + 1], g_ref[k])
    o_ref[...] = _layer_norm(dn_alpha * x1_ref[...] + gate_ref[...] * y, lng_ref[...], lnb_ref[...])


def _sc_gather_rows(table, idx):
    n = idx.shape[0]
    d = table.shape[1]
    mesh = plsc.VectorSubcoreMesh(core_axis_name="c", subcore_axis_name="s")
    n_workers = mesh.num_cores * mesh.num_subcores
    ch = SC_GATHER_ROWS
    per_w = n // n_workers
    n_ch = per_w // ch
    assert n % (n_workers * ch * 2) == 0

    @functools.partial(
        pl.kernel, mesh=mesh, out_type=jax.ShapeDtypeStruct((n, d), table.dtype),
        scratch_types=[pltpu.VMEM((ch,), I32), pltpu.VMEM((ch,), I32),
                       pltpu.VMEM((ch, d), table.dtype), pltpu.VMEM((ch, d), table.dtype),
                       pltpu.SemaphoreType.DMA, pltpu.SemaphoreType.DMA],
        name="moe_sc_gather")
    def gather(table_hbm, idx_hbm, out_hbm, idx_a, idx_b, rows_a, rows_b, sem_a, sem_b):
        wid = lax.axis_index("s") * mesh.num_cores + lax.axis_index("c")

        def chunk(i):
            return pl.ds(wid * per_w + i * ch, ch)

        def fetch(i, idx_v, rows_v, sem):
            pltpu.sync_copy(idx_hbm.at[chunk(i)], idx_v)
            pltpu.make_async_copy(table_hbm.at[idx_v], rows_v, sem).start()

        def drain(i, idx_v, rows_v, sem):
            pltpu.make_async_copy(table_hbm.at[idx_v], rows_v, sem).wait()
            pltpu.sync_copy(rows_v, out_hbm.at[chunk(i)])

        fetch(0, idx_a, rows_a, sem_a)

        @pl.loop(0, n_ch // 2)
        def _(j):
            i = 2 * j
            fetch(i + 1, idx_b, rows_b, sem_b)
            drain(i, idx_a, rows_a, sem_a)

            @pl.when(i + 2 < n_ch)
            def _():
                fetch(i + 2, idx_a, rows_a, sem_a)

            drain(i + 1, idx_b, rows_b, sem_b)

    return gather(table, idx)


def _sc_scatter_rows(rows, first_row, idx, n_out):
    n_k, n = idx.shape
    d = rows.shape[1]
    mesh = plsc.VectorSubcoreMesh(core_axis_name="c", subcore_axis_name="s")
    n_workers = mesh.num_cores * mesh.num_subcores
    ch = SC_GATHER_ROWS
    per_w = n // n_workers
    assert n % (n_workers * ch) == 0

    @functools.partial(
        pl.kernel, mesh=mesh, out_type=jax.ShapeDtypeStruct((n_out, d), rows.dtype),
        scratch_types=[pltpu.VMEM((ch,), I32), pltpu.VMEM((ch, d), rows.dtype)],
        name="moe_sc_scatter")
    def scatter(rows_hbm, idx_hbm, out_hbm, idx_v, rows_v):
        wid = lax.axis_index("s") * mesh.num_cores + lax.axis_index("c")

        @pl.loop(0, per_w // ch)
        def _(i):
            base = wid * per_w + i * ch
            pltpu.sync_copy(rows_hbm.at[pl.ds(first_row + base, ch)], rows_v)
            for k in range(n_k):
                pltpu.sync_copy(idx_hbm.at[k, pl.ds(base, ch)], idx_v)
                pltpu.sync_copy(rows_v, out_hbm.at[idx_v])

    return scatter(rows, idx)


def _combine(dest, gates_t, ys, x1, h2, mod, ln_g, ln_b, swgu, swd, n_rows, n_lat, seq, ctx_row,
             dn_alpha):
    d = x1.shape[1]
    tc = DISPATCH_TOK
    const = lambda i: (0, 0)
    n_blocks = n_rows // tc
    n_sc_blocks = (n_blocks * SC_SHARE_PERCENT) // 100
    n_tc = (n_blocks - n_sc_blocks) * tc
    n_sc = n_rows - n_tc
    gathered = _sc_gather_rows(ys, dest[:, n_tc:].reshape(TOP_K * n_sc)).reshape(TOP_K, n_sc, d // 2)
    swgu_bf, swd_bf = swgu.astype(BF16), swd.astype(BF16)
    out_tc = pl.pallas_call(
        functools.partial(_combine_kernel, dn_alpha=dn_alpha),
        grid=(n_tc // tc,),
        in_specs=[
            pl.BlockSpec((TOP_K, tc), lambda i: (0, i), memory_space=pltpu.SMEM),
            pl.BlockSpec((tc, TOP_K), lambda i: (i, 0)),
            pl.BlockSpec(memory_space=pl.ANY),
            pl.BlockSpec((tc, d), lambda i: (i, 0)),
            pl.BlockSpec((tc, d // 2), lambda i: (i, 0)),
            _mod_spec(5, tc, n_lat, seq, ctx_row, d),
            pl.BlockSpec((1, d), const), pl.BlockSpec((1, d), const),
            pl.BlockSpec(swgu.shape, const),
            pl.BlockSpec(swd.shape, const),
        ],
        out_specs=pl.BlockSpec((tc, d), lambda i: (i, 0)),
        out_shape=jax.ShapeDtypeStruct((n_rows, d), F32),
        scratch_shapes=[pltpu.VMEM((TOP_K, tc, d // 2), U32), pltpu.SemaphoreType.DMA(())],
        compiler_params=_cparams(("arbitrary",)),
        name="moe_combine",
    )(dest, gates_t, ys, x1, h2, mod, ln_g.reshape(1, d), ln_b.reshape(1, d), swgu_bf, swd_bf)

    tl = COMBINE_GATHERED_TOK
    off = n_tc // tl
    row_blk = pl.BlockSpec((tl, d), lambda i: (i + off, 0))
    return pl.pallas_call(
        functools.partial(_combine_gathered_kernel, dn_alpha=dn_alpha),
        grid=(n_sc // tl,),
        in_specs=[
            pl.BlockSpec((tl, TOP_K), lambda i: (i + off, 0)),
            pl.BlockSpec((TOP_K, tl, d // 2), lambda i: (0, i, 0)),
            row_blk, pl.BlockSpec((tl, d // 2), lambda i: (i + off, 0)),
            _mod_spec(5, tl, n_lat, seq, ctx_row, d, blk_off=off),
            pl.BlockSpec((1, d), const), pl.BlockSpec((1, d), const),
            pl.BlockSpec(swgu.shape, const),
            pl.BlockSpec(swd.shape, const),
            pl.BlockSpec(memory_space=pl.ANY),
        ],
        out_specs=row_blk,
        out_shape=jax.ShapeDtypeStruct((n_rows, d), F32),
        input_output_aliases={9: 0},
        compiler_params=_cparams(("parallel",)),
        name="moe_combine_gathered",
    )(gates_t, gathered, x1, h2, mod, ln_g.reshape(1, d), ln_b.reshape(1, d), swgu_bf, swd_bf, out_tc)


def _split_block(n_rows):
    n_blocks = n_rows // ROW_TILE
    sc_blocks = (n_blocks * SC_DISPATCH_PERCENT) // 100 // 2 * 2
    return n_blocks - sc_blocks


def _max_slots(n_tok, bm):
    return (n_tok * TOP_K + N_EXPERTS * (bm - 1) + bm - 1) // bm * bm


def _moe(x1, h2, eidx, gates, rank, counts, mod, ln_g, ln_b, wgu, wd, layer, swgu, swd,
         n_rows, n_lat, seq, ctx_row, dn_alpha, split_blk):
    bm = MOE_BM
    t0 = split_blk * ROW_TILE
    rows_a, rows_b = _max_slots(t0, bm), _max_slots(n_rows - t0, bm)
    cnt = counts.reshape(2, N_EXPERTS).astype(I32)
    pcnt = (cnt + bm - 1) // bm * bm
    pend = jnp.cumsum(pcnt, axis=1)
    ustart = pend - pcnt + jnp.array([[0], [rows_a]], I32)
    dest = jnp.concatenate([_slots(ustart[0], eidx[:, :t0], rank[:, :t0]),
                            _slots(ustart[1], eidx[:, t0:], rank[:, t0:])], axis=1)

    nblk = (pcnt // bm).T.reshape(2 * N_EXPERTS)
    gb0 = jnp.cumsum(nblk) - nblk
    u0_grp = ustart.T.reshape(2 * N_EXPERTS)
    blk = jnp.arange((rows_a + rows_b) // bm, dtype=I32)
    grp = jnp.sum((gb0[None, :] <= blk[:, None]).astype(I32), axis=1) - 1
    u0_blk = u0_grp[grp] + (blk - gb0[grp]) * bm
    n_used = jnp.sum(nblk).astype(I32).reshape(1)

    xs_a = _dispatch(dest[:, :t0], h2, rows_a)
    xs_b = _sc_scatter_rows(h2, t0, dest[:, t0:] - rows_a, rows_b)
    ys = _grouped_ffn(xs_a, xs_b, gb0, nblk, u0_blk, n_used, wgu, wd, layer)
    return _combine(dest, gates.T, ys, x1, h2, mod, ln_g, ln_b, swgu, swd,
                    n_rows, n_lat, seq, ctx_row, dn_alpha)


def kernel(x, c, ctx, c_ctx, ada_w, ada_b, ln_g, ln_b, attn_w_qkv, attn_q_norm, attn_k_norm, attn_w_o, conv_w_in, conv_taps, conv_w_out, router_w, router_bias, exp_w_gate_up, exp_w_down, shared_w_gate_up, shared_w_down):
    b, seq, d = x.shape
    cl = ctx.shape[1]
    depth = ada_w.shape[0]
    n_lat = b * seq
    n_ctx = b * cl
    dn_alpha = (2 * depth) ** 0.25
    assert depth == 2 and b < MOD_ROWS
    assert seq % ROW_TILE == 0 and n_ctx % ROW_TILE == 0 and seq % GRID_W == 0

    cond = jnp.zeros((MOD_ROWS, d), F32).at[:b].set(c).at[b].set(c_ctx)
    mod = _modulation(cond, ada_w, ada_b)
    xall = jnp.concatenate([x.reshape(n_lat, d), ctx.reshape(n_ctx, d)], axis=0)
    n_all = n_lat + n_ctx

    q, k, v = _qkv_project(xall, mod[0], attn_w_qkv[0], attn_q_norm[0], attn_k_norm[0], n_lat, seq, b)
    o = _attention(q, k, v, b, seq, cl, n_lat)
    x1, h2, eidx, gates, rank, counts = _post_mixer(
        o, xall, mod[0], ln_g[0, 0], ln_b[0, 0], attn_w_o[0], router_w[0], router_bias[0],
        n_all, n_lat, seq, b, dn_alpha, _split_block(n_all))
    xall = _moe(x1, h2, eidx, gates, rank, counts, mod[0], ln_g[0, 1], ln_b[0, 1],
                exp_w_gate_up, exp_w_down, 0, shared_w_gate_up[0], shared_w_down[0],
                n_all, n_lat, seq, b, dn_alpha, _split_block(n_all))

    zin = _in_project(xall, mod[1], conv_w_in[0], n_lat, n_lat, seq, b)
    a = _short_conv(zin, conv_taps[0], b, seq, d)
    x1, h2, eidx, gates, rank, counts = _post_mixer(
        a, xall, mod[1], ln_g[1, 0], ln_b[1, 0], conv_w_out[0], router_w[1], router_bias[1],
        n_lat, n_lat, seq, b, dn_alpha, _split_block(n_lat))
    out = _moe(x1, h2, eidx, gates, rank, counts, mod[1], ln_g[1, 1], ln_b[1, 1],
               exp_w_gate_up, exp_w_down, 1, shared_w_gate_up[1], shared_w_down[1],
               n_lat, n_lat, seq, b, dn_alpha, _split_block(n_lat))
    return out.reshape(b, seq, d)
```

```python
import functools

import jax
import jax.numpy as jnp
from jax import lax
from jax.experimental import pallas as pl
from jax.experimental.pallas import tpu as pltpu
from jax.experimental.pallas import tpu_sc as plsc

F32 = jnp.float32
BF16 = jnp.bfloat16
I32 = jnp.int32

N_HEADS = 8
N_KV_HEADS = 2
HEAD_DIM = 128
KV_GROUP = N_HEADS // N_KV_HEADS
GRID_W = 64
ROPE_THETA = 10000.0
N_EXPERTS = 256
TOP_K = 8
N_GROUPS = 8
TOPK_GROUPS = 4
PER_GROUP = N_EXPERTS // N_GROUPS
ROUTED_SCALE = 2.5
LN_EPS = 1e-5
QK_EPS = 1e-6
N_MOD = 6
MOD_ROWS = 16

LANES = 128
SUBLANES = 8
VMEM_LIMIT = 56 * 1024 * 1024

ROW_TILE = 512
ATTN_TQ = 256
ATTN_TK = 2048
MOE_BM = 256
GMM_IN_SLOTS = 4
GMM_OUT_SLOTS = 3
DISPATCH_TOK = 512
COMBINE_GATHERED_TOK = 512
SC_GATHER_ROWS = 64
SC_SHARE_PERCENT = 78
SC_DISPATCH_PERCENT = 75
CONV_LANES = 128

HIGHEST = lax.Precision.HIGHEST
LOG2_E = 1.4426950408889634


def _cparams(sem):
    return pltpu.CompilerParams(dimension_semantics=sem, vmem_limit_bytes=VMEM_LIMIT)


def _silu(v):
    return v * jax.nn.sigmoid(v)


U32 = jnp.uint32
HI_MASK = 0xFFFF0000


def _pack_rows(v):
    half = v.shape[1] // 2
    lo = lax.bitcast_convert_type(v[:, :half].astype(BF16).astype(F32), U32)
    hi = lax.bitcast_convert_type(v[:, half:].astype(BF16).astype(F32), U32)
    return (lo >> 16) | (hi & U32(HI_MASK))


def _add_weighted_rows(y, gate, packed):
    half = y.shape[1] // 2
    lo = lax.bitcast_convert_type(packed << 16, F32)
    hi = lax.bitcast_convert_type(packed & U32(HI_MASK), F32)
    return jnp.concatenate([y[:, :half] + gate * lo, y[:, half:] + gate * hi], axis=-1)


def _unpack_rows_bf16(w):
    lo = lax.bitcast_convert_type(w << 16, F32)
    hi = lax.bitcast_convert_type(w & U32(HI_MASK), F32)
    return jnp.concatenate([lo.astype(BF16), hi.astype(BF16)], axis=-1)


def _mod_kernel(c_ref, w_ref, b_ref, o_ref):
    s = _silu(c_ref[...])
    o_ref[...] = jnp.dot(s, w_ref[...], precision=HIGHEST, preferred_element_type=F32) + b_ref[...]


def _modulation(cond, ada_w, ada_b):
    depth, d, nd = ada_w.shape
    tn = 1536
    out = pl.pallas_call(
        _mod_kernel,
        grid=(depth, nd // tn),
        in_specs=[
            pl.BlockSpec((MOD_ROWS, d), lambda l, j: (0, 0)),
            pl.BlockSpec((None, d, tn), lambda l, j: (l, 0, j)),
            pl.BlockSpec((None, 1, tn), lambda l, j: (l, 0, j)),
        ],
        out_specs=pl.BlockSpec((None, MOD_ROWS, tn), lambda l, j: (l, 0, j)),
        out_shape=jax.ShapeDtypeStruct((depth, MOD_ROWS, nd), F32),
        compiler_params=_cparams(("arbitrary", "arbitrary")),
        name="adaln_modulation",
    )(cond, ada_w, ada_b.reshape(depth, 1, nd))
    return out.reshape(depth, MOD_ROWS * N_MOD, 1, d)


def _mod_spec(comp, tm, n_lat, seq, ctx_row, d, blk_off=0):
    def index(i, *_):
        row0 = (i + blk_off) * tm
        r = jnp.where(row0 < n_lat, row0 // seq, ctx_row)
        return (r * N_MOD + comp, 0, 0)

    return pl.BlockSpec((None, 1, d), index)


def _qkv_kernel(x_ref, shift_ref, scale_ref, w_ref, qg_ref, kg_ref, cos_ref, sin_ref,
                q_ref, k_ref, v_ref):
    h = (x_ref[...] * (1.0 + scale_ref[...]) + shift_ref[...]).astype(BF16)
    qkv = jnp.dot(h, w_ref[...], preferred_element_type=F32)
    cos = cos_ref[...]
    sin = sin_ref[...]
    hq = N_HEADS * HEAD_DIM
    kd = N_KV_HEADS * HEAD_DIM

    def norm_rope(t, g, post):
        t = t * lax.rsqrt(jnp.mean(t * t, axis=-1, keepdims=True) + QK_EPS) * g
        t = t * cos + pltpu.roll(t, HEAD_DIM // 2, axis=1) * sin
        return (t * post).astype(BF16)

    for hd in range(N_HEADS):
        sl = slice(hd * HEAD_DIM, (hd + 1) * HEAD_DIM)
        q_ref[:, sl] = norm_rope(qkv[:, sl], qg_ref[...], HEAD_DIM ** -0.5 * LOG2_E)
    for hd in range(N_KV_HEADS):
        sl = slice(hd * HEAD_DIM, (hd + 1) * HEAD_DIM)
        k_ref[:, sl] = norm_rope(qkv[:, hq + hd * HEAD_DIM: hq + (hd + 1) * HEAD_DIM], kg_ref[...], 1.0)
    v_ref[...] = qkv[:, hq + kd:].astype(BF16)


def _rope_tables(seq, tm):
    rows = seq // GRID_W
    row = jnp.repeat(jnp.arange(rows, dtype=F32), GRID_W)
    col = jnp.tile(jnp.arange(GRID_W, dtype=F32), rows)
    axis_dim = HEAD_DIM // 2
    freqs = ROPE_THETA ** (-jnp.arange(0, axis_dim, 2, dtype=F32) / axis_dim)
    ang = jnp.concatenate([row[:, None] * freqs, col[:, None] * freqs], axis=-1)
    cos = jnp.concatenate([jnp.cos(ang), jnp.cos(ang)], axis=-1)
    sin = jnp.concatenate([-jnp.sin(ang), jnp.sin(ang)], axis=-1)
    cos = jnp.concatenate([cos, jnp.ones((tm, HEAD_DIM), F32)], axis=0)
    sin = jnp.concatenate([sin, jnp.zeros((tm, HEAD_DIM), F32)], axis=0)
    return cos.reshape(seq // tm + 1, tm, HEAD_DIM), sin.reshape(seq // tm + 1, tm, HEAD_DIM)


def _qkv_project(xall, mod, w_qkv, q_g, k_g, n_lat, seq, ctx_row):
    t, d = xall.shape
    tm = ROW_TILE
    hq = N_HEADS * HEAD_DIM
    kd = N_KV_HEADS * HEAD_DIM
    perm = jnp.concatenate([jnp.arange(0, HEAD_DIM, 2), jnp.arange(1, HEAD_DIM, 2)])
    cols = jnp.concatenate([hd * HEAD_DIM + perm for hd in range(N_HEADS + N_KV_HEADS)]
                           + [jnp.arange(hq + kd, hq + 2 * kd)])
    w = w_qkv[:, cols].astype(BF16)
    cos, sin = _rope_tables(seq, tm)
    n_pos = seq // tm

    def pos_index(i):
        row0 = i * tm
        return (jnp.where(row0 < n_lat, (row0 % seq) // tm, n_pos), 0, 0)

    const = lambda i: (0, 0)
    return pl.pallas_call(
        _qkv_kernel,
        grid=(t // tm,),
        in_specs=[
            pl.BlockSpec((tm, d), lambda i: (i, 0)),
            _mod_spec(0, tm, n_lat, seq, ctx_row, d),
            _mod_spec(1, tm, n_lat, seq, ctx_row, d),
            pl.BlockSpec(w.shape, const),
            pl.BlockSpec((1, HEAD_DIM), const),
            pl.BlockSpec((1, HEAD_DIM), const),
            pl.BlockSpec((None, tm, HEAD_DIM), pos_index),
            pl.BlockSpec((None, tm, HEAD_DIM), pos_index),
        ],
        out_specs=[
            pl.BlockSpec((tm, hq), lambda i: (i, 0)),
            pl.BlockSpec((tm, kd), lambda i: (i, 0)),
            pl.BlockSpec((tm, kd), lambda i: (i, 0)),
        ],
        out_shape=[
            jax.ShapeDtypeStruct((t, hq), BF16),
            jax.ShapeDtypeStruct((t, kd), BF16),
            jax.ShapeDtypeStruct((t, kd), BF16),
        ],
        compiler_params=_cparams(("parallel",)),
        name="qkv_norm_rope",
    )(xall, mod, mod, w, q_g[perm].reshape(1, HEAD_DIM), k_g[perm].reshape(1, HEAD_DIM), cos, sin)


def _attn_kernel(*refs, n_lat_chunks, tk):
    if n_lat_chunks:
        q_ref, kc_ref, vc_ref, kl_ref, vl_ref, o_ref = refs
    else:
        q_ref, kc_ref, vc_ref, o_ref = refs
    tq = q_ref.shape[0]
    q = jnp.concatenate([q_ref[:, h * HEAD_DIM:(h + 1) * HEAD_DIM] for h in range(KV_GROUP)], axis=0)
    rows = KV_GROUP * tq

    def chunk(k, v, m, l, acc):
        s = lax.dot_general(q, k, (((1,), (1,)), ((), ())), preferred_element_type=F32)
        m_new = jnp.maximum(m, jnp.max(s, axis=-1, keepdims=True))
        p = jnp.exp2(s - m_new)
        a = jnp.exp2(m - m_new)
        l = a * l + jnp.sum(p, axis=-1, keepdims=True)
        acc = a * acc + jnp.dot(p.astype(BF16), v, preferred_element_type=F32)
        return m_new, l, acc

    m = jnp.full((rows, 1), -jnp.inf, F32)
    l = jnp.zeros((rows, 1), F32)
    acc = jnp.zeros((rows, HEAD_DIM), F32)
    m, l, acc = chunk(kc_ref[...], vc_ref[...], m, l, acc)
    for c in range(n_lat_chunks):
        m, l, acc = chunk(kl_ref[c * tk:(c + 1) * tk, :], vl_ref[c * tk:(c + 1) * tk, :], m, l, acc)
    o = (acc / l).astype(BF16)
    for h in range(KV_GROUP):
        o_ref[:, h * HEAD_DIM:(h + 1) * HEAD_DIM] = o[h * tq:(h + 1) * tq]


def _attention(q, k, v, b, seq, cl, n_lat):
    t = q.shape[0]
    gw = KV_GROUP * HEAD_DIM
    tq = min(ATTN_TQ, seq)
    tk = min(ATTN_TK, seq)
    nq = seq // tq
    ctx_blk0 = n_lat // cl
    hq = N_HEADS * HEAD_DIM

    ctx_kv = pl.BlockSpec((cl, HEAD_DIM), lambda bi, g, qi: (ctx_blk0 + bi, g))
    lat_kv = pl.BlockSpec((seq, HEAD_DIM), lambda bi, g, qi: (bi, g))
    lat_q = pl.BlockSpec((tq, gw), lambda bi, g, qi: (bi * nq + qi, g))
    o_lat = pl.pallas_call(
        functools.partial(_attn_kernel, n_lat_chunks=seq // tk, tk=tk),
        grid=(b, N_KV_HEADS, nq),
        in_specs=[lat_q, ctx_kv, ctx_kv, lat_kv, lat_kv],
        out_specs=lat_q,
        out_shape=jax.ShapeDtypeStruct((n_lat, hq), BF16),
        compiler_params=_cparams(("parallel", "parallel", "arbitrary")),
        name="attention_latent",
    )(q, k, v, k, v)

    ctx_q = pl.BlockSpec((cl, gw), lambda bi, g: (ctx_blk0 + bi, g))
    ctx_kv2 = pl.BlockSpec((cl, HEAD_DIM), lambda bi, g: (ctx_blk0 + bi, g))
    o_ctx = pl.pallas_call(
        functools.partial(_attn_kernel, n_lat_chunks=0, tk=tk),
        grid=(b, N_KV_HEADS),
        in_specs=[ctx_q, ctx_kv2, ctx_kv2],
        out_specs=pl.BlockSpec((cl, gw), lambda bi, g: (bi, g)),
        out_shape=jax.ShapeDtypeStruct((t - n_lat, hq), BF16),
        compiler_params=_cparams(("parallel", "parallel")),
        name="attention_context",
    )(q, k, v)
    return jnp.concatenate([o_lat, o_ctx], axis=0)


def _in_proj_kernel(x_ref, shift_ref, scale_ref, w_ref, o_ref):
    h = (x_ref[...] * (1.0 + scale_ref[...]) + shift_ref[...]).astype(BF16)
    o_ref[...] = jnp.dot(h, w_ref[...], preferred_element_type=F32)


def _in_project(xall, mod, w_in, n_rows, n_lat, seq, ctx_row):
    d = xall.shape[1]
    n_out = w_in.shape[1]
    tm = ROW_TILE
    return pl.pallas_call(
        _in_proj_kernel,
        grid=(n_rows // tm,),
        in_specs=[
            pl.BlockSpec((tm, d), lambda i: (i, 0)),
            _mod_spec(0, tm, n_lat, seq, ctx_row, d),
            _mod_spec(1, tm, n_lat, seq, ctx_row, d),
            pl.BlockSpec(w_in.shape, lambda i: (0, 0)),
        ],
        out_specs=pl.BlockSpec((tm, n_out), lambda i: (i, 0)),
        out_shape=jax.ShapeDtypeStruct((n_rows, n_out), F32),
        compiler_params=_cparams(("parallel",)),
        name="conv_in_proj",
    )(xall, mod, mod, w_in.astype(BF16))


def _conv_kernel(bg_ref, cg_ref, v_ref, taps_ref, o_ref):
    u = cg_ref[...] * v_ref[...]
    n = u.shape[0]
    pos = lax.broadcasted_iota(I32, u.shape, 0)
    prev = jnp.where(pos == 0, 0.0, pltpu.roll(u, 1, axis=0))
    nxt = jnp.where(pos == n - 1, 0.0, pltpu.roll(u, n - 1, axis=0))
    conv = prev * taps_ref[0:1, :] + u * taps_ref[1:2, :] + nxt * taps_ref[2:3, :]
    o_ref[...] = (bg_ref[...] * conv).astype(BF16)


def _short_conv(zin, taps, n_seqs, seq, d):
    tc = CONV_LANES
    nj = d // tc
    return pl.pallas_call(
        _conv_kernel,
        grid=(n_seqs, nj),
        in_specs=[
            pl.BlockSpec((seq, tc), lambda s, j: (s, j)),
            pl.BlockSpec((seq, tc), lambda s, j: (s, nj + j)),
            pl.BlockSpec((seq, tc), lambda s, j: (s, 2 * nj + j)),
            pl.BlockSpec((taps.shape[0], tc), lambda s, j: (0, j)),
        ],
        out_specs=pl.BlockSpec((seq, tc), lambda s, j: (s, j)),
        out_shape=jax.ShapeDtypeStruct((n_seqs * seq, d), BF16),
        compiler_params=_cparams(("parallel", "parallel")),
        name="short_conv",
    )(zin, zin, zin, taps)


def _layer_norm(z, g, b):
    mu = jnp.mean(z, axis=-1, keepdims=True)
    zc = z - mu
    var = jnp.mean(zc * zc, axis=-1, keepdims=True)
    return zc * lax.rsqrt(var + LN_EPS) * g + b


def _post_kernel(a_ref, x_ref, gate_ref, lng_ref, lnb_ref, shift_ref, scale_ref, w_ref,
                 rwt_ref, rb_ref, tri_ref,
                 x1_ref, h2_ref, eidx_ref, gates_ref, rank_ref, cnt_ref, rwh_ref, rwl_ref, *,
                 dn_alpha, split_blk):
    tm = a_ref.shape[0]

    @pl.when(pl.program_id(0) == 0)
    def _():
        hi = rwt_ref[...].astype(BF16)
        rwh_ref[...] = hi
        rwl_ref[...] = (rwt_ref[...] - hi.astype(F32)).astype(BF16)

    d = x_ref.shape[1]
    y = jnp.dot(a_ref[...], w_ref[...], preferred_element_type=F32)
    x1 = _layer_norm(dn_alpha * x_ref[...] + gate_ref[...] * y, lng_ref[...], lnb_ref[...])
    x1_ref[...] = x1
    h2 = x1 * (1.0 + scale_ref[...]) + shift_ref[...]
    h2_ref[...] = _pack_rows(h2)

    h_hi = h2.astype(BF16)
    h_lo = (h2 - h_hi.astype(F32)).astype(BF16)

    def nt_dot(w, h):
        return lax.dot_general(w, h, (((1,), (1,)), ((), ())), preferred_element_type=F32)

    logits = nt_dot(rwh_ref[...], h_hi) + (nt_dot(rwl_ref[...], h_hi) + nt_dot(rwh_ref[...], h_lo))
    scores = jax.nn.sigmoid(logits)
    biased = scores + rb_ref[...]
    neg = -jnp.inf
    big = jnp.int32(1 << 30)
    row = lax.broadcasted_iota(I32, (N_EXPERTS, tm), 0)

    def argmax_rows(vals, idx):
        mx = jnp.max(vals, axis=0, keepdims=True)
        return mx, jnp.min(jnp.where(vals == mx, idx, big), axis=0, keepdims=True)

    gs = []
    grp_row = lax.broadcasted_iota(I32, (PER_GROUP, tm), 0)
    for g in range(N_GROUPS):
        bg = biased[g * PER_GROUP:(g + 1) * PER_GROUP]
        ig = grp_row + g * PER_GROUP
        m1, i1 = argmax_rows(bg, ig)
        m2 = jnp.max(jnp.where(ig == i1, neg, bg), axis=0, keepdims=True)
        gs.append(m1 + m2)
    gsc = jnp.concatenate(gs, axis=0)
    grow = lax.broadcasted_iota(I32, (N_GROUPS, tm), 0)
    gsel = jnp.zeros((N_GROUPS, tm), F32)
    for _ in range(TOPK_GROUPS):
        _, gi = argmax_rows(gsc, grow)
        hit = grow == gi
        gsel = jnp.where(hit, 1.0, gsel)
        gsc = jnp.where(hit, neg, gsc)
    cur = jnp.concatenate(
        [jnp.where(gsel[g:g + 1] > 0.0, biased[g * PER_GROUP:(g + 1) * PER_GROUP], neg)
         for g in range(N_GROUPS)], axis=0)

    onehot = jnp.zeros((N_EXPERTS, tm), F32)
    idxs, gvals = [], []
    for _ in range(TOP_K):
        _, ei = argmax_rows(cur, row)
        hit = row == ei
        gvals.append(jnp.sum(jnp.where(hit, scores, 0.0), axis=0, keepdims=True))
        idxs.append(ei)
        onehot = jnp.where(hit, 1.0, onehot)
        cur = jnp.where(hit, neg, cur)
    gv = jnp.concatenate(gvals, axis=0)
    gates_ref[...] = gv / jnp.sum(gv, axis=0, keepdims=True) * ROUTED_SCALE
    eidx_ref[...] = jnp.concatenate(idxs, axis=0)

    @pl.when((pl.program_id(0) == 0) | (pl.program_id(0) == split_blk))
    def _():
        cnt_ref[...] = jnp.zeros_like(cnt_ref)

    prefix = jnp.dot(onehot.astype(BF16), tri_ref[...], preferred_element_type=F32)
    pos = prefix + cnt_ref[...]
    rank_ref[...] = jnp.concatenate(
        [jnp.sum(jnp.where(row == ei, pos, 0.0), axis=0, keepdims=True) for ei in idxs],
        axis=0).astype(I32)
    cnt_ref[...] = cnt_ref[...] + jnp.sum(onehot, axis=1, keepdims=True)


def _post_mixer(a, xall, mod, ln_g, ln_b, w, router_w, router_bias, n_rows, n_lat, seq, ctx_row,
                dn_alpha, split_blk):
    d = xall.shape[1]
    tm = ROW_TILE
    tri = (lax.broadcasted_iota(I32, (tm, tm), 0) < lax.broadcasted_iota(I32, (tm, tm), 1)).astype(BF16)
    const = lambda i: (0, 0)
    row_blk = pl.BlockSpec((tm, d), lambda i: (i, 0))
    k_blk = pl.BlockSpec((TOP_K, tm), lambda i: (0, i))
    return pl.pallas_call(
        functools.partial(_post_kernel, dn_alpha=dn_alpha, split_blk=split_blk),
        grid=(n_rows // tm,),
        in_specs=[
            row_blk, row_blk,
            _mod_spec(2, tm, n_lat, seq, ctx_row, d),
            pl.BlockSpec((1, d), const), pl.BlockSpec((1, d), const),
            _mod_spec(3, tm, n_lat, seq, ctx_row, d),
            _mod_spec(4, tm, n_lat, seq, ctx_row, d),
            pl.BlockSpec(w.shape, const),
            pl.BlockSpec((N_EXPERTS, d), const),
            pl.BlockSpec((N_EXPERTS, 1), const),
            pl.BlockSpec((tm, tm), const),
        ],
        out_specs=[
            row_blk, pl.BlockSpec((tm, d // 2), lambda i: (i, 0)),
            k_blk, k_blk, k_blk,
            pl.BlockSpec((None, N_EXPERTS, 1), lambda i: (jnp.where(i >= split_blk, 1, 0), 0, 0)),
        ],
        out_shape=[
            jax.ShapeDtypeStruct((n_rows, d), F32),
            jax.ShapeDtypeStruct((n_rows, d // 2), U32),
            jax.ShapeDtypeStruct((TOP_K, n_rows), I32),
            jax.ShapeDtypeStruct((TOP_K, n_rows), F32),
            jax.ShapeDtypeStruct((TOP_K, n_rows), I32),
            jax.ShapeDtypeStruct((2, N_EXPERTS, 1), F32),
        ],
        scratch_shapes=[pltpu.VMEM((N_EXPERTS, d), BF16), pltpu.VMEM((N_EXPERTS, d), BF16)],
        compiler_params=_cparams(("arbitrary",)),
        name="post_mixer_router",
    )(a, xall, mod, ln_g.reshape(1, d), ln_b.reshape(1, d), mod, mod, w.astype(BF16),
      router_w.T, router_bias.reshape(N_EXPERTS, 1), tri)


def _slots_kernel(pstart_ref, eidx_ref, rank_ref, dest_ref):
    e = eidx_ref[...]

    def pick(i, acc):
        return jnp.where(e == i, pstart_ref[i], acc)

    dest_ref[...] = lax.fori_loop(0, N_EXPERTS, pick, jnp.zeros_like(e)) + rank_ref[...]


def _slots(pstarts, eidx, rank):
    n_tok = eidx.shape[1]
    tn = 2048 if n_tok % 2048 == 0 else ROW_TILE
    blk = pl.BlockSpec((TOP_K, tn), lambda i, ps: (0, i))
    return pl.pallas_call(
        _slots_kernel,
        grid_spec=pltpu.PrefetchScalarGridSpec(
            num_scalar_prefetch=1, grid=(n_tok // tn,), in_specs=[blk, blk], out_specs=blk),
        out_shape=jax.ShapeDtypeStruct((TOP_K, n_tok), I32),
        compiler_params=_cparams(("arbitrary",)),
        name="moe_slots",
    )(pstarts, eidx, rank)


def _dispatch_kernel(dest_ref, h_ref, xs_hbm, sem):
    td = dest_ref.shape[1]

    def issue(t, carry):
        for k in range(TOP_K):
            pltpu.make_async_copy(h_ref.at[pl.ds(t, 1)], xs_hbm.at[pl.ds(dest_ref[k, t], 1)], sem).start()
        return carry

    lax.fori_loop(0, td, issue, 0)
    pltpu.make_async_copy(xs_hbm.at[pl.ds(0, TOP_K * td)], xs_hbm.at[pl.ds(0, TOP_K * td)], sem).wait()


def _dispatch(dest, h2, n_slots):
    n_tok = dest.shape[1]
    d = h2.shape[1]
    td = DISPATCH_TOK
    return pl.pallas_call(
        _dispatch_kernel,
        grid=(n_tok // td,),
        in_specs=[
            pl.BlockSpec((TOP_K, td), lambda i: (0, i), memory_space=pltpu.SMEM),
            pl.BlockSpec((td, d), lambda i: (i, 0)),
        ],
        out_specs=pl.BlockSpec(memory_space=pl.ANY),
        out_shape=jax.ShapeDtypeStruct((n_slots, d), h2.dtype),
        scratch_shapes=[pltpu.SemaphoreType.DMA(())],
        compiler_params=_cparams(("arbitrary",)),
        name="moe_dispatch",
    )(dest, h2)


def _gmm_kernel(gb0_ref, nblk_ref, u0_ref, n_used_ref, xa_hbm, xb_hbm, wgu_ref, wd_ref, ys_hbm,
                xbuf, ybuf, wgu_bf, wd_bf, sem_in, sem_out):
    g = pl.program_id(0)
    n_in, bm = xbuf.shape[:2]
    n_out = ybuf.shape[0]
    ff = wd_ref.shape[0]
    n_used = n_used_ref[0]
    rows_a = xa_hbm.shape[0]

    def x_copy(src_hbm, row0, b):
        rows = pl.ds(row0 if isinstance(row0, int) else pl.multiple_of(row0, bm), bm)
        return pltpu.make_async_copy(src_hbm.at[rows], xbuf.at[b % n_in], sem_in.at[b % n_in])

    def x_start(b):
        u0 = u0_ref[b]

        @pl.when(u0 < rows_a)
        def _():
            x_copy(xa_hbm, u0, b).start()

        @pl.when(u0 >= rows_a)
        def _():
            x_copy(xb_hbm, u0 - rows_a, b).start()

    def x_wait(b):
        x_copy(xa_hbm, 0, b).wait()

    def y_copy(b, slot):
        rows = pl.ds(pl.multiple_of(u0_ref[b], bm), bm)
        return pltpu.make_async_copy(ybuf.at[slot], ys_hbm.at[rows], sem_out.at[slot])

    @pl.when(g == 0)
    def _():
        for b in range(n_in - 1):
            @pl.when(b < n_used)
            def _():
                x_start(b)

    nb = nblk_ref[g]

    @pl.when(g % 2 == 0)
    def _():
        wgu_bf[...] = wgu_ref[...].astype(BF16)
        wd_bf[...] = wd_ref[...].astype(BF16)

    def block(j, carry):
        b = gb0_ref[g] + j
        slot = b % n_out
        x_wait(b)

        @pl.when(b + n_in - 1 < n_used)
        def _():
            x_start(b + n_in - 1)

        @pl.when(b >= n_out)
        def _():
            y_copy(b - n_out, slot).wait()

        gu = jnp.dot(_unpack_rows_bf16(xbuf[b % n_in]), wgu_bf[...], preferred_element_type=F32)
        act = (_silu(gu[:, :ff]) * gu[:, ff:]).astype(BF16)
        ybuf[slot] = _pack_rows(jnp.dot(act, wd_bf[...], preferred_element_type=F32))
        y_copy(b, slot).start()
        return carry

    lax.fori_loop(0, nb, block, 0)

    @pl.when(g == pl.num_programs(0) - 1)
    def _():
        for back in range(n_out, 0, -1):
            @pl.when(n_used >= back)
            def _():
                y_copy(n_used - back, (n_used - back) % n_out).wait()


def _grouped_ffn(xs_a, xs_b, gb0, nblk, u0_blk, n_used, wgu, wd, layer):
    n_slots = xs_a.shape[0] + xs_b.shape[0]
    bm = MOE_BM
    n_exp, d, ff2 = wgu.shape[1:]
    ff = wd.shape[2]
    grid_spec = pltpu.PrefetchScalarGridSpec(
        num_scalar_prefetch=4,
        grid=(2 * n_exp,),
        in_specs=[
            pl.BlockSpec(memory_space=pl.ANY),
            pl.BlockSpec(memory_space=pl.ANY),
            pl.BlockSpec((None, None, d, ff2), lambda g, *_: (layer, g // 2, 0, 0)),
            pl.BlockSpec((None, None, ff, d), lambda g, *_: (layer, g // 2, 0, 0)),
        ],
        out_specs=pl.BlockSpec(memory_space=pl.ANY),
        scratch_shapes=[
            pltpu.VMEM((GMM_IN_SLOTS, bm, d // 2), U32), pltpu.VMEM((GMM_OUT_SLOTS, bm, d // 2), U32),
            pltpu.VMEM((d, ff2), BF16), pltpu.VMEM((ff, d), BF16),
            pltpu.SemaphoreType.DMA((GMM_IN_SLOTS,)), pltpu.SemaphoreType.DMA((GMM_OUT_SLOTS,)),
        ],
    )
    return pl.pallas_call(
        _gmm_kernel,
        grid_spec=grid_spec,
        out_shape=jax.ShapeDtypeStruct((n_slots, d // 2), U32),
        compiler_params=_cparams(("arbitrary",)),
        name="moe_grouped_ffn",
    )(gb0, nblk, u0_blk, n_used, xs_a, xs_b, wgu, wd)


def _shared_expert(h2_ref, swgu_ref, swd_ref):
    ff = swd_ref.shape[0]
    gu = jnp.dot(_unpack_rows_bf16(h2_ref[...]), swgu_ref[...], preferred_element_type=F32)
    act = (_silu(gu[:, :ff]) * gu[:, ff:]).astype(BF16)
    return jnp.dot(act, swd_ref[...], preferred_element_type=F32)


def _combine_kernel(dest_ref, gates_ref, ys_hbm, x1_ref, h2_ref, gate_ref, lng_ref, lnb_ref,
                    swgu_ref, swd_ref, o_ref, buf, sem, *, dn_alpha):
    tc = x1_ref.shape[0]

    def issue(t, carry):
        for k in range(TOP_K):
            pltpu.make_async_copy(ys_hbm.at[pl.ds(dest_ref[k, t], 1)], buf.at[k, pl.ds(t, 1)], sem).start()
        return carry

    lax.fori_loop(0, tc, issue, 0)

    y = _shared_expert(h2_ref, swgu_ref, swd_ref)

    for k in range(TOP_K):
        pltpu.make_async_copy(ys_hbm.at[pl.ds(0, tc)], buf.at[k], sem).wait()
    for k in range(TOP_K):
        y = _add_weighted_rows(y, gates_ref[:, k:k + 1], buf[k])
    o_ref[...] = _layer_norm(dn_alpha * x1_ref[...] + gate_ref[...] * y, lng_ref[...], lnb_ref[...])


def _combine_gathered_kernel(gates_ref, g_ref, x1_ref, h2_ref, gate_ref, lng_ref, lnb_ref,
                             swgu_ref, swd_ref, prev_ref, o_ref, *, dn_alpha):
    del prev_ref
    y = _shared_expert(h2_ref, swgu_ref, swd_ref)
    for k in range(TOP_K):
        y = _add_weighted_rows(y, gates_ref[:, k:k + 1], g_ref[k])
    o_ref[...] = _layer_norm(dn_alpha * x1_ref[...] + gate_ref[...] * y, lng_ref[...], lnb_ref[...])


def _sc_gather_rows(table, idx):
    n = idx.shape[0]
    d = table.shape[1]
    mesh = plsc.VectorSubcoreMesh(core_axis_name="c", subcore_axis_name="s")
    n_workers = mesh.num_cores * mesh.num_subcores
    ch = SC_GATHER_ROWS
    per_w = n // n_workers
    n_ch = per_w // ch
    assert n % (n_workers * ch * 2) == 0

    @functools.partial(
        pl.kernel, mesh=mesh, out_type=jax.ShapeDtypeStruct((n, d), table.dtype),
        scratch_types=[pltpu.VMEM((ch,), I32), pltpu.VMEM((ch,), I32),
                       pltpu.VMEM((ch, d), table.dtype), pltpu.VMEM((ch, d), table.dtype),
                       pltpu.SemaphoreType.DMA, pltpu.SemaphoreType.DMA],
        name="moe_sc_gather")
    def gather(table_hbm, idx_hbm, out_hbm, idx_a, idx_b, rows_a, rows_b, sem_a, sem_b):
        wid = lax.axis_index("s") * mesh.num_cores + lax.axis_index("c")

        def chunk(i):
            return pl.ds(wid * per_w + i * ch, ch)

        def fetch(i, idx_v, rows_v, sem):
            pltpu.sync_copy(idx_hbm.at[chunk(i)], idx_v)
            pltpu.make_async_copy(table_hbm.at[idx_v], rows_v, sem).start()

        def drain(i, idx_v, rows_v, sem):
            pltpu.make_async_copy(table_hbm.at[idx_v], rows_v, sem).wait()
            pltpu.sync_copy(rows_v, out_hbm.at[chunk(i)])

        fetch(0, idx_a, rows_a, sem_a)

        @pl.loop(0, n_ch // 2)
        def _(j):
            i = 2 * j
            fetch(i + 1, idx_b, rows_b, sem_b)
            drain(i, idx_a, rows_a, sem_a)

            @pl.when(i + 2 < n_ch)
            def _():
                fetch(i + 2, idx_a, rows_a, sem_a)

            drain(i + 1, idx_b, rows_b, sem_b)

    return gather(table, idx)


def _sc_scatter_rows(rows, first_row, idx, n_out):
    n_k, n = idx.shape
    d = rows.shape[1]
    mesh = plsc.VectorSubcoreMesh(core_axis_name="c", subcore_axis_name="s")
    n_workers = mesh.num_cores * mesh.num_subcores
    ch = SC_GATHER_ROWS
    per_w = n // n_workers
    assert n % (n_workers * ch) == 0

    @functools.partial(
        pl.kernel, mesh=mesh, out_type=jax.ShapeDtypeStruct((n_out, d), rows.dtype),
        scratch_types=[pltpu.VMEM((ch,), I32), pltpu.VMEM((ch, d), rows.dtype)],
        name="moe_sc_scatter")
    def scatter(rows_hbm, idx_hbm, out_hbm, idx_v, rows_v):
        wid = lax.axis_index("s") * mesh.num_cores + lax.axis_index("c")

        @pl.loop(0, per_w // ch)
        def _(i):
            base = wid * per_w + i * ch
            pltpu.sync_copy(rows_hbm.at[pl.ds(first_row + base, ch)], rows_v)
            for k in range(n_k):
                pltpu.sync_copy(idx_hbm.at[k, pl.ds(base, ch)], idx_v)
                pltpu.sync_copy(rows_v, out_hbm.at[idx_v])

    return scatter(rows, idx)


def _combine(dest, gates_t, ys, x1, h2, mod, ln_g, ln_b, swgu, swd, n_rows, n_lat, seq, ctx_row,
             dn_alpha):
    d = x1.shape[1]
    tc = DISPATCH_TOK
    const = lambda i: (0, 0)
    n_blocks = n_rows // tc
    n_sc_blocks = (n_blocks * SC_SHARE_PERCENT) // 100
    n_tc = (n_blocks - n_sc_blocks) * tc
    n_sc = n_rows - n_tc
    gathered = _sc_gather_rows(ys, dest[:, n_tc:].reshape(TOP_K * n_sc)).reshape(TOP_K, n_sc, d // 2)
    swgu_bf, swd_bf = swgu.astype(BF16), swd.astype(BF16)
    out_tc = pl.pallas_call(
        functools.partial(_combine_kernel, dn_alpha=dn_alpha),
        grid=(n_tc // tc,),
        in_specs=[
            pl.BlockSpec((TOP_K, tc), lambda i: (0, i), memory_space=pltpu.SMEM),
            pl.BlockSpec((tc, TOP_K), lambda i: (i, 0)),
            pl.BlockSpec(memory_space=pl.ANY),
            pl.BlockSpec((tc, d), lambda i: (i, 0)),
            pl.BlockSpec((tc, d // 2), lambda i: (i, 0)),
            _mod_spec(5, tc, n_lat, seq, ctx_row, d),
            pl.BlockSpec((1, d), const), pl.BlockSpec((1, d), const),
            pl.BlockSpec(swgu.shape, const),
            pl.BlockSpec(swd.shape, const),
        ],
        out_specs=pl.BlockSpec((tc, d), lambda i: (i, 0)),
        out_shape=jax.ShapeDtypeStruct((n_rows, d), F32),
        scratch_shapes=[pltpu.VMEM((TOP_K, tc, d // 2), U32), pltpu.SemaphoreType.DMA(())],
        compiler_params=_cparams(("arbitrary",)),
        name="moe_combine",
    )(dest, gates_t, ys, x1, h2, mod, ln_g.reshape(1, d), ln_b.reshape(1, d), swgu_bf, swd_bf)

    tl = COMBINE_GATHERED_TOK
    off = n_tc // tl
    row_blk = pl.BlockSpec((tl, d), lambda i: (i + off, 0))
    return pl.pallas_call(
        functools.partial(_combine_gathered_kernel, dn_alpha=dn_alpha),
        grid=(n_sc // tl,),
        in_specs=[
            pl.BlockSpec((tl, TOP_K), lambda i: (i + off, 0)),
            pl.BlockSpec((TOP_K, tl, d // 2), lambda i: (0, i, 0)),
            row_blk, pl.BlockSpec((tl, d // 2), lambda i: (i + off, 0)),
            _mod_spec(5, tl, n_lat, seq, ctx_row, d, blk_off=off),
            pl.BlockSpec((1, d), const), pl.BlockSpec((1, d), const),
            pl.BlockSpec(swgu.shape, const),
            pl.BlockSpec(swd.shape, const),
            pl.BlockSpec(memory_space=pl.ANY),
        ],
        out_specs=row_blk,
        out_shape=jax.ShapeDtypeStruct((n_rows, d), F32),
        input_output_aliases={9: 0},
        compiler_params=_cparams(("parallel",)),
        name="moe_combine_gathered",
    )(gates_t, gathered, x1, h2, mod, ln_g.reshape(1, d), ln_b.reshape(1, d), swgu_bf, swd_bf, out_tc)


def _split_block(n_rows):
    n_blocks = n_rows // ROW_TILE
    sc_blocks = (n_blocks * SC_DISPATCH_PERCENT) // 100 // 4 * 4
    return n_blocks - sc_blocks


def _max_slots(n_tok, bm):
    return (n_tok * TOP_K + N_EXPERTS * (bm - 1) + bm - 1) // bm * bm


def _moe(x1, h2, eidx, gates, rank, counts, mod, ln_g, ln_b, wgu, wd, layer, swgu, swd,
         n_rows, n_lat, seq, ctx_row, dn_alpha, split_blk):
    bm = MOE_BM
    t0 = split_blk * ROW_TILE
    rows_a, rows_b = _max_slots(t0, bm), _max_slots(n_rows - t0, bm)
    cnt = counts.reshape(2, N_EXPERTS).astype(I32)
    pcnt = (cnt + bm - 1) // bm * bm
    pend = jnp.cumsum(pcnt, axis=1)
    ustart = pend - pcnt + jnp.array([[0], [rows_a]], I32)
    dest = jnp.concatenate([_slots(ustart[0], eidx[:, :t0], rank[:, :t0]),
                            _slots(ustart[1], eidx[:, t0:], rank[:, t0:])], axis=1)

    nblk = (pcnt // bm).T.reshape(2 * N_EXPERTS)
    gb0 = jnp.cumsum(nblk) - nblk
    u0_grp = ustart.T.reshape(2 * N_EXPERTS)
    blk = jnp.arange((rows_a + rows_b) // bm, dtype=I32)
    grp = jnp.sum((gb0[None, :] <= blk[:, None]).astype(I32), axis=1) - 1
    u0_blk = u0_grp[grp] + (blk - gb0[grp]) * bm
    n_used = jnp.sum(nblk).astype(I32).reshape(1)

    xs_a = _dispatch(dest[:, :t0], h2, rows_a)
    xs_b = _sc_scatter_rows(h2, t0, dest[:, t0:] - rows_a, rows_b)
    ys = _grouped_ffn(xs_a, xs_b, gb0, nblk, u0_blk, n_used, wgu, wd, layer)
    return _combine(dest, gates.T, ys, x1, h2, mod, ln_g, ln_b, swgu, swd,
                    n_rows, n_lat, seq, ctx_row, dn_alpha)


def kernel(x, c, ctx, c_ctx, ada_w, ada_b, ln_g, ln_b, attn_w_qkv, attn_q_norm, attn_k_norm, attn_w_o, conv_w_in, conv_taps, conv_w_out, router_w, router_bias, exp_w_gate_up, exp_w_down, shared_w_gate_up, shared_w_down):
    b, seq, d = x.shape
    cl = ctx.shape[1]
    depth = ada_w.shape[0]
    n_lat = b * seq
    n_ctx = b * cl
    dn_alpha = (2 * depth) ** 0.25
    assert depth == 2 and b < MOD_ROWS
    assert seq % ROW_TILE == 0 and n_ctx % ROW_TILE == 0 and seq % GRID_W == 0

    cond = jnp.zeros((MOD_ROWS, d), F32).at[:b].set(c).at[b].set(c_ctx)
    mod = _modulation(cond, ada_w, ada_b)
    xall = jnp.concatenate([x.reshape(n_lat, d), ctx.reshape(n_ctx, d)], axis=0)
    n_all = n_lat + n_ctx

    q, k, v = _qkv_project(xall, mod[0], attn_w_qkv[0], attn_q_norm[0], attn_k_norm[0], n_lat, seq, b)
    o = _attention(q, k, v, b, seq, cl, n_lat)
    x1, h2, eidx, gates, rank, counts = _post_mixer(
        o, xall, mod[0], ln_g[0, 0], ln_b[0, 0], attn_w_o[0], router_w[0], router_bias[0],
        n_all, n_lat, seq, b, dn_alpha, _split_block(n_all))
    xall = _moe(x1, h2, eidx, gates, rank, counts, mod[0], ln_g[0, 1], ln_b[0, 1],
                exp_w_gate_up, exp_w_down, 0, shared_w_gate_up[0], shared_w_down[0],
                n_all, n_lat, seq, b, dn_alpha, _split_block(n_all))

    zin = _in_project(xall, mod[1], conv_w_in[0], n_lat, n_lat, seq, b)
    a = _short_conv(zin, conv_taps[0], b, seq, d)
    x1, h2, eidx, gates, rank, counts = _post_mixer(
        a, xall, mod[1], ln_g[1, 0], ln_b[1, 0], conv_w_out[0], router_w[1], router_bias[1],
        n_lat, n_lat, seq, b, dn_alpha, _split_block(n_lat))
    out = _moe(x1, h2, eidx, gates, rank, counts, mod[1], ln_g[1, 1], ln_b[1, 1],
               exp_w_gate_up, exp_w_down, 1, shared_w_gate_up[1], shared_w_down[1],
               n_lat, n_lat, seq, b, dn_alpha, _split_block(n_lat))
    return out.reshape(b, seq, d)
```

```python
import functools

import jax
import jax.numpy as jnp
from jax import lax
from jax.experimental import pallas as pl
from jax.experimental.pallas import tpu as pltpu
from jax.experimental.pallas import tpu_sc as plsc

F32 = jnp.float32
BF16 = jnp.bfloat16
I32 = jnp.int32

N_HEADS = 8
N_KV_HEADS = 2
HEAD_DIM = 128
KV_GROUP = N_HEADS // N_KV_HEADS
GRID_W = 64
ROPE_THETA = 10000.0
N_EXPERTS = 256
TOP_K = 8
N_GROUPS = 8
TOPK_GROUPS = 4
PER_GROUP = N_EXPERTS // N_GROUPS
ROUTED_SCALE = 2.5
LN_EPS = 1e-5
QK_EPS = 1e-6
N_MOD = 6
MOD_ROWS = 16

LANES = 128
SUBLANES = 8
VMEM_LIMIT = 56 * 1024 * 1024

ROW_TILE = 512
ATTN_TQ = 256
ATTN_TK = 2048
MOE_BM = 256
GMM_IN_SLOTS = 4
GMM_OUT_SLOTS = 3
DISPATCH_TOK = 512
COMBINE_GATHERED_TOK = 512
SC_GATHER_ROWS = 64
SC_SHARE_PERCENT = 82
SC_DISPATCH_PERCENT = 94
CONV_LANES = 128

HIGHEST = lax.Precision.HIGHEST
LOG2_E = 1.4426950408889634


def _cparams(sem):
    return pltpu.CompilerParams(dimension_semantics=sem, vmem_limit_bytes=VMEM_LIMIT)


def _silu(v):
    return v * jax.nn.sigmoid(v)


U32 = jnp.uint32
HI_MASK = 0xFFFF0000


def _pack_rows(v):
    half = v.shape[1] // 2
    lo = lax.bitcast_convert_type(v[:, :half].astype(BF16).astype(F32), U32)
    hi = lax.bitcast_convert_type(v[:, half:].astype(BF16).astype(F32), U32)
    return (lo >> 16) | (hi & U32(HI_MASK))


def _add_weighted_rows(y, gate, packed):
    half = y.shape[1] // 2
    lo = lax.bitcast_convert_type(packed << 16, F32)
    hi = lax.bitcast_convert_type(packed & U32(HI_MASK), F32)
    return jnp.concatenate([y[:, :half] + gate * lo, y[:, half:] + gate * hi], axis=-1)


def _unpack_rows_bf16(w):
    lo = lax.bitcast_convert_type(w << 16, F32)
    hi = lax.bitcast_convert_type(w & U32(HI_MASK), F32)
    return jnp.concatenate([lo.astype(BF16), hi.astype(BF16)], axis=-1)


def _mod_kernel(c_ref, w_ref, b_ref, o_ref):
    s = _silu(c_ref[...])
    o_ref[...] = jnp.dot(s, w_ref[...], precision=HIGHEST, preferred_element_type=F32) + b_ref[...]


def _modulation(cond, ada_w, ada_b):
    depth, d, nd = ada_w.shape
    tn = 1536
    out = pl.pallas_call(
        _mod_kernel,
        grid=(depth, nd // tn),
        in_specs=[
            pl.BlockSpec((MOD_ROWS, d), lambda l, j: (0, 0)),
            pl.BlockSpec((None, d, tn), lambda l, j: (l, 0, j)),
            pl.BlockSpec((None, 1, tn), lambda l, j: (l, 0, j)),
        ],
        out_specs=pl.BlockSpec((None, MOD_ROWS, tn), lambda l, j: (l, 0, j)),
        out_shape=jax.ShapeDtypeStruct((depth, MOD_ROWS, nd), F32),
        compiler_params=_cparams(("arbitrary", "arbitrary")),
        name="adaln_modulation",
    )(cond, ada_w, ada_b.reshape(depth, 1, nd))
    return out.reshape(depth, MOD_ROWS * N_MOD, 1, d)


def _mod_spec(comp, tm, n_lat, seq, ctx_row, d, blk_off=0):
    def index(i, *_):
        row0 = (i + blk_off) * tm
        r = jnp.where(row0 < n_lat, row0 // seq, ctx_row)
        return (r * N_MOD + comp, 0, 0)

    return pl.BlockSpec((None, 1, d), index)


def _qkv_kernel(x_ref, shift_ref, scale_ref, w_ref, qg_ref, kg_ref, cos_ref, sin_ref,
                q_ref, k_ref, v_ref):
    h = (x_ref[...] * (1.0 + scale_ref[...]) + shift_ref[...]).astype(BF16)
    qkv = jnp.dot(h, w_ref[...], preferred_element_type=F32)
    cos = cos_ref[...]
    sin = sin_ref[...]
    hq = N_HEADS * HEAD_DIM
    kd = N_KV_HEADS * HEAD_DIM

    def norm_rope(t, g, post):
        t = t * lax.rsqrt(jnp.mean(t * t, axis=-1, keepdims=True) + QK_EPS) * g
        t = t * cos + pltpu.roll(t, HEAD_DIM // 2, axis=1) * sin
        return (t * post).astype(BF16)

    for hd in range(N_HEADS):
        sl = slice(hd * HEAD_DIM, (hd + 1) * HEAD_DIM)
        q_ref[:, sl] = norm_rope(qkv[:, sl], qg_ref[...], HEAD_DIM ** -0.5 * LOG2_E)
    for hd in range(N_KV_HEADS):
        sl = slice(hd * HEAD_DIM, (hd + 1) * HEAD_DIM)
        k_ref[:, sl] = norm_rope(qkv[:, hq + hd * HEAD_DIM: hq + (hd + 1) * HEAD_DIM], kg_ref[...], 1.0)
    v_ref[...] = qkv[:, hq + kd:].astype(BF16)


def _rope_tables(seq, tm):
    rows = seq // GRID_W
    row = jnp.repeat(jnp.arange(rows, dtype=F32), GRID_W)
    col = jnp.tile(jnp.arange(GRID_W, dtype=F32), rows)
    axis_dim = HEAD_DIM // 2
    freqs = ROPE_THETA ** (-jnp.arange(0, axis_dim, 2, dtype=F32) / axis_dim)
    ang = jnp.concatenate([row[:, None] * freqs, col[:, None] * freqs], axis=-1)
    cos = jnp.concatenate([jnp.cos(ang), jnp.cos(ang)], axis=-1)
    sin = jnp.concatenate([-jnp.sin(ang), jnp.sin(ang)], axis=-1)
    cos = jnp.concatenate([cos, jnp.ones((tm, HEAD_DIM), F32)], axis=0)
    sin = jnp.concatenate([sin, jnp.zeros((tm, HEAD_DIM), F32)], axis=0)
    return cos.reshape(seq // tm + 1, tm, HEAD_DIM), sin.reshape(seq // tm + 1, tm, HEAD_DIM)


def _qkv_project(xall, mod, w_qkv, q_g, k_g, n_lat, seq, ctx_row):
    t, d = xall.shape
    tm = ROW_TILE
    hq = N_HEADS * HEAD_DIM
    kd = N_KV_HEADS * HEAD_DIM
    perm = jnp.concatenate([jnp.arange(0, HEAD_DIM, 2), jnp.arange(1, HEAD_DIM, 2)])
    cols = jnp.concatenate([hd * HEAD_DIM + perm for hd in range(N_HEADS + N_KV_HEADS)]
                           + [jnp.arange(hq + kd, hq + 2 * kd)])
    w = w_qkv[:, cols].astype(BF16)
    cos, sin = _rope_tables(seq, tm)
    n_pos = seq // tm

    def pos_index(i):
        row0 = i * tm
        return (jnp.where(row0 < n_lat, (row0 % seq) // tm, n_pos), 0, 0)

    const = lambda i: (0, 0)
    return pl.pallas_call(
        _qkv_kernel,
        grid=(t // tm,),
        in_specs=[
            pl.BlockSpec((tm, d), lambda i: (i, 0)),
            _mod_spec(0, tm, n_lat, seq, ctx_row, d),
            _mod_spec(1, tm, n_lat, seq, ctx_row, d),
            pl.BlockSpec(w.shape, const),
            pl.BlockSpec((1, HEAD_DIM), const),
            pl.BlockSpec((1, HEAD_DIM), const),
            pl.BlockSpec((None, tm, HEAD_DIM), pos_index),
            pl.BlockSpec((None, tm, HEAD_DIM), pos_index),
        ],
        out_specs=[
            pl.BlockSpec((tm, hq), lambda i: (i, 0)),
            pl.BlockSpec((tm, kd), lambda i: (i, 0)),
            pl.BlockSpec((tm, kd), lambda i: (i, 0)),
        ],
        out_shape=[
            jax.ShapeDtypeStruct((t, hq), BF16),
            jax.ShapeDtypeStruct((t, kd), BF16),
            jax.ShapeDtypeStruct((t, kd), BF16),
        ],
        compiler_params=_cparams(("parallel",)),
        name="qkv_norm_rope",
    )(xall, mod, mod, w, q_g[perm].reshape(1, HEAD_DIM), k_g[perm].reshape(1, HEAD_DIM), cos, sin)


def _attn_kernel(*refs, n_lat_chunks, tk):
    if n_lat_chunks:
        q_ref, kc_ref, vc_ref, kl_ref, vl_ref, o_ref = refs
    else:
        q_ref, kc_ref, vc_ref, o_ref = refs
    tq = q_ref.shape[0]
    q = jnp.concatenate([q_ref[:, h * HEAD_DIM:(h + 1) * HEAD_DIM] for h in range(KV_GROUP)], axis=0)
    rows = KV_GROUP * tq

    def chunk(k, v, m, l, acc):
        s = lax.dot_general(q, k, (((1,), (1,)), ((), ())), preferred_element_type=F32)
        m_new = jnp.maximum(m, jnp.max(s, axis=-1, keepdims=True))
        p = jnp.exp2(s - m_new)
        a = jnp.exp2(m - m_new)
        l = a * l + jnp.sum(p, axis=-1, keepdims=True)
        acc = a * acc + jnp.dot(p.astype(BF16), v, preferred_element_type=F32)
        return m_new, l, acc

    m = jnp.full((rows, 1), -jnp.inf, F32)
    l = jnp.zeros((rows, 1), F32)
    acc = jnp.zeros((rows, HEAD_DIM), F32)
    m, l, acc = chunk(kc_ref[...], vc_ref[...], m, l, acc)
    for c in range(n_lat_chunks):
        m, l, acc = chunk(kl_ref[c * tk:(c + 1) * tk, :], vl_ref[c * tk:(c + 1) * tk, :], m, l, acc)
    o = (acc / l).astype(BF16)
    for h in range(KV_GROUP):
        o_ref[:, h * HEAD_DIM:(h + 1) * HEAD_DIM] = o[h * tq:(h + 1) * tq]


def _attention(q, k, v, b, seq, cl, n_lat):
    t = q.shape[0]
    gw = KV_GROUP * HEAD_DIM
    tq = min(ATTN_TQ, seq)
    tk = min(ATTN_TK, seq)
    nq = seq // tq
    ctx_blk0 = n_lat // cl
    hq = N_HEADS * HEAD_DIM

    ctx_kv = pl.BlockSpec((cl, HEAD_DIM), lambda bi, g, qi: (ctx_blk0 + bi, g))
    lat_kv = pl.BlockSpec((seq, HEAD_DIM), lambda bi, g, qi: (bi, g))
    lat_q = pl.BlockSpec((tq, gw), lambda bi, g, qi: (bi * nq + qi, g))
    o_lat = pl.pallas_call(
        functools.partial(_attn_kernel, n_lat_chunks=seq // tk, tk=tk),
        grid=(b, N_KV_HEADS, nq),
        in_specs=[lat_q, ctx_kv, ctx_kv, lat_kv, lat_kv],
        out_specs=lat_q,
        out_shape=jax.ShapeDtypeStruct((n_lat, hq), BF16),
        compiler_params=_cparams(("parallel", "parallel", "arbitrary")),
        name="attention_latent",
    )(q, k, v, k, v)

    ctx_q = pl.BlockSpec((cl, gw), lambda bi, g: (ctx_blk0 + bi, g))
    ctx_kv2 = pl.BlockSpec((cl, HEAD_DIM), lambda bi, g: (ctx_blk0 + bi, g))
    o_ctx = pl.pallas_call(
        functools.partial(_attn_kernel, n_lat_chunks=0, tk=tk),
        grid=(b, N_KV_HEADS),
        in_specs=[ctx_q, ctx_kv2, ctx_kv2],
        out_specs=pl.BlockSpec((cl, gw), lambda bi, g: (bi, g)),
        out_shape=jax.ShapeDtypeStruct((t - n_lat, hq), BF16),
        compiler_params=_cparams(("parallel", "parallel")),
        name="attention_context",
    )(q, k, v)
    return jnp.concatenate([o_lat, o_ctx], axis=0)


def _in_proj_kernel(x_ref, shift_ref, scale_ref, w_ref, o_ref):
    h = (x_ref[...] * (1.0 + scale_ref[...]) + shift_ref[...]).astype(BF16)
    o_ref[...] = jnp.dot(h, w_ref[...], preferred_element_type=F32)


def _in_project(xall, mod, w_in, n_rows, n_lat, seq, ctx_row):
    d = xall.shape[1]
    n_out = w_in.shape[1]
    tm = ROW_TILE
    return pl.pallas_call(
        _in_proj_kernel,
        grid=(n_rows // tm,),
        in_specs=[
            pl.BlockSpec((tm, d), lambda i: (i, 0)),
            _mod_spec(0, tm, n_lat, seq, ctx_row, d),
            _mod_spec(1, tm, n_lat, seq, ctx_row, d),
            pl.BlockSpec(w_in.shape, lambda i: (0, 0)),
        ],
        out_specs=pl.BlockSpec((tm, n_out), lambda i: (i, 0)),
        out_shape=jax.ShapeDtypeStruct((n_rows, n_out), F32),
        compiler_params=_cparams(("parallel",)),
        name="conv_in_proj",
    )(xall, mod, mod, w_in.astype(BF16))


def _conv_kernel(bg_ref, cg_ref, v_ref, taps_ref, o_ref):
    u = cg_ref[...] * v_ref[...]
    n = u.shape[0]
    pos = lax.broadcasted_iota(I32, u.shape, 0)
    prev = jnp.where(pos == 0, 0.0, pltpu.roll(u, 1, axis=0))
    nxt = jnp.where(pos == n - 1, 0.0, pltpu.roll(u, n - 1, axis=0))
    conv = prev * taps_ref[0:1, :] + u * taps_ref[1:2, :] + nxt * taps_ref[2:3, :]
    o_ref[...] = (bg_ref[...] * conv).astype(BF16)


def _short_conv(zin, taps, n_seqs, seq, d):
    tc = CONV_LANES
    nj = d // tc
    return pl.pallas_call(
        _conv_kernel,
        grid=(n_seqs, nj),
        in_specs=[
            pl.BlockSpec((seq, tc), lambda s, j: (s, j)),
            pl.BlockSpec((seq, tc), lambda s, j: (s, nj + j)),
            pl.BlockSpec((seq, tc), lambda s, j: (s, 2 * nj + j)),
            pl.BlockSpec((taps.shape[0], tc), lambda s, j: (0, j)),
        ],
        out_specs=pl.BlockSpec((seq, tc), lambda s, j: (s, j)),
        out_shape=jax.ShapeDtypeStruct((n_seqs * seq, d), BF16),
        compiler_params=_cparams(("parallel", "parallel")),
        name="short_conv",
    )(zin, zin, zin, taps)


def _layer_norm(z, g, b):
    mu = jnp.mean(z, axis=-1, keepdims=True)
    zc = z - mu
    var = jnp.mean(zc * zc, axis=-1, keepdims=True)
    return zc * lax.rsqrt(var + LN_EPS) * g + b


def _post_kernel(a_ref, x_ref, gate_ref, lng_ref, lnb_ref, shift_ref, scale_ref, w_ref,
                 rwt_ref, rb_ref, tri_ref,
                 x1_ref, h2_ref, eidx_ref, gates_ref, rank_ref, cnt_ref, rwh_ref, rwl_ref, *,
                 dn_alpha, split_blk):
    tm = a_ref.shape[0]

    @pl.when(pl.program_id(0) == 0)
    def _():
        hi = rwt_ref[...].astype(BF16)
        rwh_ref[...] = hi
        rwl_ref[...] = (rwt_ref[...] - hi.astype(F32)).astype(BF16)

    d = x_ref.shape[1]
    y = jnp.dot(a_ref[...], w_ref[...], preferred_element_type=F32)
    x1 = _layer_norm(dn_alpha * x_ref[...] + gate_ref[...] * y, lng_ref[...], lnb_ref[...])
    x1_ref[...] = x1
    h2 = x1 * (1.0 + scale_ref[...]) + shift_ref[...]
    h2_ref[...] = _pack_rows(h2)

    h_hi = h2.astype(BF16)
    h_lo = (h2 - h_hi.astype(F32)).astype(BF16)

    def nt_dot(w, h):
        return lax.dot_general(w, h, (((1,), (1,)), ((), ())), preferred_element_type=F32)

    logits = nt_dot(rwh_ref[...], h_hi) + (nt_dot(rwl_ref[...], h_hi) + nt_dot(rwh_ref[...], h_lo))
    scores = jax.nn.sigmoid(logits)
    biased = scores + rb_ref[...]
    neg = -jnp.inf
    big = jnp.int32(1 << 30)
    row = lax.broadcasted_iota(I32, (N_EXPERTS, tm), 0)

    def argmax_rows(vals, idx):
        mx = jnp.max(vals, axis=0, keepdims=True)
        return mx, jnp.min(jnp.where(vals == mx, idx, big), axis=0, keepdims=True)

    gs = []
    grp_row = lax.broadcasted_iota(I32, (PER_GROUP, tm), 0)
    for g in range(N_GROUPS):
        bg = biased[g * PER_GROUP:(g + 1) * PER_GROUP]
        ig = grp_row + g * PER_GROUP
        m1, i1 = argmax_rows(bg, ig)
        m2 = jnp.max(jnp.where(ig == i1, neg, bg), axis=0, keepdims=True)
        gs.append(m1 + m2)
    gsc = jnp.concatenate(gs, axis=0)
    grow = lax.broadcasted_iota(I32, (N_GROUPS, tm), 0)
    gsel = jnp.zeros((N_GROUPS, tm), F32)
    for _ in range(TOPK_GROUPS):
        _, gi = argmax_rows(gsc, grow)
        hit = grow == gi
        gsel = jnp.where(hit, 1.0, gsel)
        gsc = jnp.where(hit, neg, gsc)
    cur = jnp.concatenate(
        [jnp.where(gsel[g:g + 1] > 0.0, biased[g * PER_GROUP:(g + 1) * PER_GROUP], neg)
         for g in range(N_GROUPS)], axis=0)

    onehot = jnp.zeros((N_EXPERTS, tm), F32)
    idxs, gvals = [], []
    for _ in range(TOP_K):
        _, ei = argmax_rows(cur, row)
        hit = row == ei
        gvals.append(jnp.sum(jnp.where(hit, scores, 0.0), axis=0, keepdims=True))
        idxs.append(ei)
        onehot = jnp.where(hit, 1.0, onehot)
        cur = jnp.where(hit, neg, cur)
    gv = jnp.concatenate(gvals, axis=0)
    gates_ref[...] = gv / jnp.sum(gv, axis=0, keepdims=True) * ROUTED_SCALE
    eidx_ref[...] = jnp.concatenate(idxs, axis=0)

    @pl.when((pl.program_id(0) == 0) | (pl.program_id(0) == split_blk))
    def _():
        cnt_ref[...] = jnp.zeros_like(cnt_ref)

    prefix = jnp.dot(onehot.astype(BF16), tri_ref[...], preferred_element_type=F32)
    pos = prefix + cnt_ref[...]
    rank_ref[...] = jnp.concatenate(
        [jnp.sum(jnp.where(row == ei, pos, 0.0), axis=0, keepdims=True) for ei in idxs],
        axis=0).astype(I32)
    cnt_ref[...] = cnt_ref[...] + jnp.sum(onehot, axis=1, keepdims=True)


def _post_mixer(a, xall, mod, ln_g, ln_b, w, router_w, router_bias, n_rows, n_lat, seq, ctx_row,
                dn_alpha, split_blk):
    d = xall.shape[1]
    tm = ROW_TILE
    tri = (lax.broadcasted_iota(I32, (tm, tm), 0) < lax.broadcasted_iota(I32, (tm, tm), 1)).astype(BF16)
    const = lambda i: (0, 0)
    row_blk = pl.BlockSpec((tm, d), lambda i: (i, 0))
    k_blk = pl.BlockSpec((TOP_K, tm), lambda i: (0, i))
    return pl.pallas_call(
        functools.partial(_post_kernel, dn_alpha=dn_alpha, split_blk=split_blk),
        grid=(n_rows // tm,),
        in_specs=[
            row_blk, row_blk,
            _mod_spec(2, tm, n_lat, seq, ctx_row, d),
            pl.BlockSpec((1, d), const), pl.BlockSpec((1, d), const),
            _mod_spec(3, tm, n_lat, seq, ctx_row, d),
            _mod_spec(4, tm, n_lat, seq, ctx_row, d),
            pl.BlockSpec(w.shape, const),
            pl.BlockSpec((N_EXPERTS, d), const),
            pl.BlockSpec((N_EXPERTS, 1), const),
            pl.BlockSpec((tm, tm), const),
        ],
        out_specs=[
            row_blk, pl.BlockSpec((tm, d // 2), lambda i: (i, 0)),
            k_blk, k_blk, k_blk,
            pl.BlockSpec((None, N_EXPERTS, 1), lambda i: (jnp.where(i >= split_blk, 1, 0), 0, 0)),
        ],
        out_shape=[
            jax.ShapeDtypeStruct((n_rows, d), F32),
            jax.ShapeDtypeStruct((n_rows, d // 2), U32),
            jax.ShapeDtypeStruct((TOP_K, n_rows), I32),
            jax.ShapeDtypeStruct((TOP_K, n_rows), F32),
            jax.ShapeDtypeStruct((TOP_K, n_rows), I32),
            jax.ShapeDtypeStruct((2, N_EXPERTS, 1), F32),
        ],
        scratch_shapes=[pltpu.VMEM((N_EXPERTS, d), BF16), pltpu.VMEM((N_EXPERTS, d), BF16)],
        compiler_params=_cparams(("arbitrary",)),
        name="post_mixer_router",
    )(a, xall, mod, ln_g.reshape(1, d), ln_b.reshape(1, d), mod, mod, w.astype(BF16),
      router_w.T, router_bias.reshape(N_EXPERTS, 1), tri)


def _slots_kernel(pstart_ref, eidx_ref, rank_ref, dest_ref):
    e = eidx_ref[...]

    def pick(i, acc):
        return jnp.where(e == i, pstart_ref[i], acc)

    dest_ref[...] = lax.fori_loop(0, N_EXPERTS, pick, jnp.zeros_like(e)) + rank_ref[...]


def _slots(pstarts, eidx, rank):
    n_tok = eidx.shape[1]
    tn = 2048 if n_tok % 2048 == 0 else ROW_TILE
    blk = pl.BlockSpec((TOP_K, tn), lambda i, ps: (0, i))
    return pl.pallas_call(
        _slots_kernel,
        grid_spec=pltpu.PrefetchScalarGridSpec(
            num_scalar_prefetch=1, grid=(n_tok // tn,), in_specs=[blk, blk], out_specs=blk),
        out_shape=jax.ShapeDtypeStruct((TOP_K, n_tok), I32),
        compiler_params=_cparams(("arbitrary",)),
        name="moe_slots",
    )(pstarts, eidx, rank)


def _dispatch_kernel(dest_ref, h_ref, xs_hbm, sem):
    td = dest_ref.shape[1]

    def issue(t, carry):
        for k in range(TOP_K):
            pltpu.make_async_copy(h_ref.at[pl.ds(t, 1)], xs_hbm.at[pl.ds(dest_ref[k, t], 1)], sem).start()
        return carry

    lax.fori_loop(0, td, issue, 0)
    pltpu.make_async_copy(xs_hbm.at[pl.ds(0, TOP_K * td)], xs_hbm.at[pl.ds(0, TOP_K * td)], sem).wait()


def _dispatch(dest, h2, n_slots):
    n_tok = dest.shape[1]
    d = h2.shape[1]
    td = DISPATCH_TOK
    return pl.pallas_call(
        _dispatch_kernel,
        grid=(n_tok // td,),
        in_specs=[
            pl.BlockSpec((TOP_K, td), lambda i: (0, i), memory_space=pltpu.SMEM),
            pl.BlockSpec((td, d), lambda i: (i, 0)),
        ],
        out_specs=pl.BlockSpec(memory_space=pl.ANY),
        out_shape=jax.ShapeDtypeStruct((n_slots, d), h2.dtype),
        scratch_shapes=[pltpu.SemaphoreType.DMA(())],
        compiler_params=_cparams(("arbitrary",)),
        name="moe_dispatch",
    )(dest, h2)


def _gmm_kernel(gb0_ref, nblk_ref, u0_ref, n_used_ref, xa_hbm, xb_hbm, wgu_ref, wd_ref, ys_hbm,
                xbuf, ybuf, wgu_bf, wd_bf, sem_in, sem_out):
    g = pl.program_id(0)
    n_in, bm = xbuf.shape[:2]
    n_out = ybuf.shape[0]
    ff = wd_ref.shape[0]
    n_used = n_used_ref[0]
    rows_a = xa_hbm.shape[0]

    def x_copy(src_hbm, row0, b):
        rows = pl.ds(row0 if isinstance(row0, int) else pl.multiple_of(row0, bm), bm)
        return pltpu.make_async_copy(src_hbm.at[rows], xbuf.at[b % n_in], sem_in.at[b % n_in])

    def x_start(b):
        u0 = u0_ref[b]

        @pl.when(u0 < rows_a)
        def _():
            x_copy(xa_hbm, u0, b).start()

        @pl.when(u0 >= rows_a)
        def _():
            x_copy(xb_hbm, u0 - rows_a, b).start()

    def x_wait(b):
        x_copy(xa_hbm, 0, b).wait()

    def y_copy(b, slot):
        rows = pl.ds(pl.multiple_of(u0_ref[b], bm), bm)
        return pltpu.make_async_copy(ybuf.at[slot], ys_hbm.at[rows], sem_out.at[slot])

    @pl.when(g == 0)
    def _():
        for b in range(n_in - 1):
            @pl.when(b < n_used)
            def _():
                x_start(b)

    nb = nblk_ref[2 * g] + nblk_ref[2 * g + 1]

    @pl.when(nb > 0)
    def _():
        wgu_bf[...] = wgu_ref[...].astype(BF16)
        wd_bf[...] = wd_ref[...].astype(BF16)

    def block(j, carry):
        b = gb0_ref[2 * g] + j
        slot = b % n_out
        x_wait(b)

        @pl.when(b + n_in - 1 < n_used)
        def _():
            x_start(b + n_in - 1)

        @pl.when(b >= n_out)
        def _():
            y_copy(b - n_out, slot).wait()

        gu = jnp.dot(_unpack_rows_bf16(xbuf[b % n_in]), wgu_bf[...], preferred_element_type=F32)
        act = (_silu(gu[:, :ff]) * gu[:, ff:]).astype(BF16)
        ybuf[slot] = _pack_rows(jnp.dot(act, wd_bf[...], preferred_element_type=F32))
        y_copy(b, slot).start()
        return carry

    lax.fori_loop(0, nb, block, 0)

    @pl.when(g == pl.num_programs(0) - 1)
    def _():
        for back in range(n_out, 0, -1):
            @pl.when(n_used >= back)
            def _():
                y_copy(n_used - back, (n_used - back) % n_out).wait()


def _grouped_ffn(xs_a, xs_b, gb0, nblk, u0_blk, n_used, wgu, wd, layer):
    n_slots = xs_a.shape[0] + xs_b.shape[0]
    bm = MOE_BM
    n_exp, d, ff2 = wgu.shape[1:]
    ff = wd.shape[2]
    grid_spec = pltpu.PrefetchScalarGridSpec(
        num_scalar_prefetch=4,
        grid=(n_exp,),
        in_specs=[
            pl.BlockSpec(memory_space=pl.ANY),
            pl.BlockSpec(memory_space=pl.ANY),
            pl.BlockSpec((None, None, d, ff2), lambda g, *_: (layer, g, 0, 0)),
            pl.BlockSpec((None, None, ff, d), lambda g, *_: (layer, g, 0, 0)),
        ],
        out_specs=pl.BlockSpec(memory_space=pl.ANY),
        scratch_shapes=[
            pltpu.VMEM((GMM_IN_SLOTS, bm, d // 2), U32), pltpu.VMEM((GMM_OUT_SLOTS, bm, d // 2), U32),
            pltpu.VMEM((d, ff2), BF16), pltpu.VMEM((ff, d), BF16),
            pltpu.SemaphoreType.DMA((GMM_IN_SLOTS,)), pltpu.SemaphoreType.DMA((GMM_OUT_SLOTS,)),
        ],
    )
    return pl.pallas_call(
        _gmm_kernel,
        grid_spec=grid_spec,
        out_shape=jax.ShapeDtypeStruct((n_slots, d // 2), U32),
        compiler_params=_cparams(("arbitrary",)),
        name="moe_grouped_ffn",
    )(gb0, nblk, u0_blk, n_used, xs_a, xs_b, wgu, wd)


def _shared_expert(h2_ref, swgu_ref, swd_ref):
    ff = swd_ref.shape[0]
    gu = jnp.dot(_unpack_rows_bf16(h2_ref[...]), swgu_ref[...], preferred_element_type=F32)
    act = (_silu(gu[:, :ff]) * gu[:, ff:]).astype(BF16)
    return jnp.dot(act, swd_ref[...], preferred_element_type=F32)


def _combine_kernel(dest_ref, gates_ref, ys_hbm, x1_ref, h2_ref, gate_ref, lng_ref, lnb_ref,
                    swgu_ref, swd_ref, o_ref, buf, sem, *, dn_alpha):
    tc = x1_ref.shape[0]

    def issue(t, carry):
        for k in range(TOP_K):
            pltpu.make_async_copy(ys_hbm.at[pl.ds(dest_ref[k, t], 1)], buf.at[k, pl.ds(t, 1)], sem).start()
        return carry

    lax.fori_loop(0, tc, issue, 0)

    y = _shared_expert(h2_ref, swgu_ref, swd_ref)

    for k in range(TOP_K):
        pltpu.make_async_copy(ys_hbm.at[pl.ds(0, tc)], buf.at[k], sem).wait()
    for k in range(TOP_K):
        y = _add_weighted_rows(y, gates_ref[:, k:k + 1], buf[k])
    o_ref[...] = _layer_norm(dn_alpha * x1_ref[...] + gate_ref[...] * y, lng_ref[...], lnb_ref[...])


def _combine_gathered_kernel(gates_ref, g_ref, x1_ref, h2_ref, gate_ref, lng_ref, lnb_ref,
                             swgu_ref, swd_ref, prev_ref, o_ref, *, dn_alpha):
    del prev_ref
    y = _shared_expert(h2_ref, swgu_ref, swd_ref)
    for k in range(TOP_K):
        y = _add_weighted_rows(y, gates_ref[:, k:k + 1], g_ref[k])
    o_ref[...] = _layer_norm(dn_alpha * x1_ref[...] + gate_ref[...] * y, lng_ref[...], lnb_ref[...])


def _sc_gather_rows(table, idx):
    n = idx.shape[0]
    d = table.shape[1]
    mesh = plsc.VectorSubcoreMesh(core_axis_name="c", subcore_axis_name="s")
    n_workers = mesh.num_cores * mesh.num_subcores
    ch = SC_GATHER_ROWS
    per_w = n // n_workers
    n_ch = per_w // ch
    assert n % (n_workers * ch * 2) == 0

    @functools.partial(
        pl.kernel, mesh=mesh, out_type=jax.ShapeDtypeStruct((n, d), table.dtype),
        scratch_types=[pltpu.VMEM((ch,), I32), pltpu.VMEM((ch,), I32),
                       pltpu.VMEM((ch, d), table.dtype), pltpu.VMEM((ch, d), table.dtype),
                       pltpu.SemaphoreType.DMA, pltpu.SemaphoreType.DMA],
        name="moe_sc_gather")
    def gather(table_hbm, idx_hbm, out_hbm, idx_a, idx_b, rows_a, rows_b, sem_a, sem_b):
        wid = lax.axis_index("s") * mesh.num_cores + lax.axis_index("c")

        def chunk(i):
            return pl.ds(wid * per_w + i * ch, ch)

        def fetch(i, idx_v, rows_v, sem):
            pltpu.sync_copy(idx_hbm.at[chunk(i)], idx_v)
            pltpu.make_async_copy(table_hbm.at[idx_v], rows_v, sem).start()

        def drain(i, idx_v, rows_v, sem):
            pltpu.make_async_copy(table_hbm.at[idx_v], rows_v, sem).wait()
            pltpu.sync_copy(rows_v, out_hbm.at[chunk(i)])

        fetch(0, idx_a, rows_a, sem_a)

        @pl.loop(0, n_ch // 2)
        def _(j):
            i = 2 * j
            fetch(i + 1, idx_b, rows_b, sem_b)
            drain(i, idx_a, rows_a, sem_a)

            @pl.when(i + 2 < n_ch)
            def _():
                fetch(i + 2, idx_a, rows_a, sem_a)

            drain(i + 1, idx_b, rows_b, sem_b)

    return gather(table, idx)


def _sc_scatter_rows(rows, first_row, idx, n_out):
    n_k, n = idx.shape
    d = rows.shape[1]
    mesh = plsc.VectorSubcoreMesh(core_axis_name="c", subcore_axis_name="s")
    n_workers = mesh.num_cores * mesh.num_subcores
    ch = SC_GATHER_ROWS
    per_w = n // n_workers
    assert n % (n_workers * ch) == 0

    @functools.partial(
        pl.kernel, mesh=mesh, out_type=jax.ShapeDtypeStruct((n_out, d), rows.dtype),
        scratch_types=[pltpu.VMEM((ch,), I32), pltpu.VMEM((ch, d), rows.dtype)],
        name="moe_sc_scatter")
    def scatter(rows_hbm, idx_hbm, out_hbm, idx_v, rows_v):
        wid = lax.axis_index("s") * mesh.num_cores + lax.axis_index("c")

        @pl.loop(0, per_w // ch)
        def _(i):
            base = wid * per_w + i * ch
            pltpu.sync_copy(rows_hbm.at[pl.ds(first_row + base, ch)], rows_v)
            for k in range(n_k):
                pltpu.sync_copy(idx_hbm.at[k, pl.ds(base, ch)], idx_v)
                pltpu.sync_copy(rows_v, out_hbm.at[idx_v])

    return scatter(rows, idx)


def _combine(dest, gates_t, ys, x1, h2, mod, ln_g, ln_b, swgu, swd, n_rows, n_lat, seq, ctx_row,
             dn_alpha):
    d = x1.shape[1]
    tc = DISPATCH_TOK
    const = lambda i: (0, 0)
    n_blocks = n_rows // tc
    n_sc_blocks = (n_blocks * SC_SHARE_PERCENT) // 100
    n_tc = (n_blocks - n_sc_blocks) * tc
    n_sc = n_rows - n_tc
    gathered = _sc_gather_rows(ys, dest[:, n_tc:].reshape(TOP_K * n_sc)).reshape(TOP_K, n_sc, d // 2)
    swgu_bf, swd_bf = swgu.astype(BF16), swd.astype(BF16)
    out_tc = pl.pallas_call(
        functools.partial(_combine_kernel, dn_alpha=dn_alpha),
        grid=(n_tc // tc,),
        in_specs=[
            pl.BlockSpec((TOP_K, tc), lambda i: (0, i), memory_space=pltpu.SMEM),
            pl.BlockSpec((tc, TOP_K), lambda i: (i, 0)),
            pl.BlockSpec(memory_space=pl.ANY),
            pl.BlockSpec((tc, d), lambda i: (i, 0)),
            pl.BlockSpec((tc, d // 2), lambda i: (i, 0)),
            _mod_spec(5, tc, n_lat, seq, ctx_row, d),
            pl.BlockSpec((1, d), const), pl.BlockSpec((1, d), const),
            pl.BlockSpec(swgu.shape, const),
            pl.BlockSpec(swd.shape, const),
        ],
        out_specs=pl.BlockSpec((tc, d), lambda i: (i, 0)),
        out_shape=jax.ShapeDtypeStruct((n_rows, d), F32),
        scratch_shapes=[pltpu.VMEM((TOP_K, tc, d // 2), U32), pltpu.SemaphoreType.DMA(())],
        compiler_params=_cparams(("arbitrary",)),
        name="moe_combine",
    )(dest, gates_t, ys, x1, h2, mod, ln_g.reshape(1, d), ln_b.reshape(1, d), swgu_bf, swd_bf)

    tl = COMBINE_GATHERED_TOK
    off = n_tc // tl
    row_blk = pl.BlockSpec((tl, d), lambda i: (i + off, 0))
    return pl.pallas_call(
        functools.partial(_combine_gathered_kernel, dn_alpha=dn_alpha),
        grid=(n_sc // tl,),
        in_specs=[
            pl.BlockSpec((tl, TOP_K), lambda i: (i + off, 0)),
            pl.BlockSpec((TOP_K, tl, d // 2), lambda i: (0, i, 0)),
            row_blk, pl.BlockSpec((tl, d // 2), lambda i: (i + off, 0)),
            _mod_spec(5, tl, n_lat, seq, ctx_row, d, blk_off=off),
            pl.BlockSpec((1, d), const), pl.BlockSpec((1, d), const),
            pl.BlockSpec(swgu.shape, const),
            pl.BlockSpec(swd.shape, const),
            pl.BlockSpec(memory_space=pl.ANY),
        ],
        out_specs=row_blk,
        out_shape=jax.ShapeDtypeStruct((n_rows, d), F32),
        input_output_aliases={9: 0},
        compiler_params=_cparams(("parallel",)),
        name="moe_combine_gathered",
    )(gates_t, gathered, x1, h2, mod, ln_g.reshape(1, d), ln_b.reshape(1, d), swgu_bf, swd_bf, out_tc)


def _split_block(n_rows):
    n_blocks = n_rows // ROW_TILE
    sc_blocks = (n_blocks * SC_DISPATCH_PERCENT) // 100 // 4 * 4
    return n_blocks - sc_blocks


def _max_slots(n_tok, bm):
    return (n_tok * TOP_K + N_EXPERTS * (bm - 1) + bm - 1) // bm * bm


def _moe(x1, h2, eidx, gates, rank, counts, mod, ln_g, ln_b, wgu, wd, layer, swgu, swd,
         n_rows, n_lat, seq, ctx_row, dn_alpha, split_blk):
    bm = MOE_BM
    t0 = split_blk * ROW_TILE
    rows_a, rows_b = _max_slots(t0, bm), _max_slots(n_rows - t0, bm)
    cnt = counts.reshape(2, N_EXPERTS).astype(I32)
    pcnt = (cnt + bm - 1) // bm * bm
    pend = jnp.cumsum(pcnt, axis=1)
    ustart = pend - pcnt + jnp.array([[0], [rows_a]], I32)
    dest = jnp.concatenate([_slots(ustart[0], eidx[:, :t0], rank[:, :t0]),
                            _slots(ustart[1], eidx[:, t0:], rank[:, t0:])], axis=1)

    nblk = (pcnt // bm).T.reshape(2 * N_EXPERTS)
    gb0 = jnp.cumsum(nblk) - nblk
    u0_grp = ustart.T.reshape(2 * N_EXPERTS)
    blk = jnp.arange((rows_a + rows_b) // bm, dtype=I32)
    grp = jnp.sum((gb0[None, :] <= blk[:, None]).astype(I32), axis=1) - 1
    u0_blk = u0_grp[grp] + (blk - gb0[grp]) * bm
    n_used = jnp.sum(nblk).astype(I32).reshape(1)

    xs_a = _dispatch(dest[:, :t0], h2, rows_a)
    xs_b = _sc_scatter_rows(h2, t0, dest[:, t0:] - rows_a, rows_b)
    ys = _grouped_ffn(xs_a, xs_b, gb0, nblk, u0_blk, n_used, wgu, wd, layer)
    return _combine(dest, gates.T, ys, x1, h2, mod, ln_g, ln_b, swgu, swd,
                    n_rows, n_lat, seq, ctx_row, dn_alpha)


def kernel(x, c, ctx, c_ctx, ada_w, ada_b, ln_g, ln_b, attn_w_qkv, attn_q_norm, attn_k_norm, attn_w_o, conv_w_in, conv_taps, conv_w_out, router_w, router_bias, exp_w_gate_up, exp_w_down, shared_w_gate_up, shared_w_down):
    b, seq, d = x.shape
    cl = ctx.shape[1]
    depth = ada_w.shape[0]
    n_lat = b * seq
    n_ctx = b * cl
    dn_alpha = (2 * depth) ** 0.25
    assert depth == 2 and b < MOD_ROWS
    assert seq % ROW_TILE == 0 and n_ctx % ROW_TILE == 0 and seq % GRID_W == 0

    cond = jnp.zeros((MOD_ROWS, d), F32).at[:b].set(c).at[b].set(c_ctx)
    mod = _modulation(cond, ada_w, ada_b)
    xall = jnp.concatenate([x.reshape(n_lat, d), ctx.reshape(n_ctx, d)], axis=0)
    n_all = n_lat + n_ctx

    q, k, v = _qkv_project(xall, mod[0], attn_w_qkv[0], attn_q_norm[0], attn_k_norm[0], n_lat, seq, b)
    o = _attention(q, k, v, b, seq, cl, n_lat)
    x1, h2, eidx, gates, rank, counts = _post_mixer(
        o, xall, mod[0], ln_g[0, 0], ln_b[0, 0], attn_w_o[0], router_w[0], router_bias[0],
        n_all, n_lat, seq, b, dn_alpha, _split_block(n_all))
    xall = _moe(x1, h2, eidx, gates, rank, counts, mod[0], ln_g[0, 1], ln_b[0, 1],
                exp_w_gate_up, exp_w_down, 0, shared_w_gate_up[0], shared_w_down[0],
                n_all, n_lat, seq, b, dn_alpha, _split_block(n_all))

    zin = _in_project(xall, mod[1], conv_w_in[0], n_lat, n_lat, seq, b)
    a = _short_conv(zin, conv_taps[0], b, seq, d)
    x1, h2, eidx, gates, rank, counts = _post_mixer(
        a, xall, mod[1], ln_g[1, 0], ln_b[1, 0], conv_w_out[0], router_w[1], router_bias[1],
        n_lat, n_lat, seq, b, dn_alpha, _split_block(n_lat))
    out = _moe(x1, h2, eidx, gates, rank, counts, mod[1], ln_g[1, 1], ln_b[1, 1],
               exp_w_gate_up, exp_w_down, 1, shared_w_gate_up[1], shared_w_down[1],
               n_lat, n_lat, seq, b, dn_alpha, _split_block(n_lat))
    return out.reshape(b, seq, d)
```

```python
import functools

import jax
import jax.numpy as jnp
from jax import lax
from jax.experimental import pallas as pl
from jax.experimental.pallas import tpu as pltpu
from jax.experimental.pallas import tpu_sc as plsc

F32 = jnp.float32
BF16 = jnp.bfloat16
I32 = jnp.int32

N_HEADS = 8
N_KV_HEADS = 2
HEAD_DIM = 128
KV_GROUP = N_HEADS // N_KV_HEADS
GRID_W = 64
ROPE_THETA = 10000.0
N_EXPERTS = 256
TOP_K = 8
N_GROUPS = 8
TOPK_GROUPS = 4
PER_GROUP = N_EXPERTS // N_GROUPS
ROUTED_SCALE = 2.5
LN_EPS = 1e-5
QK_EPS = 1e-6
N_MOD = 6
MOD_ROWS = 16

LANES = 128
SUBLANES = 8
VMEM_LIMIT = 56 * 1024 * 1024

ROW_TILE = 512
ATTN_TQ = 256
ATTN_TK = 2048
MOE_BM = 256
GMM_IN_SLOTS = 4
GMM_OUT_SLOTS = 3
DISPATCH_TOK = 512
COMBINE_GATHERED_TOK = 512
SC_GATHER_ROWS = 64
SC_SHARE_PERCENT = 82
SC_DISPATCH_PERCENT = 100
CONV_LANES = 128

HIGHEST = lax.Precision.HIGHEST
LOG2_E = 1.4426950408889634


def _cparams(sem):
    return pltpu.CompilerParams(dimension_semantics=sem, vmem_limit_bytes=VMEM_LIMIT)


def _silu(v):
    return v * jax.nn.sigmoid(v)


U32 = jnp.uint32
HI_MASK = 0xFFFF0000


def _pack_rows(v):
    half = v.shape[1] // 2
    lo = lax.bitcast_convert_type(v[:, :half].astype(BF16).astype(F32), U32)
    hi = lax.bitcast_convert_type(v[:, half:].astype(BF16).astype(F32), U32)
    return (lo >> 16) | (hi & U32(HI_MASK))


def _add_weighted_rows(y, gate, packed):
    half = y.shape[1] // 2
    lo = lax.bitcast_convert_type(packed << 16, F32)
    hi = lax.bitcast_convert_type(packed & U32(HI_MASK), F32)
    return jnp.concatenate([y[:, :half] + gate * lo, y[:, half:] + gate * hi], axis=-1)


def _unpack_rows_bf16(w):
    lo = lax.bitcast_convert_type(w << 16, F32)
    hi = lax.bitcast_convert_type(w & U32(HI_MASK), F32)
    return jnp.concatenate([lo.astype(BF16), hi.astype(BF16)], axis=-1)


def _mod_kernel(c_ref, w_ref, b_ref, o_ref):
    s = _silu(c_ref[...])
    o_ref[...] = jnp.dot(s, w_ref[...], precision=HIGHEST, preferred_element_type=F32) + b_ref[...]


def _modulation(cond, ada_w, ada_b):
    depth, d, nd = ada_w.shape
    tn = 1536
    out = pl.pallas_call(
        _mod_kernel,
        grid=(depth, nd // tn),
        in_specs=[
            pl.BlockSpec((MOD_ROWS, d), lambda l, j: (0, 0)),
            pl.BlockSpec((None, d, tn), lambda l, j: (l, 0, j)),
            pl.BlockSpec((None, 1, tn), lambda l, j: (l, 0, j)),
        ],
        out_specs=pl.BlockSpec((None, MOD_ROWS, tn), lambda l, j: (l, 0, j)),
        out_shape=jax.ShapeDtypeStruct((depth, MOD_ROWS, nd), F32),
        compiler_params=_cparams(("arbitrary", "arbitrary")),
        name="adaln_modulation",
    )(cond, ada_w, ada_b.reshape(depth, 1, nd))
    return out.reshape(depth, MOD_ROWS * N_MOD, 1, d)


def _mod_spec(comp, tm, n_lat, seq, ctx_row, d, blk_off=0):
    def index(i, *_):
        row0 = (i + blk_off) * tm
        r = jnp.where(row0 < n_lat, row0 // seq, ctx_row)
        return (r * N_MOD + comp, 0, 0)

    return pl.BlockSpec((None, 1, d), index)


def _qkv_kernel(x_ref, shift_ref, scale_ref, w_ref, qg_ref, kg_ref, cos_ref, sin_ref,
                q_ref, k_ref, v_ref):
    h = (x_ref[...] * (1.0 + scale_ref[...]) + shift_ref[...]).astype(BF16)
    qkv = jnp.dot(h, w_ref[...], preferred_element_type=F32)
    cos = cos_ref[...]
    sin = sin_ref[...]
    hq = N_HEADS * HEAD_DIM
    kd = N_KV_HEADS * HEAD_DIM

    def norm_rope(t, g, post):
        t = t * lax.rsqrt(jnp.mean(t * t, axis=-1, keepdims=True) + QK_EPS) * g
        t = t * cos + pltpu.roll(t, HEAD_DIM // 2, axis=1) * sin
        return (t * post).astype(BF16)

    for hd in range(N_HEADS):
        sl = slice(hd * HEAD_DIM, (hd + 1) * HEAD_DIM)
        q_ref[:, sl] = norm_rope(qkv[:, sl], qg_ref[...], HEAD_DIM ** -0.5 * LOG2_E)
    for hd in range(N_KV_HEADS):
        sl = slice(hd * HEAD_DIM, (hd + 1) * HEAD_DIM)
        k_ref[:, sl] = norm_rope(qkv[:, hq + hd * HEAD_DIM: hq + (hd + 1) * HEAD_DIM], kg_ref[...], 1.0)
    v_ref[...] = qkv[:, hq + kd:].astype(BF16)


def _rope_tables(seq, tm):
    rows = seq // GRID_W
    row = jnp.repeat(jnp.arange(rows, dtype=F32), GRID_W)
    col = jnp.tile(jnp.arange(GRID_W, dtype=F32), rows)
    axis_dim = HEAD_DIM // 2
    freqs = ROPE_THETA ** (-jnp.arange(0, axis_dim, 2, dtype=F32) / axis_dim)
    ang = jnp.concatenate([row[:, None] * freqs, col[:, None] * freqs], axis=-1)
    cos = jnp.concatenate([jnp.cos(ang), jnp.cos(ang)], axis=-1)
    sin = jnp.concatenate([-jnp.sin(ang), jnp.sin(ang)], axis=-1)
    cos = jnp.concatenate([cos, jnp.ones((tm, HEAD_DIM), F32)], axis=0)
    sin = jnp.concatenate([sin, jnp.zeros((tm, HEAD_DIM), F32)], axis=0)
    return cos.reshape(seq // tm + 1, tm, HEAD_DIM), sin.reshape(seq // tm + 1, tm, HEAD_DIM)


def _qkv_project(xall, mod, w_qkv, q_g, k_g, n_lat, seq, ctx_row):
    t, d = xall.shape
    tm = ROW_TILE
    hq = N_HEADS * HEAD_DIM
    kd = N_KV_HEADS * HEAD_DIM
    perm = jnp.concatenate([jnp.arange(0, HEAD_DIM, 2), jnp.arange(1, HEAD_DIM, 2)])
    cols = jnp.concatenate([hd * HEAD_DIM + perm for hd in range(N_HEADS + N_KV_HEADS)]
                           + [jnp.arange(hq + kd, hq + 2 * kd)])
    w = w_qkv[:, cols].astype(BF16)
    cos, sin = _rope_tables(seq, tm)
    n_pos = seq // tm

    def pos_index(i):
        row0 = i * tm
        return (jnp.where(row0 < n_lat, (row0 % seq) // tm, n_pos), 0, 0)

    const = lambda i: (0, 0)
    return pl.pallas_call(
        _qkv_kernel,
        grid=(t // tm,),
        in_specs=[
            pl.BlockSpec((tm, d), lambda i: (i, 0)),
            _mod_spec(0, tm, n_lat, seq, ctx_row, d),
            _mod_spec(1, tm, n_lat, seq, ctx_row, d),
            pl.BlockSpec(w.shape, const),
            pl.BlockSpec((1, HEAD_DIM), const),
            pl.BlockSpec((1, HEAD_DIM), const),
            pl.BlockSpec((None, tm, HEAD_DIM), pos_index),
            pl.BlockSpec((None, tm, HEAD_DIM), pos_index),
        ],
        out_specs=[
            pl.BlockSpec((tm, hq), lambda i: (i, 0)),
            pl.BlockSpec((tm, kd), lambda i: (i, 0)),
            pl.BlockSpec((tm, kd), lambda i: (i, 0)),
        ],
        out_shape=[
            jax.ShapeDtypeStruct((t, hq), BF16),
            jax.ShapeDtypeStruct((t, kd), BF16),
            jax.ShapeDtypeStruct((t, kd), BF16),
        ],
        compiler_params=_cparams(("parallel",)),
        name="qkv_norm_rope",
    )(xall, mod, mod, w, q_g[perm].reshape(1, HEAD_DIM), k_g[perm].reshape(1, HEAD_DIM), cos, sin)


def _attn_kernel(*refs, n_lat_chunks, tk):
    if n_lat_chunks:
        q_ref, kc_ref, vc_ref, kl_ref, vl_ref, o_ref = refs
    else:
        q_ref, kc_ref, vc_ref, o_ref = refs
    tq = q_ref.shape[0]
    q = jnp.concatenate([q_ref[:, h * HEAD_DIM:(h + 1) * HEAD_DIM] for h in range(KV_GROUP)], axis=0)
    rows = KV_GROUP * tq

    def chunk(k, v, m, l, acc):
        s = lax.dot_general(q, k, (((1,), (1,)), ((), ())), preferred_element_type=F32)
        m_new = jnp.maximum(m, jnp.max(s, axis=-1, keepdims=True))
        p = jnp.exp2(s - m_new)
        a = jnp.exp2(m - m_new)
        l = a * l + jnp.sum(p, axis=-1, keepdims=True)
        acc = a * acc + jnp.dot(p.astype(BF16), v, preferred_element_type=F32)
        return m_new, l, acc

    m = jnp.full((rows, 1), -jnp.inf, F32)
    l = jnp.zeros((rows, 1), F32)
    acc = jnp.zeros((rows, HEAD_DIM), F32)
    m, l, acc = chunk(kc_ref[...], vc_ref[...], m, l, acc)
    for c in range(n_lat_chunks):
        m, l, acc = chunk(kl_ref[c * tk:(c + 1) * tk, :], vl_ref[c * tk:(c + 1) * tk, :], m, l, acc)
    o = (acc / l).astype(BF16)
    for h in range(KV_GROUP):
        o_ref[:, h * HEAD_DIM:(h + 1) * HEAD_DIM] = o[h * tq:(h + 1) * tq]


def _attention(q, k, v, b, seq, cl, n_lat):
    t = q.shape[0]
    gw = KV_GROUP * HEAD_DIM
    tq = min(ATTN_TQ, seq)
    tk = min(ATTN_TK, seq)
    nq = seq // tq
    ctx_blk0 = n_lat // cl
    hq = N_HEADS * HEAD_DIM

    ctx_kv = pl.BlockSpec((cl, HEAD_DIM), lambda bi, g, qi: (ctx_blk0 + bi, g))
    lat_kv = pl.BlockSpec((seq, HEAD_DIM), lambda bi, g, qi: (bi, g))
    lat_q = pl.BlockSpec((tq, gw), lambda bi, g, qi: (bi * nq + qi, g))
    o_lat = pl.pallas_call(
        functools.partial(_attn_kernel, n_lat_chunks=seq // tk, tk=tk),
        grid=(b, N_KV_HEADS, nq),
        in_specs=[lat_q, ctx_kv, ctx_kv, lat_kv, lat_kv],
        out_specs=lat_q,
        out_shape=jax.ShapeDtypeStruct((n_lat, hq), BF16),
        compiler_params=_cparams(("parallel", "parallel", "arbitrary")),
        name="attention_latent",
    )(q, k, v, k, v)

    ctx_q = pl.BlockSpec((cl, gw), lambda bi, g: (ctx_blk0 + bi, g))
    ctx_kv2 = pl.BlockSpec((cl, HEAD_DIM), lambda bi, g: (ctx_blk0 + bi, g))
    o_ctx = pl.pallas_call(
        functools.partial(_attn_kernel, n_lat_chunks=0, tk=tk),
        grid=(b, N_KV_HEADS),
        in_specs=[ctx_q, ctx_kv2, ctx_kv2],
        out_specs=pl.BlockSpec((cl, gw), lambda bi, g: (bi, g)),
        out_shape=jax.ShapeDtypeStruct((t - n_lat, hq), BF16),
        compiler_params=_cparams(("parallel", "parallel")),
        name="attention_context",
    )(q, k, v)
    return jnp.concatenate([o_lat, o_ctx], axis=0)


def _in_proj_kernel(x_ref, shift_ref, scale_ref, w_ref, o_ref):
    h = (x_ref[...] * (1.0 + scale_ref[...]) + shift_ref[...]).astype(BF16)
    o_ref[...] = jnp.dot(h, w_ref[...], preferred_element_type=F32)


def _in_project(xall, mod, w_in, n_rows, n_lat, seq, ctx_row):
    d = xall.shape[1]
    n_out = w_in.shape[1]
    tm = ROW_TILE
    return pl.pallas_call(
        _in_proj_kernel,
        grid=(n_rows // tm,),
        in_specs=[
            pl.BlockSpec((tm, d), lambda i: (i, 0)),
            _mod_spec(0, tm, n_lat, seq, ctx_row, d),
            _mod_spec(1, tm, n_lat, seq, ctx_row, d),
            pl.BlockSpec(w_in.shape, lambda i: (0, 0)),
        ],
        out_specs=pl.BlockSpec((tm, n_out), lambda i: (i, 0)),
        out_shape=jax.ShapeDtypeStruct((n_rows, n_out), F32),
        compiler_params=_cparams(("parallel",)),
        name="conv_in_proj",
    )(xall, mod, mod, w_in.astype(BF16))


def _conv_kernel(bg_ref, cg_ref, v_ref, taps_ref, o_ref):
    u = cg_ref[...] * v_ref[...]
    n = u.shape[0]
    pos = lax.broadcasted_iota(I32, u.shape, 0)
    prev = jnp.where(pos == 0, 0.0, pltpu.roll(u, 1, axis=0))
    nxt = jnp.where(pos == n - 1, 0.0, pltpu.roll(u, n - 1, axis=0))
    conv = prev * taps_ref[0:1, :] + u * taps_ref[1:2, :] + nxt * taps_ref[2:3, :]
    o_ref[...] = (bg_ref[...] * conv).astype(BF16)


def _short_conv(zin, taps, n_seqs, seq, d):
    tc = CONV_LANES
    nj = d // tc
    return pl.pallas_call(
        _conv_kernel,
        grid=(n_seqs, nj),
        in_specs=[
            pl.BlockSpec((seq, tc), lambda s, j: (s, j)),
            pl.BlockSpec((seq, tc), lambda s, j: (s, nj + j)),
            pl.BlockSpec((seq, tc), lambda s, j: (s, 2 * nj + j)),
            pl.BlockSpec((taps.shape[0], tc), lambda s, j: (0, j)),
        ],
        out_specs=pl.BlockSpec((seq, tc), lambda s, j: (s, j)),
        out_shape=jax.ShapeDtypeStruct((n_seqs * seq, d), BF16),
        compiler_params=_cparams(("parallel", "parallel")),
        name="short_conv",
    )(zin, zin, zin, taps)


def _layer_norm(z, g, b):
    mu = jnp.mean(z, axis=-1, keepdims=True)
    zc = z - mu
    var = jnp.mean(zc * zc, axis=-1, keepdims=True)
    return zc * lax.rsqrt(var + LN_EPS) * g + b


def _post_kernel(a_ref, x_ref, gate_ref, lng_ref, lnb_ref, shift_ref, scale_ref, w_ref,
                 rwt_ref, rb_ref, tri_ref,
                 x1_ref, h2_ref, eidx_ref, gates_ref, rank_ref, cnt_ref, rwh_ref, rwl_ref, *,
                 dn_alpha, split_blk):
    tm = a_ref.shape[0]

    @pl.when(pl.program_id(0) == 0)
    def _():
        hi = rwt_ref[...].astype(BF16)
        rwh_ref[...] = hi
        rwl_ref[...] = (rwt_ref[...] - hi.astype(F32)).astype(BF16)

    d = x_ref.shape[1]
    y = jnp.dot(a_ref[...], w_ref[...], preferred_element_type=F32)
    x1 = _layer_norm(dn_alpha * x_ref[...] + gate_ref[...] * y, lng_ref[...], lnb_ref[...])
    x1_ref[...] = x1
    h2 = x1 * (1.0 + scale_ref[...]) + shift_ref[...]
    h2_ref[...] = _pack_rows(h2)

    h_hi = h2.astype(BF16)
    h_lo = (h2 - h_hi.astype(F32)).astype(BF16)

    def nt_dot(w, h):
        return lax.dot_general(w, h, (((1,), (1,)), ((), ())), preferred_element_type=F32)

    logits = nt_dot(rwh_ref[...], h_hi) + (nt_dot(rwl_ref[...], h_hi) + nt_dot(rwh_ref[...], h_lo))
    scores = jax.nn.sigmoid(logits)
    biased = scores + rb_ref[...]
    neg = -jnp.inf
    big = jnp.int32(1 << 30)
    row = lax.broadcasted_iota(I32, (N_EXPERTS, tm), 0)

    def argmax_rows(vals, idx):
        mx = jnp.max(vals, axis=0, keepdims=True)
        return mx, jnp.min(jnp.where(vals == mx, idx, big), axis=0, keepdims=True)

    gs = []
    grp_row = lax.broadcasted_iota(I32, (PER_GROUP, tm), 0)
    for g in range(N_GROUPS):
        bg = biased[g * PER_GROUP:(g + 1) * PER_GROUP]
        ig = grp_row + g * PER_GROUP
        m1, i1 = argmax_rows(bg, ig)
        m2 = jnp.max(jnp.where(ig == i1, neg, bg), axis=0, keepdims=True)
        gs.append(m1 + m2)
    gsc = jnp.concatenate(gs, axis=0)
    grow = lax.broadcasted_iota(I32, (N_GROUPS, tm), 0)
    gsel = jnp.zeros((N_GROUPS, tm), F32)
    for _ in range(TOPK_GROUPS):
        _, gi = argmax_rows(gsc, grow)
        hit = grow == gi
        gsel = jnp.where(hit, 1.0, gsel)
        gsc = jnp.where(hit, neg, gsc)
    cur = jnp.concatenate(
        [jnp.where(gsel[g:g + 1] > 0.0, biased[g * PER_GROUP:(g + 1) * PER_GROUP], neg)
         for g in range(N_GROUPS)], axis=0)

    onehot = jnp.zeros((N_EXPERTS, tm), F32)
    idxs, gvals = [], []
    for _ in range(TOP_K):
        _, ei = argmax_rows(cur, row)
        hit = row == ei
        gvals.append(jnp.sum(jnp.where(hit, scores, 0.0), axis=0, keepdims=True))
        idxs.append(ei)
        onehot = jnp.where(hit, 1.0, onehot)
        cur = jnp.where(hit, neg, cur)
    gv = jnp.concatenate(gvals, axis=0)
    gates_ref[...] = gv / jnp.sum(gv, axis=0, keepdims=True) * ROUTED_SCALE
    eidx_ref[...] = jnp.concatenate(idxs, axis=0)

    @pl.when((pl.program_id(0) == 0) | (pl.program_id(0) == split_blk))
    def _():
        cnt_ref[...] = jnp.zeros_like(cnt_ref)

    prefix = jnp.dot(onehot.astype(BF16), tri_ref[...], preferred_element_type=F32)
    pos = prefix + cnt_ref[...]
    rank_ref[...] = jnp.concatenate(
        [jnp.sum(jnp.where(row == ei, pos, 0.0), axis=0, keepdims=True) for ei in idxs],
        axis=0).astype(I32)
    cnt_ref[...] = cnt_ref[...] + jnp.sum(onehot, axis=1, keepdims=True)


def _post_mixer(a, xall, mod, ln_g, ln_b, w, router_w, router_bias, n_rows, n_lat, seq, ctx_row,
                dn_alpha, split_blk):
    d = xall.shape[1]
    tm = ROW_TILE
    tri = (lax.broadcasted_iota(I32, (tm, tm), 0) < lax.broadcasted_iota(I32, (tm, tm), 1)).astype(BF16)
    const = lambda i: (0, 0)
    row_blk = pl.BlockSpec((tm, d), lambda i: (i, 0))
    k_blk = pl.BlockSpec((TOP_K, tm), lambda i: (0, i))
    return pl.pallas_call(
        functools.partial(_post_kernel, dn_alpha=dn_alpha, split_blk=split_blk),
        grid=(n_rows // tm,),
        in_specs=[
            row_blk, row_blk,
            _mod_spec(2, tm, n_lat, seq, ctx_row, d),
            pl.BlockSpec((1, d), const), pl.BlockSpec((1, d), const),
            _mod_spec(3, tm, n_lat, seq, ctx_row, d),
            _mod_spec(4, tm, n_lat, seq, ctx_row, d),
            pl.BlockSpec(w.shape, const),
            pl.BlockSpec((N_EXPERTS, d), const),
            pl.BlockSpec((N_EXPERTS, 1), const),
            pl.BlockSpec((tm, tm), const),
        ],
        out_specs=[
            row_blk, pl.BlockSpec((tm, d // 2), lambda i: (i, 0)),
            k_blk, k_blk, k_blk,
            pl.BlockSpec((None, N_EXPERTS, 1), lambda i: (jnp.where(i >= split_blk, 1, 0), 0, 0)),
        ],
        out_shape=[
            jax.ShapeDtypeStruct((n_rows, d), F32),
            jax.ShapeDtypeStruct((n_rows, d // 2), U32),
            jax.ShapeDtypeStruct((TOP_K, n_rows), I32),
            jax.ShapeDtypeStruct((TOP_K, n_rows), F32),
            jax.ShapeDtypeStruct((TOP_K, n_rows), I32),
            jax.ShapeDtypeStruct((2, N_EXPERTS, 1), F32),
        ],
        scratch_shapes=[pltpu.VMEM((N_EXPERTS, d), BF16), pltpu.VMEM((N_EXPERTS, d), BF16)],
        compiler_params=_cparams(("arbitrary",)),
        name="post_mixer_router",
    )(a, xall, mod, ln_g.reshape(1, d), ln_b.reshape(1, d), mod, mod, w.astype(BF16),
      router_w.T, router_bias.reshape(N_EXPERTS, 1), tri)


def _slots_kernel(pstart_ref, eidx_ref, rank_ref, dest_ref):
    e = eidx_ref[...]

    def pick(i, acc):
        return jnp.where(e == i, pstart_ref[i], acc)

    dest_ref[...] = lax.fori_loop(0, N_EXPERTS, pick, jnp.zeros_like(e)) + rank_ref[...]


def _slots(pstarts, eidx, rank):
    n_tok = eidx.shape[1]
    tn = 2048 if n_tok % 2048 == 0 else ROW_TILE
    blk = pl.BlockSpec((TOP_K, tn), lambda i, ps: (0, i))
    return pl.pallas_call(
        _slots_kernel,
        grid_spec=pltpu.PrefetchScalarGridSpec(
            num_scalar_prefetch=1, grid=(n_tok // tn,), in_specs=[blk, blk], out_specs=blk),
        out_shape=jax.ShapeDtypeStruct((TOP_K, n_tok), I32),
        compiler_params=_cparams(("arbitrary",)),
        name="moe_slots",
    )(pstarts, eidx, rank)


def _dispatch_kernel(dest_ref, h_ref, xs_hbm, sem):
    td = dest_ref.shape[1]

    def issue(t, carry):
        for k in range(TOP_K):
            pltpu.make_async_copy(h_ref.at[pl.ds(t, 1)], xs_hbm.at[pl.ds(dest_ref[k, t], 1)], sem).start()
        return carry

    lax.fori_loop(0, td, issue, 0)
    pltpu.make_async_copy(xs_hbm.at[pl.ds(0, TOP_K * td)], xs_hbm.at[pl.ds(0, TOP_K * td)], sem).wait()


def _dispatch(dest, h2, n_slots):
    n_tok = dest.shape[1]
    d = h2.shape[1]
    td = DISPATCH_TOK
    return pl.pallas_call(
        _dispatch_kernel,
        grid=(n_tok // td,),
        in_specs=[
            pl.BlockSpec((TOP_K, td), lambda i: (0, i), memory_space=pltpu.SMEM),
            pl.BlockSpec((td, d), lambda i: (i, 0)),
        ],
        out_specs=pl.BlockSpec(memory_space=pl.ANY),
        out_shape=jax.ShapeDtypeStruct((n_slots, d), h2.dtype),
        scratch_shapes=[pltpu.SemaphoreType.DMA(())],
        compiler_params=_cparams(("arbitrary",)),
        name="moe_dispatch",
    )(dest, h2)


def _gmm_kernel(gb0_ref, nblk_ref, u0_ref, n_used_ref, xa_hbm, xb_hbm, wgu_ref, wd_ref, ys_hbm,
                xbuf, ybuf, wgu_bf, wd_bf, sem_in, sem_out):
    g = pl.program_id(0)
    n_in, bm = xbuf.shape[:2]
    n_out = ybuf.shape[0]
    ff = wd_ref.shape[0]
    n_used = n_used_ref[0]
    rows_a = xa_hbm.shape[0]

    def x_copy(src_hbm, row0, b):
        rows = pl.ds(row0 if isinstance(row0, int) else pl.multiple_of(row0, bm), bm)
        return pltpu.make_async_copy(src_hbm.at[rows], xbuf.at[b % n_in], sem_in.at[b % n_in])

    def x_start(b):
        u0 = u0_ref[b]

        @pl.when(u0 < rows_a)
        def _():
            x_copy(xa_hbm, u0, b).start()

        @pl.when(u0 >= rows_a)
        def _():
            x_copy(xb_hbm, u0 - rows_a, b).start()

    def x_wait(b):
        x_copy(xa_hbm, 0, b).wait()

    def y_copy(b, slot):
        rows = pl.ds(pl.multiple_of(u0_ref[b], bm), bm)
        return pltpu.make_async_copy(ybuf.at[slot], ys_hbm.at[rows], sem_out.at[slot])

    @pl.when(g == 0)
    def _():
        for b in range(n_in - 1):
            @pl.when(b < n_used)
            def _():
                x_start(b)

    nb = nblk_ref[2 * g] + nblk_ref[2 * g + 1]

    @pl.when(nb > 0)
    def _():
        wgu_bf[...] = wgu_ref[...].astype(BF16)
        wd_bf[...] = wd_ref[...].astype(BF16)

    def block(j, carry):
        b = gb0_ref[2 * g] + j
        slot = b % n_out
        x_wait(b)

        @pl.when(b + n_in - 1 < n_used)
        def _():
            x_start(b + n_in - 1)

        @pl.when(b >= n_out)
        def _():
            y_copy(b - n_out, slot).wait()

        gu = jnp.dot(_unpack_rows_bf16(xbuf[b % n_in]), wgu_bf[...], preferred_element_type=F32)
        act = (_silu(gu[:, :ff]) * gu[:, ff:]).astype(BF16)
        ybuf[slot] = _pack_rows(jnp.dot(act, wd_bf[...], preferred_element_type=F32))
        y_copy(b, slot).start()
        return carry

    lax.fori_loop(0, nb, block, 0)

    @pl.when(g == pl.num_programs(0) - 1)
    def _():
        for back in range(n_out, 0, -1):
            @pl.when(n_used >= back)
            def _():
                y_copy(n_used - back, (n_used - back) % n_out).wait()


def _grouped_ffn(xs_a, xs_b, gb0, nblk, u0_blk, n_used, wgu, wd, layer):
    n_slots = xs_a.shape[0] + xs_b.shape[0]
    bm = MOE_BM
    n_exp, d, ff2 = wgu.shape[1:]
    ff = wd.shape[2]
    grid_spec = pltpu.PrefetchScalarGridSpec(
        num_scalar_prefetch=4,
        grid=(n_exp,),
        in_specs=[
            pl.BlockSpec(memory_space=pl.ANY),
            pl.BlockSpec(memory_space=pl.ANY),
            pl.BlockSpec((None, None, d, ff2), lambda g, *_: (layer, g, 0, 0)),
            pl.BlockSpec((None, None, ff, d), lambda g, *_: (layer, g, 0, 0)),
        ],
        out_specs=pl.BlockSpec(memory_space=pl.ANY),
        scratch_shapes=[
            pltpu.VMEM((GMM_IN_SLOTS, bm, d // 2), U32), pltpu.VMEM((GMM_OUT_SLOTS, bm, d // 2), U32),
            pltpu.VMEM((d, ff2), BF16), pltpu.VMEM((ff, d), BF16),
            pltpu.SemaphoreType.DMA((GMM_IN_SLOTS,)), pltpu.SemaphoreType.DMA((GMM_OUT_SLOTS,)),
        ],
    )
    return pl.pallas_call(
        _gmm_kernel,
        grid_spec=grid_spec,
        out_shape=jax.ShapeDtypeStruct((n_slots, d // 2), U32),
        compiler_params=_cparams(("arbitrary",)),
        name="moe_grouped_ffn",
    )(gb0, nblk, u0_blk, n_used, xs_a, xs_b, wgu, wd)


def _shared_expert(h2_ref, swgu_ref, swd_ref):
    ff = swd_ref.shape[0]
    gu = jnp.dot(_unpack_rows_bf16(h2_ref[...]), swgu_ref[...], preferred_element_type=F32)
    act = (_silu(gu[:, :ff]) * gu[:, ff:]).astype(BF16)
    return jnp.dot(act, swd_ref[...], preferred_element_type=F32)


def _combine_kernel(dest_ref, gates_ref, ys_hbm, x1_ref, h2_ref, gate_ref, lng_ref, lnb_ref,
                    swgu_ref, swd_ref, o_ref, buf, sem, *, dn_alpha):
    tc = x1_ref.shape[0]

    def issue(t, carry):
        for k in range(TOP_K):
            pltpu.make_async_copy(ys_hbm.at[pl.ds(dest_ref[k, t], 1)], buf.at[k, pl.ds(t, 1)], sem).start()
        return carry

    lax.fori_loop(0, tc, issue, 0)

    y = _shared_expert(h2_ref, swgu_ref, swd_ref)

    for k in range(TOP_K):
        pltpu.make_async_copy(ys_hbm.at[pl.ds(0, tc)], buf.at[k], sem).wait()
    for k in range(TOP_K):
        y = _add_weighted_rows(y, gates_ref[:, k:k + 1], buf[k])
    o_ref[...] = _layer_norm(dn_alpha * x1_ref[...] + gate_ref[...] * y, lng_ref[...], lnb_ref[...])


def _combine_gathered_kernel(gates_ref, g_ref, x1_ref, h2_ref, gate_ref, lng_ref, lnb_ref,
                             swgu_ref, swd_ref, prev_ref, o_ref, *, dn_alpha):
    del prev_ref
    y = _shared_expert(h2_ref, swgu_ref, swd_ref)
    for k in range(TOP_K):
        y = _add_weighted_rows(y, gates_ref[:, k:k + 1], g_ref[k])
    o_ref[...] = _layer_norm(dn_alpha * x1_ref[...] + gate_ref[...] * y, lng_ref[...], lnb_ref[...])


def _sc_gather_rows(table, idx):
    n = idx.shape[0]
    d = table.shape[1]
    mesh = plsc.VectorSubcoreMesh(core_axis_name="c", subcore_axis_name="s")
    n_workers = mesh.num_cores * mesh.num_subcores
    ch = SC_GATHER_ROWS
    per_w = n // n_workers
    n_ch = per_w // ch
    assert n % (n_workers * ch * 2) == 0

    @functools.partial(
        pl.kernel, mesh=mesh, out_type=jax.ShapeDtypeStruct((n, d), table.dtype),
        scratch_types=[pltpu.VMEM((ch,), I32), pltpu.VMEM((ch,), I32),
                       pltpu.VMEM((ch, d), table.dtype), pltpu.VMEM((ch, d), table.dtype),
                       pltpu.SemaphoreType.DMA, pltpu.SemaphoreType.DMA],
        name="moe_sc_gather")
    def gather(table_hbm, idx_hbm, out_hbm, idx_a, idx_b, rows_a, rows_b, sem_a, sem_b):
        wid = lax.axis_index("s") * mesh.num_cores + lax.axis_index("c")

        def chunk(i):
            return pl.ds(wid * per_w + i * ch, ch)

        def fetch(i, idx_v, rows_v, sem):
            pltpu.sync_copy(idx_hbm.at[chunk(i)], idx_v)
            pltpu.make_async_copy(table_hbm.at[idx_v], rows_v, sem).start()

        def drain(i, idx_v, rows_v, sem):
            pltpu.make_async_copy(table_hbm.at[idx_v], rows_v, sem).wait()
            pltpu.sync_copy(rows_v, out_hbm.at[chunk(i)])

        fetch(0, idx_a, rows_a, sem_a)

        @pl.loop(0, n_ch // 2)
        def _(j):
            i = 2 * j
            fetch(i + 1, idx_b, rows_b, sem_b)
            drain(i, idx_a, rows_a, sem_a)

            @pl.when(i + 2 < n_ch)
            def _():
                fetch(i + 2, idx_a, rows_a, sem_a)

            drain(i + 1, idx_b, rows_b, sem_b)

    return gather(table, idx)


def _sc_scatter_rows(rows, first_row, idx, n_out):
    n_k, n = idx.shape
    d = rows.shape[1]
    mesh = plsc.VectorSubcoreMesh(core_axis_name="c", subcore_axis_name="s")
    n_workers = mesh.num_cores * mesh.num_subcores
    ch = SC_GATHER_ROWS
    per_w = n // n_workers
    assert n % (n_workers * ch) == 0

    @functools.partial(
        pl.kernel, mesh=mesh, out_type=jax.ShapeDtypeStruct((n_out, d), rows.dtype),
        scratch_types=[pltpu.VMEM((ch,), I32), pltpu.VMEM((ch, d), rows.dtype)],
        name="moe_sc_scatter")
    def scatter(rows_hbm, idx_hbm, out_hbm, idx_v, rows_v):
        wid = lax.axis_index("s") * mesh.num_cores + lax.axis_index("c")

        @pl.loop(0, per_w // ch)
        def _(i):
            base = wid * per_w + i * ch
            pltpu.sync_copy(rows_hbm.at[pl.ds(first_row + base, ch)], rows_v)
            for k in range(n_k):
                pltpu.sync_copy(idx_hbm.at[k, pl.ds(base, ch)], idx_v)
                pltpu.sync_copy(rows_v, out_hbm.at[idx_v])

    return scatter(rows, idx)


def _combine(dest, gates_t, ys, x1, h2, mod, ln_g, ln_b, swgu, swd, n_rows, n_lat, seq, ctx_row,
             dn_alpha):
    d = x1.shape[1]
    tc = DISPATCH_TOK
    const = lambda i: (0, 0)
    n_blocks = n_rows // tc
    n_sc_blocks = (n_blocks * SC_SHARE_PERCENT) // 100
    n_tc = (n_blocks - n_sc_blocks) * tc
    n_sc = n_rows - n_tc
    gathered = _sc_gather_rows(ys, dest[:, n_tc:].reshape(TOP_K * n_sc)).reshape(TOP_K, n_sc, d // 2)
    swgu_bf, swd_bf = swgu.astype(BF16), swd.astype(BF16)
    out_tc = pl.pallas_call(
        functools.partial(_combine_kernel, dn_alpha=dn_alpha),
        grid=(n_tc // tc,),
        in_specs=[
            pl.BlockSpec((TOP_K, tc), lambda i: (0, i), memory_space=pltpu.SMEM),
            pl.BlockSpec((tc, TOP_K), lambda i: (i, 0)),
            pl.BlockSpec(memory_space=pl.ANY),
            pl.BlockSpec((tc, d), lambda i: (i, 0)),
            pl.BlockSpec((tc, d // 2), lambda i: (i, 0)),
            _mod_spec(5, tc, n_lat, seq, ctx_row, d),
            pl.BlockSpec((1, d), const), pl.BlockSpec((1, d), const),
            pl.BlockSpec(swgu.shape, const),
            pl.BlockSpec(swd.shape, const),
        ],
        out_specs=pl.BlockSpec((tc, d), lambda i: (i, 0)),
        out_shape=jax.ShapeDtypeStruct((n_rows, d), F32),
        scratch_shapes=[pltpu.VMEM((TOP_K, tc, d // 2), U32), pltpu.SemaphoreType.DMA(())],
        compiler_params=_cparams(("arbitrary",)),
        name="moe_combine",
    )(dest, gates_t, ys, x1, h2, mod, ln_g.reshape(1, d), ln_b.reshape(1, d), swgu_bf, swd_bf)

    tl = COMBINE_GATHERED_TOK
    off = n_tc // tl
    row_blk = pl.BlockSpec((tl, d), lambda i: (i + off, 0))
    return pl.pallas_call(
        functools.partial(_combine_gathered_kernel, dn_alpha=dn_alpha),
        grid=(n_sc // tl,),
        in_specs=[
            pl.BlockSpec((tl, TOP_K), lambda i: (i + off, 0)),
            pl.BlockSpec((TOP_K, tl, d // 2), lambda i: (0, i, 0)),
            row_blk, pl.BlockSpec((tl, d // 2), lambda i: (i + off, 0)),
            _mod_spec(5, tl, n_lat, seq, ctx_row, d, blk_off=off),
            pl.BlockSpec((1, d), const), pl.BlockSpec((1, d), const),
            pl.BlockSpec(swgu.shape, const),
            pl.BlockSpec(swd.shape, const),
            pl.BlockSpec(memory_space=pl.ANY),
        ],
        out_specs=row_blk,
        out_shape=jax.ShapeDtypeStruct((n_rows, d), F32),
        input_output_aliases={9: 0},
        compiler_params=_cparams(("parallel",)),
        name="moe_combine_gathered",
    )(gates_t, gathered, x1, h2, mod, ln_g.reshape(1, d), ln_b.reshape(1, d), swgu_bf, swd_bf, out_tc)


def _split_block(n_rows):
    n_blocks = n_rows // ROW_TILE
    sc_blocks = (n_blocks * SC_DISPATCH_PERCENT) // 100 // 4 * 4
    return n_blocks - sc_blocks


def _max_slots(n_tok, bm):
    return (n_tok * TOP_K + N_EXPERTS * (bm - 1) + bm - 1) // bm * bm


def _moe(x1, h2, eidx, gates, rank, counts, mod, ln_g, ln_b, wgu, wd, layer, swgu, swd,
         n_rows, n_lat, seq, ctx_row, dn_alpha, split_blk):
    bm = MOE_BM
    t0 = split_blk * ROW_TILE
    rows_a, rows_b = (_max_slots(t0, bm) if t0 else bm), _max_slots(n_rows - t0, bm)
    cnt = counts.reshape(2, N_EXPERTS).astype(I32)
    if not t0:
        cnt = cnt.at[0].set(0)
    pcnt = (cnt + bm - 1) // bm * bm
    pend = jnp.cumsum(pcnt, axis=1)
    ustart = pend - pcnt + jnp.array([[0], [rows_a]], I32)
    dest = _slots(ustart[1], eidx[:, t0:], rank[:, t0:])
    if t0:
        dest = jnp.concatenate([_slots(ustart[0], eidx[:, :t0], rank[:, :t0]), dest], axis=1)

    nblk = (pcnt // bm).T.reshape(2 * N_EXPERTS)
    gb0 = jnp.cumsum(nblk) - nblk
    u0_grp = ustart.T.reshape(2 * N_EXPERTS)
    blk = jnp.arange((rows_a + rows_b) // bm, dtype=I32)
    grp = jnp.sum((gb0[None, :] <= blk[:, None]).astype(I32), axis=1) - 1
    u0_blk = u0_grp[grp] + (blk - gb0[grp]) * bm
    n_used = jnp.sum(nblk).astype(I32).reshape(1)

    xs_a = _dispatch(dest[:, :t0], h2, rows_a) if t0 else jnp.zeros((rows_a, h2.shape[1]), h2.dtype)
    xs_b = _sc_scatter_rows(h2, t0, dest[:, t0:] - rows_a, rows_b)
    ys = _grouped_ffn(xs_a, xs_b, gb0, nblk, u0_blk, n_used, wgu, wd, layer)
    return _combine(dest, gates.T, ys, x1, h2, mod, ln_g, ln_b, swgu, swd,
                    n_rows, n_lat, seq, ctx_row, dn_alpha)


def kernel(x, c, ctx, c_ctx, ada_w, ada_b, ln_g, ln_b, attn_w_qkv, attn_q_norm, attn_k_norm, attn_w_o, conv_w_in, conv_taps, conv_w_out, router_w, router_bias, exp_w_gate_up, exp_w_down, shared_w_gate_up, shared_w_down):
    b, seq, d = x.shape
    cl = ctx.shape[1]
    depth = ada_w.shape[0]
    n_lat = b * seq
    n_ctx = b * cl
    dn_alpha = (2 * depth) ** 0.25
    assert depth == 2 and b < MOD_ROWS
    assert seq % ROW_TILE == 0 and n_ctx % ROW_TILE == 0 and seq % GRID_W == 0

    cond = jnp.zeros((MOD_ROWS, d), F32).at[:b].set(c).at[b].set(c_ctx)
    mod = _modulation(cond, ada_w, ada_b)
    xall = jnp.concatenate([x.reshape(n_lat, d), ctx.reshape(n_ctx, d)], axis=0)
    n_all = n_lat + n_ctx

    q, k, v = _qkv_project(xall, mod[0], attn_w_qkv[0], attn_q_norm[0], attn_k_norm[0], n_lat, seq, b)
    o = _attention(q, k, v, b, seq, cl, n_lat)
    x1, h2, eidx, gates, rank, counts = _post_mixer(
        o, xall, mod[0], ln_g[0, 0], ln_b[0, 0], attn_w_o[0], router_w[0], router_bias[0],
        n_all, n_lat, seq, b, dn_alpha, _split_block(n_all))
    xall = _moe(x1, h2, eidx, gates, rank, counts, mod[0], ln_g[0, 1], ln_b[0, 1],
                exp_w_gate_up, exp_w_down, 0, shared_w_gate_up[0], shared_w_down[0],
                n_all, n_lat, seq, b, dn_alpha, _split_block(n_all))

    zin = _in_project(xall, mod[1], conv_w_in[0], n_lat, n_lat, seq, b)
    a = _short_conv(zin, conv_taps[0], b, seq, d)
    x1, h2, eidx, gates, rank, counts = _post_mixer(
        a, xall, mod[1], ln_g[1, 0], ln_b[1, 0], conv_w_out[0], router_w[1], router_bias[1],
        n_lat, n_lat, seq, b, dn_alpha, _split_block(n_lat))
    out = _moe(x1, h2, eidx, gates, rank, counts, mod[1], ln_g[1, 1], ln_b[1, 1],
               exp_w_gate_up, exp_w_down, 1, shared_w_gate_up[1], shared_w_down[1],
               n_lat, n_lat, seq, b, dn_alpha, _split_block(n_lat))
    return out.reshape(b, seq, d)
```

```python
import functools

import jax
import jax.numpy as jnp
from jax import lax
from jax.experimental import pallas as pl
from jax.experimental.pallas import tpu as pltpu
from jax.experimental.pallas import tpu_sc as plsc

F32 = jnp.float32
BF16 = jnp.bfloat16
I32 = jnp.int32

N_HEADS = 8
N_KV_HEADS = 2
HEAD_DIM = 128
KV_GROUP = N_HEADS // N_KV_HEADS
GRID_W = 64
ROPE_THETA = 10000.0
N_EXPERTS = 256
TOP_K = 8
N_GROUPS = 8
TOPK_GROUPS = 4
PER_GROUP = N_EXPERTS // N_GROUPS
ROUTED_SCALE = 2.5
LN_EPS = 1e-5
QK_EPS = 1e-6
N_MOD = 6
MOD_ROWS = 16

LANES = 128
SUBLANES = 8
VMEM_LIMIT = 56 * 1024 * 1024

ROW_TILE = 512
ATTN_TQ = 256
ATTN_TK = 2048
MOE_BM = 256
GMM_IN_SLOTS = 4
GMM_OUT_SLOTS = 3
DISPATCH_TOK = 512
COMBINE_GATHERED_TOK = 512
SC_GATHER_ROWS = 64
SC_SHARE_PERCENT = 82
SC_DISPATCH_PERCENT = 100
CONV_LANES = 128

HIGHEST = lax.Precision.HIGHEST
LOG2_E = 1.4426950408889634


def _cparams(sem):
    return pltpu.CompilerParams(dimension_semantics=sem, vmem_limit_bytes=VMEM_LIMIT)


def _silu(v):
    return v * jax.nn.sigmoid(v)


U32 = jnp.uint32
HI_MASK = 0xFFFF0000


def _pack_rows(v):
    half = v.shape[1] // 2
    lo = lax.bitcast_convert_type(v[:, :half].astype(BF16).astype(F32), U32)
    hi = lax.bitcast_convert_type(v[:, half:].astype(BF16).astype(F32), U32)
    return (lo >> 16) | (hi & U32(HI_MASK))


def _add_weighted_rows(y, gate, packed):
    half = y.shape[1] // 2
    lo = lax.bitcast_convert_type(packed << 16, F32)
    hi = lax.bitcast_convert_type(packed & U32(HI_MASK), F32)
    return jnp.concatenate([y[:, :half] + gate * lo, y[:, half:] + gate * hi], axis=-1)


def _unpack_rows_bf16(w):
    lo = lax.bitcast_convert_type(w << 16, F32)
    hi = lax.bitcast_convert_type(w & U32(HI_MASK), F32)
    return jnp.concatenate([lo.astype(BF16), hi.astype(BF16)], axis=-1)


def _mod_kernel(c_ref, w_ref, b_ref, o_ref):
    s = _silu(c_ref[...])
    o_ref[...] = jnp.dot(s, w_ref[...], precision=HIGHEST, preferred_element_type=F32) + b_ref[...]


def _modulation(cond, ada_w, ada_b):
    depth, d, nd = ada_w.shape
    tn = 1536
    out = pl.pallas_call(
        _mod_kernel,
        grid=(depth, nd // tn),
        in_specs=[
            pl.BlockSpec((MOD_ROWS, d), lambda l, j: (0, 0)),
            pl.BlockSpec((None, d, tn), lambda l, j: (l, 0, j)),
            pl.BlockSpec((None, 1, tn), lambda l, j: (l, 0, j)),
        ],
        out_specs=pl.BlockSpec((None, MOD_ROWS, tn), lambda l, j: (l, 0, j)),
        out_shape=jax.ShapeDtypeStruct((depth, MOD_ROWS, nd), F32),
        compiler_params=_cparams(("arbitrary", "arbitrary")),
        name="adaln_modulation",
    )(cond, ada_w, ada_b.reshape(depth, 1, nd))
    return out.reshape(depth, MOD_ROWS * N_MOD, 1, d)


def _mod_spec(comp, tm, n_lat, seq, ctx_row, d, blk_off=0):
    def index(i, *_):
        row0 = (i + blk_off) * tm
        r = jnp.where(row0 < n_lat, row0 // seq, ctx_row)
        return (r * N_MOD + comp, 0, 0)

    return pl.BlockSpec((None, 1, d), index)


def _qkv_kernel(x_ref, shift_ref, scale_ref, w_ref, qg_ref, kg_ref, cos_ref, sin_ref,
                q_ref, k_ref, v_ref):
    h = (x_ref[...] * (1.0 + scale_ref[...]) + shift_ref[...]).astype(BF16)
    qkv = jnp.dot(h, w_ref[...], preferred_element_type=F32)
    cos = cos_ref[...]
    sin = sin_ref[...]
    hq = N_HEADS * HEAD_DIM
    kd = N_KV_HEADS * HEAD_DIM

    def norm_rope(t, g, post):
        t = t * lax.rsqrt(jnp.mean(t * t, axis=-1, keepdims=True) + QK_EPS) * g
        t = t * cos + pltpu.roll(t, HEAD_DIM // 2, axis=1) * sin
        return (t * post).astype(BF16)

    for hd in range(N_HEADS):
        sl = slice(hd * HEAD_DIM, (hd + 1) * HEAD_DIM)
        q_ref[:, sl] = norm_rope(qkv[:, sl], qg_ref[...], HEAD_DIM ** -0.5 * LOG2_E)
    for hd in range(N_KV_HEADS):
        sl = slice(hd * HEAD_DIM, (hd + 1) * HEAD_DIM)
        k_ref[:, sl] = norm_rope(qkv[:, hq + hd * HEAD_DIM: hq + (hd + 1) * HEAD_DIM], kg_ref[...], 1.0)
    v_ref[...] = qkv[:, hq + kd:].astype(BF16)


def _rope_tables(seq, tm):
    rows = seq // GRID_W
    row = jnp.repeat(jnp.arange(rows, dtype=F32), GRID_W)
    col = jnp.tile(jnp.arange(GRID_W, dtype=F32), rows)
    axis_dim = HEAD_DIM // 2
    freqs = ROPE_THETA ** (-jnp.arange(0, axis_dim, 2, dtype=F32) / axis_dim)
    ang = jnp.concatenate([row[:, None] * freqs, col[:, None] * freqs], axis=-1)
    cos = jnp.concatenate([jnp.cos(ang), jnp.cos(ang)], axis=-1)
    sin = jnp.concatenate([-jnp.sin(ang), jnp.sin(ang)], axis=-1)
    cos = jnp.concatenate([cos, jnp.ones((tm, HEAD_DIM), F32)], axis=0)
    sin = jnp.concatenate([sin, jnp.zeros((tm, HEAD_DIM), F32)], axis=0)
    return cos.reshape(seq // tm + 1, tm, HEAD_DIM), sin.reshape(seq // tm + 1, tm, HEAD_DIM)


def _qkv_project(xall, mod, w_qkv, q_g, k_g, n_lat, seq, ctx_row):
    t, d = xall.shape
    tm = ROW_TILE
    hq = N_HEADS * HEAD_DIM
    kd = N_KV_HEADS * HEAD_DIM
    perm = jnp.concatenate([jnp.arange(0, HEAD_DIM, 2), jnp.arange(1, HEAD_DIM, 2)])
    cols = jnp.concatenate([hd * HEAD_DIM + perm for hd in range(N_HEADS + N_KV_HEADS)]
                           + [jnp.arange(hq + kd, hq + 2 * kd)])
    w = w_qkv[:, cols].astype(BF16)
    cos, sin = _rope_tables(seq, tm)
    n_pos = seq // tm

    def pos_index(i):
        row0 = i * tm
        return (jnp.where(row0 < n_lat, (row0 % seq) // tm, n_pos), 0, 0)

    const = lambda i: (0, 0)
    return pl.pallas_call(
        _qkv_kernel,
        grid=(t // tm,),
        in_specs=[
            pl.BlockSpec((tm, d), lambda i: (i, 0)),
            _mod_spec(0, tm, n_lat, seq, ctx_row, d),
            _mod_spec(1, tm, n_lat, seq, ctx_row, d),
            pl.BlockSpec(w.shape, const),
            pl.BlockSpec((1, HEAD_DIM), const),
            pl.BlockSpec((1, HEAD_DIM), const),
            pl.BlockSpec((None, tm, HEAD_DIM), pos_index),
            pl.BlockSpec((None, tm, HEAD_DIM), pos_index),
        ],
        out_specs=[
            pl.BlockSpec((tm, hq), lambda i: (i, 0)),
            pl.BlockSpec((tm, kd), lambda i: (i, 0)),
            pl.BlockSpec((tm, kd), lambda i: (i, 0)),
        ],
        out_shape=[
            jax.ShapeDtypeStruct((t, hq), BF16),
            jax.ShapeDtypeStruct((t, kd), BF16),
            jax.ShapeDtypeStruct((t, kd), BF16),
        ],
        compiler_params=_cparams(("parallel",)),
        name="qkv_norm_rope",
    )(xall, mod, mod, w, q_g[perm].reshape(1, HEAD_DIM), k_g[perm].reshape(1, HEAD_DIM), cos, sin)


def _attn_kernel(*refs, n_lat_chunks, tk):
    if n_lat_chunks:
        q_ref, kc_ref, vc_ref, kl_ref, vl_ref, o_ref = refs
    else:
        q_ref, kc_ref, vc_ref, o_ref = refs
    tq = q_ref.shape[0]
    q = jnp.concatenate([q_ref[:, h * HEAD_DIM:(h + 1) * HEAD_DIM] for h in range(KV_GROUP)], axis=0)
    rows = KV_GROUP * tq

    def chunk(k, v, m, l, acc):
        s = lax.dot_general(q, k, (((1,), (1,)), ((), ())), preferred_element_type=F32)
        m_new = jnp.maximum(m, jnp.max(s, axis=-1, keepdims=True))
        p = jnp.exp2(s - m_new)
        a = jnp.exp2(m - m_new)
        l = a * l + jnp.sum(p, axis=-1, keepdims=True)
        acc = a * acc + jnp.dot(p.astype(BF16), v, preferred_element_type=F32)
        return m_new, l, acc

    m = jnp.full((rows, 1), -jnp.inf, F32)
    l = jnp.zeros((rows, 1), F32)
    acc = jnp.zeros((rows, HEAD_DIM), F32)
    m, l, acc = chunk(kc_ref[...], vc_ref[...], m, l, acc)
    for c in range(n_lat_chunks):
        m, l, acc = chunk(kl_ref[c * tk:(c + 1) * tk, :], vl_ref[c * tk:(c + 1) * tk, :], m, l, acc)
    o = (acc / l).astype(BF16)
    for h in range(KV_GROUP):
        o_ref[:, h * HEAD_DIM:(h + 1) * HEAD_DIM] = o[h * tq:(h + 1) * tq]


def _attention(q, k, v, b, seq, cl, n_lat):
    t = q.shape[0]
    gw = KV_GROUP * HEAD_DIM
    tq = min(ATTN_TQ, seq)
    tk = min(ATTN_TK, seq)
    nq = seq // tq
    ctx_blk0 = n_lat // cl
    hq = N_HEADS * HEAD_DIM

    ctx_kv = pl.BlockSpec((cl, HEAD_DIM), lambda bi, g, qi: (ctx_blk0 + bi, g))
    lat_kv = pl.BlockSpec((seq, HEAD_DIM), lambda bi, g, qi: (bi, g))
    lat_q = pl.BlockSpec((tq, gw), lambda bi, g, qi: (bi * nq + qi, g))
    o_lat = pl.pallas_call(
        functools.partial(_attn_kernel, n_lat_chunks=seq // tk, tk=tk),
        grid=(b, N_KV_HEADS, nq),
        in_specs=[lat_q, ctx_kv, ctx_kv, lat_kv, lat_kv],
        out_specs=lat_q,
        out_shape=jax.ShapeDtypeStruct((n_lat, hq), BF16),
        compiler_params=_cparams(("parallel", "parallel", "arbitrary")),
        name="attention_latent",
    )(q, k, v, k, v)

    ctx_q = pl.BlockSpec((cl, gw), lambda bi, g: (ctx_blk0 + bi, g))
    ctx_kv2 = pl.BlockSpec((cl, HEAD_DIM), lambda bi, g: (ctx_blk0 + bi, g))
    o_ctx = pl.pallas_call(
        functools.partial(_attn_kernel, n_lat_chunks=0, tk=tk),
        grid=(b, N_KV_HEADS),
        in_specs=[ctx_q, ctx_kv2, ctx_kv2],
        out_specs=pl.BlockSpec((cl, gw), lambda bi, g: (bi, g)),
        out_shape=jax.ShapeDtypeStruct((t - n_lat, hq), BF16),
        compiler_params=_cparams(("parallel", "parallel")),
        name="attention_context",
    )(q, k, v)
    return jnp.concatenate([o_lat, o_ctx], axis=0)


def _in_proj_kernel(x_ref, shift_ref, scale_ref, w_ref, o_ref):
    h = (x_ref[...] * (1.0 + scale_ref[...]) + shift_ref[...]).astype(BF16)
    o_ref[...] = jnp.dot(h, w_ref[...], preferred_element_type=F32)


def _in_project(xall, mod, w_in, n_rows, n_lat, seq, ctx_row):
    d = xall.shape[1]
    n_out = w_in.shape[1]
    tm = ROW_TILE
    return pl.pallas_call(
        _in_proj_kernel,
        grid=(n_rows // tm,),
        in_specs=[
            pl.BlockSpec((tm, d), lambda i: (i, 0)),
            _mod_spec(0, tm, n_lat, seq, ctx_row, d),
            _mod_spec(1, tm, n_lat, seq, ctx_row, d),
            pl.BlockSpec(w_in.shape, lambda i: (0, 0)),
        ],
        out_specs=pl.BlockSpec((tm, n_out), lambda i: (i, 0)),
        out_shape=jax.ShapeDtypeStruct((n_rows, n_out), F32),
        compiler_params=_cparams(("parallel",)),
        name="conv_in_proj",
    )(xall, mod, mod, w_in.astype(BF16))


def _conv_kernel(bg_ref, cg_ref, v_ref, taps_ref, o_ref):
    u = cg_ref[...] * v_ref[...]
    n = u.shape[0]
    pos = lax.broadcasted_iota(I32, u.shape, 0)
    prev = jnp.where(pos == 0, 0.0, pltpu.roll(u, 1, axis=0))
    nxt = jnp.where(pos == n - 1, 0.0, pltpu.roll(u, n - 1, axis=0))
    conv = prev * taps_ref[0:1, :] + u * taps_ref[1:2, :] + nxt * taps_ref[2:3, :]
    o_ref[...] = (bg_ref[...] * conv).astype(BF16)


def _short_conv(zin, taps, n_seqs, seq, d):
    tc = CONV_LANES
    nj = d // tc
    return pl.pallas_call(
        _conv_kernel,
        grid=(n_seqs, nj),
        in_specs=[
            pl.BlockSpec((seq, tc), lambda s, j: (s, j)),
            pl.BlockSpec((seq, tc), lambda s, j: (s, nj + j)),
            pl.BlockSpec((seq, tc), lambda s, j: (s, 2 * nj + j)),
            pl.BlockSpec((taps.shape[0], tc), lambda s, j: (0, j)),
        ],
        out_specs=pl.BlockSpec((seq, tc), lambda s, j: (s, j)),
        out_shape=jax.ShapeDtypeStruct((n_seqs * seq, d), BF16),
        compiler_params=_cparams(("parallel", "parallel")),
        name="short_conv",
    )(zin, zin, zin, taps)


def _layer_norm(z, g, b):
    mu = jnp.mean(z, axis=-1, keepdims=True)
    zc = z - mu
    var = jnp.mean(zc * zc, axis=-1, keepdims=True)
    return zc * lax.rsqrt(var + LN_EPS) * g + b


def _post_kernel(a_ref, x_ref, gate_ref, lng_ref, lnb_ref, shift_ref, scale_ref, w_ref,
                 rwt_ref, rb_ref, tri_ref,
                 x1_ref, h2_ref, eidx_ref, gates_ref, rank_ref, cnt_ref, rwh_ref, rwl_ref, *,
                 dn_alpha, split_blk):
    tm = a_ref.shape[0]

    @pl.when(pl.program_id(0) == 0)
    def _():
        hi = rwt_ref[...].astype(BF16)
        rwh_ref[...] = hi
        rwl_ref[...] = (rwt_ref[...] - hi.astype(F32)).astype(BF16)

    d = x_ref.shape[1]
    y = jnp.dot(a_ref[...], w_ref[...], preferred_element_type=F32)
    x1 = _layer_norm(dn_alpha * x_ref[...] + gate_ref[...] * y, lng_ref[...], lnb_ref[...])
    x1_ref[...] = x1
    h2 = x1 * (1.0 + scale_ref[...]) + shift_ref[...]
    h2_ref[...] = _pack_rows(h2)

    h_hi = h2.astype(BF16)
    h_lo = (h2 - h_hi.astype(F32)).astype(BF16)

    def nt_dot(w, h):
        return lax.dot_general(w, h, (((1,), (1,)), ((), ())), preferred_element_type=F32)

    logits = nt_dot(rwh_ref[...], h_hi) + (nt_dot(rwl_ref[...], h_hi) + nt_dot(rwh_ref[...], h_lo))
    scores = jax.nn.sigmoid(logits)
    biased = scores + rb_ref[...]
    neg = -jnp.inf
    big = jnp.int32(1 << 30)
    row = lax.broadcasted_iota(I32, (N_EXPERTS, tm), 0)

    def argmax_rows(vals, idx):
        mx = jnp.max(vals, axis=0, keepdims=True)
        return mx, jnp.min(jnp.where(vals == mx, idx, big), axis=0, keepdims=True)

    gs = []
    grp_row = lax.broadcasted_iota(I32, (PER_GROUP, tm), 0)
    for g in range(N_GROUPS):
        bg = biased[g * PER_GROUP:(g + 1) * PER_GROUP]
        ig = grp_row + g * PER_GROUP
        m1, i1 = argmax_rows(bg, ig)
        m2 = jnp.max(jnp.where(ig == i1, neg, bg), axis=0, keepdims=True)
        gs.append(m1 + m2)
    gsc = jnp.concatenate(gs, axis=0)
    grow = lax.broadcasted_iota(I32, (N_GROUPS, tm), 0)
    gsel = jnp.zeros((N_GROUPS, tm), F32)
    for _ in range(TOPK_GROUPS):
        _, gi = argmax_rows(gsc, grow)
        hit = grow == gi
        gsel = jnp.where(hit, 1.0, gsel)
        gsc = jnp.where(hit, neg, gsc)
    cur = jnp.concatenate(
        [jnp.where(gsel[g:g + 1] > 0.0, biased[g * PER_GROUP:(g + 1) * PER_GROUP], neg)
         for g in range(N_GROUPS)], axis=0)

    onehot = jnp.zeros((N_EXPERTS, tm), F32)
    idxs, gvals = [], []
    for _ in range(TOP_K):
        _, ei = argmax_rows(cur, row)
        hit = row == ei
        gvals.append(jnp.sum(jnp.where(hit, scores, 0.0), axis=0, keepdims=True))
        idxs.append(ei)
        onehot = jnp.where(hit, 1.0, onehot)
        cur = jnp.where(hit, neg, cur)
    gv = jnp.concatenate(gvals, axis=0)
    gates_ref[...] = gv / jnp.sum(gv, axis=0, keepdims=True) * ROUTED_SCALE
    eidx_ref[...] = jnp.concatenate(idxs, axis=0)

    @pl.when((pl.program_id(0) == 0) | (pl.program_id(0) == split_blk))
    def _():
        cnt_ref[...] = jnp.zeros_like(cnt_ref)

    prefix = jnp.dot(onehot.astype(BF16), tri_ref[...], preferred_element_type=F32)
    pos = prefix + cnt_ref[...]
    rank_ref[...] = jnp.concatenate(
        [jnp.sum(jnp.where(row == ei, pos, 0.0), axis=0, keepdims=True) for ei in idxs],
        axis=0).astype(I32)
    cnt_ref[...] = cnt_ref[...] + jnp.sum(onehot, axis=1, keepdims=True)


def _post_mixer(a, xall, mod, ln_g, ln_b, w, router_w, router_bias, n_rows, n_lat, seq, ctx_row,
                dn_alpha, split_blk):
    d = xall.shape[1]
    tm = ROW_TILE
    tri = (lax.broadcasted_iota(I32, (tm, tm), 0) < lax.broadcasted_iota(I32, (tm, tm), 1)).astype(BF16)
    const = lambda i: (0, 0)
    row_blk = pl.BlockSpec((tm, d), lambda i: (i, 0))
    k_blk = pl.BlockSpec((TOP_K, tm), lambda i: (0, i))
    return pl.pallas_call(
        functools.partial(_post_kernel, dn_alpha=dn_alpha, split_blk=split_blk),
        grid=(n_rows // tm,),
        in_specs=[
            row_blk, row_blk,
            _mod_spec(2, tm, n_lat, seq, ctx_row, d),
            pl.BlockSpec((1, d), const), pl.BlockSpec((1, d), const),
            _mod_spec(3, tm, n_lat, seq, ctx_row, d),
            _mod_spec(4, tm, n_lat, seq, ctx_row, d),
            pl.BlockSpec(w.shape, const),
            pl.BlockSpec((N_EXPERTS, d), const),
            pl.BlockSpec((N_EXPERTS, 1), const),
            pl.BlockSpec((tm, tm), const),
        ],
        out_specs=[
            row_blk, pl.BlockSpec((tm, d // 2), lambda i: (i, 0)),
            k_blk, k_blk, k_blk,
            pl.BlockSpec((None, N_EXPERTS, 1), lambda i: (jnp.where(i >= split_blk, 1, 0), 0, 0)),
        ],
        out_shape=[
            jax.ShapeDtypeStruct((n_rows, d), F32),
            jax.ShapeDtypeStruct((n_rows, d // 2), U32),
            jax.ShapeDtypeStruct((TOP_K, n_rows), I32),
            jax.ShapeDtypeStruct((TOP_K, n_rows), F32),
            jax.ShapeDtypeStruct((TOP_K, n_rows), I32),
            jax.ShapeDtypeStruct((2, N_EXPERTS, 1), F32),
        ],
        scratch_shapes=[pltpu.VMEM((N_EXPERTS, d), BF16), pltpu.VMEM((N_EXPERTS, d), BF16)],
        compiler_params=_cparams(("arbitrary",)),
        name="post_mixer_router",
    )(a, xall, mod, ln_g.reshape(1, d), ln_b.reshape(1, d), mod, mod, w.astype(BF16),
      router_w.T, router_bias.reshape(N_EXPERTS, 1), tri)


def _slots_kernel(pstart_ref, eidx_ref, rank_ref, dest_ref):
    e = eidx_ref[...]

    def pick(i, acc):
        return jnp.where(e == i, pstart_ref[i], acc)

    dest_ref[...] = lax.fori_loop(0, N_EXPERTS, pick, jnp.zeros_like(e)) + rank_ref[...]


def _slots(pstarts, eidx, rank):
    n_tok = eidx.shape[1]
    tn = 2048 if n_tok % 2048 == 0 else ROW_TILE
    blk = pl.BlockSpec((TOP_K, tn), lambda i, ps: (0, i))
    return pl.pallas_call(
        _slots_kernel,
        grid_spec=pltpu.PrefetchScalarGridSpec(
            num_scalar_prefetch=1, grid=(n_tok // tn,), in_specs=[blk, blk], out_specs=blk),
        out_shape=jax.ShapeDtypeStruct((TOP_K, n_tok), I32),
        compiler_params=_cparams(("arbitrary",)),
        name="moe_slots",
    )(pstarts, eidx, rank)


def _dispatch_kernel(dest_ref, h_ref, xs_hbm, sem):
    td = dest_ref.shape[1]

    def issue(t, carry):
        for k in range(TOP_K):
            pltpu.make_async_copy(h_ref.at[pl.ds(t, 1)], xs_hbm.at[pl.ds(dest_ref[k, t], 1)], sem).start()
        return carry

    lax.fori_loop(0, td, issue, 0)
    pltpu.make_async_copy(xs_hbm.at[pl.ds(0, TOP_K * td)], xs_hbm.at[pl.ds(0, TOP_K * td)], sem).wait()


def _dispatch(dest, h2, n_slots):
    n_tok = dest.shape[1]
    d = h2.shape[1]
    td = DISPATCH_TOK
    return pl.pallas_call(
        _dispatch_kernel,
        grid=(n_tok // td,),
        in_specs=[
            pl.BlockSpec((TOP_K, td), lambda i: (0, i), memory_space=pltpu.SMEM),
            pl.BlockSpec((td, d), lambda i: (i, 0)),
        ],
        out_specs=pl.BlockSpec(memory_space=pl.ANY),
        out_shape=jax.ShapeDtypeStruct((n_slots, d), h2.dtype),
        scratch_shapes=[pltpu.SemaphoreType.DMA(())],
        compiler_params=_cparams(("arbitrary",)),
        name="moe_dispatch",
    )(dest, h2)


def _gmm_kernel(gb0_ref, nblk_ref, u0_ref, n_used_ref, xa_hbm, xb_hbm, wgu_ref, wd_ref, ys_hbm,
                xbuf, ybuf, wgu_bf, wd_bf, sem_in, sem_out):
    g = pl.program_id(0)
    n_in, bm = xbuf.shape[:2]
    n_out = ybuf.shape[0]
    ff = wd_ref.shape[0]
    n_used = n_used_ref[0]
    rows_a = xa_hbm.shape[0]

    def x_copy(src_hbm, row0, b):
        rows = pl.ds(row0 if isinstance(row0, int) else pl.multiple_of(row0, bm), bm)
        return pltpu.make_async_copy(src_hbm.at[rows], xbuf.at[b % n_in], sem_in.at[b % n_in])

    def x_start(b):
        u0 = u0_ref[b]

        @pl.when(u0 < rows_a)
        def _():
            x_copy(xa_hbm, u0, b).start()

        @pl.when(u0 >= rows_a)
        def _():
            x_copy(xb_hbm, u0 - rows_a, b).start()

    def x_wait(b):
        x_copy(xa_hbm, 0, b).wait()

    def y_copy(b, slot):
        rows = pl.ds(pl.multiple_of(u0_ref[b], bm), bm)
        return pltpu.make_async_copy(ybuf.at[slot], ys_hbm.at[rows], sem_out.at[slot])

    @pl.when(g == 0)
    def _():
        for b in range(n_in - 1):
            @pl.when(b < n_used)
            def _():
                x_start(b)

    nb = nblk_ref[2 * g] + nblk_ref[2 * g + 1]

    @pl.when(nb > 0)
    def _():
        wgu_bf[...] = wgu_ref[...].astype(BF16)
        wd_bf[...] = wd_ref[...].astype(BF16)

    def block(j, carry):
        b = gb0_ref[2 * g] + j
        slot = b % n_out
        x_wait(b)

        @pl.when(b + n_in - 1 < n_used)
        def _():
            x_start(b + n_in - 1)

        @pl.when(b >= n_out)
        def _():
            y_copy(b - n_out, slot).wait()

        gu = jnp.dot(_unpack_rows_bf16(xbuf[b % n_in]), wgu_bf[...], preferred_element_type=F32)
        act = (_silu(gu[:, :ff]) * gu[:, ff:]).astype(BF16)
        ybuf[slot] = _pack_rows(jnp.dot(act, wd_bf[...], preferred_element_type=F32))
        y_copy(b, slot).start()
        return carry

    lax.fori_loop(0, nb, block, 0)

    @pl.when(g == pl.num_programs(0) - 1)
    def _():
        for back in range(n_out, 0, -1):
            @pl.when(n_used >= back)
            def _():
                y_copy(n_used - back, (n_used - back) % n_out).wait()


def _grouped_ffn(xs_a, xs_b, gb0, nblk, u0_blk, n_used, wgu, wd, layer):
    n_slots = xs_a.shape[0] + xs_b.shape[0]
    bm = MOE_BM
    n_exp, d, ff2 = wgu.shape[1:]
    ff = wd.shape[2]
    grid_spec = pltpu.PrefetchScalarGridSpec(
        num_scalar_prefetch=4,
        grid=(n_exp,),
        in_specs=[
            pl.BlockSpec(memory_space=pl.ANY),
            pl.BlockSpec(memory_space=pl.ANY),
            pl.BlockSpec((None, None, d, ff2), lambda g, *_: (layer, g, 0, 0)),
            pl.BlockSpec((None, None, ff, d), lambda g, *_: (layer, g, 0, 0)),
        ],
        out_specs=pl.BlockSpec(memory_space=pl.ANY),
        scratch_shapes=[
            pltpu.VMEM((GMM_IN_SLOTS, bm, d // 2), U32), pltpu.VMEM((GMM_OUT_SLOTS, bm, d // 2), U32),
            pltpu.VMEM((d, ff2), BF16), pltpu.VMEM((ff, d), BF16),
            pltpu.SemaphoreType.DMA((GMM_IN_SLOTS,)), pltpu.SemaphoreType.DMA((GMM_OUT_SLOTS,)),
        ],
    )
    return pl.pallas_call(
        _gmm_kernel,
        grid_spec=grid_spec,
        out_shape=jax.ShapeDtypeStruct((n_slots, d // 2), U32),
        compiler_params=_cparams(("arbitrary",)),
        name="moe_grouped_ffn",
    )(gb0, nblk, u0_blk, n_used, xs_a, xs_b, wgu, wd)


def _shared_expert(h2_ref, swgu_ref, swd_ref):
    ff = swd_ref.shape[0]
    gu = jnp.dot(_unpack_rows_bf16(h2_ref[...]), swgu_ref[...], preferred_element_type=F32)
    act = (_silu(gu[:, :ff]) * gu[:, ff:]).astype(BF16)
    return jnp.dot(act, swd_ref[...], preferred_element_type=F32)


def _combine_kernel(dest_ref, gates_ref, ys_hbm, x1_ref, h2_ref, gate_ref, lng_ref, lnb_ref,
                    swgu_ref, swd_ref, o_ref, buf, sem, *, dn_alpha):
    tc = x1_ref.shape[0]

    def issue(t, carry):
        for k in range(TOP_K):
            pltpu.make_async_copy(ys_hbm.at[pl.ds(dest_ref[k, t], 1)], buf.at[k, pl.ds(t, 1)], sem).start()
        return carry

    lax.fori_loop(0, tc, issue, 0)

    y = _shared_expert(h2_ref, swgu_ref, swd_ref)

    for k in range(TOP_K):
        pltpu.make_async_copy(ys_hbm.at[pl.ds(0, tc)], buf.at[k], sem).wait()
    for k in range(TOP_K):
        y = _add_weighted_rows(y, gates_ref[:, k:k + 1], buf[k])
    o_ref[...] = _layer_norm(dn_alpha * x1_ref[...] + gate_ref[...] * y, lng_ref[...], lnb_ref[...])


def _combine_gathered_kernel(gates_ref, g_ref, x1_ref, h2_ref, gate_ref, lng_ref, lnb_ref,
                             swgu_ref, swd_ref, prev_ref, o_ref, *, dn_alpha):
    del prev_ref
    y = _shared_expert(h2_ref, swgu_ref, swd_ref)
    for k in range(TOP_K):
        y = _add_weighted_rows(y, gates_ref[:, k:k + 1], g_ref[k])
    o_ref[...] = _layer_norm(dn_alpha * x1_ref[...] + gate_ref[...] * y, lng_ref[...], lnb_ref[...])


def _sc_gather_rows(table, idx):
    n = idx.shape[0]
    d = table.shape[1]
    mesh = plsc.VectorSubcoreMesh(core_axis_name="c", subcore_axis_name="s")
    n_workers = mesh.num_cores * mesh.num_subcores
    ch = SC_GATHER_ROWS
    per_w = n // n_workers
    n_ch = per_w // ch
    assert n % (n_workers * ch * 2) == 0

    @functools.partial(
        pl.kernel, mesh=mesh, out_type=jax.ShapeDtypeStruct((n, d), table.dtype),
        scratch_types=[pltpu.VMEM((n_ch, ch), I32),
                       pltpu.VMEM((ch, d), table.dtype), pltpu.VMEM((ch, d), table.dtype),
                       pltpu.SemaphoreType.DMA, pltpu.SemaphoreType.DMA],
        name="moe_sc_gather")
    def gather(table_hbm, idx_hbm, out_hbm, idx_all, rows_a, rows_b, sem_a, sem_b):
        wid = lax.axis_index("s") * mesh.num_cores + lax.axis_index("c")
        pltpu.sync_copy(idx_hbm.at[wid], idx_all)

        def fetch(i, rows_v, sem):
            pltpu.make_async_copy(table_hbm.at[idx_all.at[i]], rows_v, sem).start()

        def drain(i, rows_v, sem):
            pltpu.make_async_copy(table_hbm.at[idx_all.at[i]], rows_v, sem).wait()
            pltpu.sync_copy(rows_v, out_hbm.at[pl.ds(wid * per_w + i * ch, ch)])

        fetch(0, rows_a, sem_a)

        @pl.loop(0, n_ch // 2)
        def _(j):
            i = 2 * j
            fetch(i + 1, rows_b, sem_b)
            drain(i, rows_a, sem_a)

            @pl.when(i + 2 < n_ch)
            def _():
                fetch(i + 2, rows_a, sem_a)

            drain(i + 1, rows_b, sem_b)

    return gather(table, idx.reshape(n_workers, n_ch, ch))


def _sc_scatter_rows(rows, first_row, idx, n_out):
    n_k, n = idx.shape
    d = rows.shape[1]
    mesh = plsc.VectorSubcoreMesh(core_axis_name="c", subcore_axis_name="s")
    n_workers = mesh.num_cores * mesh.num_subcores
    ch = SC_GATHER_ROWS
    per_w = n // n_workers
    n_ch = per_w // ch
    assert n % (n_workers * ch) == 0

    @functools.partial(
        pl.kernel, mesh=mesh, out_type=jax.ShapeDtypeStruct((n_out, d), rows.dtype),
        scratch_types=[pltpu.VMEM((n_k, n_ch, ch), I32), pltpu.VMEM((ch, d), rows.dtype)],
        name="moe_sc_scatter")
    def scatter(rows_hbm, idx_hbm, out_hbm, idx_all, rows_v):
        wid = lax.axis_index("s") * mesh.num_cores + lax.axis_index("c")
        pltpu.sync_copy(idx_hbm.at[wid], idx_all)

        @pl.loop(0, n_ch)
        def _(i):
            pltpu.sync_copy(rows_hbm.at[pl.ds(first_row + wid * per_w + i * ch, ch)], rows_v)
            for k in range(n_k):
                pltpu.sync_copy(rows_v, out_hbm.at[idx_all.at[k, i]])

    return scatter(rows, idx.reshape(n_k, n_workers, n_ch, ch).transpose(1, 0, 2, 3))


def _combine(dest, gates_t, ys, x1, h2, mod, ln_g, ln_b, swgu, swd, n_rows, n_lat, seq, ctx_row,
             dn_alpha):
    d = x1.shape[1]
    tc = DISPATCH_TOK
    const = lambda i: (0, 0)
    n_blocks = n_rows // tc
    n_sc_blocks = (n_blocks * SC_SHARE_PERCENT) // 100
    n_tc = (n_blocks - n_sc_blocks) * tc
    n_sc = n_rows - n_tc
    gathered = _sc_gather_rows(ys, dest[:, n_tc:].reshape(TOP_K * n_sc)).reshape(TOP_K, n_sc, d // 2)
    swgu_bf, swd_bf = swgu.astype(BF16), swd.astype(BF16)
    out_tc = pl.pallas_call(
        functools.partial(_combine_kernel, dn_alpha=dn_alpha),
        grid=(n_tc // tc,),
        in_specs=[
            pl.BlockSpec((TOP_K, tc), lambda i: (0, i), memory_space=pltpu.SMEM),
            pl.BlockSpec((tc, TOP_K), lambda i: (i, 0)),
            pl.BlockSpec(memory_space=pl.ANY),
            pl.BlockSpec((tc, d), lambda i: (i, 0)),
            pl.BlockSpec((tc, d // 2), lambda i: (i, 0)),
            _mod_spec(5, tc, n_lat, seq, ctx_row, d),
            pl.BlockSpec((1, d), const), pl.BlockSpec((1, d), const),
            pl.BlockSpec(swgu.shape, const),
            pl.BlockSpec(swd.shape, const),
        ],
        out_specs=pl.BlockSpec((tc, d), lambda i: (i, 0)),
        out_shape=jax.ShapeDtypeStruct((n_rows, d), F32),
        scratch_shapes=[pltpu.VMEM((TOP_K, tc, d // 2), U32), pltpu.SemaphoreType.DMA(())],
        compiler_params=_cparams(("arbitrary",)),
        name="moe_combine",
    )(dest, gates_t, ys, x1, h2, mod, ln_g.reshape(1, d), ln_b.reshape(1, d), swgu_bf, swd_bf)

    tl = COMBINE_GATHERED_TOK
    off = n_tc // tl
    row_blk = pl.BlockSpec((tl, d), lambda i: (i + off, 0))
    return pl.pallas_call(
        functools.partial(_combine_gathered_kernel, dn_alpha=dn_alpha),
        grid=(n_sc // tl,),
        in_specs=[
            pl.BlockSpec((tl, TOP_K), lambda i: (i + off, 0)),
            pl.BlockSpec((TOP_K, tl, d // 2), lambda i: (0, i, 0)),
            row_blk, pl.BlockSpec((tl, d // 2), lambda i: (i + off, 0)),
            _mod_spec(5, tl, n_lat, seq, ctx_row, d, blk_off=off),
            pl.BlockSpec((1, d), const), pl.BlockSpec((1, d), const),
            pl.BlockSpec(swgu.shape, const),
            pl.BlockSpec(swd.shape, const),
            pl.BlockSpec(memory_space=pl.ANY),
        ],
        out_specs=row_blk,
        out_shape=jax.ShapeDtypeStruct((n_rows, d), F32),
        input_output_aliases={9: 0},
        compiler_params=_cparams(("parallel",)),
        name="moe_combine_gathered",
    )(gates_t, gathered, x1, h2, mod, ln_g.reshape(1, d), ln_b.reshape(1, d), swgu_bf, swd_bf, out_tc)


def _split_block(n_rows):
    n_blocks = n_rows // ROW_TILE
    sc_blocks = (n_blocks * SC_DISPATCH_PERCENT) // 100 // 4 * 4
    return n_blocks - sc_blocks


def _max_slots(n_tok, bm):
    return (n_tok * TOP_K + N_EXPERTS * (bm - 1) + bm - 1) // bm * bm


def _moe(x1, h2, eidx, gates, rank, counts, mod, ln_g, ln_b, wgu, wd, layer, swgu, swd,
         n_rows, n_lat, seq, ctx_row, dn_alpha, split_blk):
    bm = MOE_BM
    t0 = split_blk * ROW_TILE
    rows_a, rows_b = (_max_slots(t0, bm) if t0 else bm), _max_slots(n_rows - t0, bm)
    cnt = counts.reshape(2, N_EXPERTS).astype(I32)
    if not t0:
        cnt = cnt.at[0].set(0)
    pcnt = (cnt + bm - 1) // bm * bm
    pend = jnp.cumsum(pcnt, axis=1)
    ustart = pend - pcnt + jnp.array([[0], [rows_a]], I32)
    dest = _slots(ustart[1], eidx[:, t0:], rank[:, t0:])
    if t0:
        dest = jnp.concatenate([_slots(ustart[0], eidx[:, :t0], rank[:, :t0]), dest], axis=1)

    nblk = (pcnt // bm).T.reshape(2 * N_EXPERTS)
    gb0 = jnp.cumsum(nblk) - nblk
    u0_grp = ustart.T.reshape(2 * N_EXPERTS)
    blk = jnp.arange((rows_a + rows_b) // bm, dtype=I32)
    grp = jnp.sum((gb0[None, :] <= blk[:, None]).astype(I32), axis=1) - 1
    u0_blk = u0_grp[grp] + (blk - gb0[grp]) * bm
    n_used = jnp.sum(nblk).astype(I32).reshape(1)

    xs_a = _dispatch(dest[:, :t0], h2, rows_a) if t0 else jnp.zeros((rows_a, h2.shape[1]), h2.dtype)
    xs_b = _sc_scatter_rows(h2, t0, dest[:, t0:] - rows_a, rows_b)
    ys = _grouped_ffn(xs_a, xs_b, gb0, nblk, u0_blk, n_used, wgu, wd, layer)
    return _combine(dest, gates.T, ys, x1, h2, mod, ln_g, ln_b, swgu, swd,
                    n_rows, n_lat, seq, ctx_row, dn_alpha)


def kernel(x, c, ctx, c_ctx, ada_w, ada_b, ln_g, ln_b, attn_w_qkv, attn_q_norm, attn_k_norm, attn_w_o, conv_w_in, conv_taps, conv_w_out, router_w, router_bias, exp_w_gate_up, exp_w_down, shared_w_gate_up, shared_w_down):
    b, seq, d = x.shape
    cl = ctx.shape[1]
    depth = ada_w.shape[0]
    n_lat = b * seq
    n_ctx = b * cl
    dn_alpha = (2 * depth) ** 0.25
    assert depth == 2 and b < MOD_ROWS
    assert seq % ROW_TILE == 0 and n_ctx % ROW_TILE == 0 and seq % GRID_W == 0

    cond = jnp.zeros((MOD_ROWS, d), F32).at[:b].set(c).at[b].set(c_ctx)
    mod = _modulation(cond, ada_w, ada_b)
    xall = jnp.concatenate([x.reshape(n_lat, d), ctx.reshape(n_ctx, d)], axis=0)
    n_all = n_lat + n_ctx

    q, k, v = _qkv_project(xall, mod[0], attn_w_qkv[0], attn_q_norm[0], attn_k_norm[0], n_lat, seq, b)
    o = _attention(q, k, v, b, seq, cl, n_lat)
    x1, h2, eidx, gates, rank, counts = _post_mixer(
        o, xall, mod[0], ln_g[0, 0], ln_b[0, 0], attn_w_o[0], router_w[0], router_bias[0],
        n_all, n_lat, seq, b, dn_alpha, _split_block(n_all))
    xall = _moe(x1, h2, eidx, gates, rank, counts, mod[0], ln_g[0, 1], ln_b[0, 1],
                exp_w_gate_up, exp_w_down, 0, shared_w_gate_up[0], shared_w_down[0],
                n_all, n_lat, seq, b, dn_alpha, _split_block(n_all))

    zin = _in_project(xall, mod[1], conv_w_in[0], n_lat, n_lat, seq, b)
    a = _short_conv(zin, conv_taps[0], b, seq, d)
    x1, h2, eidx, gates, rank, counts = _post_mixer(
        a, xall, mod[1], ln_g[1, 0], ln_b[1, 0], conv_w_out[0], router_w[1], router_bias[1],
        n_lat, n_lat, seq, b, dn_alpha, _split_block(n_lat))
    out = _moe(x1, h2, eidx, gates, rank, counts, mod[1], ln_g[1, 1], ln_b[1, 1],
               exp_w_gate_up, exp_w_down, 1, shared_w_gate_up[1], shared_w_down[1],
               n_lat, n_lat, seq, b, dn_alpha, _split_block(n_lat))
    return out.reshape(b, seq, d)
```

```python
import functools

import jax
import jax.numpy as jnp
from jax import lax
from jax.experimental import pallas as pl
from jax.experimental.pallas import tpu as pltpu
from jax.experimental.pallas import tpu_sc as plsc

F32 = jnp.float32
BF16 = jnp.bfloat16
I32 = jnp.int32

N_HEADS = 8
N_KV_HEADS = 2
HEAD_DIM = 128
KV_GROUP = N_HEADS // N_KV_HEADS
GRID_W = 64
ROPE_THETA = 10000.0
N_EXPERTS = 256
TOP_K = 8
N_GROUPS = 8
TOPK_GROUPS = 4
PER_GROUP = N_EXPERTS // N_GROUPS
ROUTED_SCALE = 2.5
LN_EPS = 1e-5
QK_EPS = 1e-6
N_MOD = 6
MOD_ROWS = 16

LANES = 128
SUBLANES = 8
VMEM_LIMIT = 56 * 1024 * 1024

ROW_TILE = 512
ATTN_TQ = 256
ATTN_TK = 2048
MOE_BM = 256
GMM_IN_SLOTS = 8
GMM_OUT_SLOTS = 4
DISPATCH_TOK = 512
COMBINE_GATHERED_TOK = 512
SC_GATHER_ROWS = 64
SC_SHARE_PERCENT = 82
SC_DISPATCH_PERCENT = 100
CONV_LANES = 128

HIGHEST = lax.Precision.HIGHEST
LOG2_E = 1.4426950408889634


def _cparams(sem):
    return pltpu.CompilerParams(dimension_semantics=sem, vmem_limit_bytes=VMEM_LIMIT)


def _silu(v):
    return v * jax.nn.sigmoid(v)


U32 = jnp.uint32
HI_MASK = 0xFFFF0000


def _pack_rows(v):
    half = v.shape[1] // 2
    lo = lax.bitcast_convert_type(v[:, :half].astype(BF16).astype(F32), U32)
    hi = lax.bitcast_convert_type(v[:, half:].astype(BF16).astype(F32), U32)
    return (lo >> 16) | (hi & U32(HI_MASK))


def _add_weighted_rows(y, gate, packed):
    half = y.shape[1] // 2
    lo = lax.bitcast_convert_type(packed << 16, F32)
    hi = lax.bitcast_convert_type(packed & U32(HI_MASK), F32)
    return jnp.concatenate([y[:, :half] + gate * lo, y[:, half:] + gate * hi], axis=-1)


def _unpack_rows_bf16(w):
    lo = lax.bitcast_convert_type(w << 16, F32)
    hi = lax.bitcast_convert_type(w & U32(HI_MASK), F32)
    return jnp.concatenate([lo.astype(BF16), hi.astype(BF16)], axis=-1)


def _mod_kernel(c_ref, w_ref, b_ref, o_ref):
    s = _silu(c_ref[...])
    o_ref[...] = jnp.dot(s, w_ref[...], precision=HIGHEST, preferred_element_type=F32) + b_ref[...]


def _modulation(cond, ada_w, ada_b):
    depth, d, nd = ada_w.shape
    tn = 1536
    out = pl.pallas_call(
        _mod_kernel,
        grid=(depth, nd // tn),
        in_specs=[
            pl.BlockSpec((MOD_ROWS, d), lambda l, j: (0, 0)),
            pl.BlockSpec((None, d, tn), lambda l, j: (l, 0, j)),
            pl.BlockSpec((None, 1, tn), lambda l, j: (l, 0, j)),
        ],
        out_specs=pl.BlockSpec((None, MOD_ROWS, tn), lambda l, j: (l, 0, j)),
        out_shape=jax.ShapeDtypeStruct((depth, MOD_ROWS, nd), F32),
        compiler_params=_cparams(("arbitrary", "arbitrary")),
        name="adaln_modulation",
    )(cond, ada_w, ada_b.reshape(depth, 1, nd))
    return out.reshape(depth, MOD_ROWS * N_MOD, 1, d)


def _mod_spec(comp, tm, n_lat, seq, ctx_row, d, blk_off=0):
    def index(i, *_):
        row0 = (i + blk_off) * tm
        r = jnp.where(row0 < n_lat, row0 // seq, ctx_row)
        return (r * N_MOD + comp, 0, 0)

    return pl.BlockSpec((None, 1, d), index)


def _qkv_kernel(x_ref, shift_ref, scale_ref, w_ref, qg_ref, kg_ref, cos_ref, sin_ref,
                q_ref, k_ref, v_ref):
    h = (x_ref[...] * (1.0 + scale_ref[...]) + shift_ref[...]).astype(BF16)
    qkv = jnp.dot(h, w_ref[...], preferred_element_type=F32)
    cos = cos_ref[...]
    sin = sin_ref[...]
    hq = N_HEADS * HEAD_DIM
    kd = N_KV_HEADS * HEAD_DIM

    def norm_rope(t, g, post):
        t = t * lax.rsqrt(jnp.mean(t * t, axis=-1, keepdims=True) + QK_EPS) * g
        t = t * cos + pltpu.roll(t, HEAD_DIM // 2, axis=1) * sin
        return (t * post).astype(BF16)

    for hd in range(N_HEADS):
        sl = slice(hd * HEAD_DIM, (hd + 1) * HEAD_DIM)
        q_ref[:, sl] = norm_rope(qkv[:, sl], qg_ref[...], HEAD_DIM ** -0.5 * LOG2_E)
    for hd in range(N_KV_HEADS):
        sl = slice(hd * HEAD_DIM, (hd + 1) * HEAD_DIM)
        k_ref[:, sl] = norm_rope(qkv[:, hq + hd * HEAD_DIM: hq + (hd + 1) * HEAD_DIM], kg_ref[...], 1.0)
    v_ref[...] = qkv[:, hq + kd:].astype(BF16)


def _rope_tables(seq, tm):
    rows = seq // GRID_W
    row = jnp.repeat(jnp.arange(rows, dtype=F32), GRID_W)
    col = jnp.tile(jnp.arange(GRID_W, dtype=F32), rows)
    axis_dim = HEAD_DIM // 2
    freqs = ROPE_THETA ** (-jnp.arange(0, axis_dim, 2, dtype=F32) / axis_dim)
    ang = jnp.concatenate([row[:, None] * freqs, col[:, None] * freqs], axis=-1)
    cos = jnp.concatenate([jnp.cos(ang), jnp.cos(ang)], axis=-1)
    sin = jnp.concatenate([-jnp.sin(ang), jnp.sin(ang)], axis=-1)
    cos = jnp.concatenate([cos, jnp.ones((tm, HEAD_DIM), F32)], axis=0)
    sin = jnp.concatenate([sin, jnp.zeros((tm, HEAD_DIM), F32)], axis=0)
    return cos.reshape(seq // tm + 1, tm, HEAD_DIM), sin.reshape(seq // tm + 1, tm, HEAD_DIM)


def _qkv_project(xall, mod, w_qkv, q_g, k_g, n_lat, seq, ctx_row):
    t, d = xall.shape
    tm = ROW_TILE
    hq = N_HEADS * HEAD_DIM
    kd = N_KV_HEADS * HEAD_DIM
    perm = jnp.concatenate([jnp.arange(0, HEAD_DIM, 2), jnp.arange(1, HEAD_DIM, 2)])
    cols = jnp.concatenate([hd * HEAD_DIM + perm for hd in range(N_HEADS + N_KV_HEADS)]
                           + [jnp.arange(hq + kd, hq + 2 * kd)])
    w = w_qkv[:, cols].astype(BF16)
    cos, sin = _rope_tables(seq, tm)
    n_pos = seq // tm

    def pos_index(i):
        row0 = i * tm
        return (jnp.where(row0 < n_lat, (row0 % seq) // tm, n_pos), 0, 0)

    const = lambda i: (0, 0)
    return pl.pallas_call(
        _qkv_kernel,
        grid=(t // tm,),
        in_specs=[
            pl.BlockSpec((tm, d), lambda i: (i, 0)),
            _mod_spec(0, tm, n_lat, seq, ctx_row, d),
            _mod_spec(1, tm, n_lat, seq, ctx_row, d),
            pl.BlockSpec(w.shape, const),
            pl.BlockSpec((1, HEAD_DIM), const),
            pl.BlockSpec((1, HEAD_DIM), const),
            pl.BlockSpec((None, tm, HEAD_DIM), pos_index),
            pl.BlockSpec((None, tm, HEAD_DIM), pos_index),
        ],
        out_specs=[
            pl.BlockSpec((tm, hq), lambda i: (i, 0)),
            pl.BlockSpec((tm, kd), lambda i: (i, 0)),
            pl.BlockSpec((tm, kd), lambda i: (i, 0)),
        ],
        out_shape=[
            jax.ShapeDtypeStruct((t, hq), BF16),
            jax.ShapeDtypeStruct((t, kd), BF16),
            jax.ShapeDtypeStruct((t, kd), BF16),
        ],
        compiler_params=_cparams(("parallel",)),
        name="qkv_norm_rope",
    )(xall, mod, mod, w, q_g[perm].reshape(1, HEAD_DIM), k_g[perm].reshape(1, HEAD_DIM), cos, sin)


def _attn_kernel(*refs, n_lat_chunks, tk):
    if n_lat_chunks:
        q_ref, kc_ref, vc_ref, kl_ref, vl_ref, o_ref = refs
    else:
        q_ref, kc_ref, vc_ref, o_ref = refs
    tq = q_ref.shape[0]
    q = jnp.concatenate([q_ref[:, h * HEAD_DIM:(h + 1) * HEAD_DIM] for h in range(KV_GROUP)], axis=0)
    rows = KV_GROUP * tq

    def chunk(k, v, m, l, acc):
        s = lax.dot_general(q, k, (((1,), (1,)), ((), ())), preferred_element_type=F32)
        m_new = jnp.maximum(m, jnp.max(s, axis=-1, keepdims=True))
        p = jnp.exp2(s - m_new)
        a = jnp.exp2(m - m_new)
        l = a * l + jnp.sum(p, axis=-1, keepdims=True)
        acc = a * acc + jnp.dot(p.astype(BF16), v, preferred_element_type=F32)
        return m_new, l, acc

    m = jnp.full((rows, 1), -jnp.inf, F32)
    l = jnp.zeros((rows, 1), F32)
    acc = jnp.zeros((rows, HEAD_DIM), F32)
    m, l, acc = chunk(kc_ref[...], vc_ref[...], m, l, acc)
    for c in range(n_lat_chunks):
        m, l, acc = chunk(kl_ref[c * tk:(c + 1) * tk, :], vl_ref[c * tk:(c + 1) * tk, :], m, l, acc)
    o = (acc / l).astype(BF16)
    for h in range(KV_GROUP):
        o_ref[:, h * HEAD_DIM:(h + 1) * HEAD_DIM] = o[h * tq:(h + 1) * tq]


def _attention(q, k, v, b, seq, cl, n_lat):
    t = q.shape[0]
    gw = KV_GROUP * HEAD_DIM
    tq = min(ATTN_TQ, seq)
    tk = min(ATTN_TK, seq)
    nq = seq // tq
    ctx_blk0 = n_lat // cl
    hq = N_HEADS * HEAD_DIM

    ctx_kv = pl.BlockSpec((cl, HEAD_DIM), lambda bi, g, qi: (ctx_blk0 + bi, g))
    lat_kv = pl.BlockSpec((seq, HEAD_DIM), lambda bi, g, qi: (bi, g))
    lat_q = pl.BlockSpec((tq, gw), lambda bi, g, qi: (bi * nq + qi, g))
    o_lat = pl.pallas_call(
        functools.partial(_attn_kernel, n_lat_chunks=seq // tk, tk=tk),
        grid=(b, N_KV_HEADS, nq),
        in_specs=[lat_q, ctx_kv, ctx_kv, lat_kv, lat_kv],
        out_specs=lat_q,
        out_shape=jax.ShapeDtypeStruct((n_lat, hq), BF16),
        compiler_params=_cparams(("parallel", "parallel", "arbitrary")),
        name="attention_latent",
    )(q, k, v, k, v)

    ctx_q = pl.BlockSpec((cl, gw), lambda bi, g: (ctx_blk0 + bi, g))
    ctx_kv2 = pl.BlockSpec((cl, HEAD_DIM), lambda bi, g: (ctx_blk0 + bi, g))
    o_ctx = pl.pallas_call(
        functools.partial(_attn_kernel, n_lat_chunks=0, tk=tk),
        grid=(b, N_KV_HEADS),
        in_specs=[ctx_q, ctx_kv2, ctx_kv2],
        out_specs=pl.BlockSpec((cl, gw), lambda bi, g: (bi, g)),
        out_shape=jax.ShapeDtypeStruct((t - n_lat, hq), BF16),
        compiler_params=_cparams(("parallel", "parallel")),
        name="attention_context",
    )(q, k, v)
    return jnp.concatenate([o_lat, o_ctx], axis=0)


def _in_proj_kernel(x_ref, shift_ref, scale_ref, w_ref, o_ref):
    h = (x_ref[...] * (1.0 + scale_ref[...]) + shift_ref[...]).astype(BF16)
    o_ref[...] = jnp.dot(h, w_ref[...], preferred_element_type=F32)


def _in_project(xall, mod, w_in, n_rows, n_lat, seq, ctx_row):
    d = xall.shape[1]
    n_out = w_in.shape[1]
    tm = ROW_TILE
    return pl.pallas_call(
        _in_proj_kernel,
        grid=(n_rows // tm,),
        in_specs=[
            pl.BlockSpec((tm, d), lambda i: (i, 0)),
            _mod_spec(0, tm, n_lat, seq, ctx_row, d),
            _mod_spec(1, tm, n_lat, seq, ctx_row, d),
            pl.BlockSpec(w_in.shape, lambda i: (0, 0)),
        ],
        out_specs=pl.BlockSpec((tm, n_out), lambda i: (i, 0)),
        out_shape=jax.ShapeDtypeStruct((n_rows, n_out), F32),
        compiler_params=_cparams(("parallel",)),
        name="conv_in_proj",
    )(xall, mod, mod, w_in.astype(BF16))


def _conv_kernel(bg_ref, cg_ref, v_ref, taps_ref, o_ref):
    u = cg_ref[...] * v_ref[...]
    n = u.shape[0]
    pos = lax.broadcasted_iota(I32, u.shape, 0)
    prev = jnp.where(pos == 0, 0.0, pltpu.roll(u, 1, axis=0))
    nxt = jnp.where(pos == n - 1, 0.0, pltpu.roll(u, n - 1, axis=0))
    conv = prev * taps_ref[0:1, :] + u * taps_ref[1:2, :] + nxt * taps_ref[2:3, :]
    o_ref[...] = (bg_ref[...] * conv).astype(BF16)


def _short_conv(zin, taps, n_seqs, seq, d):
    tc = CONV_LANES
    nj = d // tc
    return pl.pallas_call(
        _conv_kernel,
        grid=(n_seqs, nj),
        in_specs=[
            pl.BlockSpec((seq, tc), lambda s, j: (s, j)),
            pl.BlockSpec((seq, tc), lambda s, j: (s, nj + j)),
            pl.BlockSpec((seq, tc), lambda s, j: (s, 2 * nj + j)),
            pl.BlockSpec((taps.shape[0], tc), lambda s, j: (0, j)),
        ],
        out_specs=pl.BlockSpec((seq, tc), lambda s, j: (s, j)),
        out_shape=jax.ShapeDtypeStruct((n_seqs * seq, d), BF16),
        compiler_params=_cparams(("parallel", "parallel")),
        name="short_conv",
    )(zin, zin, zin, taps)


def _layer_norm(z, g, b):
    mu = jnp.mean(z, axis=-1, keepdims=True)
    zc = z - mu
    var = jnp.mean(zc * zc, axis=-1, keepdims=True)
    return zc * lax.rsqrt(var + LN_EPS) * g + b


def _post_kernel(a_ref, x_ref, gate_ref, lng_ref, lnb_ref, shift_ref, scale_ref, w_ref,
                 rwt_ref, rb_ref, tri_ref,
                 x1_ref, h2_ref, eidx_ref, gates_ref, rank_ref, cnt_ref, rwh_ref, rwl_ref, *,
                 dn_alpha, split_blk):
    tm = a_ref.shape[0]

    @pl.when(pl.program_id(0) == 0)
    def _():
        hi = rwt_ref[...].astype(BF16)
        rwh_ref[...] = hi
        rwl_ref[...] = (rwt_ref[...] - hi.astype(F32)).astype(BF16)

    d = x_ref.shape[1]
    y = jnp.dot(a_ref[...], w_ref[...], preferred_element_type=F32)
    x1 = _layer_norm(dn_alpha * x_ref[...] + gate_ref[...] * y, lng_ref[...], lnb_ref[...])
    x1_ref[...] = x1
    h2 = x1 * (1.0 + scale_ref[...]) + shift_ref[...]
    h2_ref[...] = _pack_rows(h2)

    h_hi = h2.astype(BF16)
    h_lo = (h2 - h_hi.astype(F32)).astype(BF16)

    def nt_dot(w, h):
        return lax.dot_general(w, h, (((1,), (1,)), ((), ())), preferred_element_type=F32)

    logits = nt_dot(rwh_ref[...], h_hi) + (nt_dot(rwl_ref[...], h_hi) + nt_dot(rwh_ref[...], h_lo))
    scores = jax.nn.sigmoid(logits)
    biased = scores + rb_ref[...]
    neg = -jnp.inf
    big = jnp.int32(1 << 30)
    row = lax.broadcasted_iota(I32, (N_EXPERTS, tm), 0)

    def argmax_rows(vals, idx):
        mx = jnp.max(vals, axis=0, keepdims=True)
        return mx, jnp.min(jnp.where(vals == mx, idx, big), axis=0, keepdims=True)

    gs = []
    grp_row = lax.broadcasted_iota(I32, (PER_GROUP, tm), 0)
    for g in range(N_GROUPS):
        bg = biased[g * PER_GROUP:(g + 1) * PER_GROUP]
        ig = grp_row + g * PER_GROUP
        m1, i1 = argmax_rows(bg, ig)
        m2 = jnp.max(jnp.where(ig == i1, neg, bg), axis=0, keepdims=True)
        gs.append(m1 + m2)
    gsc = jnp.concatenate(gs, axis=0)
    grow = lax.broadcasted_iota(I32, (N_GROUPS, tm), 0)
    gsel = jnp.zeros((N_GROUPS, tm), F32)
    for _ in range(TOPK_GROUPS):
        _, gi = argmax_rows(gsc, grow)
        hit = grow == gi
        gsel = jnp.where(hit, 1.0, gsel)
        gsc = jnp.where(hit, neg, gsc)
    cur = jnp.concatenate(
        [jnp.where(gsel[g:g + 1] > 0.0, biased[g * PER_GROUP:(g + 1) * PER_GROUP], neg)
         for g in range(N_GROUPS)], axis=0)

    onehot = jnp.zeros((N_EXPERTS, tm), F32)
    idxs, gvals = [], []
    for _ in range(TOP_K):
        _, ei = argmax_rows(cur, row)
        hit = row == ei
        gvals.append(jnp.sum(jnp.where(hit, scores, 0.0), axis=0, keepdims=True))
        idxs.append(ei)
        onehot = jnp.where(hit, 1.0, onehot)
        cur = jnp.where(hit, neg, cur)
    gv = jnp.concatenate(gvals, axis=0)
    gates_ref[...] = gv / jnp.sum(gv, axis=0, keepdims=True) * ROUTED_SCALE
    eidx_ref[...] = jnp.concatenate(idxs, axis=0)

    @pl.when((pl.program_id(0) == 0) | (pl.program_id(0) == split_blk))
    def _():
        cnt_ref[...] = jnp.zeros_like(cnt_ref)

    prefix = jnp.dot(onehot.astype(BF16), tri_ref[...], preferred_element_type=F32)
    pos = prefix + cnt_ref[...]
    rank_ref[...] = jnp.concatenate(
        [jnp.sum(jnp.where(row == ei, pos, 0.0), axis=0, keepdims=True) for ei in idxs],
        axis=0).astype(I32)
    cnt_ref[...] = cnt_ref[...] + jnp.sum(onehot, axis=1, keepdims=True)


def _post_mixer(a, xall, mod, ln_g, ln_b, w, router_w, router_bias, n_rows, n_lat, seq, ctx_row,
                dn_alpha, split_blk):
    d = xall.shape[1]
    tm = ROW_TILE
    tri = (lax.broadcasted_iota(I32, (tm, tm), 0) < lax.broadcasted_iota(I32, (tm, tm), 1)).astype(BF16)
    const = lambda i: (0, 0)
    row_blk = pl.BlockSpec((tm, d), lambda i: (i, 0))
    k_blk = pl.BlockSpec((TOP_K, tm), lambda i: (0, i))
    return pl.pallas_call(
        functools.partial(_post_kernel, dn_alpha=dn_alpha, split_blk=split_blk),
        grid=(n_rows // tm,),
        in_specs=[
            row_blk, row_blk,
            _mod_spec(2, tm, n_lat, seq, ctx_row, d),
            pl.BlockSpec((1, d), const), pl.BlockSpec((1, d), const),
            _mod_spec(3, tm, n_lat, seq, ctx_row, d),
            _mod_spec(4, tm, n_lat, seq, ctx_row, d),
            pl.BlockSpec(w.shape, const),
            pl.BlockSpec((N_EXPERTS, d), const),
            pl.BlockSpec((N_EXPERTS, 1), const),
            pl.BlockSpec((tm, tm), const),
        ],
        out_specs=[
            row_blk, pl.BlockSpec((tm, d // 2), lambda i: (i, 0)),
            k_blk, k_blk, k_blk,
            pl.BlockSpec((None, N_EXPERTS, 1), lambda i: (jnp.where(i >= split_blk, 1, 0), 0, 0)),
        ],
        out_shape=[
            jax.ShapeDtypeStruct((n_rows, d), F32),
            jax.ShapeDtypeStruct((n_rows, d // 2), U32),
            jax.ShapeDtypeStruct((TOP_K, n_rows), I32),
            jax.ShapeDtypeStruct((TOP_K, n_rows), F32),
            jax.ShapeDtypeStruct((TOP_K, n_rows), I32),
            jax.ShapeDtypeStruct((2, N_EXPERTS, 1), F32),
        ],
        scratch_shapes=[pltpu.VMEM((N_EXPERTS, d), BF16), pltpu.VMEM((N_EXPERTS, d), BF16)],
        compiler_params=_cparams(("arbitrary",)),
        name="post_mixer_router",
    )(a, xall, mod, ln_g.reshape(1, d), ln_b.reshape(1, d), mod, mod, w.astype(BF16),
      router_w.T, router_bias.reshape(N_EXPERTS, 1), tri)


def _slots_kernel(pstart_ref, eidx_ref, rank_ref, dest_ref):
    e = eidx_ref[...]

    def pick(i, acc):
        return jnp.where(e == i, pstart_ref[i], acc)

    dest_ref[...] = lax.fori_loop(0, N_EXPERTS, pick, jnp.zeros_like(e)) + rank_ref[...]


def _slots(pstarts, eidx, rank):
    n_tok = eidx.shape[1]
    tn = 2048 if n_tok % 2048 == 0 else ROW_TILE
    blk = pl.BlockSpec((TOP_K, tn), lambda i, ps: (0, i))
    return pl.pallas_call(
        _slots_kernel,
        grid_spec=pltpu.PrefetchScalarGridSpec(
            num_scalar_prefetch=1, grid=(n_tok // tn,), in_specs=[blk, blk], out_specs=blk),
        out_shape=jax.ShapeDtypeStruct((TOP_K, n_tok), I32),
        compiler_params=_cparams(("arbitrary",)),
        name="moe_slots",
    )(pstarts, eidx, rank)


def _dispatch_kernel(dest_ref, h_ref, xs_hbm, sem):
    td = dest_ref.shape[1]

    def issue(t, carry):
        for k in range(TOP_K):
            pltpu.make_async_copy(h_ref.at[pl.ds(t, 1)], xs_hbm.at[pl.ds(dest_ref[k, t], 1)], sem).start()
        return carry

    lax.fori_loop(0, td, issue, 0)
    pltpu.make_async_copy(xs_hbm.at[pl.ds(0, TOP_K * td)], xs_hbm.at[pl.ds(0, TOP_K * td)], sem).wait()


def _dispatch(dest, h2, n_slots):
    n_tok = dest.shape[1]
    d = h2.shape[1]
    td = DISPATCH_TOK
    return pl.pallas_call(
        _dispatch_kernel,
        grid=(n_tok // td,),
        in_specs=[
            pl.BlockSpec((TOP_K, td), lambda i: (0, i), memory_space=pltpu.SMEM),
            pl.BlockSpec((td, d), lambda i: (i, 0)),
        ],
        out_specs=pl.BlockSpec(memory_space=pl.ANY),
        out_shape=jax.ShapeDtypeStruct((n_slots, d), h2.dtype),
        scratch_shapes=[pltpu.SemaphoreType.DMA(())],
        compiler_params=_cparams(("arbitrary",)),
        name="moe_dispatch",
    )(dest, h2)


def _gmm_kernel(gb0_ref, nblk_ref, u0_ref, n_used_ref, xa_hbm, xb_hbm, wgu_ref, wd_ref, ys_hbm,
                xbuf, ybuf, wgu_bf, wd_bf, sem_in, sem_out):
    g = pl.program_id(0)
    n_in, bm = xbuf.shape[:2]
    n_out = ybuf.shape[0]
    ff = wd_ref.shape[0]
    n_used = n_used_ref[0]
    rows_a = xa_hbm.shape[0]

    def x_copy(src_hbm, row0, b):
        rows = pl.ds(row0 if isinstance(row0, int) else pl.multiple_of(row0, bm), bm)
        return pltpu.make_async_copy(src_hbm.at[rows], xbuf.at[b % n_in], sem_in.at[b % n_in])

    def x_start(b):
        u0 = u0_ref[b]

        @pl.when(u0 < rows_a)
        def _():
            x_copy(xa_hbm, u0, b).start()

        @pl.when(u0 >= rows_a)
        def _():
            x_copy(xb_hbm, u0 - rows_a, b).start()

    def x_wait(b):
        x_copy(xa_hbm, 0, b).wait()

    def y_copy(b, slot):
        rows = pl.ds(pl.multiple_of(u0_ref[b], bm), bm)
        return pltpu.make_async_copy(ybuf.at[slot], ys_hbm.at[rows], sem_out.at[slot])

    @pl.when(g == 0)
    def _():
        for b in range(n_in - 1):
            @pl.when(b < n_used)
            def _():
                x_start(b)

    nb = nblk_ref[2 * g] + nblk_ref[2 * g + 1]

    @pl.when(nb > 0)
    def _():
        wgu_bf[...] = wgu_ref[...].astype(BF16)
        wd_bf[...] = wd_ref[...].astype(BF16)

    def block(j, carry):
        b = gb0_ref[2 * g] + j
        slot = b % n_out
        x_wait(b)

        @pl.when(b + n_in - 1 < n_used)
        def _():
            x_start(b + n_in - 1)

        @pl.when(b >= n_out)
        def _():
            y_copy(b - n_out, slot).wait()

        gu = jnp.dot(_unpack_rows_bf16(xbuf[b % n_in]), wgu_bf[...], preferred_element_type=F32)
        act = (_silu(gu[:, :ff]) * gu[:, ff:]).astype(BF16)
        ybuf[slot] = _pack_rows(jnp.dot(act, wd_bf[...], preferred_element_type=F32))
        y_copy(b, slot).start()
        return carry

    lax.fori_loop(0, nb, block, 0)

    @pl.when(g == pl.num_programs(0) - 1)
    def _():
        for back in range(n_out, 0, -1):
            @pl.when(n_used >= back)
            def _():
                y_copy(n_used - back, (n_used - back) % n_out).wait()


def _grouped_ffn(xs_a, xs_b, gb0, nblk, u0_blk, n_used, wgu, wd, layer):
    n_slots = xs_a.shape[0] + xs_b.shape[0]
    bm = MOE_BM
    n_exp, d, ff2 = wgu.shape[1:]
    ff = wd.shape[2]
    grid_spec = pltpu.PrefetchScalarGridSpec(
        num_scalar_prefetch=4,
        grid=(n_exp,),
        in_specs=[
            pl.BlockSpec(memory_space=pl.ANY),
            pl.BlockSpec(memory_space=pl.ANY),
            pl.BlockSpec((None, None, d, ff2), lambda g, *_: (layer, g, 0, 0)),
            pl.BlockSpec((None, None, ff, d), lambda g, *_: (layer, g, 0, 0)),
        ],
        out_specs=pl.BlockSpec(memory_space=pl.ANY),
        scratch_shapes=[
            pltpu.VMEM((GMM_IN_SLOTS, bm, d // 2), U32), pltpu.VMEM((GMM_OUT_SLOTS, bm, d // 2), U32),
            pltpu.VMEM((d, ff2), BF16), pltpu.VMEM((ff, d), BF16),
            pltpu.SemaphoreType.DMA((GMM_IN_SLOTS,)), pltpu.SemaphoreType.DMA((GMM_OUT_SLOTS,)),
        ],
    )
    return pl.pallas_call(
        _gmm_kernel,
        grid_spec=grid_spec,
        out_shape=jax.ShapeDtypeStruct((n_slots, d // 2), U32),
        compiler_params=_cparams(("arbitrary",)),
        name="moe_grouped_ffn",
    )(gb0, nblk, u0_blk, n_used, xs_a, xs_b, wgu, wd)


def _shared_expert(h2_ref, swgu_ref, swd_ref):
    ff = swd_ref.shape[0]
    gu = jnp.dot(_unpack_rows_bf16(h2_ref[...]), swgu_ref[...], preferred_element_type=F32)
    act = (_silu(gu[:, :ff]) * gu[:, ff:]).astype(BF16)
    return jnp.dot(act, swd_ref[...], preferred_element_type=F32)


def _combine_kernel(dest_ref, gates_ref, ys_hbm, x1_ref, h2_ref, gate_ref, lng_ref, lnb_ref,
                    swgu_ref, swd_ref, o_ref, buf, sem, *, dn_alpha):
    tc = x1_ref.shape[0]

    def issue(t, carry):
        for k in range(TOP_K):
            pltpu.make_async_copy(ys_hbm.at[pl.ds(dest_ref[k, t], 1)], buf.at[k, pl.ds(t, 1)], sem).start()
        return carry

    lax.fori_loop(0, tc, issue, 0)

    y = _shared_expert(h2_ref, swgu_ref, swd_ref)

    for k in range(TOP_K):
        pltpu.make_async_copy(ys_hbm.at[pl.ds(0, tc)], buf.at[k], sem).wait()
    for k in range(TOP_K):
        y = _add_weighted_rows(y, gates_ref[:, k:k + 1], buf[k])
    o_ref[...] = _layer_norm(dn_alpha * x1_ref[...] + gate_ref[...] * y, lng_ref[...], lnb_ref[...])


def _combine_gathered_kernel(gates_ref, g_ref, x1_ref, h2_ref, gate_ref, lng_ref, lnb_ref,
                             swgu_ref, swd_ref, prev_ref, o_ref, *, dn_alpha):
    del prev_ref
    y = _shared_expert(h2_ref, swgu_ref, swd_ref)
    for k in range(TOP_K):
        y = _add_weighted_rows(y, gates_ref[:, k:k + 1], g_ref[k])
    o_ref[...] = _layer_norm(dn_alpha * x1_ref[...] + gate_ref[...] * y, lng_ref[...], lnb_ref[...])


def _sc_gather_rows(table, idx):
    n = idx.shape[0]
    d = table.shape[1]
    mesh = plsc.VectorSubcoreMesh(core_axis_name="c", subcore_axis_name="s")
    n_workers = mesh.num_cores * mesh.num_subcores
    ch = SC_GATHER_ROWS
    per_w = n // n_workers
    n_ch = per_w // ch
    assert n % (n_workers * ch * 2) == 0

    @functools.partial(
        pl.kernel, mesh=mesh, out_type=jax.ShapeDtypeStruct((n, d), table.dtype),
        scratch_types=[pltpu.VMEM((n_ch, ch), I32),
                       pltpu.VMEM((ch, d), table.dtype), pltpu.VMEM((ch, d), table.dtype),
                       pltpu.SemaphoreType.DMA, pltpu.SemaphoreType.DMA],
        name="moe_sc_gather")
    def gather(table_hbm, idx_hbm, out_hbm, idx_all, rows_a, rows_b, sem_a, sem_b):
        wid = lax.axis_index("s") * mesh.num_cores + lax.axis_index("c")
        pltpu.sync_copy(idx_hbm.at[wid], idx_all)

        def fetch(i, rows_v, sem):
            pltpu.make_async_copy(table_hbm.at[idx_all.at[i]], rows_v, sem).start()

        def drain(i, rows_v, sem):
            pltpu.make_async_copy(table_hbm.at[idx_all.at[i]], rows_v, sem).wait()
            pltpu.sync_copy(rows_v, out_hbm.at[pl.ds(wid * per_w + i * ch, ch)])

        fetch(0, rows_a, sem_a)

        @pl.loop(0, n_ch // 2)
        def _(j):
            i = 2 * j
            fetch(i + 1, rows_b, sem_b)
            drain(i, rows_a, sem_a)

            @pl.when(i + 2 < n_ch)
            def _():
                fetch(i + 2, rows_a, sem_a)

            drain(i + 1, rows_b, sem_b)

    return gather(table, idx.reshape(n_workers, n_ch, ch))


def _sc_scatter_rows(rows, first_row, idx, n_out):
    n_k, n = idx.shape
    d = rows.shape[1]
    mesh = plsc.VectorSubcoreMesh(core_axis_name="c", subcore_axis_name="s")
    n_workers = mesh.num_cores * mesh.num_subcores
    ch = SC_GATHER_ROWS
    per_w = n // n_workers
    n_ch = per_w // ch
    assert n % (n_workers * ch) == 0

    @functools.partial(
        pl.kernel, mesh=mesh, out_type=jax.ShapeDtypeStruct((n_out, d), rows.dtype),
        scratch_types=[pltpu.VMEM((n_k, n_ch, ch), I32), pltpu.VMEM((ch, d), rows.dtype)],
        name="moe_sc_scatter")
    def scatter(rows_hbm, idx_hbm, out_hbm, idx_all, rows_v):
        wid = lax.axis_index("s") * mesh.num_cores + lax.axis_index("c")
        pltpu.sync_copy(idx_hbm.at[wid], idx_all)

        @pl.loop(0, n_ch)
        def _(i):
            pltpu.sync_copy(rows_hbm.at[pl.ds(first_row + wid * per_w + i * ch, ch)], rows_v)
            for k in range(n_k):
                pltpu.sync_copy(rows_v, out_hbm.at[idx_all.at[k, i]])

    return scatter(rows, idx.reshape(n_k, n_workers, n_ch, ch).transpose(1, 0, 2, 3))


def _combine(dest, gates_t, ys, x1, h2, mod, ln_g, ln_b, swgu, swd, n_rows, n_lat, seq, ctx_row,
             dn_alpha):
    d = x1.shape[1]
    tc = DISPATCH_TOK
    const = lambda i: (0, 0)
    n_blocks = n_rows // tc
    n_sc_blocks = (n_blocks * SC_SHARE_PERCENT) // 100
    n_tc = (n_blocks - n_sc_blocks) * tc
    n_sc = n_rows - n_tc
    gathered = _sc_gather_rows(ys, dest[:, n_tc:].reshape(TOP_K * n_sc)).reshape(TOP_K, n_sc, d // 2)
    swgu_bf, swd_bf = swgu.astype(BF16), swd.astype(BF16)
    out_tc = pl.pallas_call(
        functools.partial(_combine_kernel, dn_alpha=dn_alpha),
        grid=(n_tc // tc,),
        in_specs=[
            pl.BlockSpec((TOP_K, tc), lambda i: (0, i), memory_space=pltpu.SMEM),
            pl.BlockSpec((tc, TOP_K), lambda i: (i, 0)),
            pl.BlockSpec(memory_space=pl.ANY),
            pl.BlockSpec((tc, d), lambda i: (i, 0)),
            pl.BlockSpec((tc, d // 2), lambda i: (i, 0)),
            _mod_spec(5, tc, n_lat, seq, ctx_row, d),
            pl.BlockSpec((1, d), const), pl.BlockSpec((1, d), const),
            pl.BlockSpec(swgu.shape, const),
            pl.BlockSpec(swd.shape, const),
        ],
        out_specs=pl.BlockSpec((tc, d), lambda i: (i, 0)),
        out_shape=jax.ShapeDtypeStruct((n_rows, d), F32),
        scratch_shapes=[pltpu.VMEM((TOP_K, tc, d // 2), U32), pltpu.SemaphoreType.DMA(())],
        compiler_params=_cparams(("arbitrary",)),
        name="moe_combine",
    )(dest, gates_t, ys, x1, h2, mod, ln_g.reshape(1, d), ln_b.reshape(1, d), swgu_bf, swd_bf)

    tl = COMBINE_GATHERED_TOK
    off = n_tc // tl
    row_blk = pl.BlockSpec((tl, d), lambda i: (i + off, 0))
    return pl.pallas_call(
        functools.partial(_combine_gathered_kernel, dn_alpha=dn_alpha),
        grid=(n_sc // tl,),
        in_specs=[
            pl.BlockSpec((tl, TOP_K), lambda i: (i + off, 0)),
            pl.BlockSpec((TOP_K, tl, d // 2), lambda i: (0, i, 0)),
            row_blk, pl.BlockSpec((tl, d // 2), lambda i: (i + off, 0)),
            _mod_spec(5, tl, n_lat, seq, ctx_row, d, blk_off=off),
            pl.BlockSpec((1, d), const), pl.BlockSpec((1, d), const),
            pl.BlockSpec(swgu.shape, const),
            pl.BlockSpec(swd.shape, const),
            pl.BlockSpec(memory_space=pl.ANY),
        ],
        out_specs=row_blk,
        out_shape=jax.ShapeDtypeStruct((n_rows, d), F32),
        input_output_aliases={9: 0},
        compiler_params=_cparams(("parallel",)),
        name="moe_combine_gathered",
    )(gates_t, gathered, x1, h2, mod, ln_g.reshape(1, d), ln_b.reshape(1, d), swgu_bf, swd_bf, out_tc)


def _split_block(n_rows):
    n_blocks = n_rows // ROW_TILE
    sc_blocks = (n_blocks * SC_DISPATCH_PERCENT) // 100 // 4 * 4
    return n_blocks - sc_blocks


def _max_slots(n_tok, bm):
    return (n_tok * TOP_K + N_EXPERTS * (bm - 1) + bm - 1) // bm * bm


def _moe(x1, h2, eidx, gates, rank, counts, mod, ln_g, ln_b, wgu, wd, layer, swgu, swd,
         n_rows, n_lat, seq, ctx_row, dn_alpha, split_blk):
    bm = MOE_BM
    t0 = split_blk * ROW_TILE
    rows_a, rows_b = (_max_slots(t0, bm) if t0 else bm), _max_slots(n_rows - t0, bm)
    cnt = counts.reshape(2, N_EXPERTS).astype(I32)
    if not t0:
        cnt = cnt.at[0].set(0)
    pcnt = (cnt + bm - 1) // bm * bm
    pend = jnp.cumsum(pcnt, axis=1)
    ustart = pend - pcnt + jnp.array([[0], [rows_a]], I32)
    dest = _slots(ustart[1], eidx[:, t0:], rank[:, t0:])
    if t0:
        dest = jnp.concatenate([_slots(ustart[0], eidx[:, :t0], rank[:, :t0]), dest], axis=1)

    nblk = (pcnt // bm).T.reshape(2 * N_EXPERTS)
    gb0 = jnp.cumsum(nblk) - nblk
    u0_grp = ustart.T.reshape(2 * N_EXPERTS)
    blk = jnp.arange((rows_a + rows_b) // bm, dtype=I32)
    grp = jnp.sum((gb0[None, :] <= blk[:, None]).astype(I32), axis=1) - 1
    u0_blk = u0_grp[grp] + (blk - gb0[grp]) * bm
    n_used = jnp.sum(nblk).astype(I32).reshape(1)

    xs_a = _dispatch(dest[:, :t0], h2, rows_a) if t0 else jnp.zeros((rows_a, h2.shape[1]), h2.dtype)
    xs_b = _sc_scatter_rows(h2, t0, dest[:, t0:] - rows_a, rows_b)
    ys = _grouped_ffn(xs_a, xs_b, gb0, nblk, u0_blk, n_used, wgu, wd, layer)
    return _combine(dest, gates.T, ys, x1, h2, mod, ln_g, ln_b, swgu, swd,
                    n_rows, n_lat, seq, ctx_row, dn_alpha)


def kernel(x, c, ctx, c_ctx, ada_w, ada_b, ln_g, ln_b, attn_w_qkv, attn_q_norm, attn_k_norm, attn_w_o, conv_w_in, conv_taps, conv_w_out, router_w, router_bias, exp_w_gate_up, exp_w_down, shared_w_gate_up, shared_w_down):
    b, seq, d = x.shape
    cl = ctx.shape[1]
    depth = ada_w.shape[0]
    n_lat = b * seq
    n_ctx = b * cl
    dn_alpha = (2 * depth) ** 0.25
    assert depth == 2 and b < MOD_ROWS
    assert seq % ROW_TILE == 0 and n_ctx % ROW_TILE == 0 and seq % GRID_W == 0

    cond = jnp.zeros((MOD_ROWS, d), F32).at[:b].set(c).at[b].set(c_ctx)
    mod = _modulation(cond, ada_w, ada_b)
    xall = jnp.concatenate([x.reshape(n_lat, d), ctx.reshape(n_ctx, d)], axis=0)
    n_all = n_lat + n_ctx

    q, k, v = _qkv_project(xall, mod[0], attn_w_qkv[0], attn_q_norm[0], attn_k_norm[0], n_lat, seq, b)
    o = _attention(q, k, v, b, seq, cl, n_lat)
    x1, h2, eidx, gates, rank, counts = _post_mixer(
        o, xall, mod[0], ln_g[0, 0], ln_b[0, 0], attn_w_o[0], router_w[0], router_bias[0],
        n_all, n_lat, seq, b, dn_alpha, _split_block(n_all))
    xall = _moe(x1, h2, eidx, gates, rank, counts, mod[0], ln_g[0, 1], ln_b[0, 1],
                exp_w_gate_up, exp_w_down, 0, shared_w_gate_up[0], shared_w_down[0],
                n_all, n_lat, seq, b, dn_alpha, _split_block(n_all))

    zin = _in_project(xall, mod[1], conv_w_in[0], n_lat, n_lat, seq, b)
    a = _short_conv(zin, conv_taps[0], b, seq, d)
    x1, h2, eidx, gates, rank, counts = _post_mixer(
        a, xall, mod[1], ln_g[1, 0], ln_b[1, 0], conv_w_out[0], router_w[1], router_bias[1],
        n_lat, n_lat, seq, b, dn_alpha, _split_block(n_lat))
    out = _moe(x1, h2, eidx, gates, rank, counts, mod[1], ln_g[1, 1], ln_b[1, 1],
               exp_w_gate_up, exp_w_down, 1, shared_w_gate_up[1], shared_w_down[1],
               n_lat, n_lat, seq, b, dn_alpha, _split_block(n_lat))
    return out.reshape(b, seq, d)
```

```python
import functools

import jax
import jax.numpy as jnp
from jax import lax
from jax.experimental import pallas as pl
from jax.experimental.pallas import tpu as pltpu
from jax.experimental.pallas import tpu_sc as plsc

F32 = jnp.float32
BF16 = jnp.bfloat16
I32 = jnp.int32

N_HEADS = 8
N_KV_HEADS = 2
HEAD_DIM = 128
KV_GROUP = N_HEADS // N_KV_HEADS
GRID_W = 64
ROPE_THETA = 10000.0
N_EXPERTS = 256
TOP_K = 8
N_GROUPS = 8
TOPK_GROUPS = 4
PER_GROUP = N_EXPERTS // N_GROUPS
ROUTED_SCALE = 2.5
LN_EPS = 1e-5
QK_EPS = 1e-6
N_MOD = 6
MOD_ROWS = 16

LANES = 128
SUBLANES = 8
VMEM_LIMIT = 56 * 1024 * 1024

ROW_TILE = 512
ATTN_TQ = 256
ATTN_TK = 2048
MOE_BM = 256
GMM_IN_SLOTS = 4
GMM_OUT_SLOTS = 3
DISPATCH_TOK = 512
COMBINE_GATHERED_TOK = 512
SC_GATHER_ROWS = 64
SC_SHARE_PERCENT = 90
COMBINE_STAGES = 3
SC_DISPATCH_PERCENT = 100
CONV_LANES = 128

HIGHEST = lax.Precision.HIGHEST
LOG2_E = 1.4426950408889634


def _cparams(sem):
    return pltpu.CompilerParams(dimension_semantics=sem, vmem_limit_bytes=VMEM_LIMIT)


def _silu(v):
    return v * jax.nn.sigmoid(v)


U32 = jnp.uint32
HI_MASK = 0xFFFF0000


def _pack_rows(v):
    half = v.shape[1] // 2
    lo = lax.bitcast_convert_type(v[:, :half].astype(BF16).astype(F32), U32)
    hi = lax.bitcast_convert_type(v[:, half:].astype(BF16).astype(F32), U32)
    return (lo >> 16) | (hi & U32(HI_MASK))


def _add_weighted_rows(y, gate, packed):
    half = y.shape[1] // 2
    lo = lax.bitcast_convert_type(packed << 16, F32)
    hi = lax.bitcast_convert_type(packed & U32(HI_MASK), F32)
    return jnp.concatenate([y[:, :half] + gate * lo, y[:, half:] + gate * hi], axis=-1)


def _unpack_rows_bf16(w):
    lo = lax.bitcast_convert_type(w << 16, F32)
    hi = lax.bitcast_convert_type(w & U32(HI_MASK), F32)
    return jnp.concatenate([lo.astype(BF16), hi.astype(BF16)], axis=-1)


def _mod_kernel(c_ref, w_ref, b_ref, o_ref):
    s = _silu(c_ref[...])
    o_ref[...] = jnp.dot(s, w_ref[...], precision=HIGHEST, preferred_element_type=F32) + b_ref[...]


def _modulation(cond, ada_w, ada_b):
    depth, d, nd = ada_w.shape
    tn = 1536
    out = pl.pallas_call(
        _mod_kernel,
        grid=(depth, nd // tn),
        in_specs=[
            pl.BlockSpec((MOD_ROWS, d), lambda l, j: (0, 0)),
            pl.BlockSpec((None, d, tn), lambda l, j: (l, 0, j)),
            pl.BlockSpec((None, 1, tn), lambda l, j: (l, 0, j)),
        ],
        out_specs=pl.BlockSpec((None, MOD_ROWS, tn), lambda l, j: (l, 0, j)),
        out_shape=jax.ShapeDtypeStruct((depth, MOD_ROWS, nd), F32),
        compiler_params=_cparams(("arbitrary", "arbitrary")),
        name="adaln_modulation",
    )(cond, ada_w, ada_b.reshape(depth, 1, nd))
    return out.reshape(depth, MOD_ROWS * N_MOD, 1, d)


def _mod_spec(comp, tm, n_lat, seq, ctx_row, d, blk_off=0):
    def index(i, *_):
        row0 = (i + blk_off) * tm
        r = jnp.where(row0 < n_lat, row0 // seq, ctx_row)
        return (r * N_MOD + comp, 0, 0)

    return pl.BlockSpec((None, 1, d), index)


def _qkv_kernel(x_ref, shift_ref, scale_ref, w_ref, qg_ref, kg_ref, cos_ref, sin_ref,
                q_ref, k_ref, v_ref):
    h = (x_ref[...] * (1.0 + scale_ref[...]) + shift_ref[...]).astype(BF16)
    qkv = jnp.dot(h, w_ref[...], preferred_element_type=F32)
    cos = cos_ref[...]
    sin = sin_ref[...]
    hq = N_HEADS * HEAD_DIM
    kd = N_KV_HEADS * HEAD_DIM

    def norm_rope(t, g, post):
        t = t * lax.rsqrt(jnp.mean(t * t, axis=-1, keepdims=True) + QK_EPS) * g
        t = t * cos + pltpu.roll(t, HEAD_DIM // 2, axis=1) * sin
        return (t * post).astype(BF16)

    for hd in range(N_HEADS):
        sl = slice(hd * HEAD_DIM, (hd + 1) * HEAD_DIM)
        q_ref[:, sl] = norm_rope(qkv[:, sl], qg_ref[...], HEAD_DIM ** -0.5 * LOG2_E)
    for hd in range(N_KV_HEADS):
        sl = slice(hd * HEAD_DIM, (hd + 1) * HEAD_DIM)
        k_ref[:, sl] = norm_rope(qkv[:, hq + hd * HEAD_DIM: hq + (hd + 1) * HEAD_DIM], kg_ref[...], 1.0)
    v_ref[...] = qkv[:, hq + kd:].astype(BF16)


def _rope_tables(seq, tm):
    rows = seq // GRID_W
    row = jnp.repeat(jnp.arange(rows, dtype=F32), GRID_W)
    col = jnp.tile(jnp.arange(GRID_W, dtype=F32), rows)
    axis_dim = HEAD_DIM // 2
    freqs = ROPE_THETA ** (-jnp.arange(0, axis_dim, 2, dtype=F32) / axis_dim)
    ang = jnp.concatenate([row[:, None] * freqs, col[:, None] * freqs], axis=-1)
    cos = jnp.concatenate([jnp.cos(ang), jnp.cos(ang)], axis=-1)
    sin = jnp.concatenate([-jnp.sin(ang), jnp.sin(ang)], axis=-1)
    cos = jnp.concatenate([cos, jnp.ones((tm, HEAD_DIM), F32)], axis=0)
    sin = jnp.concatenate([sin, jnp.zeros((tm, HEAD_DIM), F32)], axis=0)
    return cos.reshape(seq // tm + 1, tm, HEAD_DIM), sin.reshape(seq // tm + 1, tm, HEAD_DIM)


def _qkv_project(xall, mod, w_qkv, q_g, k_g, n_lat, seq, ctx_row):
    t, d = xall.shape
    tm = ROW_TILE
    hq = N_HEADS * HEAD_DIM
    kd = N_KV_HEADS * HEAD_DIM
    perm = jnp.concatenate([jnp.arange(0, HEAD_DIM, 2), jnp.arange(1, HEAD_DIM, 2)])
    cols = jnp.concatenate([hd * HEAD_DIM + perm for hd in range(N_HEADS + N_KV_HEADS)]
                           + [jnp.arange(hq + kd, hq + 2 * kd)])
    w = w_qkv[:, cols].astype(BF16)
    cos, sin = _rope_tables(seq, tm)
    n_pos = seq // tm

    def pos_index(i):
        row0 = i * tm
        return (jnp.where(row0 < n_lat, (row0 % seq) // tm, n_pos), 0, 0)

    const = lambda i: (0, 0)
    return pl.pallas_call(
        _qkv_kernel,
        grid=(t // tm,),
        in_specs=[
            pl.BlockSpec((tm, d), lambda i: (i, 0)),
            _mod_spec(0, tm, n_lat, seq, ctx_row, d),
            _mod_spec(1, tm, n_lat, seq, ctx_row, d),
            pl.BlockSpec(w.shape, const),
            pl.BlockSpec((1, HEAD_DIM), const),
            pl.BlockSpec((1, HEAD_DIM), const),
            pl.BlockSpec((None, tm, HEAD_DIM), pos_index),
            pl.BlockSpec((None, tm, HEAD_DIM), pos_index),
        ],
        out_specs=[
            pl.BlockSpec((tm, hq), lambda i: (i, 0)),
            pl.BlockSpec((tm, kd), lambda i: (i, 0)),
            pl.BlockSpec((tm, kd), lambda i: (i, 0)),
        ],
        out_shape=[
            jax.ShapeDtypeStruct((t, hq), BF16),
            jax.ShapeDtypeStruct((t, kd), BF16),
            jax.ShapeDtypeStruct((t, kd), BF16),
        ],
        compiler_params=_cparams(("parallel",)),
        name="qkv_norm_rope",
    )(xall, mod, mod, w, q_g[perm].reshape(1, HEAD_DIM), k_g[perm].reshape(1, HEAD_DIM), cos, sin)


def _attn_kernel(*refs, n_lat_chunks, tk):
    if n_lat_chunks:
        q_ref, kc_ref, vc_ref, kl_ref, vl_ref, o_ref = refs
    else:
        q_ref, kc_ref, vc_ref, o_ref = refs
    tq = q_ref.shape[0]
    q = jnp.concatenate([q_ref[:, h * HEAD_DIM:(h + 1) * HEAD_DIM] for h in range(KV_GROUP)], axis=0)
    rows = KV_GROUP * tq

    def chunk(k, v, m, l, acc):
        s = lax.dot_general(q, k, (((1,), (1,)), ((), ())), preferred_element_type=F32)
        m_new = jnp.maximum(m, jnp.max(s, axis=-1, keepdims=True))
        p = jnp.exp2(s - m_new)
        a = jnp.exp2(m - m_new)
        l = a * l + jnp.sum(p, axis=-1, keepdims=True)
        acc = a * acc + jnp.dot(p.astype(BF16), v, preferred_element_type=F32)
        return m_new, l, acc

    m = jnp.full((rows, 1), -jnp.inf, F32)
    l = jnp.zeros((rows, 1), F32)
    acc = jnp.zeros((rows, HEAD_DIM), F32)
    m, l, acc = chunk(kc_ref[...], vc_ref[...], m, l, acc)
    for c in range(n_lat_chunks):
        m, l, acc = chunk(kl_ref[c * tk:(c + 1) * tk, :], vl_ref[c * tk:(c + 1) * tk, :], m, l, acc)
    o = (acc / l).astype(BF16)
    for h in range(KV_GROUP):
        o_ref[:, h * HEAD_DIM:(h + 1) * HEAD_DIM] = o[h * tq:(h + 1) * tq]


def _attention(q, k, v, b, seq, cl, n_lat):
    t = q.shape[0]
    gw = KV_GROUP * HEAD_DIM
    tq = min(ATTN_TQ, seq)
    tk = min(ATTN_TK, seq)
    nq = seq // tq
    ctx_blk0 = n_lat // cl
    hq = N_HEADS * HEAD_DIM

    ctx_kv = pl.BlockSpec((cl, HEAD_DIM), lambda bi, g, qi: (ctx_blk0 + bi, g))
    lat_kv = pl.BlockSpec((seq, HEAD_DIM), lambda bi, g, qi: (bi, g))
    lat_q = pl.BlockSpec((tq, gw), lambda bi, g, qi: (bi * nq + qi, g))
    o_lat = pl.pallas_call(
        functools.partial(_attn_kernel, n_lat_chunks=seq // tk, tk=tk),
        grid=(b, N_KV_HEADS, nq),
        in_specs=[lat_q, ctx_kv, ctx_kv, lat_kv, lat_kv],
        out_specs=lat_q,
        out_shape=jax.ShapeDtypeStruct((n_lat, hq), BF16),
        compiler_params=_cparams(("parallel", "parallel", "arbitrary")),
        name="attention_latent",
    )(q, k, v, k, v)

    ctx_q = pl.BlockSpec((cl, gw), lambda bi, g: (ctx_blk0 + bi, g))
    ctx_kv2 = pl.BlockSpec((cl, HEAD_DIM), lambda bi, g: (ctx_blk0 + bi, g))
    o_ctx = pl.pallas_call(
        functools.partial(_attn_kernel, n_lat_chunks=0, tk=tk),
        grid=(b, N_KV_HEADS),
        in_specs=[ctx_q, ctx_kv2, ctx_kv2],
        out_specs=pl.BlockSpec((cl, gw), lambda bi, g: (bi, g)),
        out_shape=jax.ShapeDtypeStruct((t - n_lat, hq), BF16),
        compiler_params=_cparams(("parallel", "parallel")),
        name="attention_context",
    )(q, k, v)
    return jnp.concatenate([o_lat, o_ctx], axis=0)


def _in_proj_kernel(x_ref, shift_ref, scale_ref, w_ref, o_ref):
    h = (x_ref[...] * (1.0 + scale_ref[...]) + shift_ref[...]).astype(BF16)
    o_ref[...] = jnp.dot(h, w_ref[...], preferred_element_type=F32)


def _in_project(xall, mod, w_in, n_rows, n_lat, seq, ctx_row):
    d = xall.shape[1]
    n_out = w_in.shape[1]
    tm = ROW_TILE
    return pl.pallas_call(
        _in_proj_kernel,
        grid=(n_rows // tm,),
        in_specs=[
            pl.BlockSpec((tm, d), lambda i: (i, 0)),
            _mod_spec(0, tm, n_lat, seq, ctx_row, d),
            _mod_spec(1, tm, n_lat, seq, ctx_row, d),
            pl.BlockSpec(w_in.shape, lambda i: (0, 0)),
        ],
        out_specs=pl.BlockSpec((tm, n_out), lambda i: (i, 0)),
        out_shape=jax.ShapeDtypeStruct((n_rows, n_out), F32),
        compiler_params=_cparams(("parallel",)),
        name="conv_in_proj",
    )(xall, mod, mod, w_in.astype(BF16))


def _conv_kernel(bg_ref, cg_ref, v_ref, taps_ref, o_ref):
    u = cg_ref[...] * v_ref[...]
    n = u.shape[0]
    pos = lax.broadcasted_iota(I32, u.shape, 0)
    prev = jnp.where(pos == 0, 0.0, pltpu.roll(u, 1, axis=0))
    nxt = jnp.where(pos == n - 1, 0.0, pltpu.roll(u, n - 1, axis=0))
    conv = prev * taps_ref[0:1, :] + u * taps_ref[1:2, :] + nxt * taps_ref[2:3, :]
    o_ref[...] = (bg_ref[...] * conv).astype(BF16)


def _short_conv(zin, taps, n_seqs, seq, d):
    tc = CONV_LANES
    nj = d // tc
    return pl.pallas_call(
        _conv_kernel,
        grid=(n_seqs, nj),
        in_specs=[
            pl.BlockSpec((seq, tc), lambda s, j: (s, j)),
            pl.BlockSpec((seq, tc), lambda s, j: (s, nj + j)),
            pl.BlockSpec((seq, tc), lambda s, j: (s, 2 * nj + j)),
            pl.BlockSpec((taps.shape[0], tc), lambda s, j: (0, j)),
        ],
        out_specs=pl.BlockSpec((seq, tc), lambda s, j: (s, j)),
        out_shape=jax.ShapeDtypeStruct((n_seqs * seq, d), BF16),
        compiler_params=_cparams(("parallel", "parallel")),
        name="short_conv",
    )(zin, zin, zin, taps)


def _layer_norm(z, g, b):
    mu = jnp.mean(z, axis=-1, keepdims=True)
    zc = z - mu
    var = jnp.mean(zc * zc, axis=-1, keepdims=True)
    return zc * lax.rsqrt(var + LN_EPS) * g + b


def _post_kernel(a_ref, x_ref, gate_ref, lng_ref, lnb_ref, shift_ref, scale_ref, w_ref,
                 rwt_ref, rb_ref, tri_ref,
                 x1_ref, h2_ref, eidx_ref, gates_ref, rank_ref, cnt_ref, rwh_ref, rwl_ref, *,
                 dn_alpha, split_blk):
    tm = a_ref.shape[0]

    @pl.when(pl.program_id(0) == 0)
    def _():
        hi = rwt_ref[...].astype(BF16)
        rwh_ref[...] = hi
        rwl_ref[...] = (rwt_ref[...] - hi.astype(F32)).astype(BF16)

    d = x_ref.shape[1]
    y = jnp.dot(a_ref[...], w_ref[...], preferred_element_type=F32)
    x1 = _layer_norm(dn_alpha * x_ref[...] + gate_ref[...] * y, lng_ref[...], lnb_ref[...])
    x1_ref[...] = x1
    h2 = x1 * (1.0 + scale_ref[...]) + shift_ref[...]
    h2_ref[...] = _pack_rows(h2)

    h_hi = h2.astype(BF16)
    h_lo = (h2 - h_hi.astype(F32)).astype(BF16)

    def nt_dot(w, h):
        return lax.dot_general(w, h, (((1,), (1,)), ((), ())), preferred_element_type=F32)

    logits = nt_dot(rwh_ref[...], h_hi) + (nt_dot(rwl_ref[...], h_hi) + nt_dot(rwh_ref[...], h_lo))
    scores = jax.nn.sigmoid(logits)
    biased = scores + rb_ref[...]
    neg = -jnp.inf
    big = jnp.int32(1 << 30)
    row = lax.broadcasted_iota(I32, (N_EXPERTS, tm), 0)

    def argmax_rows(vals, idx):
        mx = jnp.max(vals, axis=0, keepdims=True)
        return mx, jnp.min(jnp.where(vals == mx, idx, big), axis=0, keepdims=True)

    gs = []
    grp_row = lax.broadcasted_iota(I32, (PER_GROUP, tm), 0)
    for g in range(N_GROUPS):
        bg = biased[g * PER_GROUP:(g + 1) * PER_GROUP]
        ig = grp_row + g * PER_GROUP
        m1, i1 = argmax_rows(bg, ig)
        m2 = jnp.max(jnp.where(ig == i1, neg, bg), axis=0, keepdims=True)
        gs.append(m1 + m2)
    gsc = jnp.concatenate(gs, axis=0)
    grow = lax.broadcasted_iota(I32, (N_GROUPS, tm), 0)
    gsel = jnp.zeros((N_GROUPS, tm), F32)
    for _ in range(TOPK_GROUPS):
        _, gi = argmax_rows(gsc, grow)
        hit = grow == gi
        gsel = jnp.where(hit, 1.0, gsel)
        gsc = jnp.where(hit, neg, gsc)
    cur = jnp.concatenate(
        [jnp.where(gsel[g:g + 1] > 0.0, biased[g * PER_GROUP:(g + 1) * PER_GROUP], neg)
         for g in range(N_GROUPS)], axis=0)

    onehot = jnp.zeros((N_EXPERTS, tm), F32)
    idxs, gvals = [], []
    for _ in range(TOP_K):
        _, ei = argmax_rows(cur, row)
        hit = row == ei
        gvals.append(jnp.sum(jnp.where(hit, scores, 0.0), axis=0, keepdims=True))
        idxs.append(ei)
        onehot = jnp.where(hit, 1.0, onehot)
        cur = jnp.where(hit, neg, cur)
    gv = jnp.concatenate(gvals, axis=0)
    gates_ref[...] = gv / jnp.sum(gv, axis=0, keepdims=True) * ROUTED_SCALE
    eidx_ref[...] = jnp.concatenate(idxs, axis=0)

    @pl.when((pl.program_id(0) == 0) | (pl.program_id(0) == split_blk))
    def _():
        cnt_ref[...] = jnp.zeros_like(cnt_ref)

    prefix = jnp.dot(onehot.astype(BF16), tri_ref[...], preferred_element_type=F32)
    pos = prefix + cnt_ref[...]
    rank_ref[...] = jnp.concatenate(
        [jnp.sum(jnp.where(row == ei, pos, 0.0), axis=0, keepdims=True) for ei in idxs],
        axis=0).astype(I32)
    cnt_ref[...] = cnt_ref[...] + jnp.sum(onehot, axis=1, keepdims=True)


def _post_mixer(a, xall, mod, ln_g, ln_b, w, router_w, router_bias, n_rows, n_lat, seq, ctx_row,
                dn_alpha, split_blk):
    d = xall.shape[1]
    tm = ROW_TILE
    tri = (lax.broadcasted_iota(I32, (tm, tm), 0) < lax.broadcasted_iota(I32, (tm, tm), 1)).astype(BF16)
    const = lambda i: (0, 0)
    row_blk = pl.BlockSpec((tm, d), lambda i: (i, 0))
    k_blk = pl.BlockSpec((TOP_K, tm), lambda i: (0, i))
    return pl.pallas_call(
        functools.partial(_post_kernel, dn_alpha=dn_alpha, split_blk=split_blk),
        grid=(n_rows // tm,),
        in_specs=[
            row_blk, row_blk,
            _mod_spec(2, tm, n_lat, seq, ctx_row, d),
            pl.BlockSpec((1, d), const), pl.BlockSpec((1, d), const),
            _mod_spec(3, tm, n_lat, seq, ctx_row, d),
            _mod_spec(4, tm, n_lat, seq, ctx_row, d),
            pl.BlockSpec(w.shape, const),
            pl.BlockSpec((N_EXPERTS, d), const),
            pl.BlockSpec((N_EXPERTS, 1), const),
            pl.BlockSpec((tm, tm), const),
        ],
        out_specs=[
            row_blk, pl.BlockSpec((tm, d // 2), lambda i: (i, 0)),
            k_blk, k_blk, k_blk,
            pl.BlockSpec((None, N_EXPERTS, 1), lambda i: (jnp.where(i >= split_blk, 1, 0), 0, 0)),
        ],
        out_shape=[
            jax.ShapeDtypeStruct((n_rows, d), F32),
            jax.ShapeDtypeStruct((n_rows, d // 2), U32),
            jax.ShapeDtypeStruct((TOP_K, n_rows), I32),
            jax.ShapeDtypeStruct((TOP_K, n_rows), F32),
            jax.ShapeDtypeStruct((TOP_K, n_rows), I32),
            jax.ShapeDtypeStruct((2, N_EXPERTS, 1), F32),
        ],
        scratch_shapes=[pltpu.VMEM((N_EXPERTS, d), BF16), pltpu.VMEM((N_EXPERTS, d), BF16)],
        compiler_params=_cparams(("arbitrary",)),
        name="post_mixer_router",
    )(a, xall, mod, ln_g.reshape(1, d), ln_b.reshape(1, d), mod, mod, w.astype(BF16),
      router_w.T, router_bias.reshape(N_EXPERTS, 1), tri)


def _slots_kernel(pstart_ref, eidx_ref, rank_ref, dest_ref):
    e = eidx_ref[...]

    def pick(i, acc):
        return jnp.where(e == i, pstart_ref[i], acc)

    dest_ref[...] = lax.fori_loop(0, N_EXPERTS, pick, jnp.zeros_like(e)) + rank_ref[...]


def _slots(pstarts, eidx, rank):
    n_tok = eidx.shape[1]
    tn = 2048 if n_tok % 2048 == 0 else ROW_TILE
    blk = pl.BlockSpec((TOP_K, tn), lambda i, ps: (0, i))
    return pl.pallas_call(
        _slots_kernel,
        grid_spec=pltpu.PrefetchScalarGridSpec(
            num_scalar_prefetch=1, grid=(n_tok // tn,), in_specs=[blk, blk], out_specs=blk),
        out_shape=jax.ShapeDtypeStruct((TOP_K, n_tok), I32),
        compiler_params=_cparams(("arbitrary",)),
        name="moe_slots",
    )(pstarts, eidx, rank)


def _dispatch_kernel(dest_ref, h_ref, xs_hbm, sem):
    td = dest_ref.shape[1]

    def issue(t, carry):
        for k in range(TOP_K):
            pltpu.make_async_copy(h_ref.at[pl.ds(t, 1)], xs_hbm.at[pl.ds(dest_ref[k, t], 1)], sem).start()
        return carry

    lax.fori_loop(0, td, issue, 0)
    pltpu.make_async_copy(xs_hbm.at[pl.ds(0, TOP_K * td)], xs_hbm.at[pl.ds(0, TOP_K * td)], sem).wait()


def _dispatch(dest, h2, n_slots):
    n_tok = dest.shape[1]
    d = h2.shape[1]
    td = DISPATCH_TOK
    return pl.pallas_call(
        _dispatch_kernel,
        grid=(n_tok // td,),
        in_specs=[
            pl.BlockSpec((TOP_K, td), lambda i: (0, i), memory_space=pltpu.SMEM),
            pl.BlockSpec((td, d), lambda i: (i, 0)),
        ],
        out_specs=pl.BlockSpec(memory_space=pl.ANY),
        out_shape=jax.ShapeDtypeStruct((n_slots, d), h2.dtype),
        scratch_shapes=[pltpu.SemaphoreType.DMA(())],
        compiler_params=_cparams(("arbitrary",)),
        name="moe_dispatch",
    )(dest, h2)


def _gmm_kernel(gb0_ref, nblk_ref, u0_ref, n_used_ref, xa_hbm, xb_hbm, wgu_ref, wd_ref, ys_hbm,
                xbuf, ybuf, wgu_bf, wd_bf, sem_in, sem_out):
    g = pl.program_id(0)
    n_in, bm = xbuf.shape[:2]
    n_out = ybuf.shape[0]
    ff = wd_ref.shape[0]
    n_used = n_used_ref[0]
    rows_a = xa_hbm.shape[0]

    def x_copy(src_hbm, row0, b):
        rows = pl.ds(row0 if isinstance(row0, int) else pl.multiple_of(row0, bm), bm)
        return pltpu.make_async_copy(src_hbm.at[rows], xbuf.at[b % n_in], sem_in.at[b % n_in])

    def x_start(b):
        u0 = u0_ref[b]

        @pl.when(u0 < rows_a)
        def _():
            x_copy(xa_hbm, u0, b).start()

        @pl.when(u0 >= rows_a)
        def _():
            x_copy(xb_hbm, u0 - rows_a, b).start()

    def x_wait(b):
        x_copy(xa_hbm, 0, b).wait()

    def y_copy(b, slot):
        rows = pl.ds(pl.multiple_of(u0_ref[b], bm), bm)
        return pltpu.make_async_copy(ybuf.at[slot], ys_hbm.at[rows], sem_out.at[slot])

    @pl.when(g == 0)
    def _():
        for b in range(n_in - 1):
            @pl.when(b < n_used)
            def _():
                x_start(b)

    nb = nblk_ref[2 * g] + nblk_ref[2 * g + 1]

    @pl.when(nb > 0)
    def _():
        wgu_bf[...] = wgu_ref[...].astype(BF16)
        wd_bf[...] = wd_ref[...].astype(BF16)

    def block(j, carry):
        b = gb0_ref[2 * g] + j
        slot = b % n_out
        x_wait(b)

        @pl.when(b + n_in - 1 < n_used)
        def _():
            x_start(b + n_in - 1)

        @pl.when(b >= n_out)
        def _():
            y_copy(b - n_out, slot).wait()

        gu = jnp.dot(_unpack_rows_bf16(xbuf[b % n_in]), wgu_bf[...], preferred_element_type=F32)
        act = (_silu(gu[:, :ff]) * gu[:, ff:]).astype(BF16)
        ybuf[slot] = _pack_rows(jnp.dot(act, wd_bf[...], preferred_element_type=F32))
        y_copy(b, slot).start()
        return carry

    lax.fori_loop(0, nb, block, 0)

    @pl.when(g == pl.num_programs(0) - 1)
    def _():
        for back in range(n_out, 0, -1):
            @pl.when(n_used >= back)
            def _():
                y_copy(n_used - back, (n_used - back) % n_out).wait()


def _grouped_ffn(xs_a, xs_b, gb0, nblk, u0_blk, n_used, wgu, wd, layer):
    n_slots = xs_a.shape[0] + xs_b.shape[0]
    bm = MOE_BM
    n_exp, d, ff2 = wgu.shape[1:]
    ff = wd.shape[2]
    grid_spec = pltpu.PrefetchScalarGridSpec(
        num_scalar_prefetch=4,
        grid=(n_exp,),
        in_specs=[
            pl.BlockSpec(memory_space=pl.ANY),
            pl.BlockSpec(memory_space=pl.ANY),
            pl.BlockSpec((None, None, d, ff2), lambda g, *_: (layer, g, 0, 0)),
            pl.BlockSpec((None, None, ff, d), lambda g, *_: (layer, g, 0, 0)),
        ],
        out_specs=pl.BlockSpec(memory_space=pl.ANY),
        scratch_shapes=[
            pltpu.VMEM((GMM_IN_SLOTS, bm, d // 2), U32), pltpu.VMEM((GMM_OUT_SLOTS, bm, d // 2), U32),
            pltpu.VMEM((d, ff2), BF16), pltpu.VMEM((ff, d), BF16),
            pltpu.SemaphoreType.DMA((GMM_IN_SLOTS,)), pltpu.SemaphoreType.DMA((GMM_OUT_SLOTS,)),
        ],
    )
    return pl.pallas_call(
        _gmm_kernel,
        grid_spec=grid_spec,
        out_shape=jax.ShapeDtypeStruct((n_slots, d // 2), U32),
        compiler_params=_cparams(("arbitrary",)),
        name="moe_grouped_ffn",
    )(gb0, nblk, u0_blk, n_used, xs_a, xs_b, wgu, wd)


def _shared_expert(h2_ref, swgu_ref, swd_ref):
    ff = swd_ref.shape[0]
    gu = jnp.dot(_unpack_rows_bf16(h2_ref[...]), swgu_ref[...], preferred_element_type=F32)
    act = (_silu(gu[:, :ff]) * gu[:, ff:]).astype(BF16)
    return jnp.dot(act, swd_ref[...], preferred_element_type=F32)


def _combine_kernel(dest_ref, gates_ref, ys_hbm, x1_ref, h2_ref, gate_ref, lng_ref, lnb_ref,
                    swgu_ref, swd_ref, o_ref, buf, sem, *, dn_alpha):
    tc = x1_ref.shape[0]

    def issue(t, carry):
        for k in range(TOP_K):
            pltpu.make_async_copy(ys_hbm.at[pl.ds(dest_ref[k, t], 1)], buf.at[k, pl.ds(t, 1)], sem).start()
        return carry

    lax.fori_loop(0, tc, issue, 0)

    y = _shared_expert(h2_ref, swgu_ref, swd_ref)

    for k in range(TOP_K):
        pltpu.make_async_copy(ys_hbm.at[pl.ds(0, tc)], buf.at[k], sem).wait()
    for k in range(TOP_K):
        y = _add_weighted_rows(y, gates_ref[:, k:k + 1], buf[k])
    o_ref[...] = _layer_norm(dn_alpha * x1_ref[...] + gate_ref[...] * y, lng_ref[...], lnb_ref[...])


def _combine_gathered_kernel(gates_ref, g_ref, x1_ref, h2_ref, gate_ref, lng_ref, lnb_ref,
                             swgu_ref, swd_ref, prev_ref, o_ref, *, dn_alpha):
    del prev_ref
    y = _shared_expert(h2_ref, swgu_ref, swd_ref)
    for k in range(TOP_K):
        y = _add_weighted_rows(y, gates_ref[:, k:k + 1], g_ref[k])
    o_ref[...] = _layer_norm(dn_alpha * x1_ref[...] + gate_ref[...] * y, lng_ref[...], lnb_ref[...])


def _sc_gather_rows(table, idx):
    n = idx.shape[0]
    d = table.shape[1]
    mesh = plsc.VectorSubcoreMesh(core_axis_name="c", subcore_axis_name="s")
    n_workers = mesh.num_cores * mesh.num_subcores
    ch = SC_GATHER_ROWS
    per_w = n // n_workers
    n_ch = per_w // ch
    assert n % (n_workers * ch * 2) == 0

    @functools.partial(
        pl.kernel, mesh=mesh, out_type=jax.ShapeDtypeStruct((n, d), table.dtype),
        scratch_types=[pltpu.VMEM((n_ch, ch), I32),
                       pltpu.VMEM((ch, d), table.dtype), pltpu.VMEM((ch, d), table.dtype),
                       pltpu.SemaphoreType.DMA, pltpu.SemaphoreType.DMA],
        name="moe_sc_gather")
    def gather(table_hbm, idx_hbm, out_hbm, idx_all, rows_a, rows_b, sem_a, sem_b):
        wid = lax.axis_index("s") * mesh.num_cores + lax.axis_index("c")
        pltpu.sync_copy(idx_hbm.at[wid], idx_all)

        def fetch(i, rows_v, sem):
            pltpu.make_async_copy(table_hbm.at[idx_all.at[i]], rows_v, sem).start()

        def drain(i, rows_v, sem):
            pltpu.make_async_copy(table_hbm.at[idx_all.at[i]], rows_v, sem).wait()
            pltpu.sync_copy(rows_v, out_hbm.at[pl.ds(wid * per_w + i * ch, ch)])

        fetch(0, rows_a, sem_a)

        @pl.loop(0, n_ch // 2)
        def _(j):
            i = 2 * j
            fetch(i + 1, rows_b, sem_b)
            drain(i, rows_a, sem_a)

            @pl.when(i + 2 < n_ch)
            def _():
                fetch(i + 2, rows_a, sem_a)

            drain(i + 1, rows_b, sem_b)

    return gather(table, idx.reshape(n_workers, n_ch, ch))


def _sc_scatter_rows(rows, first_row, idx, n_out):
    n_k, n = idx.shape
    d = rows.shape[1]
    mesh = plsc.VectorSubcoreMesh(core_axis_name="c", subcore_axis_name="s")
    n_workers = mesh.num_cores * mesh.num_subcores
    ch = SC_GATHER_ROWS
    per_w = n // n_workers
    n_ch = per_w // ch
    assert n % (n_workers * ch) == 0

    @functools.partial(
        pl.kernel, mesh=mesh, out_type=jax.ShapeDtypeStruct((n_out, d), rows.dtype),
        scratch_types=[pltpu.VMEM((n_k, n_ch, ch), I32), pltpu.VMEM((ch, d), rows.dtype)],
        name="moe_sc_scatter")
    def scatter(rows_hbm, idx_hbm, out_hbm, idx_all, rows_v):
        wid = lax.axis_index("s") * mesh.num_cores + lax.axis_index("c")
        pltpu.sync_copy(idx_hbm.at[wid], idx_all)

        @pl.loop(0, n_ch)
        def _(i):
            pltpu.sync_copy(rows_hbm.at[pl.ds(first_row + wid * per_w + i * ch, ch)], rows_v)
            for k in range(n_k):
                pltpu.sync_copy(rows_v, out_hbm.at[idx_all.at[k, i]])

    return scatter(rows, idx.reshape(n_k, n_workers, n_ch, ch).transpose(1, 0, 2, 3))


def _combine(dest, gates_t, ys, x1, h2, mod, ln_g, ln_b, swgu, swd, n_rows, n_lat, seq, ctx_row,
             dn_alpha):
    d = x1.shape[1]
    tc = DISPATCH_TOK
    const = lambda i: (0, 0)
    n_blocks = n_rows // tc
    n_sc_blocks = (n_blocks * SC_SHARE_PERCENT) // 100
    n_tc = (n_blocks - n_sc_blocks) * tc
    swgu_bf, swd_bf = swgu.astype(BF16), swd.astype(BF16)
    tl = COMBINE_GATHERED_TOK
    stage_blk = [n_tc // tl + (n_sc_blocks * s) // COMBINE_STAGES for s in range(COMBINE_STAGES + 1)]
    gathered = []
    for s in range(COMBINE_STAGES):
        t_lo, t_hi = stage_blk[s] * tl, stage_blk[s + 1] * tl
        rows = _sc_gather_rows(ys, dest[:, t_lo:t_hi].reshape(TOP_K * (t_hi - t_lo)))
        gathered.append(rows.reshape(TOP_K, t_hi - t_lo, d // 2))
    out = pl.pallas_call(
        functools.partial(_combine_kernel, dn_alpha=dn_alpha),
        grid=(n_tc // tc,),
        in_specs=[
            pl.BlockSpec((TOP_K, tc), lambda i: (0, i), memory_space=pltpu.SMEM),
            pl.BlockSpec((tc, TOP_K), lambda i: (i, 0)),
            pl.BlockSpec(memory_space=pl.ANY),
            pl.BlockSpec((tc, d), lambda i: (i, 0)),
            pl.BlockSpec((tc, d // 2), lambda i: (i, 0)),
            _mod_spec(5, tc, n_lat, seq, ctx_row, d),
            pl.BlockSpec((1, d), const), pl.BlockSpec((1, d), const),
            pl.BlockSpec(swgu.shape, const),
            pl.BlockSpec(swd.shape, const),
        ],
        out_specs=pl.BlockSpec((tc, d), lambda i: (i, 0)),
        out_shape=jax.ShapeDtypeStruct((n_rows, d), F32),
        scratch_shapes=[pltpu.VMEM((TOP_K, tc, d // 2), U32), pltpu.SemaphoreType.DMA(())],
        compiler_params=_cparams(("arbitrary",)),
        name="moe_combine",
    )(dest, gates_t, ys, x1, h2, mod, ln_g.reshape(1, d), ln_b.reshape(1, d), swgu_bf, swd_bf)

    for s in range(COMBINE_STAGES):
        off = stage_blk[s]
        row_blk = pl.BlockSpec((tl, d), functools.partial(lambda i, o: (i + o, 0), o=off))
        half_blk = pl.BlockSpec((tl, d // 2), functools.partial(lambda i, o: (i + o, 0), o=off))
        out = pl.pallas_call(
            functools.partial(_combine_gathered_kernel, dn_alpha=dn_alpha),
            grid=(stage_blk[s + 1] - off,),
            in_specs=[
                pl.BlockSpec((tl, TOP_K), functools.partial(lambda i, o: (i + o, 0), o=off)),
                pl.BlockSpec((TOP_K, tl, d // 2), lambda i: (0, i, 0)),
                row_blk, half_blk,
                _mod_spec(5, tl, n_lat, seq, ctx_row, d, blk_off=off),
                pl.BlockSpec((1, d), const), pl.BlockSpec((1, d), const),
                pl.BlockSpec(swgu.shape, const),
                pl.BlockSpec(swd.shape, const),
                pl.BlockSpec(memory_space=pl.ANY),
            ],
            out_specs=row_blk,
            out_shape=jax.ShapeDtypeStruct((n_rows, d), F32),
            input_output_aliases={9: 0},
            compiler_params=_cparams(("parallel",)),
            name="moe_combine_gathered",
        )(gates_t, gathered[s], x1, h2, mod, ln_g.reshape(1, d), ln_b.reshape(1, d), swgu_bf, swd_bf, out)
    return out


def _split_block(n_rows):
    n_blocks = n_rows // ROW_TILE
    sc_blocks = (n_blocks * SC_DISPATCH_PERCENT) // 100 // 4 * 4
    return n_blocks - sc_blocks


def _max_slots(n_tok, bm):
    return (n_tok * TOP_K + N_EXPERTS * (bm - 1) + bm - 1) // bm * bm


def _moe(x1, h2, eidx, gates, rank, counts, mod, ln_g, ln_b, wgu, wd, layer, swgu, swd,
         n_rows, n_lat, seq, ctx_row, dn_alpha, split_blk):
    bm = MOE_BM
    t0 = split_blk * ROW_TILE
    rows_a, rows_b = (_max_slots(t0, bm) if t0 else bm), _max_slots(n_rows - t0, bm)
    cnt = counts.reshape(2, N_EXPERTS).astype(I32)
    if not t0:
        cnt = cnt.at[0].set(0)
    pcnt = (cnt + bm - 1) // bm * bm
    pend = jnp.cumsum(pcnt, axis=1)
    ustart = pend - pcnt + jnp.array([[0], [rows_a]], I32)
    dest = _slots(ustart[1], eidx[:, t0:], rank[:, t0:])
    if t0:
        dest = jnp.concatenate([_slots(ustart[0], eidx[:, :t0], rank[:, :t0]), dest], axis=1)

    nblk = (pcnt // bm).T.reshape(2 * N_EXPERTS)
    gb0 = jnp.cumsum(nblk) - nblk
    u0_grp = ustart.T.reshape(2 * N_EXPERTS)
    blk = jnp.arange((rows_a + rows_b) // bm, dtype=I32)
    grp = jnp.sum((gb0[None, :] <= blk[:, None]).astype(I32), axis=1) - 1
    u0_blk = u0_grp[grp] + (blk - gb0[grp]) * bm
    n_used = jnp.sum(nblk).astype(I32).reshape(1)

    xs_a = _dispatch(dest[:, :t0], h2, rows_a) if t0 else jnp.zeros((rows_a, h2.shape[1]), h2.dtype)
    xs_b = _sc_scatter_rows(h2, t0, dest[:, t0:] - rows_a, rows_b)
    ys = _grouped_ffn(xs_a, xs_b, gb0, nblk, u0_blk, n_used, wgu, wd, layer)
    return _combine(dest, gates.T, ys, x1, h2, mod, ln_g, ln_b, swgu, swd,
                    n_rows, n_lat, seq, ctx_row, dn_alpha)


def kernel(x, c, ctx, c_ctx, ada_w, ada_b, ln_g, ln_b, attn_w_qkv, attn_q_norm, attn_k_norm, attn_w_o, conv_w_in, conv_taps, conv_w_out, router_w, router_bias, exp_w_gate_up, exp_w_down, shared_w_gate_up, shared_w_down):
    b, seq, d = x.shape
    cl = ctx.shape[1]
    depth = ada_w.shape[0]
    n_lat = b * seq
    n_ctx = b * cl
    dn_alpha = (2 * depth) ** 0.25
    assert depth == 2 and b < MOD_ROWS
    assert seq % ROW_TILE == 0 and n_ctx % ROW_TILE == 0 and seq % GRID_W == 0

    cond = jnp.zeros((MOD_ROWS, d), F32).at[:b].set(c).at[b].set(c_ctx)
    mod = _modulation(cond, ada_w, ada_b)
    xall = jnp.concatenate([x.reshape(n_lat, d), ctx.reshape(n_ctx, d)], axis=0)
    n_all = n_lat + n_ctx

    q, k, v = _qkv_project(xall, mod[0], attn_w_qkv[0], attn_q_norm[0], attn_k_norm[0], n_lat, seq, b)
    o = _attention(q, k, v, b, seq, cl, n_lat)
    x1, h2, eidx, gates, rank, counts = _post_mixer(
        o, xall, mod[0], ln_g[0, 0], ln_b[0, 0], attn_w_o[0], router_w[0], router_bias[0],
        n_all, n_lat, seq, b, dn_alpha, _split_block(n_all))
    xall = _moe(x1, h2, eidx, gates, rank, counts, mod[0], ln_g[0, 1], ln_b[0, 1],
                exp_w_gate_up, exp_w_down, 0, shared_w_gate_up[0], shared_w_down[0],
                n_all, n_lat, seq, b, dn_alpha, _split_block(n_all))

    zin = _in_project(xall, mod[1], conv_w_in[0], n_lat, n_lat, seq, b)
    a = _short_conv(zin, conv_taps[0], b, seq, d)
    x1, h2, eidx, gates, rank, counts = _post_mixer(
        a, xall, mod[1], ln_g[1, 0], ln_b[1, 0], conv_w_out[0], router_w[1], router_bias[1],
        n_lat, n_lat, seq, b, dn_alpha, _split_block(n_lat))
    out = _moe(x1, h2, eidx, gates, rank, counts, mod[1], ln_g[1, 1], ln_b[1, 1],
               exp_w_gate_up, exp_w_down, 1, shared_w_gate_up[1], shared_w_down[1],
               n_lat, n_lat, seq, b, dn_alpha, _split_block(n_lat))
    return out.reshape(b, seq, d)
```

```python
import functools

import jax
import jax.numpy as jnp
from jax import lax
from jax.experimental import pallas as pl
from jax.experimental.pallas import tpu as pltpu
from jax.experimental.pallas import tpu_sc as plsc

F32 = jnp.float32
BF16 = jnp.bfloat16
I32 = jnp.int32

N_HEADS = 8
N_KV_HEADS = 2
HEAD_DIM = 128
KV_GROUP = N_HEADS // N_KV_HEADS
GRID_W = 64
ROPE_THETA = 10000.0
N_EXPERTS = 256
TOP_K = 8
N_GROUPS = 8
TOPK_GROUPS = 4
PER_GROUP = N_EXPERTS // N_GROUPS
ROUTED_SCALE = 2.5
LN_EPS = 1e-5
QK_EPS = 1e-6
N_MOD = 6
MOD_ROWS = 16

LANES = 128
SUBLANES = 8
VMEM_LIMIT = 56 * 1024 * 1024

ROW_TILE = 512
ATTN_TQ = 256
ATTN_TK = 2048
MOE_BM = 256
GMM_IN_SLOTS = 4
GMM_OUT_SLOTS = 3
DISPATCH_TOK = 512
COMBINE_GATHERED_TOK = 512
SC_GATHER_ROWS = 64
SC_SHARE_PERCENT = 82
SC_DISPATCH_PERCENT = 100
CONV_LANES = 128

HIGHEST = lax.Precision.HIGHEST
LOG2_E = 1.4426950408889634


def _cparams(sem):
    return pltpu.CompilerParams(dimension_semantics=sem, vmem_limit_bytes=VMEM_LIMIT)


def _silu(v):
    return v * jax.nn.sigmoid(v)


U32 = jnp.uint32
HI_MASK = 0xFFFF0000


def _pack_rows(v):
    half = v.shape[1] // 2
    lo = lax.bitcast_convert_type(v[:, :half].astype(BF16).astype(F32), U32)
    hi = lax.bitcast_convert_type(v[:, half:].astype(BF16).astype(F32), U32)
    return (lo >> 16) | (hi & U32(HI_MASK))


def _add_weighted_rows(y, gate, packed):
    half = y.shape[1] // 2
    lo = lax.bitcast_convert_type(packed << 16, F32)
    hi = lax.bitcast_convert_type(packed & U32(HI_MASK), F32)
    return jnp.concatenate([y[:, :half] + gate * lo, y[:, half:] + gate * hi], axis=-1)


def _unpack_rows_bf16(w):
    lo = lax.bitcast_convert_type(w << 16, F32)
    hi = lax.bitcast_convert_type(w & U32(HI_MASK), F32)
    return jnp.concatenate([lo.astype(BF16), hi.astype(BF16)], axis=-1)


def _mod_kernel(c_ref, w_ref, b_ref, o_ref):
    s = _silu(c_ref[...])
    o_ref[...] = jnp.dot(s, w_ref[...], precision=HIGHEST, preferred_element_type=F32) + b_ref[...]


def _modulation(cond, ada_w, ada_b):
    depth, d, nd = ada_w.shape
    tn = 1536
    out = pl.pallas_call(
        _mod_kernel,
        grid=(depth, nd // tn),
        in_specs=[
            pl.BlockSpec((MOD_ROWS, d), lambda l, j: (0, 0)),
            pl.BlockSpec((None, d, tn), lambda l, j: (l, 0, j)),
            pl.BlockSpec((None, 1, tn), lambda l, j: (l, 0, j)),
        ],
        out_specs=pl.BlockSpec((None, MOD_ROWS, tn), lambda l, j: (l, 0, j)),
        out_shape=jax.ShapeDtypeStruct((depth, MOD_ROWS, nd), F32),
        compiler_params=_cparams(("arbitrary", "arbitrary")),
        name="adaln_modulation",
    )(cond, ada_w, ada_b.reshape(depth, 1, nd))
    return out.reshape(depth, MOD_ROWS * N_MOD, 1, d)


def _mod_spec(comp, tm, n_lat, seq, ctx_row, d, blk_off=0):
    def index(i, *_):
        row0 = (i + blk_off) * tm
        r = jnp.where(row0 < n_lat, row0 // seq, ctx_row)
        return (r * N_MOD + comp, 0, 0)

    return pl.BlockSpec((None, 1, d), index)


def _qkv_kernel(x_ref, shift_ref, scale_ref, w_ref, qg_ref, kg_ref, cos_ref, sin_ref,
                q_ref, k_ref, v_ref):
    h = (x_ref[...] * (1.0 + scale_ref[...]) + shift_ref[...]).astype(BF16)
    qkv = jnp.dot(h, w_ref[...], preferred_element_type=F32)
    cos = cos_ref[...]
    sin = sin_ref[...]
    hq = N_HEADS * HEAD_DIM
    kd = N_KV_HEADS * HEAD_DIM

    def norm_rope(t, g, post):
        t = t * lax.rsqrt(jnp.mean(t * t, axis=-1, keepdims=True) + QK_EPS) * g
        t = t * cos + pltpu.roll(t, HEAD_DIM // 2, axis=1) * sin
        return (t * post).astype(BF16)

    for hd in range(N_HEADS):
        sl = slice(hd * HEAD_DIM, (hd + 1) * HEAD_DIM)
        q_ref[:, sl] = norm_rope(qkv[:, sl], qg_ref[...], HEAD_DIM ** -0.5 * LOG2_E)
    for hd in range(N_KV_HEADS):
        sl = slice(hd * HEAD_DIM, (hd + 1) * HEAD_DIM)
        k_ref[:, sl] = norm_rope(qkv[:, hq + hd * HEAD_DIM: hq + (hd + 1) * HEAD_DIM], kg_ref[...], 1.0)
    v_ref[...] = qkv[:, hq + kd:].astype(BF16)


def _rope_tables(seq, tm):
    rows = seq // GRID_W
    row = jnp.repeat(jnp.arange(rows, dtype=F32), GRID_W)
    col = jnp.tile(jnp.arange(GRID_W, dtype=F32), rows)
    axis_dim = HEAD_DIM // 2
    freqs = ROPE_THETA ** (-jnp.arange(0, axis_dim, 2, dtype=F32) / axis_dim)
    ang = jnp.concatenate([row[:, None] * freqs, col[:, None] * freqs], axis=-1)
    cos = jnp.concatenate([jnp.cos(ang), jnp.cos(ang)], axis=-1)
    sin = jnp.concatenate([-jnp.sin(ang), jnp.sin(ang)], axis=-1)
    cos = jnp.concatenate([cos, jnp.ones((tm, HEAD_DIM), F32)], axis=0)
    sin = jnp.concatenate([sin, jnp.zeros((tm, HEAD_DIM), F32)], axis=0)
    return cos.reshape(seq // tm + 1, tm, HEAD_DIM), sin.reshape(seq // tm + 1, tm, HEAD_DIM)


def _qkv_project(xall, mod, w_qkv, q_g, k_g, n_lat, seq, ctx_row):
    t, d = xall.shape
    tm = ROW_TILE
    hq = N_HEADS * HEAD_DIM
    kd = N_KV_HEADS * HEAD_DIM
    perm = jnp.concatenate([jnp.arange(0, HEAD_DIM, 2), jnp.arange(1, HEAD_DIM, 2)])
    cols = jnp.concatenate([hd * HEAD_DIM + perm for hd in range(N_HEADS + N_KV_HEADS)]
                           + [jnp.arange(hq + kd, hq + 2 * kd)])
    w = w_qkv[:, cols].astype(BF16)
    cos, sin = _rope_tables(seq, tm)
    n_pos = seq // tm

    def pos_index(i):
        row0 = i * tm
        return (jnp.where(row0 < n_lat, (row0 % seq) // tm, n_pos), 0, 0)

    const = lambda i: (0, 0)
    return pl.pallas_call(
        _qkv_kernel,
        grid=(t // tm,),
        in_specs=[
            pl.BlockSpec((tm, d), lambda i: (i, 0)),
            _mod_spec(0, tm, n_lat, seq, ctx_row, d),
            _mod_spec(1, tm, n_lat, seq, ctx_row, d),
            pl.BlockSpec(w.shape, const),
            pl.BlockSpec((1, HEAD_DIM), const),
            pl.BlockSpec((1, HEAD_DIM), const),
            pl.BlockSpec((None, tm, HEAD_DIM), pos_index),
            pl.BlockSpec((None, tm, HEAD_DIM), pos_index),
        ],
        out_specs=[
            pl.BlockSpec((tm, hq), lambda i: (i, 0)),
            pl.BlockSpec((tm, kd), lambda i: (i, 0)),
            pl.BlockSpec((tm, kd), lambda i: (i, 0)),
        ],
        out_shape=[
            jax.ShapeDtypeStruct((t, hq), BF16),
            jax.ShapeDtypeStruct((t, kd), BF16),
            jax.ShapeDtypeStruct((t, kd), BF16),
        ],
        compiler_params=_cparams(("parallel",)),
        name="qkv_norm_rope",
    )(xall, mod, mod, w, q_g[perm].reshape(1, HEAD_DIM), k_g[perm].reshape(1, HEAD_DIM), cos, sin)


def _attn_kernel(*refs, n_lat_chunks, tk):
    if n_lat_chunks:
        q_ref, kc_ref, vc_ref, kl_ref, vl_ref, o_ref = refs
    else:
        q_ref, kc_ref, vc_ref, o_ref = refs
    tq = q_ref.shape[0]
    q = jnp.concatenate([q_ref[:, h * HEAD_DIM:(h + 1) * HEAD_DIM] for h in range(KV_GROUP)], axis=0)
    rows = KV_GROUP * tq

    def chunk(k, v, m, l, acc):
        s = lax.dot_general(q, k, (((1,), (1,)), ((), ())), preferred_element_type=F32)
        m_new = jnp.maximum(m, jnp.max(s, axis=-1, keepdims=True))
        p = jnp.exp2(s - m_new)
        a = jnp.exp2(m - m_new)
        l = a * l + jnp.sum(p, axis=-1, keepdims=True)
        acc = a * acc + jnp.dot(p.astype(BF16), v, preferred_element_type=F32)
        return m_new, l, acc

    m = jnp.full((rows, 1), -jnp.inf, F32)
    l = jnp.zeros((rows, 1), F32)
    acc = jnp.zeros((rows, HEAD_DIM), F32)
    m, l, acc = chunk(kc_ref[...], vc_ref[...], m, l, acc)
    for c in range(n_lat_chunks):
        m, l, acc = chunk(kl_ref[c * tk:(c + 1) * tk, :], vl_ref[c * tk:(c + 1) * tk, :], m, l, acc)
    o = (acc / l).astype(BF16)
    for h in range(KV_GROUP):
        o_ref[:, h * HEAD_DIM:(h + 1) * HEAD_DIM] = o[h * tq:(h + 1) * tq]


def _attention(q, k, v, b, seq, cl, n_lat):
    t = q.shape[0]
    gw = KV_GROUP * HEAD_DIM
    tq = min(ATTN_TQ, seq)
    tk = min(ATTN_TK, seq)
    nq = seq // tq
    ctx_blk0 = n_lat // cl
    hq = N_HEADS * HEAD_DIM

    ctx_kv = pl.BlockSpec((cl, HEAD_DIM), lambda bi, g, qi: (ctx_blk0 + bi, g))
    lat_kv = pl.BlockSpec((seq, HEAD_DIM), lambda bi, g, qi: (bi, g))
    lat_q = pl.BlockSpec((tq, gw), lambda bi, g, qi: (bi * nq + qi, g))
    o_lat = pl.pallas_call(
        functools.partial(_attn_kernel, n_lat_chunks=seq // tk, tk=tk),
        grid=(b, N_KV_HEADS, nq),
        in_specs=[lat_q, ctx_kv, ctx_kv, lat_kv, lat_kv],
        out_specs=lat_q,
        out_shape=jax.ShapeDtypeStruct((n_lat, hq), BF16),
        compiler_params=_cparams(("parallel", "parallel", "arbitrary")),
        name="attention_latent",
    )(q, k, v, k, v)

    ctx_q = pl.BlockSpec((cl, gw), lambda bi, g: (ctx_blk0 + bi, g))
    ctx_kv2 = pl.BlockSpec((cl, HEAD_DIM), lambda bi, g: (ctx_blk0 + bi, g))
    o_ctx = pl.pallas_call(
        functools.partial(_attn_kernel, n_lat_chunks=0, tk=tk),
        grid=(b, N_KV_HEADS),
        in_specs=[ctx_q, ctx_kv2, ctx_kv2],
        out_specs=pl.BlockSpec((cl, gw), lambda bi, g: (bi, g)),
        out_shape=jax.ShapeDtypeStruct((t - n_lat, hq), BF16),
        compiler_params=_cparams(("parallel", "parallel")),
        name="attention_context",
    )(q, k, v)
    return jnp.concatenate([o_lat, o_ctx], axis=0)


def _in_proj_kernel(x_ref, shift_ref, scale_ref, w_ref, o_ref):
    h = (x_ref[...] * (1.0 + scale_ref[...]) + shift_ref[...]).astype(BF16)
    o_ref[...] = jnp.dot(h, w_ref[...], preferred_element_type=F32)


def _in_project(xall, mod, w_in, n_rows, n_lat, seq, ctx_row):
    d = xall.shape[1]
    n_out = w_in.shape[1]
    tm = ROW_TILE
    return pl.pallas_call(
        _in_proj_kernel,
        grid=(n_rows // tm,),
        in_specs=[
            pl.BlockSpec((tm, d), lambda i: (i, 0)),
            _mod_spec(0, tm, n_lat, seq, ctx_row, d),
            _mod_spec(1, tm, n_lat, seq, ctx_row, d),
            pl.BlockSpec(w_in.shape, lambda i: (0, 0)),
        ],
        out_specs=pl.BlockSpec((tm, n_out), lambda i: (i, 0)),
        out_shape=jax.ShapeDtypeStruct((n_rows, n_out), F32),
        compiler_params=_cparams(("parallel",)),
        name="conv_in_proj",
    )(xall, mod, mod, w_in.astype(BF16))


def _conv_kernel(bg_ref, cg_ref, v_ref, taps_ref, o_ref):
    u = cg_ref[...] * v_ref[...]
    n = u.shape[0]
    pos = lax.broadcasted_iota(I32, u.shape, 0)
    prev = jnp.where(pos == 0, 0.0, pltpu.roll(u, 1, axis=0))
    nxt = jnp.where(pos == n - 1, 0.0, pltpu.roll(u, n - 1, axis=0))
    conv = prev * taps_ref[0:1, :] + u * taps_ref[1:2, :] + nxt * taps_ref[2:3, :]
    o_ref[...] = (bg_ref[...] * conv).astype(BF16)


def _short_conv(zin, taps, n_seqs, seq, d):
    tc = CONV_LANES
    nj = d // tc
    return pl.pallas_call(
        _conv_kernel,
        grid=(n_seqs, nj),
        in_specs=[
            pl.BlockSpec((seq, tc), lambda s, j: (s, j)),
            pl.BlockSpec((seq, tc), lambda s, j: (s, nj + j)),
            pl.BlockSpec((seq, tc), lambda s, j: (s, 2 * nj + j)),
            pl.BlockSpec((taps.shape[0], tc), lambda s, j: (0, j)),
        ],
        out_specs=pl.BlockSpec((seq, tc), lambda s, j: (s, j)),
        out_shape=jax.ShapeDtypeStruct((n_seqs * seq, d), BF16),
        compiler_params=_cparams(("parallel", "parallel")),
        name="short_conv",
    )(zin, zin, zin, taps)


def _layer_norm(z, g, b):
    mu = jnp.mean(z, axis=-1, keepdims=True)
    zc = z - mu
    var = jnp.mean(zc * zc, axis=-1, keepdims=True)
    return zc * lax.rsqrt(var + LN_EPS) * g + b


def _post_kernel(a_ref, x_ref, gate_ref, lng_ref, lnb_ref, shift_ref, scale_ref, w_ref,
                 rwt_ref, rb_ref, tri_ref,
                 x1_ref, h2_ref, eidx_ref, gates_ref, rank_ref, cnt_ref, rwh_ref, rwl_ref, *,
                 dn_alpha, split_blk):
    tm = a_ref.shape[0]

    @pl.when(pl.program_id(0) == 0)
    def _():
        hi = rwt_ref[...].astype(BF16)
        rwh_ref[...] = hi
        rwl_ref[...] = (rwt_ref[...] - hi.astype(F32)).astype(BF16)

    d = x_ref.shape[1]
    y = jnp.dot(a_ref[...], w_ref[...], preferred_element_type=F32)
    x1 = _layer_norm(dn_alpha * x_ref[...] + gate_ref[...] * y, lng_ref[...], lnb_ref[...])
    x1_ref[...] = x1
    h2 = x1 * (1.0 + scale_ref[...]) + shift_ref[...]
    h2_ref[...] = _pack_rows(h2)

    h_hi = h2.astype(BF16)
    h_lo = (h2 - h_hi.astype(F32)).astype(BF16)

    def nt_dot(w, h):
        return lax.dot_general(w, h, (((1,), (1,)), ((), ())), preferred_element_type=F32)

    logits = nt_dot(rwh_ref[...], h_hi) + (nt_dot(rwl_ref[...], h_hi) + nt_dot(rwh_ref[...], h_lo))
    scores = jax.nn.sigmoid(logits)
    biased = scores + rb_ref[...]
    neg = -jnp.inf
    big = jnp.int32(1 << 30)
    row = lax.broadcasted_iota(I32, (N_EXPERTS, tm), 0)

    def argmax_rows(vals, idx):
        mx = jnp.max(vals, axis=0, keepdims=True)
        return mx, jnp.min(jnp.where(vals == mx, idx, big), axis=0, keepdims=True)

    gs = []
    grp_row = lax.broadcasted_iota(I32, (PER_GROUP, tm), 0)
    for g in range(N_GROUPS):
        bg = biased[g * PER_GROUP:(g + 1) * PER_GROUP]
        ig = grp_row + g * PER_GROUP
        m1, i1 = argmax_rows(bg, ig)
        m2 = jnp.max(jnp.where(ig == i1, neg, bg), axis=0, keepdims=True)
        gs.append(m1 + m2)
    gsc = jnp.concatenate(gs, axis=0)
    grow = lax.broadcasted_iota(I32, (N_GROUPS, tm), 0)
    gsel = jnp.zeros((N_GROUPS, tm), F32)
    for _ in range(TOPK_GROUPS):
        _, gi = argmax_rows(gsc, grow)
        hit = grow == gi
        gsel = jnp.where(hit, 1.0, gsel)
        gsc = jnp.where(hit, neg, gsc)
    cur = jnp.concatenate(
        [jnp.where(gsel[g:g + 1] > 0.0, biased[g * PER_GROUP:(g + 1) * PER_GROUP], neg)
         for g in range(N_GROUPS)], axis=0)

    onehot = jnp.zeros((N_EXPERTS, tm), F32)
    idxs, gvals = [], []
    for _ in range(TOP_K):
        _, ei = argmax_rows(cur, row)
        hit = row == ei
        gvals.append(jnp.sum(jnp.where(hit, scores, 0.0), axis=0, keepdims=True))
        idxs.append(ei)
        onehot = jnp.where(hit, 1.0, onehot)
        cur = jnp.where(hit, neg, cur)
    gv = jnp.concatenate(gvals, axis=0)
    gates_ref[...] = gv / jnp.sum(gv, axis=0, keepdims=True) * ROUTED_SCALE
    eidx_ref[...] = jnp.concatenate(idxs, axis=0)

    @pl.when((pl.program_id(0) == 0) | (pl.program_id(0) == split_blk))
    def _():
        cnt_ref[...] = jnp.zeros_like(cnt_ref)

    prefix = jnp.dot(onehot.astype(BF16), tri_ref[...], preferred_element_type=F32)
    pos = prefix + cnt_ref[...]
    rank_ref[...] = jnp.concatenate(
        [jnp.sum(jnp.where(row == ei, pos, 0.0), axis=0, keepdims=True) for ei in idxs],
        axis=0).astype(I32)
    cnt_ref[...] = cnt_ref[...] + jnp.sum(onehot, axis=1, keepdims=True)


def _post_mixer(a, xall, mod, ln_g, ln_b, w, router_w, router_bias, n_rows, n_lat, seq, ctx_row,
                dn_alpha, split_blk):
    d = xall.shape[1]
    tm = ROW_TILE
    n_parts = 2 if split_blk else 1
    tri = (lax.broadcasted_iota(I32, (tm, tm), 0) < lax.broadcasted_iota(I32, (tm, tm), 1)).astype(BF16)
    const = lambda i: (0, 0)
    row_blk = pl.BlockSpec((tm, d), lambda i: (i, 0))
    k_blk = pl.BlockSpec((TOP_K, tm), lambda i: (0, i))
    return pl.pallas_call(
        functools.partial(_post_kernel, dn_alpha=dn_alpha, split_blk=split_blk),
        grid=(n_rows // tm,),
        in_specs=[
            row_blk, row_blk,
            _mod_spec(2, tm, n_lat, seq, ctx_row, d),
            pl.BlockSpec((1, d), const), pl.BlockSpec((1, d), const),
            _mod_spec(3, tm, n_lat, seq, ctx_row, d),
            _mod_spec(4, tm, n_lat, seq, ctx_row, d),
            pl.BlockSpec(w.shape, const),
            pl.BlockSpec((N_EXPERTS, d), const),
            pl.BlockSpec((N_EXPERTS, 1), const),
            pl.BlockSpec((tm, tm), const),
        ],
        out_specs=[
            row_blk, pl.BlockSpec((tm, d // 2), lambda i: (i, 0)),
            k_blk, k_blk, k_blk,
            pl.BlockSpec((None, N_EXPERTS, 1), lambda i: (jnp.where(i >= split_blk, n_parts - 1, 0), 0, 0)),
        ],
        out_shape=[
            jax.ShapeDtypeStruct((n_rows, d), F32),
            jax.ShapeDtypeStruct((n_rows, d // 2), U32),
            jax.ShapeDtypeStruct((TOP_K, n_rows), I32),
            jax.ShapeDtypeStruct((TOP_K, n_rows), F32),
            jax.ShapeDtypeStruct((TOP_K, n_rows), I32),
            jax.ShapeDtypeStruct((n_parts, N_EXPERTS, 1), F32),
        ],
        scratch_shapes=[pltpu.VMEM((N_EXPERTS, d), BF16), pltpu.VMEM((N_EXPERTS, d), BF16)],
        compiler_params=_cparams(("arbitrary",)),
        name="post_mixer_router",
    )(a, xall, mod, ln_g.reshape(1, d), ln_b.reshape(1, d), mod, mod, w.astype(BF16),
      router_w.T, router_bias.reshape(N_EXPERTS, 1), tri)


def _slots_kernel(pstart_ref, eidx_ref, rank_ref, dest_ref):
    e = eidx_ref[...]

    def pick(i, acc):
        return jnp.where(e == i, pstart_ref[i], acc)

    dest_ref[...] = lax.fori_loop(0, N_EXPERTS, pick, jnp.zeros_like(e)) + rank_ref[...]


def _slots(pstarts, eidx, rank):
    n_tok = eidx.shape[1]
    tn = 2048 if n_tok % 2048 == 0 else ROW_TILE
    blk = pl.BlockSpec((TOP_K, tn), lambda i, ps: (0, i))
    return pl.pallas_call(
        _slots_kernel,
        grid_spec=pltpu.PrefetchScalarGridSpec(
            num_scalar_prefetch=1, grid=(n_tok // tn,), in_specs=[blk, blk], out_specs=blk),
        out_shape=jax.ShapeDtypeStruct((TOP_K, n_tok), I32),
        compiler_params=_cparams(("arbitrary",)),
        name="moe_slots",
    )(pstarts, eidx, rank)


def _dispatch_kernel(dest_ref, h_ref, xs_hbm, sem):
    td = dest_ref.shape[1]

    def issue(t, carry):
        for k in range(TOP_K):
            pltpu.make_async_copy(h_ref.at[pl.ds(t, 1)], xs_hbm.at[pl.ds(dest_ref[k, t], 1)], sem).start()
        return carry

    lax.fori_loop(0, td, issue, 0)
    pltpu.make_async_copy(xs_hbm.at[pl.ds(0, TOP_K * td)], xs_hbm.at[pl.ds(0, TOP_K * td)], sem).wait()


def _dispatch(dest, h2, n_slots):
    n_tok = dest.shape[1]
    d = h2.shape[1]
    td = DISPATCH_TOK
    return pl.pallas_call(
        _dispatch_kernel,
        grid=(n_tok // td,),
        in_specs=[
            pl.BlockSpec((TOP_K, td), lambda i: (0, i), memory_space=pltpu.SMEM),
            pl.BlockSpec((td, d), lambda i: (i, 0)),
        ],
        out_specs=pl.BlockSpec(memory_space=pl.ANY),
        out_shape=jax.ShapeDtypeStruct((n_slots, d), h2.dtype),
        scratch_shapes=[pltpu.SemaphoreType.DMA(())],
        compiler_params=_cparams(("arbitrary",)),
        name="moe_dispatch",
    )(dest, h2)


def _gmm_kernel(gb0_ref, nblk_ref, u0_ref, n_used_ref, xa_hbm, xb_hbm, wgu_ref, wd_ref, ys_hbm,
                xbuf, ybuf, wgu_bf, wd_bf, sem_in, sem_out):
    g = pl.program_id(0)
    n_in, bm = xbuf.shape[:2]
    n_out = ybuf.shape[0]
    ff = wd_ref.shape[0]
    n_used = n_used_ref[0]
    rows_a = xa_hbm.shape[0]

    def x_copy(src_hbm, row0, b):
        rows = pl.ds(row0 if isinstance(row0, int) else pl.multiple_of(row0, bm), bm)
        return pltpu.make_async_copy(src_hbm.at[rows], xbuf.at[b % n_in], sem_in.at[b % n_in])

    def x_start(b):
        u0 = u0_ref[b]

        @pl.when(u0 < rows_a)
        def _():
            x_copy(xa_hbm, u0, b).start()

        @pl.when(u0 >= rows_a)
        def _():
            x_copy(xb_hbm, u0 - rows_a, b).start()

    def x_wait(b):
        x_copy(xa_hbm, 0, b).wait()

    def y_copy(b, slot):
        rows = pl.ds(pl.multiple_of(u0_ref[b], bm), bm)
        return pltpu.make_async_copy(ybuf.at[slot], ys_hbm.at[rows], sem_out.at[slot])

    @pl.when(g == 0)
    def _():
        for b in range(n_in - 1):
            @pl.when(b < n_used)
            def _():
                x_start(b)

    nb = nblk_ref[2 * g] + nblk_ref[2 * g + 1]

    @pl.when(nb > 0)
    def _():
        wgu_bf[...] = wgu_ref[...].astype(BF16)
        wd_bf[...] = wd_ref[...].astype(BF16)

    def block(j, carry):
        b = gb0_ref[2 * g] + j
        slot = b % n_out
        x_wait(b)

        @pl.when(b + n_in - 1 < n_used)
        def _():
            x_start(b + n_in - 1)

        @pl.when(b >= n_out)
        def _():
            y_copy(b - n_out, slot).wait()

        gu = jnp.dot(_unpack_rows_bf16(xbuf[b % n_in]), wgu_bf[...], preferred_element_type=F32)
        act = (_silu(gu[:, :ff]) * gu[:, ff:]).astype(BF16)
        ybuf[slot] = _pack_rows(jnp.dot(act, wd_bf[...], preferred_element_type=F32))
        y_copy(b, slot).start()
        return carry

    lax.fori_loop(0, nb, block, 0)

    @pl.when(g == pl.num_programs(0) - 1)
    def _():
        for back in range(n_out, 0, -1):
            @pl.when(n_used >= back)
            def _():
                y_copy(n_used - back, (n_used - back) % n_out).wait()


def _grouped_ffn(xs_a, xs_b, gb0, nblk, u0_blk, n_used, wgu, wd, layer):
    n_slots = xs_a.shape[0] + xs_b.shape[0]
    bm = MOE_BM
    n_exp, d, ff2 = wgu.shape[1:]
    ff = wd.shape[2]
    grid_spec = pltpu.PrefetchScalarGridSpec(
        num_scalar_prefetch=4,
        grid=(n_exp,),
        in_specs=[
            pl.BlockSpec(memory_space=pl.ANY),
            pl.BlockSpec(memory_space=pl.ANY),
            pl.BlockSpec((None, None, d, ff2), lambda g, *_: (layer, g, 0, 0)),
            pl.BlockSpec((None, None, ff, d), lambda g, *_: (layer, g, 0, 0)),
        ],
        out_specs=pl.BlockSpec(memory_space=pl.ANY),
        scratch_shapes=[
            pltpu.VMEM((GMM_IN_SLOTS, bm, d // 2), U32), pltpu.VMEM((GMM_OUT_SLOTS, bm, d // 2), U32),
            pltpu.VMEM((d, ff2), BF16), pltpu.VMEM((ff, d), BF16),
            pltpu.SemaphoreType.DMA((GMM_IN_SLOTS,)), pltpu.SemaphoreType.DMA((GMM_OUT_SLOTS,)),
        ],
    )
    return pl.pallas_call(
        _gmm_kernel,
        grid_spec=grid_spec,
        out_shape=jax.ShapeDtypeStruct((n_slots, d // 2), U32),
        compiler_params=_cparams(("arbitrary",)),
        name="moe_grouped_ffn",
    )(gb0, nblk, u0_blk, n_used, xs_a, xs_b, wgu, wd)


def _shared_expert(h2_ref, swgu_ref, swd_ref):
    ff = swd_ref.shape[0]
    gu = jnp.dot(_unpack_rows_bf16(h2_ref[...]), swgu_ref[...], preferred_element_type=F32)
    act = (_silu(gu[:, :ff]) * gu[:, ff:]).astype(BF16)
    return jnp.dot(act, swd_ref[...], preferred_element_type=F32)


def _combine_kernel(dest_ref, gates_ref, ys_hbm, x1_ref, h2_ref, gate_ref, lng_ref, lnb_ref,
                    swgu_ref, swd_ref, o_ref, buf, sem, *, dn_alpha):
    tc = x1_ref.shape[0]

    def issue(t, carry):
        for k in range(TOP_K):
            pltpu.make_async_copy(ys_hbm.at[pl.ds(dest_ref[k, t], 1)], buf.at[k, pl.ds(t, 1)], sem).start()
        return carry

    lax.fori_loop(0, tc, issue, 0)

    y = _shared_expert(h2_ref, swgu_ref, swd_ref)

    for k in range(TOP_K):
        pltpu.make_async_copy(ys_hbm.at[pl.ds(0, tc)], buf.at[k], sem).wait()
    for k in range(TOP_K):
        y = _add_weighted_rows(y, gates_ref[:, k:k + 1], buf[k])
    o_ref[...] = _layer_norm(dn_alpha * x1_ref[...] + gate_ref[...] * y, lng_ref[...], lnb_ref[...])


def _combine_gathered_kernel(gates_ref, g_ref, x1_ref, h2_ref, gate_ref, lng_ref, lnb_ref,
                             swgu_ref, swd_ref, prev_ref, o_ref, *, dn_alpha):
    del prev_ref
    y = _shared_expert(h2_ref, swgu_ref, swd_ref)
    for k in range(TOP_K):
        y = _add_weighted_rows(y, gates_ref[:, k:k + 1], g_ref[k])
    o_ref[...] = _layer_norm(dn_alpha * x1_ref[...] + gate_ref[...] * y, lng_ref[...], lnb_ref[...])


def _sc_gather_rows(table, idx):
    n = idx.shape[0]
    d = table.shape[1]
    mesh = plsc.VectorSubcoreMesh(core_axis_name="c", subcore_axis_name="s")
    n_workers = mesh.num_cores * mesh.num_subcores
    ch = SC_GATHER_ROWS
    per_w = n // n_workers
    n_ch = per_w // ch
    assert n % (n_workers * ch * 2) == 0

    @functools.partial(
        pl.kernel, mesh=mesh, out_type=jax.ShapeDtypeStruct((n, d), table.dtype),
        scratch_types=[pltpu.VMEM((n_ch, ch), I32),
                       pltpu.VMEM((ch, d), table.dtype), pltpu.VMEM((ch, d), table.dtype),
                       pltpu.SemaphoreType.DMA, pltpu.SemaphoreType.DMA],
        name="moe_sc_gather")
    def gather(table_hbm, idx_hbm, out_hbm, idx_all, rows_a, rows_b, sem_a, sem_b):
        wid = lax.axis_index("s") * mesh.num_cores + lax.axis_index("c")
        pltpu.sync_copy(idx_hbm.at[wid], idx_all)

        def fetch(i, rows_v, sem):
            pltpu.make_async_copy(table_hbm.at[idx_all.at[i]], rows_v, sem).start()

        def drain(i, rows_v, sem):
            pltpu.make_async_copy(table_hbm.at[idx_all.at[i]], rows_v, sem).wait()
            pltpu.sync_copy(rows_v, out_hbm.at[pl.ds(wid * per_w + i * ch, ch)])

        fetch(0, rows_a, sem_a)

        @pl.loop(0, n_ch // 2)
        def _(j):
            i = 2 * j
            fetch(i + 1, rows_b, sem_b)
            drain(i, rows_a, sem_a)

            @pl.when(i + 2 < n_ch)
            def _():
                fetch(i + 2, rows_a, sem_a)

            drain(i + 1, rows_b, sem_b)

    return gather(table, idx.reshape(n_workers, n_ch, ch))


def _sc_scatter_rows(rows, first_row, idx, n_out):
    n_k, n = idx.shape
    d = rows.shape[1]
    mesh = plsc.VectorSubcoreMesh(core_axis_name="c", subcore_axis_name="s")
    n_workers = mesh.num_cores * mesh.num_subcores
    ch = SC_GATHER_ROWS
    per_w = n // n_workers
    n_ch = per_w // ch
    assert n % (n_workers * ch) == 0

    @functools.partial(
        pl.kernel, mesh=mesh, out_type=jax.ShapeDtypeStruct((n_out, d), rows.dtype),
        scratch_types=[pltpu.VMEM((n_k, n_ch, ch), I32), pltpu.VMEM((ch, d), rows.dtype)],
        name="moe_sc_scatter")
    def scatter(rows_hbm, idx_hbm, out_hbm, idx_all, rows_v):
        wid = lax.axis_index("s") * mesh.num_cores + lax.axis_index("c")
        pltpu.sync_copy(idx_hbm.at[wid], idx_all)

        @pl.loop(0, n_ch)
        def _(i):
            pltpu.sync_copy(rows_hbm.at[pl.ds(first_row + wid * per_w + i * ch, ch)], rows_v)
            for k in range(n_k):
                pltpu.sync_copy(rows_v, out_hbm.at[idx_all.at[k, i]])

    return scatter(rows, idx.reshape(n_k, n_workers, n_ch, ch).transpose(1, 0, 2, 3))


def _combine(dest, gates_t, ys, x1, h2, mod, ln_g, ln_b, swgu, swd, n_rows, n_lat, seq, ctx_row,
             dn_alpha):
    d = x1.shape[1]
    tc = DISPATCH_TOK
    const = lambda i: (0, 0)
    n_blocks = n_rows // tc
    n_sc_blocks = (n_blocks * SC_SHARE_PERCENT) // 100
    n_tc = (n_blocks - n_sc_blocks) * tc
    n_sc = n_rows - n_tc
    gathered = _sc_gather_rows(ys, dest[:, n_tc:].reshape(TOP_K * n_sc)).reshape(TOP_K, n_sc, d // 2)
    swgu_bf, swd_bf = swgu.astype(BF16), swd.astype(BF16)
    out_tc = pl.pallas_call(
        functools.partial(_combine_kernel, dn_alpha=dn_alpha),
        grid=(n_tc // tc,),
        in_specs=[
            pl.BlockSpec((TOP_K, tc), lambda i: (0, i), memory_space=pltpu.SMEM),
            pl.BlockSpec((tc, TOP_K), lambda i: (i, 0)),
            pl.BlockSpec(memory_space=pl.ANY),
            pl.BlockSpec((tc, d), lambda i: (i, 0)),
            pl.BlockSpec((tc, d // 2), lambda i: (i, 0)),
            _mod_spec(5, tc, n_lat, seq, ctx_row, d),
            pl.BlockSpec((1, d), const), pl.BlockSpec((1, d), const),
            pl.BlockSpec(swgu.shape, const),
            pl.BlockSpec(swd.shape, const),
        ],
        out_specs=pl.BlockSpec((tc, d), lambda i: (i, 0)),
        out_shape=jax.ShapeDtypeStruct((n_rows, d), F32),
        scratch_shapes=[pltpu.VMEM((TOP_K, tc, d // 2), U32), pltpu.SemaphoreType.DMA(())],
        compiler_params=_cparams(("arbitrary",)),
        name="moe_combine",
    )(dest, gates_t, ys, x1, h2, mod, ln_g.reshape(1, d), ln_b.reshape(1, d), swgu_bf, swd_bf)

    tl = COMBINE_GATHERED_TOK
    off = n_tc // tl
    row_blk = pl.BlockSpec((tl, d), lambda i: (i + off, 0))
    return pl.pallas_call(
        functools.partial(_combine_gathered_kernel, dn_alpha=dn_alpha),
        grid=(n_sc // tl,),
        in_specs=[
            pl.BlockSpec((tl, TOP_K), lambda i: (i + off, 0)),
            pl.BlockSpec((TOP_K, tl, d // 2), lambda i: (0, i, 0)),
            row_blk, pl.BlockSpec((tl, d // 2), lambda i: (i + off, 0)),
            _mod_spec(5, tl, n_lat, seq, ctx_row, d, blk_off=off),
            pl.BlockSpec((1, d), const), pl.BlockSpec((1, d), const),
            pl.BlockSpec(swgu.shape, const),
            pl.BlockSpec(swd.shape, const),
            pl.BlockSpec(memory_space=pl.ANY),
        ],
        out_specs=row_blk,
        out_shape=jax.ShapeDtypeStruct((n_rows, d), F32),
        input_output_aliases={9: 0},
        compiler_params=_cparams(("parallel",)),
        name="moe_combine_gathered",
    )(gates_t, gathered, x1, h2, mod, ln_g.reshape(1, d), ln_b.reshape(1, d), swgu_bf, swd_bf, out_tc)


def _split_block(n_rows):
    n_blocks = n_rows // ROW_TILE
    sc_blocks = (n_blocks * SC_DISPATCH_PERCENT) // 100 // 4 * 4
    return n_blocks - sc_blocks


def _max_slots(n_tok, bm):
    return (n_tok * TOP_K + N_EXPERTS * (bm - 1) + bm - 1) // bm * bm


def _moe(x1, h2, eidx, gates, rank, counts, mod, ln_g, ln_b, wgu, wd, layer, swgu, swd,
         n_rows, n_lat, seq, ctx_row, dn_alpha, split_blk):
    bm = MOE_BM
    t0 = split_blk * ROW_TILE
    rows_a, rows_b = (_max_slots(t0, bm) if t0 else bm), _max_slots(n_rows - t0, bm)
    cnt = counts.reshape(-1, N_EXPERTS).astype(I32)
    if not t0:
        cnt = jnp.concatenate([jnp.zeros_like(cnt), cnt], axis=0)
    pcnt = (cnt + bm - 1) // bm * bm
    pend = jnp.cumsum(pcnt, axis=1)
    ustart = pend - pcnt + jnp.array([[0], [rows_a]], I32)
    dest = _slots(ustart[1], eidx[:, t0:], rank[:, t0:])
    if t0:
        dest = jnp.concatenate([_slots(ustart[0], eidx[:, :t0], rank[:, :t0]), dest], axis=1)

    nblk = (pcnt // bm).T.reshape(2 * N_EXPERTS)
    gb0 = jnp.cumsum(nblk) - nblk
    u0_grp = ustart.T.reshape(2 * N_EXPERTS)
    blk = jnp.arange((rows_a + rows_b) // bm, dtype=I32)
    grp = jnp.sum((gb0[None, :] <= blk[:, None]).astype(I32), axis=1) - 1
    u0_blk = u0_grp[grp] + (blk - gb0[grp]) * bm
    n_used = jnp.sum(nblk).astype(I32).reshape(1)

    xs_a = _dispatch(dest[:, :t0], h2, rows_a) if t0 else jnp.zeros((rows_a, h2.shape[1]), h2.dtype)
    xs_b = _sc_scatter_rows(h2, t0, dest[:, t0:] - rows_a, rows_b)
    ys = _grouped_ffn(xs_a, xs_b, gb0, nblk, u0_blk, n_used, wgu, wd, layer)
    return _combine(dest, gates.T, ys, x1, h2, mod, ln_g, ln_b, swgu, swd,
                    n_rows, n_lat, seq, ctx_row, dn_alpha)


def kernel(x, c, ctx, c_ctx, ada_w, ada_b, ln_g, ln_b, attn_w_qkv, attn_q_norm, attn_k_norm, attn_w_o, conv_w_in, conv_taps, conv_w_out, router_w, router_bias, exp_w_gate_up, exp_w_down, shared_w_gate_up, shared_w_down):
    b, seq, d = x.shape
    cl = ctx.shape[1]
    depth = ada_w.shape[0]
    n_lat = b * seq
    n_ctx = b * cl
    dn_alpha = (2 * depth) ** 0.25
    assert depth == 2 and b < MOD_ROWS
    assert seq % ROW_TILE == 0 and n_ctx % ROW_TILE == 0 and seq % GRID_W == 0

    cond = jnp.zeros((MOD_ROWS, d), F32).at[:b].set(c).at[b].set(c_ctx)
    mod = _modulation(cond, ada_w, ada_b)
    xall = jnp.concatenate([x.reshape(n_lat, d), ctx.reshape(n_ctx, d)], axis=0)
    n_all = n_lat + n_ctx

    q, k, v = _qkv_project(xall, mod[0], attn_w_qkv[0], attn_q_norm[0], attn_k_norm[0], n_lat, seq, b)
    o = _attention(q, k, v, b, seq, cl, n_lat)
    x1, h2, eidx, gates, rank, counts = _post_mixer(
        o, xall, mod[0], ln_g[0, 0], ln_b[0, 0], attn_w_o[0], router_w[0], router_bias[0],
        n_all, n_lat, seq, b, dn_alpha, _split_block(n_all))
    xall = _moe(x1, h2, eidx, gates, rank, counts, mod[0], ln_g[0, 1], ln_b[0, 1],
                exp_w_gate_up, exp_w_down, 0, shared_w_gate_up[0], shared_w_down[0],
                n_all, n_lat, seq, b, dn_alpha, _split_block(n_all))

    zin = _in_project(xall, mod[1], conv_w_in[0], n_lat, n_lat, seq, b)
    a = _short_conv(zin, conv_taps[0], b, seq, d)
    x1, h2, eidx, gates, rank, counts = _post_mixer(
        a, xall, mod[1], ln_g[1, 0], ln_b[1, 0], conv_w_out[0], router_w[1], router_bias[1],
        n_lat, n_lat, seq, b, dn_alpha, _split_block(n_lat))
    out = _moe(x1, h2, eidx, gates, rank, counts, mod[1], ln_g[1, 1], ln_b[1, 1],
               exp_w_gate_up, exp_w_down, 1, shared_w_gate_up[1], shared_w_down[1],
               n_lat, n_lat, seq, b, dn_alpha, _split_block(n_lat))
    return out.reshape(b, seq, d)
```

```python
import functools

import jax
import jax.numpy as jnp
from jax import lax
from jax.experimental import pallas as pl
from jax.experimental.pallas import tpu as pltpu
from jax.experimental.pallas import tpu_sc as plsc

F32 = jnp.float32
BF16 = jnp.bfloat16
I32 = jnp.int32

N_HEADS = 8
N_KV_HEADS = 2
HEAD_DIM = 128
KV_GROUP = N_HEADS // N_KV_HEADS
GRID_W = 64
ROPE_THETA = 10000.0
N_EXPERTS = 256
TOP_K = 8
N_GROUPS = 8
TOPK_GROUPS = 4
PER_GROUP = N_EXPERTS // N_GROUPS
ROUTED_SCALE = 2.5
LN_EPS = 1e-5
QK_EPS = 1e-6
N_MOD = 6
MOD_ROWS = 16

LANES = 128
SUBLANES = 8
VMEM_LIMIT = 56 * 1024 * 1024

ROW_TILE = 512
ATTN_TQ = 256
ATTN_TK = 2048
MOE_BM = 256
GMM_IN_SLOTS = 4
GMM_OUT_SLOTS = 3
DISPATCH_TOK = 512
COMBINE_GATHERED_TOK = 512
SC_GATHER_ROWS = 64
SC_SHARE_PERCENT = 82
SC_DISPATCH_PERCENT = 100
CONV_LANES = 128

HIGHEST = lax.Precision.HIGHEST
LOG2_E = 1.4426950408889634


def _cparams(sem):
    return pltpu.CompilerParams(dimension_semantics=sem, vmem_limit_bytes=VMEM_LIMIT)


def _silu(v):
    return v * jax.nn.sigmoid(v)


U32 = jnp.uint32
HI_MASK = 0xFFFF0000


def _pack_rows(v):
    half = v.shape[1] // 2
    lo = lax.bitcast_convert_type(v[:, :half].astype(BF16).astype(F32), U32)
    hi = lax.bitcast_convert_type(v[:, half:].astype(BF16).astype(F32), U32)
    return (lo >> 16) | (hi & U32(HI_MASK))


def _add_weighted_rows(y, gate, packed):
    half = y.shape[1] // 2
    lo = lax.bitcast_convert_type(packed << 16, F32)
    hi = lax.bitcast_convert_type(packed & U32(HI_MASK), F32)
    return jnp.concatenate([y[:, :half] + gate * lo, y[:, half:] + gate * hi], axis=-1)


def _unpack_rows_bf16(w):
    lo = lax.bitcast_convert_type(w << 16, F32)
    hi = lax.bitcast_convert_type(w & U32(HI_MASK), F32)
    return jnp.concatenate([lo.astype(BF16), hi.astype(BF16)], axis=-1)


def _mod_kernel(c_ref, w_ref, b_ref, o_ref):
    s = _silu(c_ref[...])
    o_ref[...] = jnp.dot(s, w_ref[...], precision=HIGHEST, preferred_element_type=F32) + b_ref[...]


def _modulation(cond, ada_w, ada_b):
    depth, d, nd = ada_w.shape
    tn = 1536
    out = pl.pallas_call(
        _mod_kernel,
        grid=(depth, nd // tn),
        in_specs=[
            pl.BlockSpec((MOD_ROWS, d), lambda l, j: (0, 0)),
            pl.BlockSpec((None, d, tn), lambda l, j: (l, 0, j)),
            pl.BlockSpec((None, 1, tn), lambda l, j: (l, 0, j)),
        ],
        out_specs=pl.BlockSpec((None, MOD_ROWS, tn), lambda l, j: (l, 0, j)),
        out_shape=jax.ShapeDtypeStruct((depth, MOD_ROWS, nd), F32),
        compiler_params=_cparams(("arbitrary", "arbitrary")),
        name="adaln_modulation",
    )(cond, ada_w, ada_b.reshape(depth, 1, nd))
    return out.reshape(depth, MOD_ROWS * N_MOD, 1, d)


def _mod_spec(comp, tm, n_lat, seq, ctx_row, d, blk_off=0):
    def index(i, *_):
        row0 = (i + blk_off) * tm
        r = jnp.where(row0 < n_lat, row0 // seq, ctx_row)
        return (r * N_MOD + comp, 0, 0)

    return pl.BlockSpec((None, 1, d), index)


def _qkv_kernel(x_ref, shift_ref, scale_ref, w_ref, qg_ref, kg_ref, cos_ref, sin_ref,
                q_ref, k_ref, v_ref):
    h = (x_ref[...] * (1.0 + scale_ref[...]) + shift_ref[...]).astype(BF16)
    qkv = jnp.dot(h, w_ref[...], preferred_element_type=F32)
    cos = cos_ref[...]
    sin = sin_ref[...]
    hq = N_HEADS * HEAD_DIM
    kd = N_KV_HEADS * HEAD_DIM

    def norm_rope(t, g, post):
        t = t * lax.rsqrt(jnp.mean(t * t, axis=-1, keepdims=True) + QK_EPS) * g
        t = t * cos + pltpu.roll(t, HEAD_DIM // 2, axis=1) * sin
        return (t * post).astype(BF16)

    for hd in range(N_HEADS):
        sl = slice(hd * HEAD_DIM, (hd + 1) * HEAD_DIM)
        q_ref[:, sl] = norm_rope(qkv[:, sl], qg_ref[...], HEAD_DIM ** -0.5 * LOG2_E)
    for hd in range(N_KV_HEADS):
        sl = slice(hd * HEAD_DIM, (hd + 1) * HEAD_DIM)
        k_ref[:, sl] = norm_rope(qkv[:, hq + hd * HEAD_DIM: hq + (hd + 1) * HEAD_DIM], kg_ref[...], 1.0)
    v_ref[...] = qkv[:, hq + kd:].astype(BF16)


def _rope_tables(seq, tm):
    rows = seq // GRID_W
    row = jnp.repeat(jnp.arange(rows, dtype=F32), GRID_W)
    col = jnp.tile(jnp.arange(GRID_W, dtype=F32), rows)
    axis_dim = HEAD_DIM // 2
    freqs = ROPE_THETA ** (-jnp.arange(0, axis_dim, 2, dtype=F32) / axis_dim)
    ang = jnp.concatenate([row[:, None] * freqs, col[:, None] * freqs], axis=-1)
    cos = jnp.concatenate([jnp.cos(ang), jnp.cos(ang)], axis=-1)
    sin = jnp.concatenate([-jnp.sin(ang), jnp.sin(ang)], axis=-1)
    cos = jnp.concatenate([cos, jnp.ones((tm, HEAD_DIM), F32)], axis=0)
    sin = jnp.concatenate([sin, jnp.zeros((tm, HEAD_DIM), F32)], axis=0)
    return cos.reshape(seq // tm + 1, tm, HEAD_DIM), sin.reshape(seq // tm + 1, tm, HEAD_DIM)


def _qkv_project(xall, mod, w_qkv, q_g, k_g, n_lat, seq, ctx_row):
    t, d = xall.shape
    tm = ROW_TILE
    hq = N_HEADS * HEAD_DIM
    kd = N_KV_HEADS * HEAD_DIM
    perm = jnp.concatenate([jnp.arange(0, HEAD_DIM, 2), jnp.arange(1, HEAD_DIM, 2)])
    cols = jnp.concatenate([hd * HEAD_DIM + perm for hd in range(N_HEADS + N_KV_HEADS)]
                           + [jnp.arange(hq + kd, hq + 2 * kd)])
    w = w_qkv[:, cols].astype(BF16)
    cos, sin = _rope_tables(seq, tm)
    n_pos = seq // tm

    def pos_index(i):
        row0 = i * tm
        return (jnp.where(row0 < n_lat, (row0 % seq) // tm, n_pos), 0, 0)

    const = lambda i: (0, 0)
    return pl.pallas_call(
        _qkv_kernel,
        grid=(t // tm,),
        in_specs=[
            pl.BlockSpec((tm, d), lambda i: (i, 0)),
            _mod_spec(0, tm, n_lat, seq, ctx_row, d),
            _mod_spec(1, tm, n_lat, seq, ctx_row, d),
            pl.BlockSpec(w.shape, const),
            pl.BlockSpec((1, HEAD_DIM), const),
            pl.BlockSpec((1, HEAD_DIM), const),
            pl.BlockSpec((None, tm, HEAD_DIM), pos_index),
            pl.BlockSpec((None, tm, HEAD_DIM), pos_index),
        ],
        out_specs=[
            pl.BlockSpec((tm, hq), lambda i: (i, 0)),
            pl.BlockSpec((tm, kd), lambda i: (i, 0)),
            pl.BlockSpec((tm, kd), lambda i: (i, 0)),
        ],
        out_shape=[
            jax.ShapeDtypeStruct((t, hq), BF16),
            jax.ShapeDtypeStruct((t, kd), BF16),
            jax.ShapeDtypeStruct((t, kd), BF16),
        ],
        compiler_params=_cparams(("parallel",)),
        name="qkv_norm_rope",
    )(xall, mod, mod, w, q_g[perm].reshape(1, HEAD_DIM), k_g[perm].reshape(1, HEAD_DIM), cos, sin)


def _attn_kernel(*refs, n_lat_chunks, tk):
    if n_lat_chunks:
        q_ref, kc_ref, vc_ref, kl_ref, vl_ref, o_ref = refs
    else:
        q_ref, kc_ref, vc_ref, o_ref = refs
    tq = q_ref.shape[0]
    q = jnp.concatenate([q_ref[:, h * HEAD_DIM:(h + 1) * HEAD_DIM] for h in range(KV_GROUP)], axis=0)
    rows = KV_GROUP * tq

    def chunk(k, v, m, l, acc):
        s = lax.dot_general(q, k, (((1,), (1,)), ((), ())), preferred_element_type=F32)
        m_new = jnp.maximum(m, jnp.max(s, axis=-1, keepdims=True))
        p = jnp.exp2(s - m_new)
        a = jnp.exp2(m - m_new)
        l = a * l + jnp.sum(p, axis=-1, keepdims=True)
        acc = a * acc + jnp.dot(p.astype(BF16), v, preferred_element_type=F32)
        return m_new, l, acc

    m = jnp.full((rows, 1), -jnp.inf, F32)
    l = jnp.zeros((rows, 1), F32)
    acc = jnp.zeros((rows, HEAD_DIM), F32)
    m, l, acc = chunk(kc_ref[...], vc_ref[...], m, l, acc)
    for c in range(n_lat_chunks):
        m, l, acc = chunk(kl_ref[c * tk:(c + 1) * tk, :], vl_ref[c * tk:(c + 1) * tk, :], m, l, acc)
    o = (acc / l).astype(BF16)
    for h in range(KV_GROUP):
        o_ref[:, h * HEAD_DIM:(h + 1) * HEAD_DIM] = o[h * tq:(h + 1) * tq]


def _attention(q, k, v, b, seq, cl, n_lat):
    t = q.shape[0]
    gw = KV_GROUP * HEAD_DIM
    tq = min(ATTN_TQ, seq)
    tk = min(ATTN_TK, seq)
    nq = seq // tq
    ctx_blk0 = n_lat // cl
    hq = N_HEADS * HEAD_DIM

    ctx_kv = pl.BlockSpec((cl, HEAD_DIM), lambda bi, g, qi: (ctx_blk0 + bi, g))
    lat_kv = pl.BlockSpec((seq, HEAD_DIM), lambda bi, g, qi: (bi, g))
    lat_q = pl.BlockSpec((tq, gw), lambda bi, g, qi: (bi * nq + qi, g))
    o_lat = pl.pallas_call(
        functools.partial(_attn_kernel, n_lat_chunks=seq // tk, tk=tk),
        grid=(b, N_KV_HEADS, nq),
        in_specs=[lat_q, ctx_kv, ctx_kv, lat_kv, lat_kv],
        out_specs=lat_q,
        out_shape=jax.ShapeDtypeStruct((t, hq), BF16),
        compiler_params=_cparams(("parallel", "parallel", "arbitrary")),
        name="attention_latent",
    )(q, k, v, k, v)

    ctx_q = pl.BlockSpec((cl, gw), lambda bi, g: (ctx_blk0 + bi, g))
    ctx_kv2 = pl.BlockSpec((cl, HEAD_DIM), lambda bi, g: (ctx_blk0 + bi, g))
    return pl.pallas_call(
        functools.partial(_attn_ctx_kernel, tk=tk),
        grid=(b, N_KV_HEADS),
        in_specs=[ctx_q, ctx_kv2, ctx_kv2, pl.BlockSpec(memory_space=pl.ANY)],
        out_specs=ctx_q,
        out_shape=jax.ShapeDtypeStruct((t, hq), BF16),
        input_output_aliases={3: 0},
        compiler_params=_cparams(("parallel", "parallel")),
        name="attention_context",
    )(q, k, v, o_lat)


def _attn_ctx_kernel(q_ref, kc_ref, vc_ref, prev_ref, o_ref, *, tk):
    del prev_ref
    _attn_kernel(q_ref, kc_ref, vc_ref, o_ref, n_lat_chunks=0, tk=tk)


def _in_proj_kernel(x_ref, shift_ref, scale_ref, w_ref, o_ref):
    h = (x_ref[...] * (1.0 + scale_ref[...]) + shift_ref[...]).astype(BF16)
    o_ref[...] = jnp.dot(h, w_ref[...], preferred_element_type=F32)


def _in_project(xall, mod, w_in, n_rows, n_lat, seq, ctx_row):
    d = xall.shape[1]
    n_out = w_in.shape[1]
    tm = ROW_TILE
    return pl.pallas_call(
        _in_proj_kernel,
        grid=(n_rows // tm,),
        in_specs=[
            pl.BlockSpec((tm, d), lambda i: (i, 0)),
            _mod_spec(0, tm, n_lat, seq, ctx_row, d),
            _mod_spec(1, tm, n_lat, seq, ctx_row, d),
            pl.BlockSpec(w_in.shape, lambda i: (0, 0)),
        ],
        out_specs=pl.BlockSpec((tm, n_out), lambda i: (i, 0)),
        out_shape=jax.ShapeDtypeStruct((n_rows, n_out), F32),
        compiler_params=_cparams(("parallel",)),
        name="conv_in_proj",
    )(xall, mod, mod, w_in.astype(BF16))


def _conv_kernel(bg_ref, cg_ref, v_ref, taps_ref, o_ref):
    u = cg_ref[...] * v_ref[...]
    n = u.shape[0]
    pos = lax.broadcasted_iota(I32, u.shape, 0)
    prev = jnp.where(pos == 0, 0.0, pltpu.roll(u, 1, axis=0))
    nxt = jnp.where(pos == n - 1, 0.0, pltpu.roll(u, n - 1, axis=0))
    conv = prev * taps_ref[0:1, :] + u * taps_ref[1:2, :] + nxt * taps_ref[2:3, :]
    o_ref[...] = (bg_ref[...] * conv).astype(BF16)


def _short_conv(zin, taps, n_seqs, seq, d):
    tc = CONV_LANES
    nj = d // tc
    return pl.pallas_call(
        _conv_kernel,
        grid=(n_seqs, nj),
        in_specs=[
            pl.BlockSpec((seq, tc), lambda s, j: (s, j)),
            pl.BlockSpec((seq, tc), lambda s, j: (s, nj + j)),
            pl.BlockSpec((seq, tc), lambda s, j: (s, 2 * nj + j)),
            pl.BlockSpec((taps.shape[0], tc), lambda s, j: (0, j)),
        ],
        out_specs=pl.BlockSpec((seq, tc), lambda s, j: (s, j)),
        out_shape=jax.ShapeDtypeStruct((n_seqs * seq, d), BF16),
        compiler_params=_cparams(("parallel", "parallel")),
        name="short_conv",
    )(zin, zin, zin, taps)


def _layer_norm(z, g, b):
    mu = jnp.mean(z, axis=-1, keepdims=True)
    zc = z - mu
    var = jnp.mean(zc * zc, axis=-1, keepdims=True)
    return zc * lax.rsqrt(var + LN_EPS) * g + b


def _post_kernel(a_ref, x_ref, gate_ref, lng_ref, lnb_ref, shift_ref, scale_ref, w_ref,
                 rwt_ref, rb_ref, tri_ref,
                 x1_ref, h2_ref, eidx_ref, gates_ref, rank_ref, cnt_ref, rwh_ref, rwl_ref, *,
                 dn_alpha, split_blk):
    tm = a_ref.shape[0]

    @pl.when(pl.program_id(0) == 0)
    def _():
        hi = rwt_ref[...].astype(BF16)
        rwh_ref[...] = hi
        rwl_ref[...] = (rwt_ref[...] - hi.astype(F32)).astype(BF16)

    d = x_ref.shape[1]
    y = jnp.dot(a_ref[...], w_ref[...], preferred_element_type=F32)
    x1 = _layer_norm(dn_alpha * x_ref[...] + gate_ref[...] * y, lng_ref[...], lnb_ref[...])
    x1_ref[...] = x1
    h2 = x1 * (1.0 + scale_ref[...]) + shift_ref[...]
    h2_ref[...] = _pack_rows(h2)

    h_hi = h2.astype(BF16)
    h_lo = (h2 - h_hi.astype(F32)).astype(BF16)

    def nt_dot(w, h):
        return lax.dot_general(w, h, (((1,), (1,)), ((), ())), preferred_element_type=F32)

    logits = nt_dot(rwh_ref[...], h_hi) + (nt_dot(rwl_ref[...], h_hi) + nt_dot(rwh_ref[...], h_lo))
    scores = jax.nn.sigmoid(logits)
    biased = scores + rb_ref[...]
    neg = -jnp.inf
    big = jnp.int32(1 << 30)
    row = lax.broadcasted_iota(I32, (N_EXPERTS, tm), 0)

    def argmax_rows(vals, idx):
        mx = jnp.max(vals, axis=0, keepdims=True)
        return mx, jnp.min(jnp.where(vals == mx, idx, big), axis=0, keepdims=True)

    gs = []
    grp_row = lax.broadcasted_iota(I32, (PER_GROUP, tm), 0)
    for g in range(N_GROUPS):
        bg = biased[g * PER_GROUP:(g + 1) * PER_GROUP]
        ig = grp_row + g * PER_GROUP
        m1, i1 = argmax_rows(bg, ig)
        m2 = jnp.max(jnp.where(ig == i1, neg, bg), axis=0, keepdims=True)
        gs.append(m1 + m2)
    gsc = jnp.concatenate(gs, axis=0)
    grow = lax.broadcasted_iota(I32, (N_GROUPS, tm), 0)
    gsel = jnp.zeros((N_GROUPS, tm), F32)
    for _ in range(TOPK_GROUPS):
        _, gi = argmax_rows(gsc, grow)
        hit = grow == gi
        gsel = jnp.where(hit, 1.0, gsel)
        gsc = jnp.where(hit, neg, gsc)
    cur = jnp.concatenate(
        [jnp.where(gsel[g:g + 1] > 0.0, biased[g * PER_GROUP:(g + 1) * PER_GROUP], neg)
         for g in range(N_GROUPS)], axis=0)

    onehot = jnp.zeros((N_EXPERTS, tm), F32)
    idxs, gvals = [], []
    for _ in range(TOP_K):
        _, ei = argmax_rows(cur, row)
        hit = row == ei
        gvals.append(jnp.sum(jnp.where(hit, scores, 0.0), axis=0, keepdims=True))
        idxs.append(ei)
        onehot = jnp.where(hit, 1.0, onehot)
        cur = jnp.where(hit, neg, cur)
    gv = jnp.concatenate(gvals, axis=0)
    gates_ref[...] = gv / jnp.sum(gv, axis=0, keepdims=True) * ROUTED_SCALE
    eidx_ref[...] = jnp.concatenate(idxs, axis=0)

    @pl.when((pl.program_id(0) == 0) | (pl.program_id(0) == split_blk))
    def _():
        cnt_ref[...] = jnp.zeros_like(cnt_ref)

    prefix = jnp.dot(onehot.astype(BF16), tri_ref[...], preferred_element_type=F32)
    pos = prefix + cnt_ref[...]
    rank_ref[...] = jnp.concatenate(
        [jnp.sum(jnp.where(row == ei, pos, 0.0), axis=0, keepdims=True) for ei in idxs],
        axis=0).astype(I32)
    cnt_ref[...] = cnt_ref[...] + jnp.sum(onehot, axis=1, keepdims=True)


def _post_mixer(a, xall, mod, ln_g, ln_b, w, router_w, router_bias, n_rows, n_lat, seq, ctx_row,
                dn_alpha, split_blk):
    d = xall.shape[1]
    tm = ROW_TILE
    n_parts = 2 if split_blk else 1
    tri = (lax.broadcasted_iota(I32, (tm, tm), 0) < lax.broadcasted_iota(I32, (tm, tm), 1)).astype(BF16)
    const = lambda i: (0, 0)
    row_blk = pl.BlockSpec((tm, d), lambda i: (i, 0))
    k_blk = pl.BlockSpec((TOP_K, tm), lambda i: (0, i))
    return pl.pallas_call(
        functools.partial(_post_kernel, dn_alpha=dn_alpha, split_blk=split_blk),
        grid=(n_rows // tm,),
        in_specs=[
            row_blk, row_blk,
            _mod_spec(2, tm, n_lat, seq, ctx_row, d),
            pl.BlockSpec((1, d), const), pl.BlockSpec((1, d), const),
            _mod_spec(3, tm, n_lat, seq, ctx_row, d),
            _mod_spec(4, tm, n_lat, seq, ctx_row, d),
            pl.BlockSpec(w.shape, const),
            pl.BlockSpec((N_EXPERTS, d), const),
            pl.BlockSpec((N_EXPERTS, 1), const),
            pl.BlockSpec((tm, tm), const),
        ],
        out_specs=[
            row_blk, pl.BlockSpec((tm, d // 2), lambda i: (i, 0)),
            k_blk, k_blk, k_blk,
            pl.BlockSpec((None, N_EXPERTS, 1), lambda i: (jnp.where(i >= split_blk, n_parts - 1, 0), 0, 0)),
        ],
        out_shape=[
            jax.ShapeDtypeStruct((n_rows, d), F32),
            jax.ShapeDtypeStruct((n_rows, d // 2), U32),
            jax.ShapeDtypeStruct((TOP_K, n_rows), I32),
            jax.ShapeDtypeStruct((TOP_K, n_rows), F32),
            jax.ShapeDtypeStruct((TOP_K, n_rows), I32),
            jax.ShapeDtypeStruct((n_parts, N_EXPERTS, 1), F32),
        ],
        scratch_shapes=[pltpu.VMEM((N_EXPERTS, d), BF16), pltpu.VMEM((N_EXPERTS, d), BF16)],
        compiler_params=_cparams(("arbitrary",)),
        name="post_mixer_router",
    )(a, xall, mod, ln_g.reshape(1, d), ln_b.reshape(1, d), mod, mod, w.astype(BF16),
      router_w.T, router_bias.reshape(N_EXPERTS, 1), tri)


def _slots_kernel(pstart_ref, eidx_ref, rank_ref, dest_ref):
    e = eidx_ref[...]

    def pick(i, acc):
        return jnp.where(e == i, pstart_ref[i], acc)

    dest_ref[...] = lax.fori_loop(0, N_EXPERTS, pick, jnp.zeros_like(e)) + rank_ref[...]


def _slots(pstarts, eidx, rank):
    n_tok = eidx.shape[1]
    tn = 2048 if n_tok % 2048 == 0 else ROW_TILE
    blk = pl.BlockSpec((TOP_K, tn), lambda i, ps: (0, i))
    return pl.pallas_call(
        _slots_kernel,
        grid_spec=pltpu.PrefetchScalarGridSpec(
            num_scalar_prefetch=1, grid=(n_tok // tn,), in_specs=[blk, blk], out_specs=blk),
        out_shape=jax.ShapeDtypeStruct((TOP_K, n_tok), I32),
        compiler_params=_cparams(("arbitrary",)),
        name="moe_slots",
    )(pstarts, eidx, rank)


def _dispatch_kernel(dest_ref, h_ref, xs_hbm, sem):
    td = dest_ref.shape[1]

    def issue(t, carry):
        for k in range(TOP_K):
            pltpu.make_async_copy(h_ref.at[pl.ds(t, 1)], xs_hbm.at[pl.ds(dest_ref[k, t], 1)], sem).start()
        return carry

    lax.fori_loop(0, td, issue, 0)
    pltpu.make_async_copy(xs_hbm.at[pl.ds(0, TOP_K * td)], xs_hbm.at[pl.ds(0, TOP_K * td)], sem).wait()


def _dispatch(dest, h2, n_slots):
    n_tok = dest.shape[1]
    d = h2.shape[1]
    td = DISPATCH_TOK
    return pl.pallas_call(
        _dispatch_kernel,
        grid=(n_tok // td,),
        in_specs=[
            pl.BlockSpec((TOP_K, td), lambda i: (0, i), memory_space=pltpu.SMEM),
            pl.BlockSpec((td, d), lambda i: (i, 0)),
        ],
        out_specs=pl.BlockSpec(memory_space=pl.ANY),
        out_shape=jax.ShapeDtypeStruct((n_slots, d), h2.dtype),
        scratch_shapes=[pltpu.SemaphoreType.DMA(())],
        compiler_params=_cparams(("arbitrary",)),
        name="moe_dispatch",
    )(dest, h2)


def _gmm_kernel(gb0_ref, nblk_ref, u0_ref, n_used_ref, xa_hbm, xb_hbm, wgu_ref, wd_ref, ys_hbm,
                xbuf, ybuf, wgu_bf, wd_bf, sem_in, sem_out):
    g = pl.program_id(0)
    n_in, bm = xbuf.shape[:2]
    n_out = ybuf.shape[0]
    ff = wd_ref.shape[0]
    n_used = n_used_ref[0]
    rows_a = xa_hbm.shape[0]

    def x_copy(src_hbm, row0, b):
        rows = pl.ds(row0 if isinstance(row0, int) else pl.multiple_of(row0, bm), bm)
        return pltpu.make_async_copy(src_hbm.at[rows], xbuf.at[b % n_in], sem_in.at[b % n_in])

    def x_start(b):
        u0 = u0_ref[b]

        @pl.when(u0 < rows_a)
        def _():
            x_copy(xa_hbm, u0, b).start()

        @pl.when(u0 >= rows_a)
        def _():
            x_copy(xb_hbm, u0 - rows_a, b).start()

    def x_wait(b):
        x_copy(xa_hbm, 0, b).wait()

    def y_copy(b, slot):
        rows = pl.ds(pl.multiple_of(u0_ref[b], bm), bm)
        return pltpu.make_async_copy(ybuf.at[slot], ys_hbm.at[rows], sem_out.at[slot])

    @pl.when(g == 0)
    def _():
        for b in range(n_in - 1):
            @pl.when(b < n_used)
            def _():
                x_start(b)

    nb = nblk_ref[2 * g] + nblk_ref[2 * g + 1]

    @pl.when(nb > 0)
    def _():
        wgu_bf[...] = wgu_ref[...].astype(BF16)
        wd_bf[...] = wd_ref[...].astype(BF16)

    def block(j, carry):
        b = gb0_ref[2 * g] + j
        slot = b % n_out
        x_wait(b)

        @pl.when(b + n_in - 1 < n_used)
        def _():
            x_start(b + n_in - 1)

        @pl.when(b >= n_out)
        def _():
            y_copy(b - n_out, slot).wait()

        gu = jnp.dot(_unpack_rows_bf16(xbuf[b % n_in]), wgu_bf[...], preferred_element_type=F32)
        act = (_silu(gu[:, :ff]) * gu[:, ff:]).astype(BF16)
        ybuf[slot] = _pack_rows(jnp.dot(act, wd_bf[...], preferred_element_type=F32))
        y_copy(b, slot).start()
        return carry

    lax.fori_loop(0, nb, block, 0)

    @pl.when(g == pl.num_programs(0) - 1)
    def _():
        for back in range(n_out, 0, -1):
            @pl.when(n_used >= back)
            def _():
                y_copy(n_used - back, (n_used - back) % n_out).wait()


def _grouped_ffn(xs_a, xs_b, gb0, nblk, u0_blk, n_used, wgu, wd, layer):
    n_slots = xs_a.shape[0] + xs_b.shape[0]
    bm = MOE_BM
    n_exp, d, ff2 = wgu.shape[1:]
    ff = wd.shape[2]
    grid_spec = pltpu.PrefetchScalarGridSpec(
        num_scalar_prefetch=4,
        grid=(n_exp,),
        in_specs=[
            pl.BlockSpec(memory_space=pl.ANY),
            pl.BlockSpec(memory_space=pl.ANY),
            pl.BlockSpec((None, None, d, ff2), lambda g, *_: (layer, g, 0, 0)),
            pl.BlockSpec((None, None, ff, d), lambda g, *_: (layer, g, 0, 0)),
        ],
        out_specs=pl.BlockSpec(memory_space=pl.ANY),
        scratch_shapes=[
            pltpu.VMEM((GMM_IN_SLOTS, bm, d // 2), U32), pltpu.VMEM((GMM_OUT_SLOTS, bm, d // 2), U32),
            pltpu.VMEM((d, ff2), BF16), pltpu.VMEM((ff, d), BF16),
            pltpu.SemaphoreType.DMA((GMM_IN_SLOTS,)), pltpu.SemaphoreType.DMA((GMM_OUT_SLOTS,)),
        ],
    )
    return pl.pallas_call(
        _gmm_kernel,
        grid_spec=grid_spec,
        out_shape=jax.ShapeDtypeStruct((n_slots, d // 2), U32),
        compiler_params=_cparams(("arbitrary",)),
        name="moe_grouped_ffn",
    )(gb0, nblk, u0_blk, n_used, xs_a, xs_b, wgu, wd)


def _shared_expert(h2_ref, swgu_ref, swd_ref):
    ff = swd_ref.shape[0]
    gu = jnp.dot(_unpack_rows_bf16(h2_ref[...]), swgu_ref[...], preferred_element_type=F32)
    act = (_silu(gu[:, :ff]) * gu[:, ff:]).astype(BF16)
    return jnp.dot(act, swd_ref[...], preferred_element_type=F32)


def _combine_kernel(dest_ref, gates_ref, ys_hbm, x1_ref, h2_ref, gate_ref, lng_ref, lnb_ref,
                    swgu_ref, swd_ref, o_ref, buf, sem, *, dn_alpha):
    tc = x1_ref.shape[0]

    def issue(t, carry):
        for k in range(TOP_K):
            pltpu.make_async_copy(ys_hbm.at[pl.ds(dest_ref[k, t], 1)], buf.at[k, pl.ds(t, 1)], sem).start()
        return carry

    lax.fori_loop(0, tc, issue, 0)

    y = _shared_expert(h2_ref, swgu_ref, swd_ref)

    for k in range(TOP_K):
        pltpu.make_async_copy(ys_hbm.at[pl.ds(0, tc)], buf.at[k], sem).wait()
    for k in range(TOP_K):
        y = _add_weighted_rows(y, gates_ref[:, k:k + 1], buf[k])
    o_ref[...] = _layer_norm(dn_alpha * x1_ref[...] + gate_ref[...] * y, lng_ref[...], lnb_ref[...])


def _combine_gathered_kernel(gates_ref, g_ref, x1_ref, h2_ref, gate_ref, lng_ref, lnb_ref,
                             swgu_ref, swd_ref, prev_ref, o_ref, *, dn_alpha):
    del prev_ref
    y = _shared_expert(h2_ref, swgu_ref, swd_ref)
    for k in range(TOP_K):
        y = _add_weighted_rows(y, gates_ref[:, k:k + 1], g_ref[k])
    o_ref[...] = _layer_norm(dn_alpha * x1_ref[...] + gate_ref[...] * y, lng_ref[...], lnb_ref[...])


def _sc_gather_rows(table, idx):
    n = idx.shape[0]
    d = table.shape[1]
    mesh = plsc.VectorSubcoreMesh(core_axis_name="c", subcore_axis_name="s")
    n_workers = mesh.num_cores * mesh.num_subcores
    ch = SC_GATHER_ROWS
    per_w = n // n_workers
    n_ch = per_w // ch
    assert n % (n_workers * ch * 2) == 0

    @functools.partial(
        pl.kernel, mesh=mesh, out_type=jax.ShapeDtypeStruct((n, d), table.dtype),
        scratch_types=[pltpu.VMEM((n_ch, ch), I32),
                       pltpu.VMEM((ch, d), table.dtype), pltpu.VMEM((ch, d), table.dtype),
                       pltpu.SemaphoreType.DMA, pltpu.SemaphoreType.DMA],
        name="moe_sc_gather")
    def gather(table_hbm, idx_hbm, out_hbm, idx_all, rows_a, rows_b, sem_a, sem_b):
        wid = lax.axis_index("s") * mesh.num_cores + lax.axis_index("c")
        pltpu.sync_copy(idx_hbm.at[wid], idx_all)

        def fetch(i, rows_v, sem):
            pltpu.make_async_copy(table_hbm.at[idx_all.at[i]], rows_v, sem).start()

        def drain(i, rows_v, sem):
            pltpu.make_async_copy(table_hbm.at[idx_all.at[i]], rows_v, sem).wait()
            pltpu.sync_copy(rows_v, out_hbm.at[pl.ds(wid * per_w + i * ch, ch)])

        fetch(0, rows_a, sem_a)

        @pl.loop(0, n_ch // 2)
        def _(j):
            i = 2 * j
            fetch(i + 1, rows_b, sem_b)
            drain(i, rows_a, sem_a)

            @pl.when(i + 2 < n_ch)
            def _():
                fetch(i + 2, rows_a, sem_a)

            drain(i + 1, rows_b, sem_b)

    return gather(table, idx.reshape(n_workers, n_ch, ch))


def _sc_scatter_rows(rows, first_row, idx, n_out):
    n_k, n = idx.shape
    d = rows.shape[1]
    mesh = plsc.VectorSubcoreMesh(core_axis_name="c", subcore_axis_name="s")
    n_workers = mesh.num_cores * mesh.num_subcores
    ch = SC_GATHER_ROWS
    per_w = n // n_workers
    n_ch = per_w // ch
    assert n % (n_workers * ch) == 0

    @functools.partial(
        pl.kernel, mesh=mesh, out_type=jax.ShapeDtypeStruct((n_out, d), rows.dtype),
        scratch_types=[pltpu.VMEM((n_k, n_ch, ch), I32), pltpu.VMEM((ch, d), rows.dtype)],
        name="moe_sc_scatter")
    def scatter(rows_hbm, idx_hbm, out_hbm, idx_all, rows_v):
        wid = lax.axis_index("s") * mesh.num_cores + lax.axis_index("c")
        pltpu.sync_copy(idx_hbm.at[wid], idx_all)

        @pl.loop(0, n_ch)
        def _(i):
            pltpu.sync_copy(rows_hbm.at[pl.ds(first_row + wid * per_w + i * ch, ch)], rows_v)
            for k in range(n_k):
                pltpu.sync_copy(rows_v, out_hbm.at[idx_all.at[k, i]])

    return scatter(rows, idx.reshape(n_k, n_workers, n_ch, ch).transpose(1, 0, 2, 3))


def _combine(dest, gates_t, ys, x1, h2, mod, ln_g, ln_b, swgu, swd, n_rows, n_lat, seq, ctx_row,
             dn_alpha):
    d = x1.shape[1]
    tc = DISPATCH_TOK
    const = lambda i: (0, 0)
    n_blocks = n_rows // tc
    n_sc_blocks = (n_blocks * SC_SHARE_PERCENT) // 100
    n_tc = (n_blocks - n_sc_blocks) * tc
    n_sc = n_rows - n_tc
    gathered = _sc_gather_rows(ys, dest[:, n_tc:].reshape(TOP_K * n_sc)).reshape(TOP_K, n_sc, d // 2)
    swgu_bf, swd_bf = swgu.astype(BF16), swd.astype(BF16)
    out_tc = pl.pallas_call(
        functools.partial(_combine_kernel, dn_alpha=dn_alpha),
        grid=(n_tc // tc,),
        in_specs=[
            pl.BlockSpec((TOP_K, tc), lambda i: (0, i), memory_space=pltpu.SMEM),
            pl.BlockSpec((tc, TOP_K), lambda i: (i, 0)),
            pl.BlockSpec(memory_space=pl.ANY),
            pl.BlockSpec((tc, d), lambda i: (i, 0)),
            pl.BlockSpec((tc, d // 2), lambda i: (i, 0)),
            _mod_spec(5, tc, n_lat, seq, ctx_row, d),
            pl.BlockSpec((1, d), const), pl.BlockSpec((1, d), const),
            pl.BlockSpec(swgu.shape, const),
            pl.BlockSpec(swd.shape, const),
        ],
        out_specs=pl.BlockSpec((tc, d), lambda i: (i, 0)),
        out_shape=jax.ShapeDtypeStruct((n_rows, d), F32),
        scratch_shapes=[pltpu.VMEM((TOP_K, tc, d // 2), U32), pltpu.SemaphoreType.DMA(())],
        compiler_params=_cparams(("arbitrary",)),
        name="moe_combine",
    )(dest, gates_t, ys, x1, h2, mod, ln_g.reshape(1, d), ln_b.reshape(1, d), swgu_bf, swd_bf)

    tl = COMBINE_GATHERED_TOK
    off = n_tc // tl
    row_blk = pl.BlockSpec((tl, d), lambda i: (i + off, 0))
    return pl.pallas_call(
        functools.partial(_combine_gathered_kernel, dn_alpha=dn_alpha),
        grid=(n_sc // tl,),
        in_specs=[
            pl.BlockSpec((tl, TOP_K), lambda i: (i + off, 0)),
            pl.BlockSpec((TOP_K, tl, d // 2), lambda i: (0, i, 0)),
            row_blk, pl.BlockSpec((tl, d // 2), lambda i: (i + off, 0)),
            _mod_spec(5, tl, n_lat, seq, ctx_row, d, blk_off=off),
            pl.BlockSpec((1, d), const), pl.BlockSpec((1, d), const),
            pl.BlockSpec(swgu.shape, const),
            pl.BlockSpec(swd.shape, const),
            pl.BlockSpec(memory_space=pl.ANY),
        ],
        out_specs=row_blk,
        out_shape=jax.ShapeDtypeStruct((n_rows, d), F32),
        input_output_aliases={9: 0},
        compiler_params=_cparams(("parallel",)),
        name="moe_combine_gathered",
    )(gates_t, gathered, x1, h2, mod, ln_g.reshape(1, d), ln_b.reshape(1, d), swgu_bf, swd_bf, out_tc)


def _split_block(n_rows):
    n_blocks = n_rows // ROW_TILE
    sc_blocks = (n_blocks * SC_DISPATCH_PERCENT) // 100 // 4 * 4
    return n_blocks - sc_blocks


def _max_slots(n_tok, bm):
    return (n_tok * TOP_K + N_EXPERTS * (bm - 1) + bm - 1) // bm * bm


def _moe(x1, h2, eidx, gates, rank, counts, mod, ln_g, ln_b, wgu, wd, layer, swgu, swd,
         n_rows, n_lat, seq, ctx_row, dn_alpha, split_blk):
    bm = MOE_BM
    t0 = split_blk * ROW_TILE
    rows_a, rows_b = (_max_slots(t0, bm) if t0 else bm), _max_slots(n_rows - t0, bm)
    cnt = counts.reshape(-1, N_EXPERTS).astype(I32)
    if not t0:
        cnt = jnp.concatenate([jnp.zeros_like(cnt), cnt], axis=0)
    pcnt = (cnt + bm - 1) // bm * bm
    pend = jnp.cumsum(pcnt, axis=1)
    ustart = pend - pcnt + jnp.array([[0], [rows_a]], I32)
    dest = _slots(ustart[1], eidx[:, t0:], rank[:, t0:])
    if t0:
        dest = jnp.concatenate([_slots(ustart[0], eidx[:, :t0], rank[:, :t0]), dest], axis=1)

    nblk = (pcnt // bm).T.reshape(2 * N_EXPERTS)
    gb0 = jnp.cumsum(nblk) - nblk
    u0_grp = ustart.T.reshape(2 * N_EXPERTS)
    blk = jnp.arange((rows_a + rows_b) // bm, dtype=I32)
    grp = jnp.sum((gb0[None, :] <= blk[:, None]).astype(I32), axis=1) - 1
    u0_blk = u0_grp[grp] + (blk - gb0[grp]) * bm
    n_used = jnp.sum(nblk).astype(I32).reshape(1)

    xs_a = _dispatch(dest[:, :t0], h2, rows_a) if t0 else jnp.zeros((rows_a, h2.shape[1]), h2.dtype)
    xs_b = _sc_scatter_rows(h2, t0, dest[:, t0:] - rows_a, rows_b)
    ys = _grouped_ffn(xs_a, xs_b, gb0, nblk, u0_blk, n_used, wgu, wd, layer)
    return _combine(dest, gates.T, ys, x1, h2, mod, ln_g, ln_b, swgu, swd,
                    n_rows, n_lat, seq, ctx_row, dn_alpha)


def kernel(x, c, ctx, c_ctx, ada_w, ada_b, ln_g, ln_b, attn_w_qkv, attn_q_norm, attn_k_norm, attn_w_o, conv_w_in, conv_taps, conv_w_out, router_w, router_bias, exp_w_gate_up, exp_w_down, shared_w_gate_up, shared_w_down):
    b, seq, d = x.shape
    cl = ctx.shape[1]
    depth = ada_w.shape[0]
    n_lat = b * seq
    n_ctx = b * cl
    dn_alpha = (2 * depth) ** 0.25
    assert depth == 2 and b < MOD_ROWS
    assert seq % ROW_TILE == 0 and n_ctx % ROW_TILE == 0 and seq % GRID_W == 0

    cond = jnp.zeros((MOD_ROWS, d), F32).at[:b].set(c).at[b].set(c_ctx)
    mod = _modulation(cond, ada_w, ada_b)
    xall = jnp.concatenate([x.reshape(n_lat, d), ctx.reshape(n_ctx, d)], axis=0)
    n_all = n_lat + n_ctx

    q, k, v = _qkv_project(xall, mod[0], attn_w_qkv[0], attn_q_norm[0], attn_k_norm[0], n_lat, seq, b)
    o = _attention(q, k, v, b, seq, cl, n_lat)
    x1, h2, eidx, gates, rank, counts = _post_mixer(
        o, xall, mod[0], ln_g[0, 0], ln_b[0, 0], attn_w_o[0], router_w[0], router_bias[0],
        n_all, n_lat, seq, b, dn_alpha, _split_block(n_all))
    xall = _moe(x1, h2, eidx, gates, rank, counts, mod[0], ln_g[0, 1], ln_b[0, 1],
                exp_w_gate_up, exp_w_down, 0, shared_w_gate_up[0], shared_w_down[0],
                n_all, n_lat, seq, b, dn_alpha, _split_block(n_all))

    zin = _in_project(xall, mod[1], conv_w_in[0], n_lat, n_lat, seq, b)
    a = _short_conv(zin, conv_taps[0], b, seq, d)
    x1, h2, eidx, gates, rank, counts = _post_mixer(
        a, xall, mod[1], ln_g[1, 0], ln_b[1, 0], conv_w_out[0], router_w[1], router_bias[1],
        n_lat, n_lat, seq, b, dn_alpha, _split_block(n_lat))
    out = _moe(x1, h2, eidx, gates, rank, counts, mod[1], ln_g[1, 1], ln_b[1, 1],
               exp_w_gate_up, exp_w_down, 1, shared_w_gate_up[1], shared_w_down[1],
               n_lat, n_lat, seq, b, dn_alpha, _split_block(n_lat))
    return out.reshape(b, seq, d)
```
